```python
import math
import jax
import jax.numpy as jnp
from jax import lax
import numpy as np

D_MODEL = 1024
BATCH = 32
SEQ = 2048
DEPTH = 1

CTX_LEN = 256
GRID_W = 64
MIX_W = D_MODEL
HY_W = MIX_W // 2
HG_W = MIX_W - HY_W
HY_ORDER = 2
HY_EMB = 33
HY_BANDS = (HY_EMB - 1) // 2
HY_FILTER_HIDDEN = 64
HY_DECAY_TARGET = 1e-2
HY_FAST_DECAY_PCT = 0.3
HY_SLOW_DECAY_PCT = 1.5
HG_HEAD_DIM = 128
HG_HEADS = HG_W // HG_HEAD_DIM
HG_SCALE = HG_HEAD_DIM ** -0.5
HG_CHUNK = 64
N_EXPERTS = 256
TOP_K = 8
N_GROUPS = 8
TOPK_GROUPS = 4
EXPERT_FF = 256
SHARED_FF = 256
ROUTED_SCALE = 2.5
MOE_BLOCK = 128
NORM_EPS = 1e-6
IN_COLS = 3 * HY_W + 5 * HG_W

kernel_name = 'hyena_hgrn2_moe_diffusion_block'


def rmsnorm(x, g):
    xf = x.astype(jnp.float32)
    y = xf * lax.rsqrt(jnp.mean(xf * xf, axis=-1, keepdims=True) + NORM_EPS)
    return (y * g.astype(jnp.float32)).astype(x.dtype)


def centred_conv3(u, w, b):
    pad = [(0, 0)] * (u.ndim - 2) + [(1, 1), (0, 0)]
    up = jnp.pad(u, pad)
    return w[0] * up[..., :-2, :] + w[1] * up[..., 1:-1, :] + w[2] * up[..., 2:, :] + b


def hyena_filter_spectrum(L, fw1, fb1, fw2, fb2, fw3, freq):
    f32 = jnp.float32
    pos = jnp.arange(L, dtype=f32)[:, None]
    t = pos / max(L - 1, 1)
    w = (2.0 * math.pi / L) * pos
    bands = jnp.linspace(1e-4, HY_BANDS - 1, HY_BANDS, dtype=f32)[None, :]
    feats = jnp.concatenate([t, jnp.cos(bands * w), -jnp.sin(bands * w)], axis=-1)
    fr = freq.astype(f32)
    h = jnp.sin(fr * (feats @ fw1.astype(f32) + fb1.astype(f32)))
    h = jnp.sin(fr * (h @ fw2.astype(f32) + fb2.astype(f32)))
    h = (h @ fw3.astype(f32)).reshape(L, HY_ORDER, 2, HY_W)
    max_decay = math.log(HY_DECAY_TARGET) / HY_FAST_DECAY_PCT
    min_decay = math.log(HY_DECAY_TARGET) / HY_SLOW_DECAY_PCT
    deltas = jnp.abs(jnp.linspace(min_decay, max_decay, HY_W, dtype=f32))
    h = h * jnp.exp(-t * deltas)[:, None, None, :]
    h = h / jnp.sum(jnp.abs(h), axis=(0, 2), keepdims=True)
    fwd, bwd = h[:, :, 0], h[:, :, 1]
    k_circ = jnp.concatenate([fwd[:1] + bwd[:1], fwd[1:], jnp.zeros_like(fwd[:1]), bwd[1:][::-1]], axis=0)
    return jnp.fft.rfft(k_circ, axis=0)


def hyena_group(p, rows, conv_w, conv_b, fw1, fb1, fw2, fb2, fw3, freq, d_skip):
    B, L, C3 = p.shape
    u = p.astype(jnp.float32)
    cw, cb = conv_w.astype(jnp.float32), conv_b.astype(jnp.float32)
    if rows is None:
        u = centred_conv3(u, cw, cb)
    else:
        u = centred_conv3(u.reshape(B, rows, GRID_W, C3), cw, cb).reshape(B, L, C3)
    x1, x2, v = jnp.split(u, 3, axis=-1)
    k_hat = hyena_filter_spectrum(L, fw1, fb1, fw2, fb2, fw3, freq)
    ds = d_skip.astype(jnp.float32)

    def long_conv(z, o):
        z_hat = jnp.fft.rfft(z, n=2 * L, axis=1)
        y = jnp.fft.irfft(z_hat * k_hat[:, o], n=2 * L, axis=1)[:, :L]
        return y + z * ds[o]

    z = x1 * long_conv(v, 0)
    y = x2 * long_conv(z, 1)
    return y.astype(p.dtype)


def gla_chunk(q, k, v, log_f, s0):
    B, H, L, DK = q.shape
    DV = v.shape[-1]
    N = L // HG_CHUNK
    q, k, log_f = (a.reshape(B, H, N, HG_CHUNK, DK) for a in (q, k, log_f))
    v = v.reshape(B, H, N, HG_CHUNK, DV)
    b = jnp.cumsum(log_f, axis=3)
    b_last = b[:, :, :, -1:, :]
    b_mid = b[:, :, :, HG_CHUNK // 2 - 1:HG_CHUNK // 2, :]
    qm = q * jnp.exp(b - b_mid)
    km = k * jnp.exp(b_mid - b)
    mask = jnp.tril(jnp.ones((HG_CHUNK, HG_CHUNK), dtype=bool))
    att = jnp.where(mask, jnp.einsum('bhncd,bhnsd->bhncs', qm, km), 0.0)
    o_intra = jnp.einsum('bhncs,bhnsv->bhncv', att, v)
    u = jnp.einsum('bhnsd,bhnsv->bhndv', k * jnp.exp(b_last - b), v)
    decay = jnp.exp(b_last[:, :, :, 0, :])

    def step(s, inp):
        dec, un = inp
        return dec[..., None] * s + un, s

    s_final, s_prev = lax.scan(step, s0, (jnp.moveaxis(decay, 2, 0), jnp.moveaxis(u, 2, 0)))
    s_prev = jnp.moveaxis(s_prev, 0, 2)
    o_inter = jnp.einsum('bhncd,bhndv->bhncv', q * jnp.exp(b), s_prev)
    return (o_intra + o_inter).reshape(B, H, L, DV), s_final


def hgrn2_group(p, lb_f, lb_b, norm_g, s0_f, s0_b):
    B, L, _ = p.shape
    pf = p.astype(jnp.float32)
    q, f_fwd, f_bwd, i, g = jnp.split(pf, 5, axis=-1)
    heads = lambda a: a.reshape(B, L, HG_HEADS, HG_HEAD_DIM).transpose(0, 2, 1, 3)
    q = heads(jax.nn.silu(q) * HG_SCALE)
    v = heads(i)

    def forget(fr, lb):
        lb = lb.astype(jnp.float32)
        log_f = jnp.log(lb + (1.0 - lb) * jax.nn.sigmoid(fr))
        k = (1.0 - lb) * jax.nn.sigmoid(-fr)
        return heads(log_f), heads(k)

    lf_f, k_f = forget(f_fwd, lb_f)
    lf_b, k_b = forget(f_bwd, lb_b)
    o_f, s_f = gla_chunk(q, k_f, v, lf_f, s0_f)
    rev = lambda a: jnp.flip(a, axis=2)
    o_b, s_b = gla_chunk(rev(q), rev(k_b), rev(v), rev(lf_b), s0_b)
    o = o_f + rev(o_b)
    o = o * lax.rsqrt(jnp.mean(o * o, axis=-1, keepdims=True) + NORM_EPS) * norm_g.astype(jnp.float32)
    o = o.transpose(0, 2, 1, 3).reshape(B, L, HG_W) * jax.nn.silu(g)
    return o.astype(p.dtype), s_f, s_b


def moe_ffn(h, w_router, router_bias, ew_gate, ew_up, ew_down, sw_gate, sw_up, sw_down):
    T, D = h.shape
    f32 = jnp.float32
    scores = jax.nn.sigmoid(h.astype(f32) @ w_router.astype(f32))
    biased = scores + router_bias.astype(f32)
    grp = biased.reshape(T, N_GROUPS, N_EXPERTS // N_GROUPS)
    grp_score = jnp.sum(lax.top_k(grp, 2)[0], axis=-1)
    _, top_grp = lax.top_k(grp_score, TOPK_GROUPS)
    grp_mask = jnp.sum(jax.nn.one_hot(top_grp, N_GROUPS, dtype=f32), axis=1) > 0
    expert_mask = jnp.repeat(grp_mask, N_EXPERTS // N_GROUPS, axis=1)
    _, top_e = lax.top_k(jnp.where(expert_mask, biased, -jnp.inf), TOP_K)
    w = jnp.take_along_axis(scores, top_e, axis=1)
    w = ROUTED_SCALE * w / jnp.sum(w, axis=-1, keepdims=True)
    n_assign = T * TOP_K
    flat_e = top_e.reshape(n_assign)
    flat_tok = jnp.repeat(jnp.arange(T, dtype=jnp.int32), TOP_K)
    order = jnp.argsort(flat_e)
    e_sorted = flat_e[order]
    counts = jnp.bincount(flat_e, length=N_EXPERTS)
    starts = jnp.cumsum(counts) - counts
    padded = (counts + MOE_BLOCK - 1) // MOE_BLOCK * MOE_BLOCK
    p_ends = jnp.cumsum(padded)
    p_starts = p_ends - padded
    dest = p_starts[e_sorted] + (jnp.arange(n_assign, dtype=jnp.int32) - starts[e_sorted])
    n_blocks = -(-n_assign // MOE_BLOCK) + N_EXPERTS
    slot_tok = jnp.full((n_blocks * MOE_BLOCK,), T, jnp.int32).at[dest].set(flat_tok[order])
    slot_w = jnp.zeros((n_blocks * MOE_BLOCK,), h.dtype).at[dest].set(w.reshape(n_assign)[order].astype(h.dtype))
    block_e = jnp.minimum(jnp.searchsorted(p_ends, jnp.arange(n_blocks, dtype=jnp.int32) * MOE_BLOCK, side='right'), N_EXPERTS - 1)
    h_pad = jnp.concatenate([h, jnp.zeros((1, D), h.dtype)], axis=0)

    def expert_block(acc, blk):
        tok, wt, e = blk
        xb = h_pad[tok]
        yb = (jax.nn.silu(xb @ ew_gate[e]) * (xb @ ew_up[e])) @ ew_down[e]
        return acc.at[tok].add(yb * wt[:, None]), None

    acc, _ = lax.scan(expert_block, jnp.zeros_like(h_pad),
                      (slot_tok.reshape(n_blocks, MOE_BLOCK), slot_w.reshape(n_blocks, MOE_BLOCK), block_e))
    shared = (jax.nn.silu(h @ sw_gate) * (h @ sw_up)) @ sw_down
    return acc[:T] + shared


def setup_inputs(seed: int = 0) -> dict:
    key = jax.random.key(seed)
    ks = jax.random.split(key, 30)
    nrm = lambda i, shape, scale: scale * jax.random.normal(ks[i], shape, jnp.float32)
    D, L_ = D_MODEL, DEPTH
    return {
        'x': nrm(0, (BATCH, SEQ, D), 1.0),
        'c': nrm(1, (BATCH, D), 1.0),
        'ctx': nrm(2, (BATCH, CTX_LEN, D), 1.0),
        'c_ctx': nrm(3, (D,), 1.0),
        'w_mod': nrm(4, (L_, D, 6 * D), 0.5 * D ** -0.5),
        'b_mod': nrm(5, (L_, 6 * D), 0.02),
        'norm1_g': 1.0 + nrm(6, (L_, D), 0.05),
        'norm2_g': 1.0 + nrm(7, (L_, D), 0.05),
        'w_in': nrm(8, (L_, D, IN_COLS), D ** -0.5),
        'w_out': nrm(9, (L_, MIX_W, D), MIX_W ** -0.5),
        'hy_conv_w': nrm(10, (L_, 3, 3 * HY_W), 0.5),
        'hy_conv_b': nrm(11, (L_, 3 * HY_W), 0.02),
        'hy_fw1': nrm(12, (L_, HY_EMB, HY_FILTER_HIDDEN), HY_EMB ** -0.5),
        'hy_fb1': nrm(13, (L_, HY_FILTER_HIDDEN), 0.1),
        'hy_fw2': nrm(14, (L_, HY_FILTER_HIDDEN, HY_FILTER_HIDDEN), HY_FILTER_HIDDEN ** -0.5),
        'hy_fb2': nrm(15, (L_, HY_FILTER_HIDDEN), 0.1),
        'hy_fw3': nrm(16, (L_, HY_FILTER_HIDDEN, HY_ORDER * 2 * HY_W), HY_FILTER_HIDDEN ** -0.5),
        'hy_freq': 1.0 + nrm(17, (L_, HY_FILTER_HIDDEN), 0.1),
        'hy_d': nrm(18, (L_, HY_ORDER, HY_W), 0.5),
        'hg_lb_logits': nrm(19, (2, L_ + 1, HG_W), 0.1),
        'hg_norm_g': 1.0 + nrm(20, (L_, HG_HEAD_DIM), 0.05),
        'w_router': nrm(21, (L_, D, N_EXPERTS), D ** -0.5),
        'router_bias': nrm(22, (L_, N_EXPERTS), 0.01),
        'ew_gate': nrm(23, (L_, N_EXPERTS, D, EXPERT_FF), D ** -0.5),
        'ew_up': nrm(24, (L_, N_EXPERTS, D, EXPERT_FF), D ** -0.5),
        'ew_down': nrm(25, (L_, N_EXPERTS, EXPERT_FF, D), EXPERT_FF ** -0.5),
        'sw_gate': nrm(26, (L_, D, SHARED_FF), D ** -0.5),
        'sw_up': nrm(27, (L_, D, SHARED_FF), D ** -0.5),
        'sw_down': nrm(28, (L_, SHARED_FF, D), SHARED_FF ** -0.5),
        'final_g': 1.0 + nrm(29, (D,), 0.05),
    }


def reference(x, c, ctx, c_ctx, w_mod, b_mod, norm1_g, norm2_g, w_in, w_out, hy_conv_w, hy_conv_b,
              hy_fw1, hy_fb1, hy_fw2, hy_fb2, hy_fw3, hy_freq, hy_d, hg_lb_logits, hg_norm_g,
              w_router, router_bias, ew_gate, ew_up, ew_down, sw_gate, sw_up, sw_down, final_g):
    B, L, D = x.shape
    rows = L // GRID_W
    lower_bounds = jnp.cumsum(jax.nn.softmax(hg_lb_logits.astype(jnp.float32), axis=1), axis=1)
    s_zero = jnp.zeros((B, HG_HEADS, HG_HEAD_DIM, HG_HEAD_DIM), jnp.float32)
    for l in range(DEPTH):
        mod_x = jax.nn.silu(c) @ w_mod[l] + b_mod[l]
        mod_c = jax.nn.silu(c_ctx) @ w_mod[l] + b_mod[l]
        sh1, sc1, g1, sh2, sc2, g2 = jnp.split(mod_x[:, None, :], 6, axis=-1)
        csh1, csc1, cg1, csh2, csc2, cg2 = jnp.split(mod_c, 6, axis=-1)
        hx = rmsnorm(x, norm1_g[l]) * (1 + sc1) + sh1
        hc = rmsnorm(ctx, norm1_g[l]) * (1 + csc1) + csh1
        px = hx @ w_in[l]
        pc = hc @ w_in[l]
        hy_params = (hy_conv_w[l], hy_conv_b[l], hy_fw1[l], hy_fb1[l], hy_fw2[l], hy_fb2[l],
                     hy_fw3[l], hy_freq[l], hy_d[l])
        lb_f, lb_b = lower_bounds[0, l], lower_bounds[1, l]
        yc_hg, s_f, s_b = hgrn2_group(pc[..., 3 * HY_W:], lb_f, lb_b, hg_norm_g[l], s_zero, s_zero)
        yx_hg, _, _ = hgrn2_group(px[..., 3 * HY_W:], lb_f, lb_b, hg_norm_g[l], s_f, s_b)
        yx_hy = hyena_group(px[..., :3 * HY_W], rows, *hy_params)
        x = x + g1 * (jnp.concatenate([yx_hy, yx_hg], axis=-1) @ w_out[l])
        moe_params = (w_router[l], router_bias[l], ew_gate[l], ew_up[l], ew_down[l],
                      sw_gate[l], sw_up[l], sw_down[l])
        hx2 = rmsnorm(x, norm2_g[l]) * (1 + sc2) + sh2
        if l == DEPTH - 1:
            x = x + g2 * moe_ffn(hx2.reshape(B * L, D), *moe_params).reshape(B, L, D)
        else:
            yc_hy = hyena_group(pc[..., :3 * HY_W], None, *hy_params)
            ctx = ctx + cg1 * (jnp.concatenate([yc_hy, yc_hg], axis=-1) @ w_out[l])
            hc2 = rmsnorm(ctx, norm2_g[l]) * (1 + csc2) + csh2
            out = moe_ffn(jnp.concatenate([hx2.reshape(B * L, D), hc2.reshape(-1, D)], axis=0), *moe_params)
            x = x + g2 * out[:B * L].reshape(B, L, D)
            ctx = ctx + cg2 * out[B * L:].reshape(ctx.shape)
    return rmsnorm(x, final_g)
```

```python
import functools
import math

import jax
import jax.numpy as jnp
from jax import lax
from jax.experimental import pallas as pl
from jax.experimental.pallas import tpu as pltpu

F32 = jnp.float32
BF16 = jnp.bfloat16
I32 = jnp.int32
HIGHEST = lax.Precision.HIGHEST

GRID_W = 64
HY_W = 512
HG_W = 512
HY_EMB = 33
HY_BANDS = 16
HY_DECAY_TARGET = 1e-2
HY_FAST_DECAY_PCT = 0.3
HY_SLOW_DECAY_PCT = 1.5
HG_HEAD_DIM = 128
HG_HEADS = 4
HG_SCALE = HG_HEAD_DIM ** -0.5
HG_CHUNK = 64
N_EXPERTS = 256
TOP_K = 8
N_GROUPS = 8
TOPK_GROUPS = 4
GROUP_SIZE = N_EXPERTS // N_GROUPS
ROUTED_SCALE = 2.5
NORM_EPS = 1e-6

VMEM_LIMIT_BYTES = 56 * 1024 * 1024
LANES = 128
FFN_BLOCK = 256
HY_CT = 256


def _params(sem, vmem=VMEM_LIMIT_BYTES):
    return pltpu.CompilerParams(dimension_semantics=sem, vmem_limit_bytes=vmem)


def _silu(x):
    return x * jax.nn.sigmoid(x)


def _dot(a, b):
    return jnp.dot(a, b, preferred_element_type=F32)


def _dot_nt(a, b):
    return lax.dot_general(a, b, (((1,), (1,)), ((), ())), preferred_element_type=F32)


def _dot_tn(a, b):
    return lax.dot_general(a, b, (((0,), (0,)), ((), ())), preferred_element_type=F32)


def _mod_kernel(c_ref, w_ref, b_ref, o_ref):
    s = _silu(c_ref[...])
    o_ref[...] = jnp.dot(s, w_ref[...], preferred_element_type=F32, precision=HIGHEST) + b_ref[...]


def _modulation(cc, w_mod, b_mod):
    rows, d = cc.shape
    n = w_mod.shape[1]
    tn = 1024
    return pl.pallas_call(
        _mod_kernel,
        grid=(n // tn,),
        in_specs=[pl.BlockSpec((rows, d), lambda j: (0, 0)),
                  pl.BlockSpec((d, tn), lambda j: (0, j)),
                  pl.BlockSpec((1, tn), lambda j: (0, j))],
        out_specs=pl.BlockSpec((rows, tn), lambda j: (0, j)),
        out_shape=jax.ShapeDtypeStruct((rows, n), F32),
        compiler_params=_params(("parallel",)),
        name="mod",
    )(cc, w_mod, b_mod.reshape(1, n))


def _rms_mod(x, g, sc, sh):
    y = x * lax.rsqrt(jnp.mean(x * x, axis=-1, keepdims=True) + NORM_EPS) * g
    return y * (1.0 + sc) + sh


def _inproj_kernel(x_ref, g_ref, sc_ref, sh_ref, w_ref, o_ref, h_scr):
    @pl.when(pl.program_id(1) == 0)
    def _():
        h_scr[...] = _rms_mod(x_ref[...], g_ref[...], sc_ref[...], sh_ref[...]).astype(BF16)

    o_ref[...] = _dot(h_scr[...], w_ref[...]).astype(o_ref.dtype)


def _inproj(x2d, seq, g, sc, sh, w_bf, tm, tn):
    t, d = x2d.shape
    n = w_bf.shape[1]
    per = seq // tm
    return pl.pallas_call(
        _inproj_kernel,
        grid=(t // tm, n // tn),
        in_specs=[pl.BlockSpec((tm, d), lambda i, j: (i, 0)),
                  pl.BlockSpec((1, d), lambda i, j: (0, 0)),
                  pl.BlockSpec((None, 1, d), lambda i, j: (i // per, 0, 0)),
                  pl.BlockSpec((None, 1, d), lambda i, j: (i // per, 0, 0)),
                  pl.BlockSpec((d, tn), lambda i, j: (0, j))],
        out_specs=pl.BlockSpec((tm, tn), lambda i, j: (i, j)),
        out_shape=jax.ShapeDtypeStruct((t, n), BF16),
        scratch_shapes=[pltpu.VMEM((tm, d), BF16)],
        compiler_params=_params(("parallel", "arbitrary")),
        name="inproj",
    )(x2d, g.reshape(1, d), sc, sh, w_bf)


def _hg_scan(f_ref, v_ref, lb, st, nchunks, reverse, q_ref=None, emit=None):
    c = HG_CHUNK
    r = lax.broadcasted_iota(I32, (c, c), 0)
    s = lax.broadcasted_iota(I32, (c, c), 1)
    if reverse:
        mask, mid, last = r <= s, c // 2, 0
    else:
        mask, mid, last = r >= s, c // 2 - 1, c - 1
    tri = jnp.where(mask, 1.0, 0.0).astype(BF16)

    def body(n, st):
        ci = (nchunks - 1 - n) if reverse else n
        start = pl.multiple_of(ci * c, c)
        fr = f_ref[pl.ds(start, c), :].astype(F32)
        v = v_ref[pl.ds(start, c), :]
        sig = jax.nn.sigmoid(fr)
        lf = jnp.log(lb + (1.0 - lb) * sig)
        k = (1.0 - lb) * (1.0 - sig)
        hi = lf.astype(BF16)
        lo = (lf - hi.astype(F32)).astype(BF16)
        bc = _dot(tri, hi) + _dot(tri, lo)
        b_last = bc[last:last + 1, :]
        kd = (k * jnp.exp(b_last - bc)).astype(BF16)
        ut = _dot_tn(v, kd)
        if q_ref is not None:
            qr = q_ref[pl.ds(start, c), :].astype(F32)
            q = _silu(qr) * HG_SCALE
            b_mid = bc[mid:mid + 1, :]
            qm = (q * jnp.exp(bc - b_mid)).astype(BF16)
            km = (k * jnp.exp(b_mid - bc)).astype(BF16)
            att = jnp.where(mask, _dot_nt(qm, km), 0.0).astype(BF16)
            qe = (q * jnp.exp(bc)).astype(BF16)
            o = _dot(att, v) + _dot_nt(qe, st.astype(BF16))
            emit(start, o)
        return st * jnp.exp(b_last) + ut

    return lax.fori_loop(0, nchunks, body, st)


def _hgrn_kernel(q_ref, ff_ref, fb_ref, i_ref, g_ref, cff_ref, cfb_ref, ci_ref,
                 lbf_ref, lbb_ref, ng_ref, o_ref, oacc):
    seq = q_ref.shape[0]
    ctx = cff_ref.shape[0]
    c = HG_CHUNK
    lbf = lbf_ref[...]
    lbb = lbb_ref[...]
    zero = jnp.zeros((HG_HEAD_DIM, HG_HEAD_DIM), F32)
    s_f = _hg_scan(cff_ref, ci_ref, lbf, zero, ctx // c, False)
    s_b = _hg_scan(cfb_ref, ci_ref, lbb, zero, ctx // c, True)

    def emit_fwd(start, o):
        oacc[pl.ds(start, c), :] = o

    _hg_scan(ff_ref, i_ref, lbf, s_f, seq // c, False, q_ref, emit_fwd)

    ng = ng_ref[...]

    def emit_bwd(start, ob):
        o = oacc[pl.ds(start, c), :] + ob
        on = o * lax.rsqrt(jnp.mean(o * o, axis=-1, keepdims=True) + NORM_EPS) * ng
        gate = g_ref[pl.ds(start, c), :].astype(F32)
        o_ref[pl.ds(start, c), :] = (on * _silu(gate)).astype(o_ref.dtype)

    _hg_scan(fb_ref, i_ref, lbb, s_b, seq // c, True, q_ref, emit_bwd)


def _hgrn(px3, pc3, lb_f, lb_b, norm_g):
    b, seq, _ = px3.shape
    ctx = pc3.shape[1]
    dh = HG_HEAD_DIM
    base = 3 * HY_W // dh
    per = HG_W // dh

    def xspec(j):
        return pl.BlockSpec((None, seq, dh), lambda bi, h: (bi, 0, base + j * per + h))

    def cspec(j):
        return pl.BlockSpec((None, ctx, dh), lambda bi, h: (bi, 0, j * per + h))

    vec = pl.BlockSpec((1, dh), lambda bi, h: (0, h))
    return pl.pallas_call(
        _hgrn_kernel,
        grid=(b, HG_HEADS),
        in_specs=[xspec(0), xspec(1), xspec(2), xspec(3), xspec(4), cspec(0), cspec(1), cspec(2),
                  vec, vec, pl.BlockSpec((1, dh), lambda bi, h: (0, 0))],
        out_specs=pl.BlockSpec((None, seq, dh), lambda bi, h: (bi, 0, h)),
        out_shape=jax.ShapeDtypeStruct((b, seq, HG_W), BF16),
        scratch_shapes=[pltpu.VMEM((seq, dh), F32)],
        compiler_params=_params(("parallel", "parallel")),
        name="hgrn",
    )(px3, px3, px3, px3, px3, pc3, pc3, pc3, lb_f.reshape(1, HG_W), lb_b.reshape(1, HG_W),
      norm_g.reshape(1, dh))


def _dft_mats(seq):
    f = jnp.arange(seq, dtype=I32)
    m = (f[:, None] * f[None, :]) % (2 * seq)
    ang = m.astype(F32) * (math.pi / seq)
    return jnp.cos(ang).astype(BF16), jnp.sin(ang).astype(BF16)


def _hyfilt_kernel(feat_ref, w1_ref, b1_ref, w2_ref, b2_ref, fr_ref, w3_ref, t_ref, dl_ref,
                   a_ref, s_ref, u1_ref, u2_ref, un_ref):
    seq = feat_ref.shape[0]
    fr = fr_ref[...]
    h = jnp.sin(fr * (jnp.dot(feat_ref[...], w1_ref[...], preferred_element_type=F32, precision=HIGHEST)
                      + b1_ref[...]))
    h = jnp.sin(fr * (jnp.dot(h, w2_ref[...], preferred_element_type=F32, precision=HIGHEST) + b2_ref[...]))
    window = jnp.exp(-t_ref[...] * dl_ref[...])
    row = lax.broadcasted_iota(I32, (seq, 1), 0)
    sgn = jnp.where(row % 2 == 0, 1.0, -1.0)
    cf = jnp.where(row == 0, 1.0, 2.0) * (1.0 / (2 * seq))
    for o in range(2):
        w3 = w3_ref[:, o * 2 * HY_W:(o + 1) * 2 * HY_W]
        ho = jnp.dot(h, w3, preferred_element_type=F32, precision=HIGHEST)
        fwd = ho[:, :HY_W] * window
        bwd = ho[:, HY_W:] * window
        norm = (jnp.sum(jnp.abs(fwd), axis=0, keepdims=True)
                + jnp.sum(jnp.abs(bwd), axis=0, keepdims=True))
        inv = 1.0 / norm
        ksum = (fwd + bwd) * inv
        kdif = (bwd - fwd) * inv
        kr = _dot(a_ref[...], ksum.astype(BF16))
        ki = _dot(s_ref[...], kdif.astype(BF16))
        u1_ref[o] = (kr * cf).astype(u1_ref.dtype)
        u2_ref[o] = (ki * cf).astype(u2_ref.dtype)
        un_ref[o] = jnp.sum(sgn * ksum, axis=0, keepdims=True) * (1.0 / (2 * seq))


def _hyena_filters(seq, fw1, fb1, fw2, fb2, fw3, freq, a_mat, s_mat):
    pos = jnp.arange(seq, dtype=F32)[:, None]
    t = pos / max(seq - 1, 1)
    w = (2.0 * math.pi / seq) * pos
    bands = jnp.linspace(1e-4, HY_BANDS - 1, HY_BANDS, dtype=F32)[None, :]
    feats = jnp.concatenate([t, jnp.cos(bands * w), -jnp.sin(bands * w)], axis=-1)
    feats = jnp.pad(feats, ((0, 0), (0, LANES - HY_EMB)))
    w1 = jnp.pad(fw1, ((0, LANES - HY_EMB), (0, 0)))
    max_decay = math.log(HY_DECAY_TARGET) / HY_FAST_DECAY_PCT
    min_decay = math.log(HY_DECAY_TARGET) / HY_SLOW_DECAY_PCT
    deltas = jnp.abs(jnp.linspace(min_decay, max_decay, HY_W, dtype=F32))[None, :]
    hid = fw2.shape[0]
    return pl.pallas_call(
        _hyfilt_kernel,
        out_shape=(jax.ShapeDtypeStruct((2, seq, HY_W), BF16),
                   jax.ShapeDtypeStruct((2, seq, HY_W), BF16),
                   jax.ShapeDtypeStruct((2, 1, HY_W), F32)),
        compiler_params=pltpu.CompilerParams(vmem_limit_bytes=VMEM_LIMIT_BYTES),
        name="hyfilt",
    )(feats, w1, fb1.reshape(1, hid), fw2, fb2.reshape(1, hid), freq.reshape(1, hid), fw3, t, deltas,
      a_mat, s_mat)


def _hyena_kernel(x1_ref, x2_ref, v_ref, w1_ref, w2_ref, wv_ref, b1_ref, b2_ref, bv_ref,
                  u1_ref, u2_ref, un_ref, d_ref, a_ref, s_ref, o_ref, z_scr, zb_scr, re_scr, im_scr):
    seq = x1_ref.shape[0]
    rb = min(seq, 512)
    nrb = seq // rb
    row = lax.broadcasted_iota(I32, (rb, 1), 0)
    col = row % GRID_W
    first = col == 0
    lastc = col == GRID_W - 1
    sgn = jnp.where(row % 2 == 0, 1.0, -1.0)

    def conv3(p_ref, w_ref, b_ref, rows):
        p = p_ref[rows, :].astype(F32)
        prev = jnp.where(first, 0.0, pltpu.roll(p, 1, axis=0))
        nxt = jnp.where(lastc, 0.0, pltpu.roll(p, rb - 1, axis=0))
        w = w_ref[...]
        return w[0:1, :] * prev + w[1:2, :] * p + w[2:3, :] * nxt + b_ref[...]

    def forward(o):
        nyq = None
        for i in range(nrb):
            rows = slice(i * rb, (i + 1) * rb)
            part = jnp.sum(sgn * z_scr[rows, :], axis=0, keepdims=True)
            nyq = part if nyq is None else nyq + part
        for i in range(nrb):
            rows = slice(i * rb, (i + 1) * rb)
            p = _dot(a_ref[rows, :], zb_scr[...])
            q = _dot(s_ref[rows, :], zb_scr[...])
            u1 = u1_ref[o, rows, :].astype(F32)
            u2 = u2_ref[o, rows, :].astype(F32)
            re_scr[rows, :] = (p * u1 + q * u2).astype(BF16)
            im_scr[rows, :] = (q * u1 - p * u2).astype(BF16)
        return nyq * un_ref[o]

    def inverse(o, nyq, rows):
        y = _dot(a_ref[rows, :], re_scr[...]) + _dot(s_ref[rows, :], im_scr[...]) + sgn * nyq
        return y + z_scr[rows, :] * d_ref[o:o + 1, :]

    for i in range(nrb):
        rows = slice(i * rb, (i + 1) * rb)
        v = conv3(v_ref, wv_ref, bv_ref, rows)
        z_scr[rows, :] = v
        zb_scr[rows, :] = v.astype(BF16)
    nyq = forward(0)
    for i in range(nrb):
        rows = slice(i * rb, (i + 1) * rb)
        z = conv3(x1_ref, w1_ref, b1_ref, rows) * inverse(0, nyq, rows)
        z_scr[rows, :] = z
        zb_scr[rows, :] = z.astype(BF16)
    nyq = forward(1)
    for i in range(nrb):
        rows = slice(i * rb, (i + 1) * rb)
        y = conv3(x2_ref, w2_ref, b2_ref, rows) * inverse(1, nyq, rows)
        o_ref[rows, :] = y.astype(o_ref.dtype)


def _hyena(px3, conv_w, conv_b, u1, u2, un, d_skip, a_mat, s_mat):
    b, seq, _ = px3.shape
    ct = HY_CT
    nc = HY_W // ct

    def xspec(j):
        return pl.BlockSpec((None, seq, ct), lambda c, bi: (bi, 0, j * nc + c))

    def wspec(j, rows):
        return pl.BlockSpec((rows, ct), lambda c, bi: (0, j * nc + c))

    uspec = pl.BlockSpec((2, seq, ct), lambda c, bi: (0, 0, c))
    const = pl.BlockSpec((seq, seq), lambda c, bi: (0, 0), pipeline_mode=pl.Buffered(1))
    cb = conv_b.reshape(1, 3 * HY_W)
    return pl.pallas_call(
        _hyena_kernel,
        grid=(nc, b),
        in_specs=[xspec(0), xspec(1), xspec(2), wspec(0, 3), wspec(1, 3), wspec(2, 3),
                  wspec(0, 1), wspec(1, 1), wspec(2, 1), uspec, uspec,
                  pl.BlockSpec((2, 1, ct), lambda c, bi: (0, 0, c)),
                  pl.BlockSpec((2, ct), lambda c, bi: (0, c)), const, const],
        out_specs=pl.BlockSpec((None, seq, ct), lambda c, bi: (bi, 0, c)),
        out_shape=jax.ShapeDtypeStruct((b, seq, HY_W), BF16),
        scratch_shapes=[pltpu.VMEM((seq, ct), F32), pltpu.VMEM((seq, ct), BF16),
                        pltpu.VMEM((seq, ct), BF16), pltpu.VMEM((seq, ct), BF16)],
        compiler_params=_params(("parallel", "parallel")),
        name="hyena",
    )(px3, px3, px3, conv_w, conv_w, conv_w, cb, cb, cb, u1, u2, un, d_skip, a_mat, s_mat)


def _select_rows(rows):
    n = rows[0].shape[1]
    idx = lax.broadcasted_iota(I32, (len(rows), n), 0)
    out = jnp.zeros((len(rows), n), rows[0].dtype)
    for k, r in enumerate(rows):
        out = jnp.where(idx == k, r, out)
    return out


def _first_argmax(x, iota, size):
    m = jnp.max(x, axis=0, keepdims=True)
    return jnp.min(jnp.where(x == m, iota, size), axis=0, keepdims=True)


def _mid_kernel(yhy_ref, yhg_ref, x_ref, woa_ref, wob_ref, g1_ref, sc_ref, sh_ref, g2_ref, ng_ref,
                wr_ref, rb_ref, swg_ref, swu_ref, swd_ref,
                xs_ref, h_ref, e_ref, w_ref, rk_ref, cnt_ref, base):
    tm = x_ref.shape[0]
    ne = N_EXPERTS

    @pl.when(pl.program_id(0) == 0)
    def _():
        base[...] = jnp.zeros_like(base)

    mix = _dot(yhy_ref[...], woa_ref[...]) + _dot(yhg_ref[...], wob_ref[...])
    xm = x_ref[...] + g1_ref[...] * mix
    h = _rms_mod(xm, ng_ref[...], sc_ref[...], sh_ref[...])
    h_ref[...] = h
    hb = h.astype(BF16)
    act = (_silu(_dot(hb, swg_ref[...])) * _dot(hb, swu_ref[...])).astype(BF16)
    xs_ref[...] = xm + g2_ref[...] * _dot(act, swd_ref[...])

    logits = lax.dot_general(wr_ref[...], h, (((1,), (1,)), ((), ())), preferred_element_type=F32,
                             precision=HIGHEST)
    scores = jax.nn.sigmoid(logits)
    biased = scores + rb_ref[...]
    neg = -jnp.inf
    iota_g = lax.broadcasted_iota(I32, (GROUP_SIZE, tm), 0)
    grp = []
    for g in range(N_GROUPS):
        blk = biased[g * GROUP_SIZE:(g + 1) * GROUP_SIZE, :]
        m1 = jnp.max(blk, axis=0, keepdims=True)
        i1 = jnp.min(jnp.where(blk == m1, iota_g, GROUP_SIZE), axis=0, keepdims=True)
        m2 = jnp.max(jnp.where(iota_g == i1, neg, blk), axis=0, keepdims=True)
        grp.append(m1 + m2)
    gsc = _select_rows(grp)
    iota8 = lax.broadcasted_iota(I32, (N_GROUPS, tm), 0)
    gsel = iota8 < 0
    for _ in range(TOPK_GROUPS):
        hit = iota8 == _first_argmax(gsc, iota8, N_GROUPS)
        gsel = gsel | hit
        gsc = jnp.where(hit, neg, gsc)
    gself = jnp.where(gsel, 1.0, 0.0)
    iota_e = lax.broadcasted_iota(I32, (ne, tm), 0)
    gid = iota_e // GROUP_SIZE
    emask = jnp.zeros((ne, tm), F32)
    for g in range(N_GROUPS):
        emask = jnp.where(gid == g, gself[g:g + 1, :], emask)
    cur = jnp.where(emask > 0.0, biased, neg)

    sel = iota_e < 0
    idxs, wts = [], []
    for _ in range(TOP_K):
        idx = _first_argmax(cur, iota_e, ne)
        hit = iota_e == idx
        wts.append(jnp.sum(jnp.where(hit, scores, 0.0), axis=0, keepdims=True))
        idxs.append(idx)
        sel = sel | hit
        cur = jnp.where(hit, neg, cur)
    wsum = wts[0]
    for wk in wts[1:]:
        wsum = wsum + wk
    scale = ROUTED_SCALE / wsum

    r = lax.broadcasted_iota(I32, (tm, tm), 0)
    s = lax.broadcasted_iota(I32, (tm, tm), 1)
    upper = jnp.where(r < s, 1.0, 0.0).astype(BF16)
    sel_f = jnp.where(sel, 1.0, 0.0)
    rank_all = base[...] + _dot(sel_f.astype(BF16), upper)
    ranks = [jnp.sum(jnp.where(iota_e == idx, rank_all, 0.0), axis=0, keepdims=True) for idx in idxs]
    base[...] = base[...] + jnp.sum(sel_f, axis=1, keepdims=True)

    e_ref[...] = _select_rows(idxs)
    w_ref[...] = _select_rows([wk * scale for wk in wts])
    rk_ref[...] = _select_rows(ranks).astype(I32)
    cnt_ref[...] = base[...].astype(I32)


def _mid(yhy, yhg, x2d, seq, woa, wob, g1, sc2, sh2, g2, norm_g, wr_t, rbias, swg, swu, swd, tm):
    t, d = x2d.shape
    per = seq // tm
    ff = swg.shape[1]

    def full(shape):
        return pl.BlockSpec(shape, lambda i: (0,) * len(shape))

    mspec = pl.BlockSpec((None, 1, d), lambda i: (i // per, 0, 0))
    kt = pl.BlockSpec((TOP_K, tm), lambda i: (0, i))
    return pl.pallas_call(
        _mid_kernel,
        grid=(t // tm,),
        in_specs=[pl.BlockSpec((tm, HY_W), lambda i: (i, 0)), pl.BlockSpec((tm, HG_W), lambda i: (i, 0)),
                  pl.BlockSpec((tm, d), lambda i: (i, 0)), full((HY_W, d)), full((HG_W, d)),
                  mspec, mspec, mspec, mspec, full((1, d)), full((N_EXPERTS, d)), full((N_EXPERTS, 1)),
                  full((d, ff)), full((d, ff)), full((ff, d))],
        out_specs=[pl.BlockSpec((tm, d), lambda i: (i, 0)), pl.BlockSpec((tm, d), lambda i: (i, 0)),
                   kt, kt, kt, full((N_EXPERTS, 1))],
        out_shape=(jax.ShapeDtypeStruct((t, d), F32), jax.ShapeDtypeStruct((t, d), F32),
                   jax.ShapeDtypeStruct((TOP_K, t), I32), jax.ShapeDtypeStruct((TOP_K, t), F32),
                   jax.ShapeDtypeStruct((TOP_K, t), I32), jax.ShapeDtypeStruct((N_EXPERTS, 1), I32)),
        scratch_shapes=[pltpu.VMEM((N_EXPERTS, 1), F32)],
        compiler_params=_params(("arbitrary",)),
        name="mid",
    )(yhy, yhg, x2d, woa, wob, g1, sc2, sh2, g2, norm_g.reshape(1, d), wr_t, rbias.reshape(N_EXPERTS, 1),
      swg, swu, swd)


def _row_copy_wait(src_rows, dst_rows, sem, times):
    for _ in range(times):
        pltpu.make_async_copy(src_rows, dst_rows, sem).wait()


def _dispatch_kernel(e_ref, rk_ref, ps_ref, h_ref, xs_hbm, zbuf, sem):
    tm = h_ref.shape[0]
    bm = zbuf.shape[0]

    @pl.when(pl.program_id(0) == 0)
    def _():
        zbuf[...] = jnp.zeros_like(zbuf)

        def zcopy(e):
            start = pl.multiple_of(ps_ref[e + 1] - bm, bm)
            return pltpu.make_async_copy(zbuf, xs_hbm.at[pl.ds(start, bm), :], sem)

        def start(e, carry):
            @pl.when(ps_ref[e + 1] > ps_ref[e])
            def _():
                zcopy(e).start()
            return carry

        def wait(e, carry):
            @pl.when(ps_ref[e + 1] > ps_ref[e])
            def _():
                zcopy(e).wait()
            return carry

        lax.fori_loop(0, N_EXPERTS, start, 0)
        lax.fori_loop(0, N_EXPERTS, wait, 0)

        def tcopy(j):
            return pltpu.make_async_copy(zbuf, xs_hbm.at[pl.ds(pl.multiple_of(j * bm, bm), bm), :], sem)

        def tstart(j, carry):
            tcopy(j).start()
            return carry

        def twait(j, carry):
            tcopy(j).wait()
            return carry

        first_tail = ps_ref[N_EXPERTS] // bm
        lax.fori_loop(first_tail, xs_hbm.shape[0] // bm, tstart, 0)
        lax.fori_loop(first_tail, xs_hbm.shape[0] // bm, twait, 0)

    def issue(t, carry):
        for k in range(TOP_K):
            pos = ps_ref[e_ref[k * tm + t]] + rk_ref[k * tm + t]
            pltpu.make_async_copy(h_ref.at[pl.ds(t, 1), :], xs_hbm.at[pl.ds(pos, 1), :], sem).start()
        return carry

    lax.fori_loop(0, tm, issue, 0)
    _row_copy_wait(h_ref, xs_hbm.at[pl.ds(0, tm), :], sem, TOP_K)


def _dispatch(h2d, e_tiles, rk_tiles, ps, n_slots, tm):
    t, d = h2d.shape
    smem_tile = pl.BlockSpec((TOP_K * tm,), lambda i: (i,), memory_space=pltpu.SMEM)
    return pl.pallas_call(
        _dispatch_kernel,
        grid=(t // tm,),
        in_specs=[smem_tile, smem_tile,
                  pl.BlockSpec((N_EXPERTS + 1,), lambda i: (0,), memory_space=pltpu.SMEM),
                  pl.BlockSpec((tm, d), lambda i: (i, 0))],
        out_specs=pl.BlockSpec(memory_space=pl.ANY),
        out_shape=jax.ShapeDtypeStruct((n_slots, d), F32),
        scratch_shapes=[pltpu.VMEM((FFN_BLOCK, d), F32), pltpu.SemaphoreType.DMA(())],
        compiler_params=_params(("arbitrary",)),
        name="dispatch",
    )(e_tiles, rk_tiles, ps, h2d)


def _ffn_kernel(be_ref, na_ref, x_ref, wg_ref, wu_ref, wd_ref, o_ref):
    i = pl.program_id(0)

    @pl.when(i < na_ref[0])
    def _():
        xb = x_ref[...].astype(BF16)
        gate = _dot(xb, wg_ref[...].astype(BF16))
        up = _dot(xb, wu_ref[...].astype(BF16))
        act = (_silu(gate) * up).astype(BF16)
        o_ref[...] = _dot(act, wd_ref[...].astype(BF16))

    @pl.when(i >= na_ref[0])
    def _():
        o_ref[...] = jnp.zeros_like(o_ref)


def _ffn(xs, block_e, n_active, ew_gate, ew_up, ew_down):
    n_slots, d = xs.shape
    bm = FFN_BLOCK
    ff = ew_gate.shape[2]
    grid_spec = pltpu.PrefetchScalarGridSpec(
        num_scalar_prefetch=2,
        grid=(n_slots // bm,),
        in_specs=[pl.BlockSpec((bm, d), lambda i, be, na: (jnp.minimum(i, na[0] - 1), 0)),
                  pl.BlockSpec((None, d, ff), lambda i, be, na: (be[i], 0, 0)),
                  pl.BlockSpec((None, d, ff), lambda i, be, na: (be[i], 0, 0)),
                  pl.BlockSpec((None, ff, d), lambda i, be, na: (be[i], 0, 0))],
        out_specs=pl.BlockSpec((bm, d), lambda i, be, na: (i, 0)),
    )
    return pl.pallas_call(
        _ffn_kernel,
        grid_spec=grid_spec,
        out_shape=jax.ShapeDtypeStruct((n_slots, d), F32),
        compiler_params=_params(("arbitrary",)),
        name="ffn",
    )(block_e, n_active, xs, ew_gate, ew_up, ew_down)


def _combine_kernel(e_ref, rk_ref, ps_ref, w_ref, xs_ref, g2_ref, fg_ref, y_hbm, o_ref, ybuf, sem):
    tm = xs_ref.shape[0]

    def issue(t, carry):
        for k in range(TOP_K):
            pos = ps_ref[e_ref[k * tm + t]] + rk_ref[k * tm + t]
            pltpu.make_async_copy(y_hbm.at[pl.ds(pos, 1), :], ybuf.at[k, pl.ds(t, 1), :], sem).start()
        return carry

    lax.fori_loop(0, tm, issue, 0)
    _row_copy_wait(y_hbm.at[pl.ds(0, tm), :], ybuf.at[0], sem, TOP_K)

    w = w_ref[...]
    routed = w[:, 0:1] * ybuf[0]
    for k in range(1, TOP_K):
        routed = routed + w[:, k:k + 1] * ybuf[k]
    x = xs_ref[...] + g2_ref[...] * routed
    o_ref[...] = x * lax.rsqrt(jnp.mean(x * x, axis=-1, keepdims=True) + NORM_EPS) * fg_ref[...]


def _combine(y_sorted, e_tiles, rk_tiles, ps, w_tk, xs2d, seq, g2, final_g, tm):
    t, d = xs2d.shape
    per = seq // tm
    smem_tile = pl.BlockSpec((TOP_K * tm,), lambda i: (i,), memory_space=pltpu.SMEM)
    return pl.pallas_call(
        _combine_kernel,
        grid=(t // tm,),
        in_specs=[smem_tile, smem_tile,
                  pl.BlockSpec((N_EXPERTS + 1,), lambda i: (0,), memory_space=pltpu.SMEM),
                  pl.BlockSpec((tm, TOP_K), lambda i: (i, 0)),
                  pl.BlockSpec((tm, d), lambda i: (i, 0)),
                  pl.BlockSpec((None, 1, d), lambda i: (i // per, 0, 0)),
                  pl.BlockSpec((1, d), lambda i: (0, 0)),
                  pl.BlockSpec(memory_space=pl.ANY)],
        out_specs=pl.BlockSpec((tm, d), lambda i: (i, 0)),
        out_shape=jax.ShapeDtypeStruct((t, d), F32),
        scratch_shapes=[pltpu.VMEM((TOP_K, tm, d), F32), pltpu.SemaphoreType.DMA(())],
        compiler_params=_params(("arbitrary",)),
        name="combine",
    )(e_tiles, rk_tiles, ps, w_tk, xs2d, g2, final_g.reshape(1, d), y_sorted)


def _tile_major(a_kt, tm):
    k, t = a_kt.shape
    return a_kt.reshape(k, t // tm, tm).transpose(1, 0, 2).reshape(-1)


def kernel(x, c, ctx, c_ctx, w_mod, b_mod, norm1_g, norm2_g, w_in, w_out, hy_conv_w, hy_conv_b,
           hy_fw1, hy_fb1, hy_fw2, hy_fb2, hy_fw3, hy_freq, hy_d, hg_lb_logits, hg_norm_g,
           w_router, router_bias, ew_gate, ew_up, ew_down, sw_gate, sw_up, sw_down, final_g):
    b, seq, d = x.shape
    ctx_len = ctx.shape[1]
    t = b * seq
    layer = 0

    lower = jnp.cumsum(jax.nn.softmax(hg_lb_logits.astype(F32), axis=1), axis=1)
    lb_f, lb_b = lower[0, layer], lower[1, layer]

    rows = -(-(b + 1) // 8) * 8
    cc = jnp.concatenate([c, c_ctx[None, :], jnp.zeros((rows - b - 1, d), F32)], axis=0)
    mod = _modulation(cc, w_mod[layer], b_mod[layer])
    sh1, sc1, g1, sh2, sc2, g2 = (m.reshape(b, 1, d) for m in jnp.split(mod[:b], 6, axis=-1))
    csh1, csc1 = (jnp.broadcast_to(m.reshape(1, 1, d), (b, 1, d))
                  for m in jnp.split(mod[b:b + 1], 6, axis=-1)[:2])

    w_in_bf = w_in[layer].astype(BF16)
    x2d = x.reshape(t, d)
    px = _inproj(x2d, seq, norm1_g[layer], sc1, sh1, w_in_bf, min(seq, 1024), 1024)
    lo = 3 * HY_W + HG_W
    pc = _inproj(ctx.reshape(b * ctx_len, d), ctx_len, norm1_g[layer], csc1, csh1,
                 w_in_bf[:, lo:lo + 3 * HG_W], ctx_len, HG_W)
    px3 = px.reshape(b, seq, -1)
    pc3 = pc.reshape(b, ctx_len, -1)

    y_hg = _hgrn(px3, pc3, lb_f, lb_b, hg_norm_g[layer])

    a_mat, s_mat = _dft_mats(seq)
    u1, u2, un = _hyena_filters(seq, hy_fw1[layer], hy_fb1[layer], hy_fw2[layer], hy_fb2[layer],
                                hy_fw3[layer], hy_freq[layer], a_mat, s_mat)
    y_hy = _hyena(px3, hy_conv_w[layer], hy_conv_b[layer], u1, u2, un, hy_d[layer], a_mat, s_mat)

    w_out_bf = w_out[layer].astype(BF16)
    tm_mid = min(seq, 512)
    xs, h2, e_kt, w_kt, rk_kt, counts = _mid(
        y_hy.reshape(t, HY_W), y_hg.reshape(t, HG_W), x2d, seq, w_out_bf[:HY_W], w_out_bf[HY_W:],
        g1, sc2, sh2, g2, norm2_g[layer], w_router[layer].T, router_bias[layer],
        sw_gate[layer].astype(BF16), sw_up[layer].astype(BF16), sw_down[layer].astype(BF16), tm_mid)

    bm = FFN_BLOCK
    counts = counts.reshape(N_EXPERTS)
    padded = (counts + bm - 1) // bm * bm
    p_ends = jnp.cumsum(padded)
    ps = jnp.concatenate([p_ends - padded, p_ends[-1:]]).astype(I32)
    n_blocks = t * TOP_K // bm + N_EXPERTS
    block_e = jnp.minimum(jnp.searchsorted(p_ends, jnp.arange(n_blocks, dtype=I32) * bm, side='right'),
                          N_EXPERTS - 1).astype(I32)
    n_active = (p_ends[-1:] // bm).astype(I32)

    tm_moe = min(seq, 256)
    e_tiles = _tile_major(e_kt, tm_moe)
    rk_tiles = _tile_major(rk_kt, tm_moe)
    x_sorted = _dispatch(h2, e_tiles, rk_tiles, ps, n_blocks * bm, tm_moe)
    y_sorted = _ffn(x_sorted, block_e, n_active, ew_gate[layer], ew_up[layer], ew_down[layer])
    out = _combine(y_sorted, e_tiles, rk_tiles, ps, w_kt.T, xs, seq, g2, final_g, tm_moe)
    return out.reshape(b, seq, d)
```

```python
import functools
import math

import jax
import jax.numpy as jnp
from jax import lax
from jax.experimental import pallas as pl
from jax.experimental.pallas import tpu as pltpu

F32 = jnp.float32
BF16 = jnp.bfloat16
I32 = jnp.int32
HIGHEST = lax.Precision.HIGHEST

GRID_W = 64
HY_W = 512
HG_W = 512
HY_EMB = 33
HY_BANDS = 16
HY_DECAY_TARGET = 1e-2
HY_FAST_DECAY_PCT = 0.3
HY_SLOW_DECAY_PCT = 1.5
HG_HEAD_DIM = 128
HG_HEADS = 4
HG_SCALE = HG_HEAD_DIM ** -0.5
HG_CHUNK = 64
N_EXPERTS = 256
TOP_K = 8
N_GROUPS = 8
TOPK_GROUPS = 4
GROUP_SIZE = N_EXPERTS // N_GROUPS
ROUTED_SCALE = 2.5
NORM_EPS = 1e-6

VMEM_LIMIT_BYTES = 56 * 1024 * 1024
LANES = 128
FFN_BLOCK = 256
HY_CT = 256


def _params(sem, vmem=VMEM_LIMIT_BYTES):
    return pltpu.CompilerParams(dimension_semantics=sem, vmem_limit_bytes=vmem)


def _silu(x):
    return x * jax.nn.sigmoid(x)


def _dot(a, b):
    return jnp.dot(a, b, preferred_element_type=F32)


def _dot_nt(a, b):
    return lax.dot_general(a, b, (((1,), (1,)), ((), ())), preferred_element_type=F32)


def _dot_tn(a, b):
    return lax.dot_general(a, b, (((0,), (0,)), ((), ())), preferred_element_type=F32)


def _mod_kernel(c_ref, w_ref, b_ref, o_ref):
    s = _silu(c_ref[...])
    o_ref[...] = jnp.dot(s, w_ref[...], preferred_element_type=F32, precision=HIGHEST) + b_ref[...]


def _modulation(cc, w_mod, b_mod):
    rows, d = cc.shape
    n = w_mod.shape[1]
    tn = 1024
    return pl.pallas_call(
        _mod_kernel,
        grid=(n // tn,),
        in_specs=[pl.BlockSpec((rows, d), lambda j: (0, 0)),
                  pl.BlockSpec((d, tn), lambda j: (0, j)),
                  pl.BlockSpec((1, tn), lambda j: (0, j))],
        out_specs=pl.BlockSpec((rows, tn), lambda j: (0, j)),
        out_shape=jax.ShapeDtypeStruct((rows, n), F32),
        compiler_params=_params(("parallel",)),
        name="mod",
    )(cc, w_mod, b_mod.reshape(1, n))


def _rms_mod(x, g, sc, sh):
    y = x * lax.rsqrt(jnp.mean(x * x, axis=-1, keepdims=True) + NORM_EPS) * g
    return y * (1.0 + sc) + sh


def _inproj_kernel(x_ref, g_ref, sc_ref, sh_ref, w_ref, o_ref, h_scr):
    @pl.when(pl.program_id(1) == 0)
    def _():
        h_scr[...] = _rms_mod(x_ref[...], g_ref[...], sc_ref[...], sh_ref[...]).astype(BF16)

    o_ref[...] = _dot(h_scr[...], w_ref[...]).astype(o_ref.dtype)


def _inproj(x2d, seq, g, sc, sh, w_bf, tm, tn):
    t, d = x2d.shape
    n = w_bf.shape[1]
    per = seq // tm
    return pl.pallas_call(
        _inproj_kernel,
        grid=(t // tm, n // tn),
        in_specs=[pl.BlockSpec((tm, d), lambda i, j: (i, 0)),
                  pl.BlockSpec((1, d), lambda i, j: (0, 0)),
                  pl.BlockSpec((None, 1, d), lambda i, j: (i // per, 0, 0)),
                  pl.BlockSpec((None, 1, d), lambda i, j: (i // per, 0, 0)),
                  pl.BlockSpec((d, tn), lambda i, j: (0, j))],
        out_specs=pl.BlockSpec((tm, tn), lambda i, j: (i, j)),
        out_shape=jax.ShapeDtypeStruct((t, n), BF16),
        scratch_shapes=[pltpu.VMEM((tm, d), BF16)],
        compiler_params=_params(("parallel", "arbitrary")),
        name="inproj",
    )(x2d, g.reshape(1, d), sc, sh, w_bf)


def _hg_steps(chains):
    c = HG_CHUNK
    r = lax.broadcasted_iota(I32, (c, c), 0)
    s = lax.broadcasted_iota(I32, (c, c), 1)
    geo = {False: (r >= s, c // 2 - 1, c - 1), True: (r <= s, c // 2, 0)}
    tri = {rev: jnp.where(g[0], 1.0, 0.0).astype(BF16) for rev, g in geo.items()}

    work = []
    for ch in chains:
        lb = ch["lb"]
        sig = jax.nn.sigmoid(ch["fr"])
        lf = jnp.log(lb + (1.0 - lb) * sig)
        k = (1.0 - lb) * (1.0 - sig)
        hi = lf.astype(BF16)
        lo = (lf - hi.astype(F32)).astype(BF16)
        t = tri[ch["rev"]]
        work.append(dict(k=k, bc=_dot(t, hi) + _dot(t, lo)))
    for ch, w in zip(chains, work):
        mask, mid, last = geo[ch["rev"]]
        bc = w["bc"]
        b_mid = bc[mid:mid + 1, :]
        b_last = bc[last:last + 1, :]
        km = w["k"] * jnp.exp(b_mid - bc)
        kd = (km * jnp.exp(b_last - b_mid)).astype(BF16)
        w["ut"] = _dot_tn(ch["v"], kd)
        w["decay"] = jnp.exp(b_last)
        if ch["q"] is not None:
            qm = ch["q"] * jnp.exp(bc - b_mid)
            w["att"] = _dot_nt(qm.astype(BF16), km.astype(BF16))
            qe = (qm * jnp.exp(b_mid)).astype(BF16)
            w["inter"] = _dot_nt(qe, ch["st"].astype(BF16))
    out = []
    for ch, w in zip(chains, work):
        o = None
        if ch["q"] is not None:
            att = jnp.where(geo[ch["rev"]][0], w["att"], 0.0).astype(BF16)
            o = _dot(att, ch["v"]) + w["inter"]
        out.append((o, ch["st"] * w["decay"] + w["ut"]))
    return out


def _hgrn_kernel(q_ref, ff_ref, fb_ref, i_ref, g_ref, cff_ref, cfb_ref, ci_ref,
                 lbf_ref, lbb_ref, ng_ref, o_ref, qs_scr, of_scr, ob_scr, st_scr):
    seq = q_ref.shape[0]
    ctx = cff_ref.shape[0]
    c = HG_CHUNK
    dh = HG_HEAD_DIM
    rb = min(seq, 256)

    for i in range(seq // rb):
        rows = slice(i * rb, (i + 1) * rb)
        qs_scr[rows, :] = (_silu(q_ref[rows, :].astype(F32)) * HG_SCALE).astype(BF16)
    st_scr[...] = jnp.zeros_like(st_scr)

    def chains(nchunks, ffr, fbr, vr, with_q):
        def body(n, carry):
            chains = []
            for h in range(HG_HEADS):
                cols = slice(h * dh, (h + 1) * dh)
                for rev, fref, lbref in ((False, ffr, lbf_ref), (True, fbr, lbb_ref)):
                    ci = (nchunks - 1 - n) if rev else n
                    rows = pl.ds(pl.multiple_of(ci * c, c), c)
                    chains.append(dict(
                        rev=rev, rows=rows, cols=cols, fr=fref[rows, cols].astype(F32), v=vr[rows, cols],
                        lb=lbref[:, cols], st=st_scr[len(chains)],
                        q=qs_scr[rows, cols].astype(F32) if with_q else None))
            for slot, (ch, (o, st)) in enumerate(zip(chains, _hg_steps(chains))):
                st_scr[slot] = st
                if with_q:
                    (ob_scr if ch["rev"] else of_scr)[ch["rows"], ch["cols"]] = o
            return carry

        lax.fori_loop(0, nchunks, body, 0)

    chains(ctx // c, cff_ref, cfb_ref, ci_ref, False)
    chains(seq // c, ff_ref, fb_ref, i_ref, True)

    ng = ng_ref[...]
    for i in range(seq // rb):
        rows = slice(i * rb, (i + 1) * rb)
        gate = _silu(g_ref[rows, :].astype(F32))
        for h in range(HG_HEADS):
            cols = slice(h * dh, (h + 1) * dh)
            o = of_scr[rows, cols] + ob_scr[rows, cols]
            on = o * lax.rsqrt(jnp.mean(o * o, axis=-1, keepdims=True) + NORM_EPS) * ng
            o_ref[rows, cols] = (on * gate[:, cols]).astype(o_ref.dtype)


def _hgrn(px3, pc3, lb_f, lb_b, norm_g):
    b, seq, _ = px3.shape
    ctx = pc3.shape[1]
    dh = HG_HEAD_DIM
    base = 3 * HY_W // HG_W

    def xspec(j):
        return pl.BlockSpec((None, seq, HG_W), lambda bi: (bi, 0, base + j))

    def cspec(j):
        return pl.BlockSpec((None, ctx, HG_W), lambda bi: (bi, 0, j))

    vec = pl.BlockSpec((1, HG_W), lambda bi: (0, 0))
    return pl.pallas_call(
        _hgrn_kernel,
        grid=(b,),
        in_specs=[xspec(0), xspec(1), xspec(2), xspec(3), xspec(4), cspec(0), cspec(1), cspec(2),
                  vec, vec, pl.BlockSpec((1, dh), lambda bi: (0, 0))],
        out_specs=pl.BlockSpec((None, seq, HG_W), lambda bi: (bi, 0, 0)),
        out_shape=jax.ShapeDtypeStruct((b, seq, HG_W), BF16),
        scratch_shapes=[pltpu.VMEM((seq, HG_W), BF16), pltpu.VMEM((seq, HG_W), F32),
                        pltpu.VMEM((seq, HG_W), F32), pltpu.VMEM((2 * HG_HEADS, dh, dh), F32)],
        compiler_params=_params(("parallel",)),
        name="hgrn",
    )(px3, px3, px3, px3, px3, pc3, pc3, pc3, lb_f.reshape(1, HG_W), lb_b.reshape(1, HG_W),
      norm_g.reshape(1, dh))


def _dft_mats(seq):
    f = jnp.arange(seq, dtype=I32)
    m = (f[:, None] * f[None, :]) % (2 * seq)
    ang = m.astype(F32) * (math.pi / seq)
    return jnp.cos(ang).astype(BF16), jnp.sin(ang).astype(BF16)


def _hyfilt_kernel(feat_ref, w1_ref, b1_ref, w2_ref, b2_ref, fr_ref, w3_ref, t_ref, dl_ref,
                   a_ref, s_ref, u1_ref, u2_ref, un_ref):
    seq = feat_ref.shape[0]
    fr = fr_ref[...]
    h = jnp.sin(fr * (jnp.dot(feat_ref[...], w1_ref[...], preferred_element_type=F32, precision=HIGHEST)
                      + b1_ref[...]))
    h = jnp.sin(fr * (jnp.dot(h, w2_ref[...], preferred_element_type=F32, precision=HIGHEST) + b2_ref[...]))
    window = jnp.exp(-t_ref[...] * dl_ref[...])
    row = lax.broadcasted_iota(I32, (seq, 1), 0)
    sgn = jnp.where(row % 2 == 0, 1.0, -1.0)
    cf = jnp.where(row == 0, 1.0, 2.0) * (1.0 / (2 * seq))
    for o in range(2):
        w3 = w3_ref[:, o * 2 * HY_W:(o + 1) * 2 * HY_W]
        ho = jnp.dot(h, w3, preferred_element_type=F32, precision=HIGHEST)
        fwd = ho[:, :HY_W] * window
        bwd = ho[:, HY_W:] * window
        norm = (jnp.sum(jnp.abs(fwd), axis=0, keepdims=True)
                + jnp.sum(jnp.abs(bwd), axis=0, keepdims=True))
        inv = 1.0 / norm
        ksum = (fwd + bwd) * inv
        kdif = (bwd - fwd) * inv
        kr = _dot(a_ref[...], ksum.astype(BF16))
        ki = _dot(s_ref[...], kdif.astype(BF16))
        u1_ref[o] = (kr * cf).astype(u1_ref.dtype)
        u2_ref[o] = (ki * cf).astype(u2_ref.dtype)
        un_ref[o] = jnp.sum(sgn * ksum, axis=0, keepdims=True) * (1.0 / (2 * seq))


def _hyena_filters(seq, fw1, fb1, fw2, fb2, fw3, freq, a_mat, s_mat):
    pos = jnp.arange(seq, dtype=F32)[:, None]
    t = pos / max(seq - 1, 1)
    w = (2.0 * math.pi / seq) * pos
    bands = jnp.linspace(1e-4, HY_BANDS - 1, HY_BANDS, dtype=F32)[None, :]
    feats = jnp.concatenate([t, jnp.cos(bands * w), -jnp.sin(bands * w)], axis=-1)
    feats = jnp.pad(feats, ((0, 0), (0, LANES - HY_EMB)))
    w1 = jnp.pad(fw1, ((0, LANES - HY_EMB), (0, 0)))
    max_decay = math.log(HY_DECAY_TARGET) / HY_FAST_DECAY_PCT
    min_decay = math.log(HY_DECAY_TARGET) / HY_SLOW_DECAY_PCT
    deltas = jnp.abs(jnp.linspace(min_decay, max_decay, HY_W, dtype=F32))[None, :]
    hid = fw2.shape[0]
    return pl.pallas_call(
        _hyfilt_kernel,
        out_shape=(jax.ShapeDtypeStruct((2, seq, HY_W), BF16),
                   jax.ShapeDtypeStruct((2, seq, HY_W), BF16),
                   jax.ShapeDtypeStruct((2, 1, HY_W), F32)),
        compiler_params=pltpu.CompilerParams(vmem_limit_bytes=VMEM_LIMIT_BYTES),
        name="hyfilt",
    )(feats, w1, fb1.reshape(1, hid), fw2, fb2.reshape(1, hid), freq.reshape(1, hid), fw3, t, deltas,
      a_mat, s_mat)


def _hyena_kernel(x1_ref, x2_ref, v_ref, w1_ref, w2_ref, wv_ref, b1_ref, b2_ref, bv_ref,
                  u1_ref, u2_ref, un_ref, d_ref, a_ref, s_ref, o_ref, z_scr, zb_scr, re_scr, im_scr):
    seq = x1_ref.shape[0]
    rb = min(seq, 512)
    nrb = seq // rb
    row = lax.broadcasted_iota(I32, (rb, 1), 0)
    col = row % GRID_W
    first = col == 0
    lastc = col == GRID_W - 1
    sgn = jnp.where(row % 2 == 0, 1.0, -1.0)

    def conv3(p_ref, w_ref, b_ref, rows):
        p = p_ref[rows, :].astype(F32)
        prev = jnp.where(first, 0.0, pltpu.roll(p, 1, axis=0))
        nxt = jnp.where(lastc, 0.0, pltpu.roll(p, rb - 1, axis=0))
        w = w_ref[...]
        return w[0:1, :] * prev + w[1:2, :] * p + w[2:3, :] * nxt + b_ref[...]

    def forward(o):
        nyq = None
        for i in range(nrb):
            rows = slice(i * rb, (i + 1) * rb)
            part = jnp.sum(sgn * z_scr[rows, :], axis=0, keepdims=True)
            nyq = part if nyq is None else nyq + part
        for i in range(nrb):
            rows = slice(i * rb, (i + 1) * rb)
            p = _dot(a_ref[rows, :], zb_scr[...])
            q = _dot(s_ref[rows, :], zb_scr[...])
            u1 = u1_ref[o, rows, :].astype(F32)
            u2 = u2_ref[o, rows, :].astype(F32)
            re_scr[rows, :] = (p * u1 + q * u2).astype(BF16)
            im_scr[rows, :] = (q * u1 - p * u2).astype(BF16)
        return nyq * un_ref[o]

    def inverse(o, nyq, rows):
        y = _dot(a_ref[rows, :], re_scr[...]) + _dot(s_ref[rows, :], im_scr[...]) + sgn * nyq
        return y + z_scr[rows, :] * d_ref[o:o + 1, :]

    for i in range(nrb):
        rows = slice(i * rb, (i + 1) * rb)
        v = conv3(v_ref, wv_ref, bv_ref, rows)
        z_scr[rows, :] = v
        zb_scr[rows, :] = v.astype(BF16)
    nyq = forward(0)
    for i in range(nrb):
        rows = slice(i * rb, (i + 1) * rb)
        z = conv3(x1_ref, w1_ref, b1_ref, rows) * inverse(0, nyq, rows)
        z_scr[rows, :] = z
        zb_scr[rows, :] = z.astype(BF16)
    nyq = forward(1)
    for i in range(nrb):
        rows = slice(i * rb, (i + 1) * rb)
        y = conv3(x2_ref, w2_ref, b2_ref, rows) * inverse(1, nyq, rows)
        o_ref[rows, :] = y.astype(o_ref.dtype)


def _hyena(px3, conv_w, conv_b, u1, u2, un, d_skip, a_mat, s_mat):
    b, seq, _ = px3.shape
    ct = HY_CT
    nc = HY_W // ct

    def xspec(j):
        return pl.BlockSpec((None, seq, ct), lambda c, bi: (bi, 0, j * nc + c))

    def wspec(j, rows):
        return pl.BlockSpec((rows, ct), lambda c, bi: (0, j * nc + c))

    uspec = pl.BlockSpec((2, seq, ct), lambda c, bi: (0, 0, c))
    const = pl.BlockSpec((seq, seq), lambda c, bi: (0, 0), pipeline_mode=pl.Buffered(1))
    cb = conv_b.reshape(1, 3 * HY_W)
    return pl.pallas_call(
        _hyena_kernel,
        grid=(nc, b),
        in_specs=[xspec(0), xspec(1), xspec(2), wspec(0, 3), wspec(1, 3), wspec(2, 3),
                  wspec(0, 1), wspec(1, 1), wspec(2, 1), uspec, uspec,
                  pl.BlockSpec((2, 1, ct), lambda c, bi: (0, 0, c)),
                  pl.BlockSpec((2, ct), lambda c, bi: (0, c)), const, const],
        out_specs=pl.BlockSpec((None, seq, ct), lambda c, bi: (bi, 0, c)),
        out_shape=jax.ShapeDtypeStruct((b, seq, HY_W), BF16),
        scratch_shapes=[pltpu.VMEM((seq, ct), F32), pltpu.VMEM((seq, ct), BF16),
                        pltpu.VMEM((seq, ct), BF16), pltpu.VMEM((seq, ct), BF16)],
        compiler_params=_params(("parallel", "parallel")),
        name="hyena",
    )(px3, px3, px3, conv_w, conv_w, conv_w, cb, cb, cb, u1, u2, un, d_skip, a_mat, s_mat)


def _select_rows(rows):
    n = rows[0].shape[1]
    idx = lax.broadcasted_iota(I32, (len(rows), n), 0)
    out = jnp.zeros((len(rows), n), rows[0].dtype)
    for k, r in enumerate(rows):
        out = jnp.where(idx == k, r, out)
    return out


def _first_argmax(x, iota, size):
    m = jnp.max(x, axis=0, keepdims=True)
    return jnp.min(jnp.where(x == m, iota, size), axis=0, keepdims=True)


def _mid_kernel(yhy_ref, yhg_ref, x_ref, woa_ref, wob_ref, g1_ref, sc_ref, sh_ref, g2_ref, ng_ref,
                wr_ref, rb_ref, swg_ref, swu_ref, swd_ref,
                xs_ref, h_ref, e_ref, w_ref, rk_ref, cnt_ref, base):
    tm = x_ref.shape[0]
    ne = N_EXPERTS

    @pl.when(pl.program_id(0) == 0)
    def _():
        base[...] = jnp.zeros_like(base)

    mix = _dot(yhy_ref[...], woa_ref[...]) + _dot(yhg_ref[...], wob_ref[...])
    xm = x_ref[...] + g1_ref[...] * mix
    h = _rms_mod(xm, ng_ref[...], sc_ref[...], sh_ref[...])
    h_ref[...] = h
    hb = h.astype(BF16)
    act = (_silu(_dot(hb, swg_ref[...])) * _dot(hb, swu_ref[...])).astype(BF16)
    xs_ref[...] = xm + g2_ref[...] * _dot(act, swd_ref[...])

    logits = lax.dot_general(wr_ref[...], h, (((1,), (1,)), ((), ())), preferred_element_type=F32,
                             precision=HIGHEST)
    scores = jax.nn.sigmoid(logits)
    biased = scores + rb_ref[...]
    neg = -jnp.inf
    iota_g = lax.broadcasted_iota(I32, (GROUP_SIZE, tm), 0)
    grp = []
    for g in range(N_GROUPS):
        blk = biased[g * GROUP_SIZE:(g + 1) * GROUP_SIZE, :]
        m1 = jnp.max(blk, axis=0, keepdims=True)
        i1 = jnp.min(jnp.where(blk == m1, iota_g, GROUP_SIZE), axis=0, keepdims=True)
        m2 = jnp.max(jnp.where(iota_g == i1, neg, blk), axis=0, keepdims=True)
        grp.append(m1 + m2)
    gsc = _select_rows(grp)
    iota8 = lax.broadcasted_iota(I32, (N_GROUPS, tm), 0)
    gsel = iota8 < 0
    for _ in range(TOPK_GROUPS):
        hit = iota8 == _first_argmax(gsc, iota8, N_GROUPS)
        gsel = gsel | hit
        gsc = jnp.where(hit, neg, gsc)
    gself = jnp.where(gsel, 1.0, 0.0)
    iota_e = lax.broadcasted_iota(I32, (ne, tm), 0)
    gid = iota_e // GROUP_SIZE
    emask = jnp.zeros((ne, tm), F32)
    for g in range(N_GROUPS):
        emask = jnp.where(gid == g, gself[g:g + 1, :], emask)
    cur = jnp.where(emask > 0.0, biased, neg)

    sel = iota_e < 0
    idxs, wts = [], []
    for _ in range(TOP_K):
        idx = _first_argmax(cur, iota_e, ne)
        hit = iota_e == idx
        wts.append(jnp.sum(jnp.where(hit, scores, 0.0), axis=0, keepdims=True))
        idxs.append(idx)
        sel = sel | hit
        cur = jnp.where(hit, neg, cur)
    wsum = wts[0]
    for wk in wts[1:]:
        wsum = wsum + wk
    scale = ROUTED_SCALE / wsum

    r = lax.broadcasted_iota(I32, (tm, tm), 0)
    s = lax.broadcasted_iota(I32, (tm, tm), 1)
    upper = jnp.where(r < s, 1.0, 0.0).astype(BF16)
    sel_f = jnp.where(sel, 1.0, 0.0)
    rank_all = base[...] + _dot(sel_f.astype(BF16), upper)
    ranks = [jnp.sum(jnp.where(iota_e == idx, rank_all, 0.0), axis=0, keepdims=True) for idx in idxs]
    base[...] = base[...] + jnp.sum(sel_f, axis=1, keepdims=True)

    e_ref[...] = _select_rows(idxs)
    w_ref[...] = _select_rows([wk * scale for wk in wts])
    rk_ref[...] = _select_rows(ranks).astype(I32)
    cnt_ref[...] = base[...].astype(I32)


def _mid(yhy, yhg, x2d, seq, woa, wob, g1, sc2, sh2, g2, norm_g, wr_t, rbias, swg, swu, swd, tm):
    t, d = x2d.shape
    per = seq // tm
    ff = swg.shape[1]

    def full(shape):
        return pl.BlockSpec(shape, lambda i: (0,) * len(shape))

    mspec = pl.BlockSpec((None, 1, d), lambda i: (i // per, 0, 0))
    kt = pl.BlockSpec((TOP_K, tm), lambda i: (0, i))
    return pl.pallas_call(
        _mid_kernel,
        grid=(t // tm,),
        in_specs=[pl.BlockSpec((tm, HY_W), lambda i: (i, 0)), pl.BlockSpec((tm, HG_W), lambda i: (i, 0)),
                  pl.BlockSpec((tm, d), lambda i: (i, 0)), full((HY_W, d)), full((HG_W, d)),
                  mspec, mspec, mspec, mspec, full((1, d)), full((N_EXPERTS, d)), full((N_EXPERTS, 1)),
                  full((d, ff)), full((d, ff)), full((ff, d))],
        out_specs=[pl.BlockSpec((tm, d), lambda i: (i, 0)), pl.BlockSpec((tm, d), lambda i: (i, 0)),
                   kt, kt, kt, full((N_EXPERTS, 1))],
        out_shape=(jax.ShapeDtypeStruct((t, d), F32), jax.ShapeDtypeStruct((t, d), F32),
                   jax.ShapeDtypeStruct((TOP_K, t), I32), jax.ShapeDtypeStruct((TOP_K, t), F32),
                   jax.ShapeDtypeStruct((TOP_K, t), I32), jax.ShapeDtypeStruct((N_EXPERTS, 1), I32)),
        scratch_shapes=[pltpu.VMEM((N_EXPERTS, 1), F32)],
        compiler_params=_params(("arbitrary",)),
        name="mid",
    )(yhy, yhg, x2d, woa, wob, g1, sc2, sh2, g2, norm_g.reshape(1, d), wr_t, rbias.reshape(N_EXPERTS, 1),
      swg, swu, swd)


def _row_copy_wait(src_rows, dst_rows, sem, times):
    for _ in range(times):
        pltpu.make_async_copy(src_rows, dst_rows, sem).wait()


def _dispatch_kernel(e_ref, rk_ref, ps_ref, h_ref, xs_hbm, zbuf, sem):
    tm = h_ref.shape[0]
    bm = zbuf.shape[0]

    @pl.when(pl.program_id(0) == 0)
    def _():
        zbuf[...] = jnp.zeros_like(zbuf)

        def zcopy(e):
            start = pl.multiple_of(ps_ref[e + 1] - bm, bm)
            return pltpu.make_async_copy(zbuf, xs_hbm.at[pl.ds(start, bm), :], sem)

        def start(e, carry):
            @pl.when(ps_ref[e + 1] > ps_ref[e])
            def _():
                zcopy(e).start()
            return carry

        def wait(e, carry):
            @pl.when(ps_ref[e + 1] > ps_ref[e])
            def _():
                zcopy(e).wait()
            return carry

        lax.fori_loop(0, N_EXPERTS, start, 0)
        lax.fori_loop(0, N_EXPERTS, wait, 0)

        def tcopy(j):
            return pltpu.make_async_copy(zbuf, xs_hbm.at[pl.ds(pl.multiple_of(j * bm, bm), bm), :], sem)

        def tstart(j, carry):
            tcopy(j).start()
            return carry

        def twait(j, carry):
            tcopy(j).wait()
            return carry

        first_tail = ps_ref[N_EXPERTS] // bm
        lax.fori_loop(first_tail, xs_hbm.shape[0] // bm, tstart, 0)
        lax.fori_loop(first_tail, xs_hbm.shape[0] // bm, twait, 0)

    def issue(t, carry):
        for k in range(TOP_K):
            pos = ps_ref[e_ref[k * tm + t]] + rk_ref[k * tm + t]
            pltpu.make_async_copy(h_ref.at[pl.ds(t, 1), :], xs_hbm.at[pl.ds(pos, 1), :], sem).start()
        return carry

    lax.fori_loop(0, tm, issue, 0)
    _row_copy_wait(h_ref, xs_hbm.at[pl.ds(0, tm), :], sem, TOP_K)


def _dispatch(h2d, e_tiles, rk_tiles, ps, n_slots, tm):
    t, d = h2d.shape
    smem_tile = pl.BlockSpec((TOP_K * tm,), lambda i: (i,), memory_space=pltpu.SMEM)
    return pl.pallas_call(
        _dispatch_kernel,
        grid=(t // tm,),
        in_specs=[smem_tile, smem_tile,
                  pl.BlockSpec((N_EXPERTS + 1,), lambda i: (0,), memory_space=pltpu.SMEM),
                  pl.BlockSpec((tm, d), lambda i: (i, 0))],
        out_specs=pl.BlockSpec(memory_space=pl.ANY),
        out_shape=jax.ShapeDtypeStruct((n_slots, d), F32),
        scratch_shapes=[pltpu.VMEM((FFN_BLOCK, d), F32), pltpu.SemaphoreType.DMA(())],
        compiler_params=_params(("arbitrary",)),
        name="dispatch",
    )(e_tiles, rk_tiles, ps, h2d)


def _ffn_kernel(be_ref, na_ref, x_ref, wg_ref, wu_ref, wd_ref, o_ref):
    i = pl.program_id(0)

    @pl.when(i < na_ref[0])
    def _():
        xb = x_ref[...].astype(BF16)
        gate = _dot(xb, wg_ref[...].astype(BF16))
        up = _dot(xb, wu_ref[...].astype(BF16))
        act = (_silu(gate) * up).astype(BF16)
        o_ref[...] = _dot(act, wd_ref[...].astype(BF16))

    @pl.when(i >= na_ref[0])
    def _():
        o_ref[...] = jnp.zeros_like(o_ref)


def _ffn(xs, block_e, n_active, ew_gate, ew_up, ew_down):
    n_slots, d = xs.shape
    bm = FFN_BLOCK
    ff = ew_gate.shape[2]
    grid_spec = pltpu.PrefetchScalarGridSpec(
        num_scalar_prefetch=2,
        grid=(n_slots // bm,),
        in_specs=[pl.BlockSpec((bm, d), lambda i, be, na: (jnp.minimum(i, na[0] - 1), 0)),
                  pl.BlockSpec((None, d, ff), lambda i, be, na: (be[i], 0, 0)),
                  pl.BlockSpec((None, d, ff), lambda i, be, na: (be[i], 0, 0)),
                  pl.BlockSpec((None, ff, d), lambda i, be, na: (be[i], 0, 0))],
        out_specs=pl.BlockSpec((bm, d), lambda i, be, na: (i, 0)),
    )
    return pl.pallas_call(
        _ffn_kernel,
        grid_spec=grid_spec,
        out_shape=jax.ShapeDtypeStruct((n_slots, d), F32),
        compiler_params=_params(("arbitrary",)),
        name="ffn",
    )(block_e, n_active, xs, ew_gate, ew_up, ew_down)


def _combine_kernel(e_ref, rk_ref, ps_ref, w_ref, xs_ref, g2_ref, fg_ref, y_hbm, o_ref, ybuf, sem):
    tm = xs_ref.shape[0]

    def issue(t, carry):
        for k in range(TOP_K):
            pos = ps_ref[e_ref[k * tm + t]] + rk_ref[k * tm + t]
            pltpu.make_async_copy(y_hbm.at[pl.ds(pos, 1), :], ybuf.at[k, pl.ds(t, 1), :], sem).start()
        return carry

    lax.fori_loop(0, tm, issue, 0)
    _row_copy_wait(y_hbm.at[pl.ds(0, tm), :], ybuf.at[0], sem, TOP_K)

    w = w_ref[...]
    routed = w[:, 0:1] * ybuf[0]
    for k in range(1, TOP_K):
        routed = routed + w[:, k:k + 1] * ybuf[k]
    x = xs_ref[...] + g2_ref[...] * routed
    o_ref[...] = x * lax.rsqrt(jnp.mean(x * x, axis=-1, keepdims=True) + NORM_EPS) * fg_ref[...]


def _combine(y_sorted, e_tiles, rk_tiles, ps, w_tk, xs2d, seq, g2, final_g, tm):
    t, d = xs2d.shape
    per = seq // tm
    smem_tile = pl.BlockSpec((TOP_K * tm,), lambda i: (i,), memory_space=pltpu.SMEM)
    return pl.pallas_call(
        _combine_kernel,
        grid=(t // tm,),
        in_specs=[smem_tile, smem_tile,
                  pl.BlockSpec((N_EXPERTS + 1,), lambda i: (0,), memory_space=pltpu.SMEM),
                  pl.BlockSpec((tm, TOP_K), lambda i: (i, 0)),
                  pl.BlockSpec((tm, d), lambda i: (i, 0)),
                  pl.BlockSpec((None, 1, d), lambda i: (i // per, 0, 0)),
                  pl.BlockSpec((1, d), lambda i: (0, 0)),
                  pl.BlockSpec(memory_space=pl.ANY)],
        out_specs=pl.BlockSpec((tm, d), lambda i: (i, 0)),
        out_shape=jax.ShapeDtypeStruct((t, d), F32),
        scratch_shapes=[pltpu.VMEM((TOP_K, tm, d), F32), pltpu.SemaphoreType.DMA(())],
        compiler_params=_params(("arbitrary",)),
        name="combine",
    )(e_tiles, rk_tiles, ps, w_tk, xs2d, g2, final_g.reshape(1, d), y_sorted)


def _tile_major(a_kt, tm):
    k, t = a_kt.shape
    return a_kt.reshape(k, t // tm, tm).transpose(1, 0, 2).reshape(-1)


def kernel(x, c, ctx, c_ctx, w_mod, b_mod, norm1_g, norm2_g, w_in, w_out, hy_conv_w, hy_conv_b,
           hy_fw1, hy_fb1, hy_fw2, hy_fb2, hy_fw3, hy_freq, hy_d, hg_lb_logits, hg_norm_g,
           w_router, router_bias, ew_gate, ew_up, ew_down, sw_gate, sw_up, sw_down, final_g):
    b, seq, d = x.shape
    ctx_len = ctx.shape[1]
    t = b * seq
    layer = 0

    lower = jnp.cumsum(jax.nn.softmax(hg_lb_logits.astype(F32), axis=1), axis=1)
    lb_f, lb_b = lower[0, layer], lower[1, layer]

    rows = -(-(b + 1) // 8) * 8
    cc = jnp.concatenate([c, c_ctx[None, :], jnp.zeros((rows - b - 1, d), F32)], axis=0)
    mod = _modulation(cc, w_mod[layer], b_mod[layer])
    sh1, sc1, g1, sh2, sc2, g2 = (m.reshape(b, 1, d) for m in jnp.split(mod[:b], 6, axis=-1))
    csh1, csc1 = (jnp.broadcast_to(m.reshape(1, 1, d), (b, 1, d))
                  for m in jnp.split(mod[b:b + 1], 6, axis=-1)[:2])

    w_in_bf = w_in[layer].astype(BF16)
    x2d = x.reshape(t, d)
    px = _inproj(x2d, seq, norm1_g[layer], sc1, sh1, w_in_bf, min(seq, 1024), 1024)
    lo = 3 * HY_W + HG_W
    pc = _inproj(ctx.reshape(b * ctx_len, d), ctx_len, norm1_g[layer], csc1, csh1,
                 w_in_bf[:, lo:lo + 3 * HG_W], ctx_len, HG_W)
    px3 = px.reshape(b, seq, -1)
    pc3 = pc.reshape(b, ctx_len, -1)

    y_hg = _hgrn(px3, pc3, lb_f, lb_b, hg_norm_g[layer])

    a_mat, s_mat = _dft_mats(seq)
    u1, u2, un = _hyena_filters(seq, hy_fw1[layer], hy_fb1[layer], hy_fw2[layer], hy_fb2[layer],
                                hy_fw3[layer], hy_freq[layer], a_mat, s_mat)
    y_hy = _hyena(px3, hy_conv_w[layer], hy_conv_b[layer], u1, u2, un, hy_d[layer], a_mat, s_mat)

    w_out_bf = w_out[layer].astype(BF16)
    tm_mid = min(seq, 512)
    xs, h2, e_kt, w_kt, rk_kt, counts = _mid(
        y_hy.reshape(t, HY_W), y_hg.reshape(t, HG_W), x2d, seq, w_out_bf[:HY_W], w_out_bf[HY_W:],
        g1, sc2, sh2, g2, norm2_g[layer], w_router[layer].T, router_bias[layer],
        sw_gate[layer].astype(BF16), sw_up[layer].astype(BF16), sw_down[layer].astype(BF16), tm_mid)

    bm = FFN_BLOCK
    counts = counts.reshape(N_EXPERTS)
    padded = (counts + bm - 1) // bm * bm
    p_ends = jnp.cumsum(padded)
    ps = jnp.concatenate([p_ends - padded, p_ends[-1:]]).astype(I32)
    n_blocks = t * TOP_K // bm + N_EXPERTS
    block_e = jnp.minimum(jnp.searchsorted(p_ends, jnp.arange(n_blocks, dtype=I32) * bm, side='right'),
                          N_EXPERTS - 1).astype(I32)
    n_active = (p_ends[-1:] // bm).astype(I32)

    tm_moe = min(seq, 256)
    e_tiles = _tile_major(e_kt, tm_moe)
    rk_tiles = _tile_major(rk_kt, tm_moe)
    x_sorted = _dispatch(h2, e_tiles, rk_tiles, ps, n_blocks * bm, tm_moe)
    y_sorted = _ffn(x_sorted, block_e, n_active, ew_gate[layer], ew_up[layer], ew_down[layer])
    out = _combine(y_sorted, e_tiles, rk_tiles, ps, w_kt.T, xs, seq, g2, final_g, tm_moe)
    return out.reshape(b, seq, d)
```

```python
import functools
import math

import jax
import jax.numpy as jnp
from jax import lax
from jax.experimental import pallas as pl
from jax.experimental.pallas import tpu as pltpu

F32 = jnp.float32
BF16 = jnp.bfloat16
I32 = jnp.int32
HIGHEST = lax.Precision.HIGHEST

GRID_W = 64
HY_W = 512
HG_W = 512
HY_EMB = 33
HY_BANDS = 16
HY_DECAY_TARGET = 1e-2
HY_FAST_DECAY_PCT = 0.3
HY_SLOW_DECAY_PCT = 1.5
HG_HEAD_DIM = 128
HG_HEADS = 4
HG_SCALE = HG_HEAD_DIM ** -0.5
HG_CHUNK = 64
N_EXPERTS = 256
TOP_K = 8
N_GROUPS = 8
TOPK_GROUPS = 4
GROUP_SIZE = N_EXPERTS // N_GROUPS
ROUTED_SCALE = 2.5
NORM_EPS = 1e-6

VMEM_LIMIT_BYTES = 56 * 1024 * 1024
LANES = 128
FFN_BLOCK = 256
MOE_TILE = 512
SLOT_ALIGN = 8
MOE_ROW_CHUNK = 512
MOE_CHUNKS = 1024
HY_CT = 256


def _params(sem, vmem=VMEM_LIMIT_BYTES):
    return pltpu.CompilerParams(dimension_semantics=sem, vmem_limit_bytes=vmem)


def _silu(x):
    return x * jax.nn.sigmoid(x)


def _dot(a, b):
    return jnp.dot(a, b, preferred_element_type=F32)


def _dot_nt(a, b):
    return lax.dot_general(a, b, (((1,), (1,)), ((), ())), preferred_element_type=F32)


def _dot_tn(a, b):
    return lax.dot_general(a, b, (((0,), (0,)), ((), ())), preferred_element_type=F32)


def _mod_kernel(c_ref, w_ref, b_ref, o_ref):
    s = _silu(c_ref[...])
    o_ref[...] = jnp.dot(s, w_ref[...], preferred_element_type=F32, precision=HIGHEST) + b_ref[...]


def _modulation(cc, w_mod, b_mod):
    rows, d = cc.shape
    n = w_mod.shape[1]
    tn = 1024
    return pl.pallas_call(
        _mod_kernel,
        grid=(n // tn,),
        in_specs=[pl.BlockSpec((rows, d), lambda j: (0, 0)),
                  pl.BlockSpec((d, tn), lambda j: (0, j)),
                  pl.BlockSpec((1, tn), lambda j: (0, j))],
        out_specs=pl.BlockSpec((rows, tn), lambda j: (0, j)),
        out_shape=jax.ShapeDtypeStruct((rows, n), F32),
        compiler_params=_params(("parallel",)),
        name="mod",
    )(cc, w_mod, b_mod.reshape(1, n))


def _rms_mod(x, g, sc, sh):
    y = x * lax.rsqrt(jnp.mean(x * x, axis=-1, keepdims=True) + NORM_EPS) * g
    return y * (1.0 + sc) + sh


def _inproj_kernel(x_ref, g_ref, sc_ref, sh_ref, w_ref, o_ref, h_scr):
    @pl.when(pl.program_id(1) == 0)
    def _():
        h_scr[...] = _rms_mod(x_ref[...], g_ref[...], sc_ref[...], sh_ref[...]).astype(BF16)

    o_ref[...] = _dot(h_scr[...], w_ref[...]).astype(o_ref.dtype)


def _inproj(x2d, seq, g, sc, sh, w_bf, tm, tn):
    t, d = x2d.shape
    n = w_bf.shape[1]
    per = seq // tm
    return pl.pallas_call(
        _inproj_kernel,
        grid=(t // tm, n // tn),
        in_specs=[pl.BlockSpec((tm, d), lambda i, j: (i, 0)),
                  pl.BlockSpec((1, d), lambda i, j: (0, 0)),
                  pl.BlockSpec((None, 1, d), lambda i, j: (i // per, 0, 0)),
                  pl.BlockSpec((None, 1, d), lambda i, j: (i // per, 0, 0)),
                  pl.BlockSpec((d, tn), lambda i, j: (0, j))],
        out_specs=pl.BlockSpec((tm, tn), lambda i, j: (i, j)),
        out_shape=jax.ShapeDtypeStruct((t, n), BF16),
        scratch_shapes=[pltpu.VMEM((tm, d), BF16)],
        compiler_params=_params(("parallel", "arbitrary")),
        name="inproj",
    )(x2d, g.reshape(1, d), sc, sh, w_bf)


def _hg_steps(chains):
    c = HG_CHUNK
    r = lax.broadcasted_iota(I32, (c, c), 0)
    s = lax.broadcasted_iota(I32, (c, c), 1)
    geo = {False: (r >= s, c // 2 - 1, c - 1), True: (r <= s, c // 2, 0)}
    tri = {rev: jnp.where(g[0], 1.0, 0.0).astype(BF16) for rev, g in geo.items()}

    work = []
    for ch in chains:
        lb = ch["lb"]
        sig = jax.nn.sigmoid(ch["fr"])
        lf = jnp.log(lb + (1.0 - lb) * sig)
        k = (1.0 - lb) * (1.0 - sig)
        hi = lf.astype(BF16)
        lo = (lf - hi.astype(F32)).astype(BF16)
        t = tri[ch["rev"]]
        work.append(dict(k=k, bc=_dot(t, hi) + _dot(t, lo)))
    for ch, w in zip(chains, work):
        mask, mid, last = geo[ch["rev"]]
        bc = w["bc"]
        b_mid = bc[mid:mid + 1, :]
        b_last = bc[last:last + 1, :]
        km = w["k"] * jnp.exp(b_mid - bc)
        kd = (km * jnp.exp(b_last - b_mid)).astype(BF16)
        w["ut"] = _dot_tn(ch["v"], kd)
        w["decay"] = jnp.exp(b_last)
        if ch["q"] is not None:
            qm = ch["q"] * jnp.exp(bc - b_mid)
            w["att"] = _dot_nt(qm.astype(BF16), km.astype(BF16))
            qe = (qm * jnp.exp(b_mid)).astype(BF16)
            w["inter"] = _dot_nt(qe, ch["st"].astype(BF16))
    out = []
    for ch, w in zip(chains, work):
        o = None
        if ch["q"] is not None:
            att = jnp.where(geo[ch["rev"]][0], w["att"], 0.0).astype(BF16)
            o = _dot(att, ch["v"]) + w["inter"]
        out.append((o, ch["st"] * w["decay"] + w["ut"]))
    return out


def _hgrn_kernel(q_ref, ff_ref, fb_ref, i_ref, g_ref, cff_ref, cfb_ref, ci_ref,
                 lbf_ref, lbb_ref, ng_ref, o_ref, qs_scr, of_scr, ob_scr, st_scr):
    seq = q_ref.shape[0]
    ctx = cff_ref.shape[0]
    c = HG_CHUNK
    dh = HG_HEAD_DIM
    rb = min(seq, 256)

    for i in range(seq // rb):
        rows = slice(i * rb, (i + 1) * rb)
        qs_scr[rows, :] = (_silu(q_ref[rows, :].astype(F32)) * HG_SCALE).astype(BF16)
    st_scr[...] = jnp.zeros_like(st_scr)

    def chains(nchunks, ffr, fbr, vr, with_q):
        def body(n, carry):
            chains = []
            for h in range(HG_HEADS):
                cols = slice(h * dh, (h + 1) * dh)
                for rev, fref, lbref in ((False, ffr, lbf_ref), (True, fbr, lbb_ref)):
                    ci = (nchunks - 1 - n) if rev else n
                    rows = pl.ds(pl.multiple_of(ci * c, c), c)
                    chains.append(dict(
                        rev=rev, rows=rows, cols=cols, fr=fref[rows, cols].astype(F32), v=vr[rows, cols],
                        lb=lbref[:, cols], st=st_scr[len(chains)],
                        q=qs_scr[rows, cols].astype(F32) if with_q else None))
            for slot, (ch, (o, st)) in enumerate(zip(chains, _hg_steps(chains))):
                st_scr[slot] = st
                if with_q:
                    (ob_scr if ch["rev"] else of_scr)[ch["rows"], ch["cols"]] = o
            return carry

        lax.fori_loop(0, nchunks, body, 0)

    chains(ctx // c, cff_ref, cfb_ref, ci_ref, False)
    chains(seq // c, ff_ref, fb_ref, i_ref, True)

    ng = ng_ref[...]
    for i in range(seq // rb):
        rows = slice(i * rb, (i + 1) * rb)
        gate = _silu(g_ref[rows, :].astype(F32))
        for h in range(HG_HEADS):
            cols = slice(h * dh, (h + 1) * dh)
            o = of_scr[rows, cols] + ob_scr[rows, cols]
            on = o * lax.rsqrt(jnp.mean(o * o, axis=-1, keepdims=True) + NORM_EPS) * ng
            o_ref[rows, cols] = (on * gate[:, cols]).astype(o_ref.dtype)


def _hgrn(px3, pc3, lb_f, lb_b, norm_g):
    b, seq, _ = px3.shape
    ctx = pc3.shape[1]
    dh = HG_HEAD_DIM
    base = 3 * HY_W // HG_W

    def xspec(j):
        return pl.BlockSpec((None, seq, HG_W), lambda bi: (bi, 0, base + j))

    def cspec(j):
        return pl.BlockSpec((None, ctx, HG_W), lambda bi: (bi, 0, j))

    vec = pl.BlockSpec((1, HG_W), lambda bi: (0, 0))
    return pl.pallas_call(
        _hgrn_kernel,
        grid=(b,),
        in_specs=[xspec(0), xspec(1), xspec(2), xspec(3), xspec(4), cspec(0), cspec(1), cspec(2),
                  vec, vec, pl.BlockSpec((1, dh), lambda bi: (0, 0))],
        out_specs=pl.BlockSpec((None, seq, HG_W), lambda bi: (bi, 0, 0)),
        out_shape=jax.ShapeDtypeStruct((b, seq, HG_W), BF16),
        scratch_shapes=[pltpu.VMEM((seq, HG_W), BF16), pltpu.VMEM((seq, HG_W), F32),
                        pltpu.VMEM((seq, HG_W), F32), pltpu.VMEM((2 * HG_HEADS, dh, dh), F32)],
        compiler_params=_params(("parallel",)),
        name="hgrn",
    )(px3, px3, px3, px3, px3, pc3, pc3, pc3, lb_f.reshape(1, HG_W), lb_b.reshape(1, HG_W),
      norm_g.reshape(1, dh))


def _dft_mats(seq):
    f = jnp.arange(seq, dtype=I32)
    m = (f[:, None] * f[None, :]) % (2 * seq)
    ang = m.astype(F32) * (math.pi / seq)
    return jnp.cos(ang).astype(BF16), jnp.sin(ang).astype(BF16)


def _hyfilt_kernel(feat_ref, w1_ref, b1_ref, w2_ref, b2_ref, fr_ref, w3_ref, t_ref, dl_ref,
                   a_ref, s_ref, u1_ref, u2_ref, un_ref):
    seq = feat_ref.shape[0]
    fr = fr_ref[...]
    h = jnp.sin(fr * (jnp.dot(feat_ref[...], w1_ref[...], preferred_element_type=F32, precision=HIGHEST)
                      + b1_ref[...]))
    h = jnp.sin(fr * (jnp.dot(h, w2_ref[...], preferred_element_type=F32, precision=HIGHEST) + b2_ref[...]))
    window = jnp.exp(-t_ref[...] * dl_ref[...])
    row = lax.broadcasted_iota(I32, (seq, 1), 0)
    sgn = jnp.where(row % 2 == 0, 1.0, -1.0)
    cf = jnp.where(row == 0, 1.0, 2.0) * (1.0 / (2 * seq))
    for o in range(2):
        w3 = w3_ref[:, o * 2 * HY_W:(o + 1) * 2 * HY_W]
        ho = jnp.dot(h, w3, preferred_element_type=F32, precision=HIGHEST)
        fwd = ho[:, :HY_W] * window
        bwd = ho[:, HY_W:] * window
        norm = (jnp.sum(jnp.abs(fwd), axis=0, keepdims=True)
                + jnp.sum(jnp.abs(bwd), axis=0, keepdims=True))
        inv = 1.0 / norm
        ksum = (fwd + bwd) * inv
        kdif = (bwd - fwd) * inv
        kr = _dot(a_ref[...], ksum.astype(BF16))
        ki = _dot(s_ref[...], kdif.astype(BF16))
        u1_ref[o] = (kr * cf).astype(u1_ref.dtype)
        u2_ref[o] = (ki * cf).astype(u2_ref.dtype)
        un_ref[o] = jnp.sum(sgn * ksum, axis=0, keepdims=True) * (1.0 / (2 * seq))


def _hyena_filters(seq, fw1, fb1, fw2, fb2, fw3, freq, a_mat, s_mat):
    pos = jnp.arange(seq, dtype=F32)[:, None]
    t = pos / max(seq - 1, 1)
    w = (2.0 * math.pi / seq) * pos
    bands = jnp.linspace(1e-4, HY_BANDS - 1, HY_BANDS, dtype=F32)[None, :]
    feats = jnp.concatenate([t, jnp.cos(bands * w), -jnp.sin(bands * w)], axis=-1)
    feats = jnp.pad(feats, ((0, 0), (0, LANES - HY_EMB)))
    w1 = jnp.pad(fw1, ((0, LANES - HY_EMB), (0, 0)))
    max_decay = math.log(HY_DECAY_TARGET) / HY_FAST_DECAY_PCT
    min_decay = math.log(HY_DECAY_TARGET) / HY_SLOW_DECAY_PCT
    deltas = jnp.abs(jnp.linspace(min_decay, max_decay, HY_W, dtype=F32))[None, :]
    hid = fw2.shape[0]
    return pl.pallas_call(
        _hyfilt_kernel,
        out_shape=(jax.ShapeDtypeStruct((2, seq, HY_W), BF16),
                   jax.ShapeDtypeStruct((2, seq, HY_W), BF16),
                   jax.ShapeDtypeStruct((2, 1, HY_W), F32)),
        compiler_params=pltpu.CompilerParams(vmem_limit_bytes=VMEM_LIMIT_BYTES),
        name="hyfilt",
    )(feats, w1, fb1.reshape(1, hid), fw2, fb2.reshape(1, hid), freq.reshape(1, hid), fw3, t, deltas,
      a_mat, s_mat)


def _hyena_kernel(x1_ref, x2_ref, v_ref, w1_ref, w2_ref, wv_ref, b1_ref, b2_ref, bv_ref,
                  u1_ref, u2_ref, un_ref, d_ref, a_ref, s_ref, o_ref, z_scr, zb_scr, re_scr, im_scr):
    seq = x1_ref.shape[0]
    rb = min(seq, 512)
    nrb = seq // rb
    row = lax.broadcasted_iota(I32, (rb, 1), 0)
    col = row % GRID_W
    first = col == 0
    lastc = col == GRID_W - 1
    sgn = jnp.where(row % 2 == 0, 1.0, -1.0)

    def conv3(p_ref, w_ref, b_ref, rows):
        p = p_ref[rows, :].astype(F32)
        prev = jnp.where(first, 0.0, pltpu.roll(p, 1, axis=0))
        nxt = jnp.where(lastc, 0.0, pltpu.roll(p, rb - 1, axis=0))
        w = w_ref[...]
        return w[0:1, :] * prev + w[1:2, :] * p + w[2:3, :] * nxt + b_ref[...]

    def forward(o):
        nyq = None
        for i in range(nrb):
            rows = slice(i * rb, (i + 1) * rb)
            part = jnp.sum(sgn * z_scr[rows, :], axis=0, keepdims=True)
            nyq = part if nyq is None else nyq + part
        for i in range(nrb):
            rows = slice(i * rb, (i + 1) * rb)
            p = _dot(a_ref[rows, :], zb_scr[...])
            q = _dot(s_ref[rows, :], zb_scr[...])
            u1 = u1_ref[o, rows, :].astype(F32)
            u2 = u2_ref[o, rows, :].astype(F32)
            re_scr[rows, :] = (p * u1 + q * u2).astype(BF16)
            im_scr[rows, :] = (q * u1 - p * u2).astype(BF16)
        return nyq * un_ref[o]

    def inverse(o, nyq, rows):
        y = _dot(a_ref[rows, :], re_scr[...]) + _dot(s_ref[rows, :], im_scr[...]) + sgn * nyq
        return y + z_scr[rows, :] * d_ref[o:o + 1, :]

    for i in range(nrb):
        rows = slice(i * rb, (i + 1) * rb)
        v = conv3(v_ref, wv_ref, bv_ref, rows)
        z_scr[rows, :] = v
        zb_scr[rows, :] = v.astype(BF16)
    nyq = forward(0)
    for i in range(nrb):
        rows = slice(i * rb, (i + 1) * rb)
        z = conv3(x1_ref, w1_ref, b1_ref, rows) * inverse(0, nyq, rows)
        z_scr[rows, :] = z
        zb_scr[rows, :] = z.astype(BF16)
    nyq = forward(1)
    for i in range(nrb):
        rows = slice(i * rb, (i + 1) * rb)
        y = conv3(x2_ref, w2_ref, b2_ref, rows) * inverse(1, nyq, rows)
        o_ref[rows, :] = y.astype(o_ref.dtype)


def _hyena(px3, conv_w, conv_b, u1, u2, un, d_skip, a_mat, s_mat):
    b, seq, _ = px3.shape
    ct = HY_CT
    nc = HY_W // ct

    def xspec(j):
        return pl.BlockSpec((None, seq, ct), lambda c, bi: (bi, 0, j * nc + c))

    def wspec(j, rows):
        return pl.BlockSpec((rows, ct), lambda c, bi: (0, j * nc + c))

    uspec = pl.BlockSpec((2, seq, ct), lambda c, bi: (0, 0, c))
    const = pl.BlockSpec((seq, seq), lambda c, bi: (0, 0), pipeline_mode=pl.Buffered(1))
    cb = conv_b.reshape(1, 3 * HY_W)
    return pl.pallas_call(
        _hyena_kernel,
        grid=(nc, b),
        in_specs=[xspec(0), xspec(1), xspec(2), wspec(0, 3), wspec(1, 3), wspec(2, 3),
                  wspec(0, 1), wspec(1, 1), wspec(2, 1), uspec, uspec,
                  pl.BlockSpec((2, 1, ct), lambda c, bi: (0, 0, c)),
                  pl.BlockSpec((2, ct), lambda c, bi: (0, c)), const, const],
        out_specs=pl.BlockSpec((None, seq, ct), lambda c, bi: (bi, 0, c)),
        out_shape=jax.ShapeDtypeStruct((b, seq, HY_W), BF16),
        scratch_shapes=[pltpu.VMEM((seq, ct), F32), pltpu.VMEM((seq, ct), BF16),
                        pltpu.VMEM((seq, ct), BF16), pltpu.VMEM((seq, ct), BF16)],
        compiler_params=_params(("parallel", "parallel")),
        name="hyena",
    )(px3, px3, px3, conv_w, conv_w, conv_w, cb, cb, cb, u1, u2, un, d_skip, a_mat, s_mat)


def _select_rows(rows):
    n = rows[0].shape[1]
    idx = lax.broadcasted_iota(I32, (len(rows), n), 0)
    out = jnp.zeros((len(rows), n), rows[0].dtype)
    for k, r in enumerate(rows):
        out = jnp.where(idx == k, r, out)
    return out


def _first_argmax(x, iota, size):
    m = jnp.max(x, axis=0, keepdims=True)
    return jnp.min(jnp.where(x == m, iota, size), axis=0, keepdims=True)


def _mid_kernel(yhy_ref, yhg_ref, x_ref, woa_ref, wob_ref, g1_ref, sc_ref, sh_ref, g2_ref, ng_ref,
                wr_ref, rb_ref, swg_ref, swu_ref, swd_ref,
                xs_ref, h_ref, lrow_ref, w_ref, ce_ref, rel_ref, nch_ref, cnt_ref, base):
    tm = x_ref.shape[0]
    ne = N_EXPERTS

    @pl.when(pl.program_id(0) == 0)
    def _():
        base[...] = jnp.zeros_like(base)

    mix = _dot(yhy_ref[...], woa_ref[...]) + _dot(yhg_ref[...], wob_ref[...])
    xm = x_ref[...] + g1_ref[...] * mix
    h = _rms_mod(xm, ng_ref[...], sc_ref[...], sh_ref[...])
    hb = h.astype(BF16)
    h_ref[...] = hb
    act = (_silu(_dot(hb, swg_ref[...])) * _dot(hb, swu_ref[...])).astype(BF16)
    xs_ref[...] = xm + g2_ref[...] * _dot(act, swd_ref[...])

    logits = lax.dot_general(wr_ref[...], h, (((1,), (1,)), ((), ())), preferred_element_type=F32,
                             precision=HIGHEST)
    scores = jax.nn.sigmoid(logits)
    biased = scores + rb_ref[...]
    neg = -jnp.inf
    iota_g = lax.broadcasted_iota(I32, (GROUP_SIZE, tm), 0)
    grp = []
    for g in range(N_GROUPS):
        blk = biased[g * GROUP_SIZE:(g + 1) * GROUP_SIZE, :]
        m1 = jnp.max(blk, axis=0, keepdims=True)
        i1 = jnp.min(jnp.where(blk == m1, iota_g, GROUP_SIZE), axis=0, keepdims=True)
        m2 = jnp.max(jnp.where(iota_g == i1, neg, blk), axis=0, keepdims=True)
        grp.append(m1 + m2)
    gsc = _select_rows(grp)
    iota8 = lax.broadcasted_iota(I32, (N_GROUPS, tm), 0)
    gsel = iota8 < 0
    for _ in range(TOPK_GROUPS):
        hit = iota8 == _first_argmax(gsc, iota8, N_GROUPS)
        gsel = gsel | hit
        gsc = jnp.where(hit, neg, gsc)
    gself = jnp.where(gsel, 1.0, 0.0)
    iota_e = lax.broadcasted_iota(I32, (ne, tm), 0)
    gid = iota_e // GROUP_SIZE
    emask = jnp.zeros((ne, tm), F32)
    for g in range(N_GROUPS):
        emask = jnp.where(gid == g, gself[g:g + 1, :], emask)
    cur = jnp.where(emask > 0.0, biased, neg)

    sel = iota_e < 0
    idxs, wts = [], []
    for _ in range(TOP_K):
        idx = _first_argmax(cur, iota_e, ne)
        hit = iota_e == idx
        wts.append(jnp.sum(jnp.where(hit, scores, 0.0), axis=0, keepdims=True))
        idxs.append(idx)
        sel = sel | hit
        cur = jnp.where(hit, neg, cur)
    wsum = wts[0]
    for wk in wts[1:]:
        wsum = wsum + wk
    scale = ROUTED_SCALE / wsum

    r = lax.broadcasted_iota(I32, (tm, tm), 0)
    s = lax.broadcasted_iota(I32, (tm, tm), 1)
    upper = jnp.where(r < s, 1.0, 0.0).astype(BF16)
    sel_f = jnp.where(sel, 1.0, 0.0)
    prefix = _dot(sel_f.astype(BF16), upper)
    cnt = jnp.sum(sel_f, axis=1, keepdims=True)
    cnt_al = jnp.floor((cnt + (SLOT_ALIGN - 1)) * (1.0 / SLOT_ALIGN)) * SLOT_ALIGN
    re = lax.broadcasted_iota(I32, (ne, ne), 0)
    ce = lax.broadcasted_iota(I32, (ne, ne), 1)
    lower = jnp.where(ce < re, 1.0, 0.0).astype(BF16)
    loc = _dot(lower, jnp.broadcast_to(cnt_al, (ne, LANES)).astype(BF16))[:, 0:1]
    lrow_all = loc + prefix
    lrows = [jnp.sum(jnp.where(iota_e == idx, lrow_all, 0.0), axis=0, keepdims=True) for idx in idxs]

    nc = ce_ref.shape[-1]
    jrow = lax.broadcasted_iota(I32, (1, nc), 1).astype(F32) * SLOT_ALIGN
    owner = jnp.sum(jnp.where(loc + cnt_al <= jrow, 1.0, 0.0), axis=0, keepdims=True)
    owner = jnp.minimum(owner, ne - 1.0).astype(I32)
    iota_ec = lax.broadcasted_iota(I32, (ne, nc), 0)
    rel = jnp.sum(jnp.where(iota_ec == owner, base[...] - loc, 0.0), axis=0, keepdims=True) + jrow
    base[...] = base[...] + cnt_al

    lrow_ref[...] = _select_rows(lrows).astype(I32)
    w_ref[...] = _select_rows([wk * scale for wk in wts])
    ce_ref[...] = owner
    rel_ref[...] = rel.astype(I32)
    nch = jnp.sum(cnt_al, axis=0, keepdims=True) * (1.0 / SLOT_ALIGN)
    nch_ref[...] = jnp.broadcast_to(nch, nch_ref.shape).astype(I32)
    cnt_ref[...] = base[...].astype(I32)


def _mid(yhy, yhg, x2d, seq, woa, wob, g1, sc2, sh2, g2, norm_g, wr_t, rbias, swg, swu, swd, tm):
    t, d = x2d.shape
    per = seq // tm
    ff = swg.shape[1]

    def full(shape):
        return pl.BlockSpec(shape, lambda i: (0,) * len(shape))

    mspec = pl.BlockSpec((None, 1, d), lambda i: (i // per, 0, 0))
    kt = pl.BlockSpec((TOP_K, tm), lambda i: (0, i))
    nt = t // tm

    def per_tile(n):
        return pl.BlockSpec((None, 1, n), lambda i: (i, 0, 0))

    return pl.pallas_call(
        _mid_kernel,
        grid=(nt,),
        in_specs=[pl.BlockSpec((tm, HY_W), lambda i: (i, 0)), pl.BlockSpec((tm, HG_W), lambda i: (i, 0)),
                  pl.BlockSpec((tm, d), lambda i: (i, 0)), full((HY_W, d)), full((HG_W, d)),
                  mspec, mspec, mspec, mspec, full((1, d)), full((N_EXPERTS, d)), full((N_EXPERTS, 1)),
                  full((d, ff)), full((d, ff)), full((ff, d))],
        out_specs=[pl.BlockSpec((tm, d), lambda i: (i, 0)), pl.BlockSpec((tm, d), lambda i: (i, 0)),
                   kt, kt, per_tile(MOE_CHUNKS), per_tile(MOE_CHUNKS), per_tile(LANES), full((N_EXPERTS, 1))],
        out_shape=(jax.ShapeDtypeStruct((t, d), F32), jax.ShapeDtypeStruct((t, d), BF16),
                   jax.ShapeDtypeStruct((TOP_K, t), I32), jax.ShapeDtypeStruct((TOP_K, t), F32),
                   jax.ShapeDtypeStruct((nt, 1, MOE_CHUNKS), I32), jax.ShapeDtypeStruct((nt, 1, MOE_CHUNKS), I32),
                   jax.ShapeDtypeStruct((nt, 1, LANES), I32), jax.ShapeDtypeStruct((N_EXPERTS, 1), I32)),
        scratch_shapes=[pltpu.VMEM((N_EXPERTS, 1), F32)],
        compiler_params=_params(("arbitrary",)),
        name="mid",
    )(yhy, yhg, x2d, woa, wob, g1, sc2, sh2, g2, norm_g.reshape(1, d), wr_t, rbias.reshape(N_EXPERTS, 1),
      swg, swu, swd)


def _pack_bf16_pairs(x):
    n = x.shape[1] // 2
    lo = lax.shift_right_logical(pltpu.bitcast(x[:, :n], I32), 16)
    hi = pltpu.bitcast(x[:, n:], I32) & -65536
    return hi | lo


def _unpack_bf16_pairs(p):
    lo = pltpu.bitcast(lax.shift_left(p, 16), F32).astype(BF16)
    hi = pltpu.bitcast(p & -65536, F32).astype(BF16)
    return lo, hi


def _chunk_copies(ce_ref, rel_ref, ps_ref, nch, sem, make):
    def copy(j):
        glob = pl.multiple_of(ps_ref[ce_ref[j]] + rel_ref[j], SLOT_ALIGN)
        return make(pl.multiple_of(j * SLOT_ALIGN, SLOT_ALIGN), glob)

    def start(j, carry):
        copy(j).start()
        return carry

    def wait(j, carry):
        copy(j).wait()
        return carry

    lax.fori_loop(0, nch, start, 0)
    lax.fori_loop(0, nch, wait, 0)


def _dispatch_kernel(nch_ref, ce_ref, rel_ref, ps_ref, h_ref, lrow_ref, xs_hbm, xloc, zbuf, sem):
    tm = h_ref.shape[0]
    bm = zbuf.shape[0]
    rc = MOE_ROW_CHUNK

    @pl.when(pl.program_id(0) == 0)
    def _():
        zbuf[...] = jnp.zeros_like(zbuf)

        def zcopy(e):
            start = pl.multiple_of(ps_ref[e + 1] - bm, bm)
            return pltpu.make_async_copy(zbuf, xs_hbm.at[pl.ds(start, bm), :], sem)

        def start(e, carry):
            @pl.when(ps_ref[e + 1] > ps_ref[e])
            def _():
                zcopy(e).start()
            return carry

        def wait(e, carry):
            @pl.when(ps_ref[e + 1] > ps_ref[e])
            def _():
                zcopy(e).wait()
            return carry

        lax.fori_loop(0, N_EXPERTS, start, 0)
        lax.fori_loop(0, N_EXPERTS, wait, 0)

        def tcopy(j):
            return pltpu.make_async_copy(zbuf, xs_hbm.at[pl.ds(pl.multiple_of(j * bm, bm), bm), :], sem)

        def tstart(j, carry):
            tcopy(j).start()
            return carry

        def twait(j, carry):
            tcopy(j).wait()
            return carry

        first_tail = ps_ref[N_EXPERTS] // bm
        lax.fori_loop(first_tail, xs_hbm.shape[0] // bm, tstart, 0)
        lax.fori_loop(first_tail, xs_hbm.shape[0] // bm, twait, 0)

    nch = nch_ref[pl.program_id(0)]
    lrow = lrow_ref[...]

    def build(c, carry):
        r0 = pl.multiple_of(c * rc, rc)
        rid = lax.broadcasted_iota(I32, (rc, tm), 0) + r0
        hit = rid == lrow[0:1, :]
        for k in range(1, TOP_K):
            hit = hit | (rid == lrow[k:k + 1, :])
        onehot = jnp.where(hit, 1.0, 0.0).astype(BF16)
        xloc[pl.ds(r0, rc), :] = _pack_bf16_pairs(_dot(onehot, h_ref[...]))
        return carry

    lax.fori_loop(0, (nch * SLOT_ALIGN + rc - 1) // rc, build, 0)
    _chunk_copies(ce_ref, rel_ref, ps_ref, nch, sem,
                  lambda loc, glob: pltpu.make_async_copy(xloc.at[pl.ds(loc, SLOT_ALIGN), :],
                                                          xs_hbm.at[pl.ds(glob, SLOT_ALIGN), :], sem))


def _local_rows(tm):
    bound = TOP_K * tm + N_EXPERTS * (SLOT_ALIGN - 1)
    return -(-bound // MOE_ROW_CHUNK) * MOE_ROW_CHUNK


def _moe_smem_specs():
    chunk_list = pl.BlockSpec((MOE_CHUNKS,), lambda i, nch: (i,), memory_space=pltpu.SMEM)
    return [chunk_list, chunk_list,
            pl.BlockSpec((N_EXPERTS + 1,), lambda i, nch: (0,), memory_space=pltpu.SMEM)]


def _dispatch(h2d, lrow_kt, ce, rel, nch, ps, n_slots, tm):
    t, d = h2d.shape
    grid_spec = pltpu.PrefetchScalarGridSpec(
        num_scalar_prefetch=1,
        grid=(t // tm,),
        in_specs=_moe_smem_specs() + [pl.BlockSpec((tm, d), lambda i, nch: (i, 0)),
                                      pl.BlockSpec((TOP_K, tm), lambda i, nch: (0, i))],
        out_specs=pl.BlockSpec(memory_space=pl.ANY),
        scratch_shapes=[pltpu.VMEM((_local_rows(tm), d // 2), I32), pltpu.VMEM((FFN_BLOCK, d // 2), I32),
                        pltpu.SemaphoreType.DMA(())],
    )

    return pl.pallas_call(
        _dispatch_kernel,
        grid_spec=grid_spec,
        out_shape=jax.ShapeDtypeStruct((n_slots, d // 2), I32),
        compiler_params=_params(("arbitrary",)),
        name="dispatch",
    )(nch, ce, rel, ps, h2d, lrow_kt)


def _ffn_kernel(be_ref, na_ref, x_ref, wg_ref, wu_ref, wd_ref, o_ref):
    i = pl.program_id(0)

    @pl.when(i < na_ref[0])
    def _():
        half = x_ref.shape[1]
        lo, hi = _unpack_bf16_pairs(x_ref[...])

        def proj(w_ref):
            return _dot(lo, w_ref[:half, :].astype(BF16)) + _dot(hi, w_ref[half:, :].astype(BF16))

        act = (_silu(proj(wg_ref)) * proj(wu_ref)).astype(BF16)
        y = _dot(act, wd_ref[...].astype(BF16))
        o_ref[...] = _pack_bf16_pairs(y.astype(BF16).astype(F32))

    @pl.when(i >= na_ref[0])
    def _():
        o_ref[...] = jnp.zeros_like(o_ref)


def _ffn(xs, block_e, n_active, ew_gate, ew_up, ew_down):
    n_slots, half = xs.shape
    d = 2 * half
    bm = FFN_BLOCK
    ff = ew_gate.shape[2]
    grid_spec = pltpu.PrefetchScalarGridSpec(
        num_scalar_prefetch=2,
        grid=(n_slots // bm,),
        in_specs=[pl.BlockSpec((bm, half), lambda i, be, na: (jnp.minimum(i, na[0] - 1), 0)),
                  pl.BlockSpec((None, d, ff), lambda i, be, na: (be[i], 0, 0)),
                  pl.BlockSpec((None, d, ff), lambda i, be, na: (be[i], 0, 0)),
                  pl.BlockSpec((None, ff, d), lambda i, be, na: (be[i], 0, 0))],
        out_specs=pl.BlockSpec((bm, half), lambda i, be, na: (i, 0)),
    )
    return pl.pallas_call(
        _ffn_kernel,
        grid_spec=grid_spec,
        out_shape=jax.ShapeDtypeStruct((n_slots, half), I32),
        compiler_params=_params(("arbitrary",)),
        name="ffn",
    )(block_e, n_active, xs, ew_gate, ew_up, ew_down)


def _combine_kernel(nch_ref, ce_ref, rel_ref, ps_ref, lrow_ref, w_ref, xs_ref, g2_ref, fg_ref, y_hbm,
                    o_ref, yloc, acc, sem):
    tm = xs_ref.shape[0]
    half = yloc.shape[1]
    rc = MOE_ROW_CHUNK

    @pl.when(pl.program_id(0) == 0)
    def _():
        yloc[...] = jnp.zeros_like(yloc)

    nch = nch_ref[pl.program_id(0)]
    _chunk_copies(ce_ref, rel_ref, ps_ref, nch, sem,
                  lambda loc, glob: pltpu.make_async_copy(y_hbm.at[pl.ds(glob, SLOT_ALIGN), :],
                                                          yloc.at[pl.ds(loc, SLOT_ALIGN), :], sem))

    lrow = lrow_ref[...]
    w = w_ref[...]
    acc[...] = jnp.zeros_like(acc)

    def gather(c, carry):
        r0 = pl.multiple_of(c * rc, rc)
        rid = lax.broadcasted_iota(I32, (tm, rc), 1) + r0
        wm = jnp.where(rid == lrow[:, 0:1], w[:, 0:1], 0.0)
        for k in range(1, TOP_K):
            wm = jnp.where(rid == lrow[:, k:k + 1], w[:, k:k + 1], wm)
        wm = wm.astype(BF16)
        lo, hi = _unpack_bf16_pairs(yloc[pl.ds(r0, rc), :])
        acc[:, :half] += _dot(wm, lo)
        acc[:, half:] += _dot(wm, hi)
        return carry

    lax.fori_loop(0, (nch * SLOT_ALIGN + rc - 1) // rc, gather, 0)
    x = xs_ref[...] + g2_ref[...] * acc[...]
    o_ref[...] = x * lax.rsqrt(jnp.mean(x * x, axis=-1, keepdims=True) + NORM_EPS) * fg_ref[...]


def _combine(y_sorted, lrow_tk, w_tk, ce, rel, nch, ps, xs2d, seq, g2, final_g, tm):
    t, d = xs2d.shape
    per = seq // tm
    grid_spec = pltpu.PrefetchScalarGridSpec(
        num_scalar_prefetch=1,
        grid=(t // tm,),
        in_specs=_moe_smem_specs() + [pl.BlockSpec((tm, TOP_K), lambda i, nch: (i, 0)),
                                      pl.BlockSpec((tm, TOP_K), lambda i, nch: (i, 0)),
                                      pl.BlockSpec((tm, d), lambda i, nch: (i, 0)),
                                      pl.BlockSpec((None, 1, d), lambda i, nch: (i // per, 0, 0)),
                                      pl.BlockSpec((1, d), lambda i, nch: (0, 0)),
                                      pl.BlockSpec(memory_space=pl.ANY)],
        out_specs=pl.BlockSpec((tm, d), lambda i, nch: (i, 0)),
        scratch_shapes=[pltpu.VMEM((_local_rows(tm), d // 2), I32), pltpu.VMEM((tm, d), F32),
                        pltpu.SemaphoreType.DMA(())],
    )
    return pl.pallas_call(
        _combine_kernel,
        grid_spec=grid_spec,
        out_shape=jax.ShapeDtypeStruct((t, d), F32),
        compiler_params=_params(("arbitrary",)),
        name="combine",
    )(nch, ce, rel, ps, lrow_tk, w_tk, xs2d, g2, final_g.reshape(1, d), y_sorted)


def kernel(x, c, ctx, c_ctx, w_mod, b_mod, norm1_g, norm2_g, w_in, w_out, hy_conv_w, hy_conv_b,
           hy_fw1, hy_fb1, hy_fw2, hy_fb2, hy_fw3, hy_freq, hy_d, hg_lb_logits, hg_norm_g,
           w_router, router_bias, ew_gate, ew_up, ew_down, sw_gate, sw_up, sw_down, final_g):
    b, seq, d = x.shape
    ctx_len = ctx.shape[1]
    t = b * seq
    layer = 0

    lower = jnp.cumsum(jax.nn.softmax(hg_lb_logits.astype(F32), axis=1), axis=1)
    lb_f, lb_b = lower[0, layer], lower[1, layer]

    rows = -(-(b + 1) // 8) * 8
    cc = jnp.concatenate([c, c_ctx[None, :], jnp.zeros((rows - b - 1, d), F32)], axis=0)
    mod = _modulation(cc, w_mod[layer], b_mod[layer])
    sh1, sc1, g1, sh2, sc2, g2 = (m.reshape(b, 1, d) for m in jnp.split(mod[:b], 6, axis=-1))
    csh1, csc1 = (jnp.broadcast_to(m.reshape(1, 1, d), (b, 1, d))
                  for m in jnp.split(mod[b:b + 1], 6, axis=-1)[:2])

    w_in_bf = w_in[layer].astype(BF16)
    x2d = x.reshape(t, d)
    px = _inproj(x2d, seq, norm1_g[layer], sc1, sh1, w_in_bf, min(seq, 1024), 1024)
    lo = 3 * HY_W + HG_W
    pc = _inproj(ctx.reshape(b * ctx_len, d), ctx_len, norm1_g[layer], csc1, csh1,
                 w_in_bf[:, lo:lo + 3 * HG_W], ctx_len, HG_W)
    px3 = px.reshape(b, seq, -1)
    pc3 = pc.reshape(b, ctx_len, -1)

    y_hg = _hgrn(px3, pc3, lb_f, lb_b, hg_norm_g[layer])

    a_mat, s_mat = _dft_mats(seq)
    u1, u2, un = _hyena_filters(seq, hy_fw1[layer], hy_fb1[layer], hy_fw2[layer], hy_fb2[layer],
                                hy_fw3[layer], hy_freq[layer], a_mat, s_mat)
    y_hy = _hyena(px3, hy_conv_w[layer], hy_conv_b[layer], u1, u2, un, hy_d[layer], a_mat, s_mat)

    w_out_bf = w_out[layer].astype(BF16)
    tm = min(seq, MOE_TILE)
    nt = t // tm
    xs, h2, lrow_kt, w_kt, ce, rel, nch, counts = _mid(
        y_hy.reshape(t, HY_W), y_hg.reshape(t, HG_W), x2d, seq, w_out_bf[:HY_W], w_out_bf[HY_W:],
        g1, sc2, sh2, g2, norm2_g[layer], w_router[layer].T, router_bias[layer],
        sw_gate[layer].astype(BF16), sw_up[layer].astype(BF16), sw_down[layer].astype(BF16), tm)

    bm = FFN_BLOCK
    counts = counts.reshape(N_EXPERTS)
    padded = (counts + bm - 1) // bm * bm
    p_ends = jnp.cumsum(padded)
    ps = jnp.concatenate([p_ends - padded, p_ends[-1:]]).astype(I32)
    n_blocks = -(-(t * TOP_K + nt * N_EXPERTS * (SLOT_ALIGN - 1)) // bm) + N_EXPERTS
    block_e = jnp.minimum(jnp.searchsorted(p_ends, jnp.arange(n_blocks, dtype=I32) * bm, side='right'),
                          N_EXPERTS - 1).astype(I32)
    n_active = (p_ends[-1:] // bm).astype(I32)

    ce, rel, nch = ce.reshape(-1), rel.reshape(-1), nch[:, 0, 0]
    x_sorted = _dispatch(h2, lrow_kt, ce, rel, nch, ps, n_blocks * bm, tm)
    y_sorted = _ffn(x_sorted, block_e, n_active, ew_gate[layer], ew_up[layer], ew_down[layer])
    out = _combine(y_sorted, lrow_kt.T, w_kt.T, ce, rel, nch, ps, xs, seq, g2, final_g, tm)
    return out.reshape(b, seq, d)
```

```python
import functools
import math

import jax
import jax.numpy as jnp
from jax import lax
from jax.experimental import pallas as pl
from jax.experimental.pallas import tpu as pltpu

F32 = jnp.float32
BF16 = jnp.bfloat16
I32 = jnp.int32
HIGHEST = lax.Precision.HIGHEST

GRID_W = 64
HY_W = 512
HG_W = 512
HY_EMB = 33
HY_BANDS = 16
HY_DECAY_TARGET = 1e-2
HY_FAST_DECAY_PCT = 0.3
HY_SLOW_DECAY_PCT = 1.5
HG_HEAD_DIM = 128
HG_HEADS = 4
HG_SCALE = HG_HEAD_DIM ** -0.5
HG_CHUNK = 64
N_EXPERTS = 256
TOP_K = 8
N_GROUPS = 8
TOPK_GROUPS = 4
GROUP_SIZE = N_EXPERTS // N_GROUPS
ROUTED_SCALE = 2.5
NORM_EPS = 1e-6

VMEM_LIMIT_BYTES = 56 * 1024 * 1024
LANES = 128
FFN_BLOCK = 512
MOE_TILE = 512
SLOT_ALIGN = 8
MOE_ROW_CHUNK = 512
MOE_CHUNKS = 1024
HY_CT = 256


def _params(sem, vmem=VMEM_LIMIT_BYTES):
    return pltpu.CompilerParams(dimension_semantics=sem, vmem_limit_bytes=vmem)


def _silu(x):
    return x * jax.nn.sigmoid(x)


def _dot(a, b):
    return jnp.dot(a, b, preferred_element_type=F32)


def _dot_nt(a, b):
    return lax.dot_general(a, b, (((1,), (1,)), ((), ())), preferred_element_type=F32)


def _dot_tn(a, b):
    return lax.dot_general(a, b, (((0,), (0,)), ((), ())), preferred_element_type=F32)


def _mod_kernel(c_ref, w_ref, b_ref, o_ref):
    s = _silu(c_ref[...])
    o_ref[...] = jnp.dot(s, w_ref[...], preferred_element_type=F32, precision=HIGHEST) + b_ref[...]


def _modulation(cc, w_mod, b_mod):
    rows, d = cc.shape
    n = w_mod.shape[1]
    tn = 1024
    return pl.pallas_call(
        _mod_kernel,
        grid=(n // tn,),
        in_specs=[pl.BlockSpec((rows, d), lambda j: (0, 0)),
                  pl.BlockSpec((d, tn), lambda j: (0, j)),
                  pl.BlockSpec((1, tn), lambda j: (0, j))],
        out_specs=pl.BlockSpec((rows, tn), lambda j: (0, j)),
        out_shape=jax.ShapeDtypeStruct((rows, n), F32),
        compiler_params=_params(("parallel",)),
        name="mod",
    )(cc, w_mod, b_mod.reshape(1, n))


def _rms_mod(x, g, sc, sh):
    y = x * lax.rsqrt(jnp.mean(x * x, axis=-1, keepdims=True) + NORM_EPS) * g
    return y * (1.0 + sc) + sh


def _inproj_kernel(x_ref, g_ref, sc_ref, sh_ref, w_ref, o_ref, h_scr):
    @pl.when(pl.program_id(1) == 0)
    def _():
        h_scr[...] = _rms_mod(x_ref[...], g_ref[...], sc_ref[...], sh_ref[...]).astype(BF16)

    o_ref[...] = _dot(h_scr[...], w_ref[...]).astype(o_ref.dtype)


def _inproj(x2d, seq, g, sc, sh, w_bf, tm, tn):
    t, d = x2d.shape
    n = w_bf.shape[1]
    per = seq // tm
    return pl.pallas_call(
        _inproj_kernel,
        grid=(t // tm, n // tn),
        in_specs=[pl.BlockSpec((tm, d), lambda i, j: (i, 0)),
                  pl.BlockSpec((1, d), lambda i, j: (0, 0)),
                  pl.BlockSpec((None, 1, d), lambda i, j: (i // per, 0, 0)),
                  pl.BlockSpec((None, 1, d), lambda i, j: (i // per, 0, 0)),
                  pl.BlockSpec((d, tn), lambda i, j: (0, j))],
        out_specs=pl.BlockSpec((tm, tn), lambda i, j: (i, j)),
        out_shape=jax.ShapeDtypeStruct((t, n), BF16),
        scratch_shapes=[pltpu.VMEM((tm, d), BF16)],
        compiler_params=_params(("parallel", "arbitrary")),
        name="inproj",
    )(x2d, g.reshape(1, d), sc, sh, w_bf)


def _hg_steps(chains):
    c = HG_CHUNK
    r = lax.broadcasted_iota(I32, (c, c), 0)
    s = lax.broadcasted_iota(I32, (c, c), 1)
    geo = {False: (r >= s, c // 2 - 1, c - 1), True: (r <= s, c // 2, 0)}
    tri = {rev: jnp.where(g[0], 1.0, 0.0).astype(BF16) for rev, g in geo.items()}

    work = []
    for ch in chains:
        lb = ch["lb"]
        sig = jax.nn.sigmoid(ch["fr"])
        lf = jnp.log(lb + (1.0 - lb) * sig)
        k = (1.0 - lb) * (1.0 - sig)
        hi = lf.astype(BF16)
        lo = (lf - hi.astype(F32)).astype(BF16)
        t = tri[ch["rev"]]
        work.append(dict(k=k, bc=_dot(t, hi) + _dot(t, lo)))
    for ch, w in zip(chains, work):
        mask, mid, last = geo[ch["rev"]]
        bc = w["bc"]
        b_mid = bc[mid:mid + 1, :]
        b_last = bc[last:last + 1, :]
        km = w["k"] * jnp.exp(b_mid - bc)
        kd = (km * jnp.exp(b_last - b_mid)).astype(BF16)
        w["ut"] = _dot_tn(ch["v"], kd)
        w["decay"] = jnp.exp(b_last)
        if ch["q"] is not None:
            qm = ch["q"] * jnp.exp(bc - b_mid)
            w["att"] = _dot_nt(qm.astype(BF16), km.astype(BF16))
            qe = (qm * jnp.exp(b_mid)).astype(BF16)
            w["inter"] = _dot_nt(qe, ch["st"].astype(BF16))
    out = []
    for ch, w in zip(chains, work):
        o = None
        if ch["q"] is not None:
            att = jnp.where(geo[ch["rev"]][0], w["att"], 0.0).astype(BF16)
            o = _dot(att, ch["v"]) + w["inter"]
        out.append((o, ch["st"] * w["decay"] + w["ut"]))
    return out


def _hgrn_kernel(q_ref, ff_ref, fb_ref, i_ref, g_ref, cff_ref, cfb_ref, ci_ref,
                 lbf_ref, lbb_ref, ng_ref, o_ref, qs_scr, of_scr, ob_scr, st_scr):
    seq = q_ref.shape[0]
    ctx = cff_ref.shape[0]
    c = HG_CHUNK
    dh = HG_HEAD_DIM
    rb = min(seq, 256)

    for i in range(seq // rb):
        rows = slice(i * rb, (i + 1) * rb)
        qs_scr[rows, :] = (_silu(q_ref[rows, :].astype(F32)) * HG_SCALE).astype(BF16)
    st_scr[...] = jnp.zeros_like(st_scr)

    def chains(nchunks, ffr, fbr, vr, with_q):
        def body(n, carry):
            chains = []
            for h in range(HG_HEADS):
                cols = slice(h * dh, (h + 1) * dh)
                for rev, fref, lbref in ((False, ffr, lbf_ref), (True, fbr, lbb_ref)):
                    ci = (nchunks - 1 - n) if rev else n
                    rows = pl.ds(pl.multiple_of(ci * c, c), c)
                    chains.append(dict(
                        rev=rev, rows=rows, cols=cols, fr=fref[rows, cols].astype(F32), v=vr[rows, cols],
                        lb=lbref[:, cols], st=st_scr[len(chains)],
                        q=qs_scr[rows, cols].astype(F32) if with_q else None))
            for slot, (ch, (o, st)) in enumerate(zip(chains, _hg_steps(chains))):
                st_scr[slot] = st
                if with_q:
                    (ob_scr if ch["rev"] else of_scr)[ch["rows"], ch["cols"]] = o
            return carry

        lax.fori_loop(0, nchunks, body, 0)

    chains(ctx // c, cff_ref, cfb_ref, ci_ref, False)
    chains(seq // c, ff_ref, fb_ref, i_ref, True)

    ng = ng_ref[...]
    for i in range(seq // rb):
        rows = slice(i * rb, (i + 1) * rb)
        gate = _silu(g_ref[rows, :].astype(F32))
        for h in range(HG_HEADS):
            cols = slice(h * dh, (h + 1) * dh)
            o = of_scr[rows, cols] + ob_scr[rows, cols]
            on = o * lax.rsqrt(jnp.mean(o * o, axis=-1, keepdims=True) + NORM_EPS) * ng
            o_ref[rows, cols] = (on * gate[:, cols]).astype(o_ref.dtype)


def _hgrn(px3, pc3, lb_f, lb_b, norm_g):
    b, seq, _ = px3.shape
    ctx = pc3.shape[1]
    dh = HG_HEAD_DIM
    base = 3 * HY_W // HG_W

    def xspec(j):
        return pl.BlockSpec((None, seq, HG_W), lambda bi: (bi, 0, base + j))

    def cspec(j):
        return pl.BlockSpec((None, ctx, HG_W), lambda bi: (bi, 0, j))

    vec = pl.BlockSpec((1, HG_W), lambda bi: (0, 0))
    return pl.pallas_call(
        _hgrn_kernel,
        grid=(b,),
        in_specs=[xspec(0), xspec(1), xspec(2), xspec(3), xspec(4), cspec(0), cspec(1), cspec(2),
                  vec, vec, pl.BlockSpec((1, dh), lambda bi: (0, 0))],
        out_specs=pl.BlockSpec((None, seq, HG_W), lambda bi: (bi, 0, 0)),
        out_shape=jax.ShapeDtypeStruct((b, seq, HG_W), BF16),
        scratch_shapes=[pltpu.VMEM((seq, HG_W), BF16), pltpu.VMEM((seq, HG_W), F32),
                        pltpu.VMEM((seq, HG_W), F32), pltpu.VMEM((2 * HG_HEADS, dh, dh), F32)],
        compiler_params=_params(("parallel",)),
        name="hgrn",
    )(px3, px3, px3, px3, px3, pc3, pc3, pc3, lb_f.reshape(1, HG_W), lb_b.reshape(1, HG_W),
      norm_g.reshape(1, dh))


def _dft_mats(seq):
    f = jnp.arange(seq, dtype=I32)
    m = (f[:, None] * f[None, :]) % (2 * seq)
    ang = m.astype(F32) * (math.pi / seq)
    return jnp.cos(ang).astype(BF16), jnp.sin(ang).astype(BF16)


def _hyfilt_kernel(feat_ref, w1_ref, b1_ref, w2_ref, b2_ref, fr_ref, w3_ref, t_ref, dl_ref,
                   a_ref, s_ref, u1_ref, u2_ref, un_ref):
    seq = feat_ref.shape[0]
    fr = fr_ref[...]
    h = jnp.sin(fr * (jnp.dot(feat_ref[...], w1_ref[...], preferred_element_type=F32, precision=HIGHEST)
                      + b1_ref[...]))
    h = jnp.sin(fr * (jnp.dot(h, w2_ref[...], preferred_element_type=F32, precision=HIGHEST) + b2_ref[...]))
    window = jnp.exp(-t_ref[...] * dl_ref[...])
    row = lax.broadcasted_iota(I32, (seq, 1), 0)
    sgn = jnp.where(row % 2 == 0, 1.0, -1.0)
    cf = jnp.where(row == 0, 1.0, 2.0) * (1.0 / (2 * seq))
    for o in range(2):
        w3 = w3_ref[:, o * 2 * HY_W:(o + 1) * 2 * HY_W]
        ho = jnp.dot(h, w3, preferred_element_type=F32, precision=HIGHEST)
        fwd = ho[:, :HY_W] * window
        bwd = ho[:, HY_W:] * window
        norm = (jnp.sum(jnp.abs(fwd), axis=0, keepdims=True)
                + jnp.sum(jnp.abs(bwd), axis=0, keepdims=True))
        inv = 1.0 / norm
        ksum = (fwd + bwd) * inv
        kdif = (bwd - fwd) * inv
        kr = _dot(a_ref[...], ksum.astype(BF16))
        ki = _dot(s_ref[...], kdif.astype(BF16))
        u1_ref[o] = (kr * cf).astype(u1_ref.dtype)
        u2_ref[o] = (ki * cf).astype(u2_ref.dtype)
        un_ref[o] = jnp.sum(sgn * ksum, axis=0, keepdims=True) * (1.0 / (2 * seq))


def _hyena_filters(seq, fw1, fb1, fw2, fb2, fw3, freq, a_mat, s_mat):
    pos = jnp.arange(seq, dtype=F32)[:, None]
    t = pos / max(seq - 1, 1)
    w = (2.0 * math.pi / seq) * pos
    bands = jnp.linspace(1e-4, HY_BANDS - 1, HY_BANDS, dtype=F32)[None, :]
    feats = jnp.concatenate([t, jnp.cos(bands * w), -jnp.sin(bands * w)], axis=-1)
    feats = jnp.pad(feats, ((0, 0), (0, LANES - HY_EMB)))
    w1 = jnp.pad(fw1, ((0, LANES - HY_EMB), (0, 0)))
    max_decay = math.log(HY_DECAY_TARGET) / HY_FAST_DECAY_PCT
    min_decay = math.log(HY_DECAY_TARGET) / HY_SLOW_DECAY_PCT
    deltas = jnp.abs(jnp.linspace(min_decay, max_decay, HY_W, dtype=F32))[None, :]
    hid = fw2.shape[0]
    return pl.pallas_call(
        _hyfilt_kernel,
        out_shape=(jax.ShapeDtypeStruct((2, seq, HY_W), BF16),
                   jax.ShapeDtypeStruct((2, seq, HY_W), BF16),
                   jax.ShapeDtypeStruct((2, 1, HY_W), F32)),
        compiler_params=pltpu.CompilerParams(vmem_limit_bytes=VMEM_LIMIT_BYTES),
        name="hyfilt",
    )(feats, w1, fb1.reshape(1, hid), fw2, fb2.reshape(1, hid), freq.reshape(1, hid), fw3, t, deltas,
      a_mat, s_mat)


def _hyena_kernel(x1_ref, x2_ref, v_ref, w1_ref, w2_ref, wv_ref, b1_ref, b2_ref, bv_ref,
                  u1_ref, u2_ref, un_ref, d_ref, a_ref, s_ref, o_ref, z_scr, zb_scr, re_scr, im_scr):
    seq = x1_ref.shape[0]
    rb = min(seq, 512)
    nrb = seq // rb
    row = lax.broadcasted_iota(I32, (rb, 1), 0)
    col = row % GRID_W
    first = col == 0
    lastc = col == GRID_W - 1
    sgn = jnp.where(row % 2 == 0, 1.0, -1.0)

    def conv3(p_ref, w_ref, b_ref, rows):
        p = p_ref[rows, :].astype(F32)
        prev = jnp.where(first, 0.0, pltpu.roll(p, 1, axis=0))
        nxt = jnp.where(lastc, 0.0, pltpu.roll(p, rb - 1, axis=0))
        w = w_ref[...]
        return w[0:1, :] * prev + w[1:2, :] * p + w[2:3, :] * nxt + b_ref[...]

    def forward(o):
        nyq = None
        for i in range(nrb):
            rows = slice(i * rb, (i + 1) * rb)
            part = jnp.sum(sgn * z_scr[rows, :], axis=0, keepdims=True)
            nyq = part if nyq is None else nyq + part
        for i in range(nrb):
            rows = slice(i * rb, (i + 1) * rb)
            p = _dot(a_ref[rows, :], zb_scr[...])
            q = _dot(s_ref[rows, :], zb_scr[...])
            u1 = u1_ref[o, rows, :].astype(F32)
            u2 = u2_ref[o, rows, :].astype(F32)
            re_scr[rows, :] = (p * u1 + q * u2).astype(BF16)
            im_scr[rows, :] = (q * u1 - p * u2).astype(BF16)
        return nyq * un_ref[o]

    def inverse(o, nyq, rows):
        y = _dot(a_ref[rows, :], re_scr[...]) + _dot(s_ref[rows, :], im_scr[...]) + sgn * nyq
        return y + z_scr[rows, :] * d_ref[o:o + 1, :]

    for i in range(nrb):
        rows = slice(i * rb, (i + 1) * rb)
        v = conv3(v_ref, wv_ref, bv_ref, rows)
        z_scr[rows, :] = v
        zb_scr[rows, :] = v.astype(BF16)
    nyq = forward(0)
    for i in range(nrb):
        rows = slice(i * rb, (i + 1) * rb)
        z = conv3(x1_ref, w1_ref, b1_ref, rows) * inverse(0, nyq, rows)
        z_scr[rows, :] = z
        zb_scr[rows, :] = z.astype(BF16)
    nyq = forward(1)
    for i in range(nrb):
        rows = slice(i * rb, (i + 1) * rb)
        y = conv3(x2_ref, w2_ref, b2_ref, rows) * inverse(1, nyq, rows)
        o_ref[rows, :] = y.astype(o_ref.dtype)


def _hyena(px3, conv_w, conv_b, u1, u2, un, d_skip, a_mat, s_mat):
    b, seq, _ = px3.shape
    ct = HY_CT
    nc = HY_W // ct

    def xspec(j):
        return pl.BlockSpec((None, seq, ct), lambda c, bi: (bi, 0, j * nc + c))

    def wspec(j, rows):
        return pl.BlockSpec((rows, ct), lambda c, bi: (0, j * nc + c))

    uspec = pl.BlockSpec((2, seq, ct), lambda c, bi: (0, 0, c))
    const = pl.BlockSpec((seq, seq), lambda c, bi: (0, 0), pipeline_mode=pl.Buffered(1))
    cb = conv_b.reshape(1, 3 * HY_W)
    return pl.pallas_call(
        _hyena_kernel,
        grid=(nc, b),
        in_specs=[xspec(0), xspec(1), xspec(2), wspec(0, 3), wspec(1, 3), wspec(2, 3),
                  wspec(0, 1), wspec(1, 1), wspec(2, 1), uspec, uspec,
                  pl.BlockSpec((2, 1, ct), lambda c, bi: (0, 0, c)),
                  pl.BlockSpec((2, ct), lambda c, bi: (0, c)), const, const],
        out_specs=pl.BlockSpec((None, seq, ct), lambda c, bi: (bi, 0, c)),
        out_shape=jax.ShapeDtypeStruct((b, seq, HY_W), BF16),
        scratch_shapes=[pltpu.VMEM((seq, ct), F32), pltpu.VMEM((seq, ct), BF16),
                        pltpu.VMEM((seq, ct), BF16), pltpu.VMEM((seq, ct), BF16)],
        compiler_params=_params(("parallel", "parallel")),
        name="hyena",
    )(px3, px3, px3, conv_w, conv_w, conv_w, cb, cb, cb, u1, u2, un, d_skip, a_mat, s_mat)


def _select_rows(rows):
    n = rows[0].shape[1]
    idx = lax.broadcasted_iota(I32, (len(rows), n), 0)
    out = jnp.zeros((len(rows), n), rows[0].dtype)
    for k, r in enumerate(rows):
        out = jnp.where(idx == k, r, out)
    return out


def _first_argmax(x, iota, size):
    m = jnp.max(x, axis=0, keepdims=True)
    return jnp.min(jnp.where(x == m, iota, size), axis=0, keepdims=True)


def _mid_kernel(yhy_ref, yhg_ref, x_ref, woa_ref, wob_ref, g1_ref, sc_ref, sh_ref, g2_ref, ng_ref,
                wr_ref, rb_ref, swg_ref, swu_ref, swd_ref,
                xs_ref, h_ref, lrow_ref, w_ref, ce_ref, rel_ref, nch_ref, cnt_ref, base):
    tm = x_ref.shape[0]
    ne = N_EXPERTS

    @pl.when(pl.program_id(0) == 0)
    def _():
        base[...] = jnp.zeros_like(base)

    mix = _dot(yhy_ref[...], woa_ref[...]) + _dot(yhg_ref[...], wob_ref[...])
    xm = x_ref[...] + g1_ref[...] * mix
    h = _rms_mod(xm, ng_ref[...], sc_ref[...], sh_ref[...])
    hb = h.astype(BF16)
    h_ref[...] = hb
    act = (_silu(_dot(hb, swg_ref[...])) * _dot(hb, swu_ref[...])).astype(BF16)
    xs_ref[...] = xm + g2_ref[...] * _dot(act, swd_ref[...])

    logits = lax.dot_general(wr_ref[...], h, (((1,), (1,)), ((), ())), preferred_element_type=F32,
                             precision=HIGHEST)
    scores = jax.nn.sigmoid(logits)
    biased = scores + rb_ref[...]
    neg = -jnp.inf
    iota_g = lax.broadcasted_iota(I32, (GROUP_SIZE, tm), 0)
    grp = []
    for g in range(N_GROUPS):
        blk = biased[g * GROUP_SIZE:(g + 1) * GROUP_SIZE, :]
        m1 = jnp.max(blk, axis=0, keepdims=True)
        i1 = jnp.min(jnp.where(blk == m1, iota_g, GROUP_SIZE), axis=0, keepdims=True)
        m2 = jnp.max(jnp.where(iota_g == i1, neg, blk), axis=0, keepdims=True)
        grp.append(m1 + m2)
    gsc = _select_rows(grp)
    iota8 = lax.broadcasted_iota(I32, (N_GROUPS, tm), 0)
    gsel = iota8 < 0
    for _ in range(TOPK_GROUPS):
        hit = iota8 == _first_argmax(gsc, iota8, N_GROUPS)
        gsel = gsel | hit
        gsc = jnp.where(hit, neg, gsc)
    gself = jnp.where(gsel, 1.0, 0.0)
    iota_e = lax.broadcasted_iota(I32, (ne, tm), 0)
    gid = iota_e // GROUP_SIZE
    emask = jnp.zeros((ne, tm), F32)
    for g in range(N_GROUPS):
        emask = jnp.where(gid == g, gself[g:g + 1, :], emask)
    cur = jnp.where(emask > 0.0, biased, neg)

    sel = iota_e < 0
    idxs, wts = [], []
    for _ in range(TOP_K):
        idx = _first_argmax(cur, iota_e, ne)
        hit = iota_e == idx
        wts.append(jnp.sum(jnp.where(hit, scores, 0.0), axis=0, keepdims=True))
        idxs.append(idx)
        sel = sel | hit
        cur = jnp.where(hit, neg, cur)
    wsum = wts[0]
    for wk in wts[1:]:
        wsum = wsum + wk
    scale = ROUTED_SCALE / wsum

    r = lax.broadcasted_iota(I32, (tm, tm), 0)
    s = lax.broadcasted_iota(I32, (tm, tm), 1)
    upper = jnp.where(r < s, 1.0, 0.0).astype(BF16)
    sel_f = jnp.where(sel, 1.0, 0.0)
    prefix = _dot(sel_f.astype(BF16), upper)
    cnt = jnp.sum(sel_f, axis=1, keepdims=True)
    cnt_al = jnp.floor((cnt + (SLOT_ALIGN - 1)) * (1.0 / SLOT_ALIGN)) * SLOT_ALIGN
    re = lax.broadcasted_iota(I32, (ne, ne), 0)
    ce = lax.broadcasted_iota(I32, (ne, ne), 1)
    lower = jnp.where(ce < re, 1.0, 0.0).astype(BF16)
    loc = _dot(lower, jnp.broadcast_to(cnt_al, (ne, LANES)).astype(BF16))[:, 0:1]
    lrow_all = loc + prefix
    lrows = [jnp.sum(jnp.where(iota_e == idx, lrow_all, 0.0), axis=0, keepdims=True) for idx in idxs]

    nc = ce_ref.shape[-1]
    jrow = lax.broadcasted_iota(I32, (1, nc), 1).astype(F32) * SLOT_ALIGN
    owner = jnp.sum(jnp.where(loc + cnt_al <= jrow, 1.0, 0.0), axis=0, keepdims=True)
    owner = jnp.minimum(owner, ne - 1.0).astype(I32)
    iota_ec = lax.broadcasted_iota(I32, (ne, nc), 0)
    rel = jnp.sum(jnp.where(iota_ec == owner, base[...] - loc, 0.0), axis=0, keepdims=True) + jrow
    base[...] = base[...] + cnt_al

    lrow_ref[...] = _select_rows(lrows).astype(I32)
    w_ref[...] = _select_rows([wk * scale for wk in wts])
    ce_ref[...] = owner
    rel_ref[...] = rel.astype(I32)
    nch = jnp.sum(cnt_al, axis=0, keepdims=True) * (1.0 / SLOT_ALIGN)
    nch_ref[...] = jnp.broadcast_to(nch, nch_ref.shape).astype(I32)
    cnt_ref[...] = base[...].astype(I32)


def _mid(yhy, yhg, x2d, seq, woa, wob, g1, sc2, sh2, g2, norm_g, wr_t, rbias, swg, swu, swd, tm):
    t, d = x2d.shape
    per = seq // tm
    ff = swg.shape[1]

    def full(shape):
        return pl.BlockSpec(shape, lambda i: (0,) * len(shape))

    mspec = pl.BlockSpec((None, 1, d), lambda i: (i // per, 0, 0))
    kt = pl.BlockSpec((TOP_K, tm), lambda i: (0, i))
    nt = t // tm

    def per_tile(n):
        return pl.BlockSpec((None, 1, n), lambda i: (i, 0, 0))

    return pl.pallas_call(
        _mid_kernel,
        grid=(nt,),
        in_specs=[pl.BlockSpec((tm, HY_W), lambda i: (i, 0)), pl.BlockSpec((tm, HG_W), lambda i: (i, 0)),
                  pl.BlockSpec((tm, d), lambda i: (i, 0)), full((HY_W, d)), full((HG_W, d)),
                  mspec, mspec, mspec, mspec, full((1, d)), full((N_EXPERTS, d)), full((N_EXPERTS, 1)),
                  full((d, ff)), full((d, ff)), full((ff, d))],
        out_specs=[pl.BlockSpec((tm, d), lambda i: (i, 0)), pl.BlockSpec((tm, d), lambda i: (i, 0)),
                   kt, kt, per_tile(MOE_CHUNKS), per_tile(MOE_CHUNKS), per_tile(LANES), full((N_EXPERTS, 1))],
        out_shape=(jax.ShapeDtypeStruct((t, d), F32), jax.ShapeDtypeStruct((t, d), BF16),
                   jax.ShapeDtypeStruct((TOP_K, t), I32), jax.ShapeDtypeStruct((TOP_K, t), F32),
                   jax.ShapeDtypeStruct((nt, 1, MOE_CHUNKS), I32), jax.ShapeDtypeStruct((nt, 1, MOE_CHUNKS), I32),
                   jax.ShapeDtypeStruct((nt, 1, LANES), I32), jax.ShapeDtypeStruct((N_EXPERTS, 1), I32)),
        scratch_shapes=[pltpu.VMEM((N_EXPERTS, 1), F32)],
        compiler_params=_params(("arbitrary",)),
        name="mid",
    )(yhy, yhg, x2d, woa, wob, g1, sc2, sh2, g2, norm_g.reshape(1, d), wr_t, rbias.reshape(N_EXPERTS, 1),
      swg, swu, swd)


def _pack_bf16_pairs(x):
    n = x.shape[1] // 2
    lo = lax.shift_right_logical(pltpu.bitcast(x[:, :n], I32), 16)
    hi = pltpu.bitcast(x[:, n:], I32) & -65536
    return hi | lo


def _unpack_bf16_pairs(p):
    lo = pltpu.bitcast(lax.shift_left(p, 16), F32).astype(BF16)
    hi = pltpu.bitcast(p & -65536, F32).astype(BF16)
    return lo, hi


def _chunk_copies(ce_ref, rel_ref, ps_ref, nch, sem, make):
    def copy(j):
        glob = pl.multiple_of(ps_ref[ce_ref[j]] + rel_ref[j], SLOT_ALIGN)
        return make(pl.multiple_of(j * SLOT_ALIGN, SLOT_ALIGN), glob)

    def start(j, carry):
        copy(j).start()
        return carry

    def wait(j, carry):
        copy(j).wait()
        return carry

    lax.fori_loop(0, nch, start, 0)
    lax.fori_loop(0, nch, wait, 0)


def _dispatch_kernel(nch_ref, ce_ref, rel_ref, ps_ref, h_ref, lrow_ref, xs_hbm, xloc, zbuf, sem):
    tm = h_ref.shape[0]
    bm = zbuf.shape[0]
    rc = MOE_ROW_CHUNK

    @pl.when(pl.program_id(0) == 0)
    def _():
        zbuf[...] = jnp.zeros_like(zbuf)

        def zcopy(e):
            start = pl.multiple_of(ps_ref[e + 1] - bm, bm)
            return pltpu.make_async_copy(zbuf, xs_hbm.at[pl.ds(start, bm), :], sem)

        def start(e, carry):
            @pl.when(ps_ref[e + 1] > ps_ref[e])
            def _():
                zcopy(e).start()
            return carry

        def wait(e, carry):
            @pl.when(ps_ref[e + 1] > ps_ref[e])
            def _():
                zcopy(e).wait()
            return carry

        lax.fori_loop(0, N_EXPERTS, start, 0)
        lax.fori_loop(0, N_EXPERTS, wait, 0)

        def tcopy(j):
            return pltpu.make_async_copy(zbuf, xs_hbm.at[pl.ds(pl.multiple_of(j * bm, bm), bm), :], sem)

        def tstart(j, carry):
            tcopy(j).start()
            return carry

        def twait(j, carry):
            tcopy(j).wait()
            return carry

        first_tail = ps_ref[N_EXPERTS] // bm
        lax.fori_loop(first_tail, xs_hbm.shape[0] // bm, tstart, 0)
        lax.fori_loop(first_tail, xs_hbm.shape[0] // bm, twait, 0)

    nch = nch_ref[pl.program_id(0)]
    lrow = lrow_ref[...].astype(F32)
    sub = rc // 2
    rid = lax.broadcasted_iota(I32, (sub, tm), 0).astype(F32).astype(BF16)
    one = jnp.ones((sub, tm), BF16)

    def select(r0):
        rel = (lrow - r0.astype(F32)).astype(BF16)
        onehot = jnp.zeros((sub, tm), BF16)
        for k in range(TOP_K):
            onehot = jnp.where(rid == rel[k:k + 1, :], one, onehot)
        xloc[pl.ds(r0, sub), :] = _pack_bf16_pairs(_dot(onehot, h_ref[...]))

    def build(c, carry):
        r0 = pl.multiple_of(c * rc, rc)
        select(r0)
        select(r0 + sub)
        return carry

    lax.fori_loop(0, (nch * SLOT_ALIGN + rc - 1) // rc, build, 0)
    _chunk_copies(ce_ref, rel_ref, ps_ref, nch, sem,
                  lambda loc, glob: pltpu.make_async_copy(xloc.at[pl.ds(loc, SLOT_ALIGN), :],
                                                          xs_hbm.at[pl.ds(glob, SLOT_ALIGN), :], sem))


def _local_rows(tm):
    bound = TOP_K * tm + N_EXPERTS * (SLOT_ALIGN - 1)
    return -(-bound // MOE_ROW_CHUNK) * MOE_ROW_CHUNK


def _moe_smem_specs():
    chunk_list = pl.BlockSpec((MOE_CHUNKS,), lambda i, nch: (i,), memory_space=pltpu.SMEM)
    return [chunk_list, chunk_list,
            pl.BlockSpec((N_EXPERTS + 1,), lambda i, nch: (0,), memory_space=pltpu.SMEM)]


def _dispatch(h2d, lrow_kt, ce, rel, nch, ps, n_slots, tm):
    t, d = h2d.shape
    grid_spec = pltpu.PrefetchScalarGridSpec(
        num_scalar_prefetch=1,
        grid=(t // tm,),
        in_specs=_moe_smem_specs() + [pl.BlockSpec((tm, d), lambda i, nch: (i, 0)),
                                      pl.BlockSpec((TOP_K, tm), lambda i, nch: (0, i))],
        out_specs=pl.BlockSpec(memory_space=pl.ANY),
        scratch_shapes=[pltpu.VMEM((_local_rows(tm), d // 2), I32), pltpu.VMEM((FFN_BLOCK, d // 2), I32),
                        pltpu.SemaphoreType.DMA(())],
    )

    return pl.pallas_call(
        _dispatch_kernel,
        grid_spec=grid_spec,
        out_shape=jax.ShapeDtypeStruct((n_slots, d // 2), I32),
        compiler_params=_params(("arbitrary",)),
        name="dispatch",
    )(nch, ce, rel, ps, h2d, lrow_kt)


def _ffn_kernel(be_ref, na_ref, x_ref, wg_ref, wu_ref, wd_ref, o_ref):
    i = pl.program_id(0)

    @pl.when(i < na_ref[0])
    def _():
        half = x_ref.shape[1]
        lo, hi = _unpack_bf16_pairs(x_ref[...])

        def proj(w_ref):
            return _dot(lo, w_ref[:half, :].astype(BF16)) + _dot(hi, w_ref[half:, :].astype(BF16))

        act = (_silu(proj(wg_ref)) * proj(wu_ref)).astype(BF16)
        y = _dot(act, wd_ref[...].astype(BF16))
        o_ref[...] = _pack_bf16_pairs(y.astype(BF16).astype(F32))

    @pl.when(i >= na_ref[0])
    def _():
        o_ref[...] = jnp.zeros_like(o_ref)


def _ffn(xs, block_e, n_active, ew_gate, ew_up, ew_down):
    n_slots, half = xs.shape
    d = 2 * half
    bm = FFN_BLOCK
    ff = ew_gate.shape[2]
    grid_spec = pltpu.PrefetchScalarGridSpec(
        num_scalar_prefetch=2,
        grid=(n_slots // bm,),
        in_specs=[pl.BlockSpec((bm, half), lambda i, be, na: (jnp.minimum(i, na[0] - 1), 0)),
                  pl.BlockSpec((None, d, ff), lambda i, be, na: (be[i], 0, 0)),
                  pl.BlockSpec((None, d, ff), lambda i, be, na: (be[i], 0, 0)),
                  pl.BlockSpec((None, ff, d), lambda i, be, na: (be[i], 0, 0))],
        out_specs=pl.BlockSpec((bm, half), lambda i, be, na: (i, 0)),
    )
    return pl.pallas_call(
        _ffn_kernel,
        grid_spec=grid_spec,
        out_shape=jax.ShapeDtypeStruct((n_slots, half), I32),
        compiler_params=_params(("arbitrary",)),
        name="ffn",
    )(block_e, n_active, xs, ew_gate, ew_up, ew_down)


def _combine_kernel(nch_ref, ce_ref, rel_ref, ps_ref, lrow_ref, w_ref, xs_ref, g2_ref, fg_ref, y_hbm,
                    o_ref, yloc, acc, sem):
    tm = xs_ref.shape[0]
    half = yloc.shape[1]
    rc = MOE_ROW_CHUNK

    @pl.when(pl.program_id(0) == 0)
    def _():
        yloc[...] = jnp.zeros_like(yloc)

    nch = nch_ref[pl.program_id(0)]
    _chunk_copies(ce_ref, rel_ref, ps_ref, nch, sem,
                  lambda loc, glob: pltpu.make_async_copy(y_hbm.at[pl.ds(glob, SLOT_ALIGN), :],
                                                          yloc.at[pl.ds(loc, SLOT_ALIGN), :], sem))

    lrow = lrow_ref[...].astype(F32)
    wrow = w_ref[...].astype(BF16)
    acc[...] = jnp.zeros_like(acc)
    sub = rc // 2
    rid = lax.broadcasted_iota(I32, (sub, tm), 0).astype(F32).astype(BF16)

    def weights(r0):
        rel = (lrow - r0.astype(F32)).astype(BF16)
        wt = jnp.zeros((sub, tm), BF16)
        for k in range(TOP_K):
            wt = jnp.where(rid == rel[k:k + 1, :], jnp.broadcast_to(wrow[k:k + 1, :], (sub, tm)), wt)
        return wt

    def gather(c, carry):
        r0 = pl.multiple_of(c * rc, rc)
        wt_a = weights(r0)
        wt_b = weights(r0 + sub)
        lo_a, hi_a = _unpack_bf16_pairs(yloc[pl.ds(r0, sub), :])
        lo_b, hi_b = _unpack_bf16_pairs(yloc[pl.ds(r0 + sub, sub), :])
        acc[:, :half] += _dot_tn(wt_a, lo_a) + _dot_tn(wt_b, lo_b)
        acc[:, half:] += _dot_tn(wt_a, hi_a) + _dot_tn(wt_b, hi_b)
        return carry

    lax.fori_loop(0, (nch * SLOT_ALIGN + rc - 1) // rc, gather, 0)
    x = xs_ref[...] + g2_ref[...] * acc[...]
    o_ref[...] = x * lax.rsqrt(jnp.mean(x * x, axis=-1, keepdims=True) + NORM_EPS) * fg_ref[...]


def _combine(y_sorted, lrow_kt, w_kt, ce, rel, nch, ps, xs2d, seq, g2, final_g, tm):
    t, d = xs2d.shape
    per = seq // tm
    grid_spec = pltpu.PrefetchScalarGridSpec(
        num_scalar_prefetch=1,
        grid=(t // tm,),
        in_specs=_moe_smem_specs() + [pl.BlockSpec((TOP_K, tm), lambda i, nch: (0, i)),
                                      pl.BlockSpec((TOP_K, tm), lambda i, nch: (0, i)),
                                      pl.BlockSpec((tm, d), lambda i, nch: (i, 0)),
                                      pl.BlockSpec((None, 1, d), lambda i, nch: (i // per, 0, 0)),
                                      pl.BlockSpec((1, d), lambda i, nch: (0, 0)),
                                      pl.BlockSpec(memory_space=pl.ANY)],
        out_specs=pl.BlockSpec((tm, d), lambda i, nch: (i, 0)),
        scratch_shapes=[pltpu.VMEM((_local_rows(tm), d // 2), I32), pltpu.VMEM((tm, d), F32),
                        pltpu.SemaphoreType.DMA(())],
    )
    return pl.pallas_call(
        _combine_kernel,
        grid_spec=grid_spec,
        out_shape=jax.ShapeDtypeStruct((t, d), F32),
        compiler_params=_params(("arbitrary",)),
        name="combine",
    )(nch, ce, rel, ps, lrow_kt, w_kt, xs2d, g2, final_g.reshape(1, d), y_sorted)


def kernel(x, c, ctx, c_ctx, w_mod, b_mod, norm1_g, norm2_g, w_in, w_out, hy_conv_w, hy_conv_b,
           hy_fw1, hy_fb1, hy_fw2, hy_fb2, hy_fw3, hy_freq, hy_d, hg_lb_logits, hg_norm_g,
           w_router, router_bias, ew_gate, ew_up, ew_down, sw_gate, sw_up, sw_down, final_g):
    b, seq, d = x.shape
    ctx_len = ctx.shape[1]
    t = b * seq
    layer = 0

    lower = jnp.cumsum(jax.nn.softmax(hg_lb_logits.astype(F32), axis=1), axis=1)
    lb_f, lb_b = lower[0, layer], lower[1, layer]

    rows = -(-(b + 1) // 8) * 8
    cc = jnp.concatenate([c, c_ctx[None, :], jnp.zeros((rows - b - 1, d), F32)], axis=0)
    mod = _modulation(cc, w_mod[layer], b_mod[layer])
    sh1, sc1, g1, sh2, sc2, g2 = (m.reshape(b, 1, d) for m in jnp.split(mod[:b], 6, axis=-1))
    csh1, csc1 = (jnp.broadcast_to(m.reshape(1, 1, d), (b, 1, d))
                  for m in jnp.split(mod[b:b + 1], 6, axis=-1)[:2])

    w_in_bf = w_in[layer].astype(BF16)
    x2d = x.reshape(t, d)
    px = _inproj(x2d, seq, norm1_g[layer], sc1, sh1, w_in_bf, min(seq, 1024), 1024)
    lo = 3 * HY_W + HG_W
    pc = _inproj(ctx.reshape(b * ctx_len, d), ctx_len, norm1_g[layer], csc1, csh1,
                 w_in_bf[:, lo:lo + 3 * HG_W], ctx_len, HG_W)
    px3 = px.reshape(b, seq, -1)
    pc3 = pc.reshape(b, ctx_len, -1)

    y_hg = _hgrn(px3, pc3, lb_f, lb_b, hg_norm_g[layer])

    a_mat, s_mat = _dft_mats(seq)
    u1, u2, un = _hyena_filters(seq, hy_fw1[layer], hy_fb1[layer], hy_fw2[layer], hy_fb2[layer],
                                hy_fw3[layer], hy_freq[layer], a_mat, s_mat)
    y_hy = _hyena(px3, hy_conv_w[layer], hy_conv_b[layer], u1, u2, un, hy_d[layer], a_mat, s_mat)

    w_out_bf = w_out[layer].astype(BF16)
    tm = min(seq, MOE_TILE)
    nt = t // tm
    xs, h2, lrow_kt, w_kt, ce, rel, nch, counts = _mid(
        y_hy.reshape(t, HY_W), y_hg.reshape(t, HG_W), x2d, seq, w_out_bf[:HY_W], w_out_bf[HY_W:],
        g1, sc2, sh2, g2, norm2_g[layer], w_router[layer].T, router_bias[layer],
        sw_gate[layer].astype(BF16), sw_up[layer].astype(BF16), sw_down[layer].astype(BF16), tm)

    bm = FFN_BLOCK
    counts = counts.reshape(N_EXPERTS)
    padded = (counts + bm - 1) // bm * bm
    p_ends = jnp.cumsum(padded)
    ps = jnp.concatenate([p_ends - padded, p_ends[-1:]]).astype(I32)
    n_blocks = -(-(t * TOP_K + nt * N_EXPERTS * (SLOT_ALIGN - 1)) // bm) + N_EXPERTS
    block_e = jnp.minimum(jnp.searchsorted(p_ends, jnp.arange(n_blocks, dtype=I32) * bm, side='right'),
                          N_EXPERTS - 1).astype(I32)
    n_active = (p_ends[-1:] // bm).astype(I32)

    ce, rel, nch = ce.reshape(-1), rel.reshape(-1), nch[:, 0, 0]
    x_sorted = _dispatch(h2, lrow_kt, ce, rel, nch, ps, n_blocks * bm, tm)
    y_sorted = _ffn(x_sorted, block_e, n_active, ew_gate[layer], ew_up[layer], ew_down[layer])
    out = _combine(y_sorted, lrow_kt, w_kt, ce, rel, nch, ps, xs, seq, g2, final_g, tm)
    return out.reshape(b, seq, d)
```

```python
import functools
import math

import jax
import jax.numpy as jnp
from jax import lax
from jax.experimental import pallas as pl
from jax.experimental.pallas import tpu as pltpu

F32 = jnp.float32
BF16 = jnp.bfloat16
I32 = jnp.int32
HIGHEST = lax.Precision.HIGHEST

GRID_W = 64
HY_W = 512
HG_W = 512
HY_EMB = 33
HY_BANDS = 16
HY_DECAY_TARGET = 1e-2
HY_FAST_DECAY_PCT = 0.3
HY_SLOW_DECAY_PCT = 1.5
HG_HEAD_DIM = 128
HG_HEADS = 4
HG_SCALE = HG_HEAD_DIM ** -0.5
HG_CHUNK = 64
N_EXPERTS = 256
TOP_K = 8
N_GROUPS = 8
TOPK_GROUPS = 4
GROUP_SIZE = N_EXPERTS // N_GROUPS
ROUTED_SCALE = 2.5
NORM_EPS = 1e-6

VMEM_LIMIT_BYTES = 56 * 1024 * 1024
LANES = 128
FFN_BLOCK = 512
MOE_TILE = 512
SLOT_ALIGN = 8
MOE_ROW_CHUNK = 512
MOE_CHUNKS = 1024
HY_CT = 256


def _params(sem, vmem=VMEM_LIMIT_BYTES):
    return pltpu.CompilerParams(dimension_semantics=sem, vmem_limit_bytes=vmem)


def _silu(x):
    return x * jax.nn.sigmoid(x)


def _dot(a, b):
    return jnp.dot(a, b, preferred_element_type=F32)


def _dot_nt(a, b):
    return lax.dot_general(a, b, (((1,), (1,)), ((), ())), preferred_element_type=F32)


def _dot_tn(a, b):
    return lax.dot_general(a, b, (((0,), (0,)), ((), ())), preferred_element_type=F32)


def _mod_kernel(c_ref, w_ref, b_ref, o_ref):
    s = _silu(c_ref[...])
    o_ref[...] = jnp.dot(s, w_ref[...], preferred_element_type=F32, precision=HIGHEST) + b_ref[...]


def _modulation(cc, w_mod, b_mod):
    rows, d = cc.shape
    n = w_mod.shape[1]
    tn = 1024
    return pl.pallas_call(
        _mod_kernel,
        grid=(n // tn,),
        in_specs=[pl.BlockSpec((rows, d), lambda j: (0, 0)),
                  pl.BlockSpec((d, tn), lambda j: (0, j)),
                  pl.BlockSpec((1, tn), lambda j: (0, j))],
        out_specs=pl.BlockSpec((rows, tn), lambda j: (0, j)),
        out_shape=jax.ShapeDtypeStruct((rows, n), F32),
        compiler_params=_params(("parallel",)),
        name="mod",
    )(cc, w_mod, b_mod.reshape(1, n))


def _rms_mod(x, g, sc, sh):
    y = x * lax.rsqrt(jnp.mean(x * x, axis=-1, keepdims=True) + NORM_EPS) * g
    return y * (1.0 + sc) + sh


def _inproj_kernel(x_ref, g_ref, sc_ref, sh_ref, w_ref, o_ref, h_scr):
    @pl.when(pl.program_id(1) == 0)
    def _():
        h_scr[...] = _rms_mod(x_ref[...], g_ref[...], sc_ref[...], sh_ref[...]).astype(BF16)

    o_ref[...] = _dot(h_scr[...], w_ref[...]).astype(o_ref.dtype)


def _inproj(x2d, seq, g, sc, sh, w_bf, tm, tn):
    t, d = x2d.shape
    n = w_bf.shape[1]
    per = seq // tm
    return pl.pallas_call(
        _inproj_kernel,
        grid=(t // tm, n // tn),
        in_specs=[pl.BlockSpec((tm, d), lambda i, j: (i, 0)),
                  pl.BlockSpec((1, d), lambda i, j: (0, 0)),
                  pl.BlockSpec((None, 1, d), lambda i, j: (i // per, 0, 0)),
                  pl.BlockSpec((None, 1, d), lambda i, j: (i // per, 0, 0)),
                  pl.BlockSpec((d, tn), lambda i, j: (0, j))],
        out_specs=pl.BlockSpec((tm, tn), lambda i, j: (i, j)),
        out_shape=jax.ShapeDtypeStruct((t, n), BF16),
        scratch_shapes=[pltpu.VMEM((tm, d), BF16)],
        compiler_params=_params(("parallel", "arbitrary")),
        name="inproj",
    )(x2d, g.reshape(1, d), sc, sh, w_bf)


def _hg_steps(chains):
    c = HG_CHUNK
    r = lax.broadcasted_iota(I32, (c, c), 0)
    s = lax.broadcasted_iota(I32, (c, c), 1)
    geo = {False: (r >= s, c // 2 - 1, c - 1), True: (r <= s, c // 2, 0)}
    tri = {rev: jnp.where(g[0], 1.0, 0.0).astype(BF16) for rev, g in geo.items()}

    work = []
    for ch in chains:
        lb = ch["lb"]
        sig = jax.nn.sigmoid(ch["fr"])
        lf = jnp.log(lb + (1.0 - lb) * sig)
        k = (1.0 - lb) * (1.0 - sig)
        hi = lf.astype(BF16)
        lo = (lf - hi.astype(F32)).astype(BF16)
        t = tri[ch["rev"]]
        work.append(dict(k=k, bc=_dot(t, hi) + _dot(t, lo)))
    for ch, w in zip(chains, work):
        mask, mid, last = geo[ch["rev"]]
        bc = w["bc"]
        b_mid = bc[mid:mid + 1, :]
        b_last = bc[last:last + 1, :]
        km = w["k"] * jnp.exp(b_mid - bc)
        kd = (km * jnp.exp(b_last - b_mid)).astype(BF16)
        w["ut"] = _dot_tn(ch["v"], kd)
        w["decay"] = jnp.exp(b_last)
        if ch["q"] is not None:
            qm = ch["q"] * jnp.exp(bc - b_mid)
            w["att"] = _dot_nt(qm.astype(BF16), km.astype(BF16))
            qe = (qm * jnp.exp(b_mid)).astype(BF16)
            w["inter"] = _dot_nt(qe, ch["st"].astype(BF16))
    out = []
    for ch, w in zip(chains, work):
        o = None
        if ch["q"] is not None:
            att = jnp.where(geo[ch["rev"]][0], w["att"], 0.0).astype(BF16)
            o = _dot(att, ch["v"]) + w["inter"]
        out.append((o, ch["st"] * w["decay"] + w["ut"]))
    return out


def _hgrn_kernel(q_ref, ff_ref, fb_ref, i_ref, g_ref, cff_ref, cfb_ref, ci_ref,
                 lbf_ref, lbb_ref, ng_ref, o_ref, qs_scr, of_scr, ob_scr, st_scr):
    seq = q_ref.shape[0]
    ctx = cff_ref.shape[0]
    c = HG_CHUNK
    dh = HG_HEAD_DIM
    rb = min(seq, 256)

    for i in range(seq // rb):
        rows = slice(i * rb, (i + 1) * rb)
        qs_scr[rows, :] = (_silu(q_ref[rows, :].astype(F32)) * HG_SCALE).astype(BF16)
    st_scr[...] = jnp.zeros_like(st_scr)

    def chains(nchunks, ffr, fbr, vr, with_q):
        def body(n, carry):
            chains = []
            for h in range(HG_HEADS):
                cols = slice(h * dh, (h + 1) * dh)
                for rev, fref, lbref in ((False, ffr, lbf_ref), (True, fbr, lbb_ref)):
                    ci = (nchunks - 1 - n) if rev else n
                    rows = pl.ds(pl.multiple_of(ci * c, c), c)
                    chains.append(dict(
                        rev=rev, rows=rows, cols=cols, fr=fref[rows, cols].astype(F32), v=vr[rows, cols],
                        lb=lbref[:, cols], st=st_scr[len(chains)],
                        q=qs_scr[rows, cols].astype(F32) if with_q else None))
            for slot, (ch, (o, st)) in enumerate(zip(chains, _hg_steps(chains))):
                st_scr[slot] = st
                if with_q:
                    (ob_scr if ch["rev"] else of_scr)[ch["rows"], ch["cols"]] = o
            return carry

        lax.fori_loop(0, nchunks, body, 0)

    chains(ctx // c, cff_ref, cfb_ref, ci_ref, False)
    chains(seq // c, ff_ref, fb_ref, i_ref, True)

    ng = ng_ref[...]
    for i in range(seq // rb):
        rows = slice(i * rb, (i + 1) * rb)
        gate = _silu(g_ref[rows, :].astype(F32))
        for h in range(HG_HEADS):
            cols = slice(h * dh, (h + 1) * dh)
            o = of_scr[rows, cols] + ob_scr[rows, cols]
            on = o * lax.rsqrt(jnp.mean(o * o, axis=-1, keepdims=True) + NORM_EPS) * ng
            o_ref[rows, cols] = (on * gate[:, cols]).astype(o_ref.dtype)


def _hgrn(px3, pc3, lb_f, lb_b, norm_g):
    b, seq, _ = px3.shape
    ctx = pc3.shape[1]
    dh = HG_HEAD_DIM
    base = 3 * HY_W // HG_W

    def xspec(j):
        return pl.BlockSpec((None, seq, HG_W), lambda bi: (bi, 0, base + j))

    def cspec(j):
        return pl.BlockSpec((None, ctx, HG_W), lambda bi: (bi, 0, j))

    vec = pl.BlockSpec((1, HG_W), lambda bi: (0, 0))
    return pl.pallas_call(
        _hgrn_kernel,
        grid=(b,),
        in_specs=[xspec(0), xspec(1), xspec(2), xspec(3), xspec(4), cspec(0), cspec(1), cspec(2),
                  vec, vec, pl.BlockSpec((1, dh), lambda bi: (0, 0))],
        out_specs=pl.BlockSpec((None, seq, HG_W), lambda bi: (bi, 0, 0)),
        out_shape=jax.ShapeDtypeStruct((b, seq, HG_W), BF16),
        scratch_shapes=[pltpu.VMEM((seq, HG_W), BF16), pltpu.VMEM((seq, HG_W), F32),
                        pltpu.VMEM((seq, HG_W), F32), pltpu.VMEM((2 * HG_HEADS, dh, dh), F32)],
        compiler_params=_params(("parallel",)),
        name="hgrn",
    )(px3, px3, px3, px3, px3, pc3, pc3, pc3, lb_f.reshape(1, HG_W), lb_b.reshape(1, HG_W),
      norm_g.reshape(1, dh))


def _dft_mats(seq):
    f = jnp.arange(seq, dtype=I32)
    m = (f[:, None] * f[None, :]) % (2 * seq)
    ang = m.astype(F32) * (math.pi / seq)
    return jnp.cos(ang).astype(BF16), jnp.sin(ang).astype(BF16)


def _hyfilt_kernel(feat_ref, w1_ref, b1_ref, w2_ref, b2_ref, fr_ref, w3_ref, t_ref, dl_ref,
                   a_ref, s_ref, u1_ref, u2_ref, un_ref):
    seq = feat_ref.shape[0]
    fr = fr_ref[...]
    h = jnp.sin(fr * (jnp.dot(feat_ref[...], w1_ref[...], preferred_element_type=F32, precision=HIGHEST)
                      + b1_ref[...]))
    h = jnp.sin(fr * (jnp.dot(h, w2_ref[...], preferred_element_type=F32, precision=HIGHEST) + b2_ref[...]))
    window = jnp.exp(-t_ref[...] * dl_ref[...])
    row = lax.broadcasted_iota(I32, (seq, 1), 0)
    sgn = jnp.where(row % 2 == 0, 1.0, -1.0)
    cf = jnp.where(row == 0, 1.0, 2.0) * (1.0 / (2 * seq))
    for o in range(2):
        w3 = w3_ref[:, o * 2 * HY_W:(o + 1) * 2 * HY_W]
        ho = jnp.dot(h, w3, preferred_element_type=F32, precision=HIGHEST)
        fwd = ho[:, :HY_W] * window
        bwd = ho[:, HY_W:] * window
        norm = (jnp.sum(jnp.abs(fwd), axis=0, keepdims=True)
                + jnp.sum(jnp.abs(bwd), axis=0, keepdims=True))
        inv = 1.0 / norm
        ksum = (fwd + bwd) * inv
        kdif = (bwd - fwd) * inv
        kr = _dot(a_ref[...], ksum.astype(BF16))
        ki = _dot(s_ref[...], kdif.astype(BF16))
        u1_ref[o] = (kr * cf).astype(u1_ref.dtype)
        u2_ref[o] = (ki * cf).astype(u2_ref.dtype)
        un_ref[o] = jnp.sum(sgn * ksum, axis=0, keepdims=True) * (1.0 / (2 * seq))


def _hyena_filters(seq, fw1, fb1, fw2, fb2, fw3, freq, a_mat, s_mat):
    pos = jnp.arange(seq, dtype=F32)[:, None]
    t = pos / max(seq - 1, 1)
    w = (2.0 * math.pi / seq) * pos
    bands = jnp.linspace(1e-4, HY_BANDS - 1, HY_BANDS, dtype=F32)[None, :]
    feats = jnp.concatenate([t, jnp.cos(bands * w), -jnp.sin(bands * w)], axis=-1)
    feats = jnp.pad(feats, ((0, 0), (0, LANES - HY_EMB)))
    w1 = jnp.pad(fw1, ((0, LANES - HY_EMB), (0, 0)))
    max_decay = math.log(HY_DECAY_TARGET) / HY_FAST_DECAY_PCT
    min_decay = math.log(HY_DECAY_TARGET) / HY_SLOW_DECAY_PCT
    deltas = jnp.abs(jnp.linspace(min_decay, max_decay, HY_W, dtype=F32))[None, :]
    hid = fw2.shape[0]
    return pl.pallas_call(
        _hyfilt_kernel,
        out_shape=(jax.ShapeDtypeStruct((2, seq, HY_W), BF16),
                   jax.ShapeDtypeStruct((2, seq, HY_W), BF16),
                   jax.ShapeDtypeStruct((2, 1, HY_W), F32)),
        compiler_params=pltpu.CompilerParams(vmem_limit_bytes=VMEM_LIMIT_BYTES),
        name="hyfilt",
    )(feats, w1, fb1.reshape(1, hid), fw2, fb2.reshape(1, hid), freq.reshape(1, hid), fw3, t, deltas,
      a_mat, s_mat)


def _hyena_kernel(x1_ref, x2_ref, v_ref, w1_ref, w2_ref, wv_ref, b1_ref, b2_ref, bv_ref,
                  u1_ref, u2_ref, un_ref, d_ref, a_ref, s_ref, o_ref, z_scr, zb_scr, re_scr, im_scr):
    seq = x1_ref.shape[0]
    rb = min(seq, 512)
    nrb = seq // rb
    row = lax.broadcasted_iota(I32, (rb, 1), 0)
    col = row % GRID_W
    first = col == 0
    lastc = col == GRID_W - 1
    sgn = jnp.where(row % 2 == 0, 1.0, -1.0)

    def conv3(p_ref, w_ref, b_ref, rows):
        p = p_ref[rows, :].astype(F32)
        prev = jnp.where(first, 0.0, pltpu.roll(p, 1, axis=0))
        nxt = jnp.where(lastc, 0.0, pltpu.roll(p, rb - 1, axis=0))
        w = w_ref[...]
        return w[0:1, :] * prev + w[1:2, :] * p + w[2:3, :] * nxt + b_ref[...]

    def forward(o):
        nyq = None
        for i in range(nrb):
            rows = slice(i * rb, (i + 1) * rb)
            part = jnp.sum(sgn * z_scr[rows, :], axis=0, keepdims=True)
            nyq = part if nyq is None else nyq + part
        for i in range(nrb):
            rows = slice(i * rb, (i + 1) * rb)
            p = _dot(a_ref[rows, :], zb_scr[...])
            q = _dot(s_ref[rows, :], zb_scr[...])
            u1 = u1_ref[o, rows, :].astype(F32)
            u2 = u2_ref[o, rows, :].astype(F32)
            re_scr[rows, :] = (p * u1 + q * u2).astype(BF16)
            im_scr[rows, :] = (q * u1 - p * u2).astype(BF16)
        return nyq * un_ref[o]

    def inverse(o, nyq, rows):
        y = _dot(a_ref[rows, :], re_scr[...]) + _dot(s_ref[rows, :], im_scr[...]) + sgn * nyq
        return y + z_scr[rows, :] * d_ref[o:o + 1, :]

    for i in range(nrb):
        rows = slice(i * rb, (i + 1) * rb)
        v = conv3(v_ref, wv_ref, bv_ref, rows)
        z_scr[rows, :] = v
        zb_scr[rows, :] = v.astype(BF16)
    nyq = forward(0)
    for i in range(nrb):
        rows = slice(i * rb, (i + 1) * rb)
        z = conv3(x1_ref, w1_ref, b1_ref, rows) * inverse(0, nyq, rows)
        z_scr[rows, :] = z
        zb_scr[rows, :] = z.astype(BF16)
    nyq = forward(1)
    for i in range(nrb):
        rows = slice(i * rb, (i + 1) * rb)
        y = conv3(x2_ref, w2_ref, b2_ref, rows) * inverse(1, nyq, rows)
        o_ref[rows, :] = y.astype(o_ref.dtype)


def _hyena(px3, conv_w, conv_b, u1, u2, un, d_skip, a_mat, s_mat):
    b, seq, _ = px3.shape
    ct = HY_CT
    nc = HY_W // ct

    def xspec(j):
        return pl.BlockSpec((None, seq, ct), lambda c, bi: (bi, 0, j * nc + c))

    def wspec(j, rows):
        return pl.BlockSpec((rows, ct), lambda c, bi: (0, j * nc + c))

    uspec = pl.BlockSpec((2, seq, ct), lambda c, bi: (0, 0, c))
    const = pl.BlockSpec((seq, seq), lambda c, bi: (0, 0), pipeline_mode=pl.Buffered(1))
    cb = conv_b.reshape(1, 3 * HY_W)
    return pl.pallas_call(
        _hyena_kernel,
        grid=(nc, b),
        in_specs=[xspec(0), xspec(1), xspec(2), wspec(0, 3), wspec(1, 3), wspec(2, 3),
                  wspec(0, 1), wspec(1, 1), wspec(2, 1), uspec, uspec,
                  pl.BlockSpec((2, 1, ct), lambda c, bi: (0, 0, c)),
                  pl.BlockSpec((2, ct), lambda c, bi: (0, c)), const, const],
        out_specs=pl.BlockSpec((None, seq, ct), lambda c, bi: (bi, 0, c)),
        out_shape=jax.ShapeDtypeStruct((b, seq, HY_W), BF16),
        scratch_shapes=[pltpu.VMEM((seq, ct), F32), pltpu.VMEM((seq, ct), BF16),
                        pltpu.VMEM((seq, ct), BF16), pltpu.VMEM((seq, ct), BF16)],
        compiler_params=_params(("parallel", "parallel")),
        name="hyena",
    )(px3, px3, px3, conv_w, conv_w, conv_w, cb, cb, cb, u1, u2, un, d_skip, a_mat, s_mat)


def _select_rows(rows):
    n = rows[0].shape[1]
    idx = lax.broadcasted_iota(I32, (len(rows), n), 0)
    out = jnp.zeros((len(rows), n), rows[0].dtype)
    for k, r in enumerate(rows):
        out = jnp.where(idx == k, r, out)
    return out


def _first_argmax(x, iota, size):
    m = jnp.max(x, axis=0, keepdims=True)
    return jnp.min(jnp.where(x == m, iota, size), axis=0, keepdims=True)


def _mid_kernel(yhy_ref, yhg_ref, x_ref, woa_ref, wob_ref, g1_ref, sc_ref, sh_ref, g2_ref, ng_ref,
                wr_ref, rb_ref, swg_ref, swu_ref, swd_ref,
                xs_ref, h_ref, lrow_ref, w_ref, ce_ref, rel_ref, nch_ref, cnt_ref, base):
    tm = x_ref.shape[0]
    ne = N_EXPERTS

    @pl.when(pl.program_id(0) == 0)
    def _():
        base[...] = jnp.zeros_like(base)

    mix = _dot(yhy_ref[...], woa_ref[...]) + _dot(yhg_ref[...], wob_ref[...])
    xm = x_ref[...] + g1_ref[...] * mix
    h = _rms_mod(xm, ng_ref[...], sc_ref[...], sh_ref[...])
    hb = h.astype(BF16)
    h_ref[...] = hb
    act = (_silu(_dot(hb, swg_ref[...])) * _dot(hb, swu_ref[...])).astype(BF16)
    xs_ref[...] = xm + g2_ref[...] * _dot(act, swd_ref[...])

    logits = lax.dot_general(wr_ref[...], h, (((1,), (1,)), ((), ())), preferred_element_type=F32,
                             precision=HIGHEST)
    scores = jax.nn.sigmoid(logits)
    biased = scores + rb_ref[...]
    neg = -jnp.inf
    iota_g = lax.broadcasted_iota(I32, (GROUP_SIZE, tm), 0)
    grp = []
    for g in range(N_GROUPS):
        blk = biased[g * GROUP_SIZE:(g + 1) * GROUP_SIZE, :]
        m1 = jnp.max(blk, axis=0, keepdims=True)
        i1 = jnp.min(jnp.where(blk == m1, iota_g, GROUP_SIZE), axis=0, keepdims=True)
        m2 = jnp.max(jnp.where(iota_g == i1, neg, blk), axis=0, keepdims=True)
        grp.append(m1 + m2)
    gsc = _select_rows(grp)
    iota8 = lax.broadcasted_iota(I32, (N_GROUPS, tm), 0)
    gsel = iota8 < 0
    for _ in range(TOPK_GROUPS):
        hit = iota8 == _first_argmax(gsc, iota8, N_GROUPS)
        gsel = gsel | hit
        gsc = jnp.where(hit, neg, gsc)
    gself = jnp.where(gsel, 1.0, 0.0)
    iota_e = lax.broadcasted_iota(I32, (ne, tm), 0)
    gid = iota_e // GROUP_SIZE
    emask = jnp.zeros((ne, tm), F32)
    for g in range(N_GROUPS):
        emask = jnp.where(gid == g, gself[g:g + 1, :], emask)
    cur = jnp.where(emask > 0.0, biased, neg)

    sel = iota_e < 0
    idxs, wts = [], []
    for _ in range(TOP_K):
        idx = _first_argmax(cur, iota_e, ne)
        hit = iota_e == idx
        wts.append(jnp.sum(jnp.where(hit, scores, 0.0), axis=0, keepdims=True))
        idxs.append(idx)
        sel = sel | hit
        cur = jnp.where(hit, neg, cur)
    wsum = wts[0]
    for wk in wts[1:]:
        wsum = wsum + wk
    scale = ROUTED_SCALE / wsum

    r = lax.broadcasted_iota(I32, (tm, tm), 0)
    s = lax.broadcasted_iota(I32, (tm, tm), 1)
    upper = jnp.where(r < s, 1.0, 0.0).astype(BF16)
    sel_f = jnp.where(sel, 1.0, 0.0)
    prefix = _dot(sel_f.astype(BF16), upper)
    cnt = jnp.sum(sel_f, axis=1, keepdims=True)
    cnt_al = jnp.floor((cnt + (SLOT_ALIGN - 1)) * (1.0 / SLOT_ALIGN)) * SLOT_ALIGN
    re = lax.broadcasted_iota(I32, (ne, ne), 0)
    ce = lax.broadcasted_iota(I32, (ne, ne), 1)
    lower = jnp.where(ce < re, 1.0, 0.0).astype(BF16)
    loc = _dot(lower, jnp.broadcast_to(cnt_al, (ne, LANES)).astype(BF16))[:, 0:1]
    lrow_all = loc + prefix
    lrows = [jnp.sum(jnp.where(iota_e == idx, lrow_all, 0.0), axis=0, keepdims=True) for idx in idxs]

    nc = ce_ref.shape[-1]
    jrow = lax.broadcasted_iota(I32, (1, nc), 1).astype(F32) * SLOT_ALIGN
    owner = jnp.sum(jnp.where(loc + cnt_al <= jrow, 1.0, 0.0), axis=0, keepdims=True)
    owner = jnp.minimum(owner, ne - 1.0).astype(I32)
    iota_ec = lax.broadcasted_iota(I32, (ne, nc), 0)
    rel = jnp.sum(jnp.where(iota_ec == owner, base[...] - loc, 0.0), axis=0, keepdims=True) + jrow
    base[...] = base[...] + cnt_al
    used = jnp.sum(cnt_al, axis=0, keepdims=True)
    spare = jnp.minimum(jrow - used, (MOE_ROW_CHUNK - SLOT_ALIGN) * 1.0)
    spare = spare + (pl.program_id(0) % 2).astype(F32) * MOE_ROW_CHUNK
    unused = jrow >= used
    owner = jnp.where(unused, ne, owner)
    rel = jnp.where(unused, spare, rel)

    lrow_ref[...] = _select_rows(lrows).astype(I32)
    w_ref[...] = _select_rows([wk * scale for wk in wts])
    ce_ref[...] = owner
    rel_ref[...] = rel.astype(I32)
    nch = used * (1.0 / SLOT_ALIGN)
    nch_ref[...] = jnp.broadcast_to(nch, nch_ref.shape).astype(I32)
    cnt_ref[...] = base[...].astype(I32)


def _mid(yhy, yhg, x2d, seq, woa, wob, g1, sc2, sh2, g2, norm_g, wr_t, rbias, swg, swu, swd, tm):
    t, d = x2d.shape
    per = seq // tm
    ff = swg.shape[1]

    def full(shape):
        return pl.BlockSpec(shape, lambda i: (0,) * len(shape))

    mspec = pl.BlockSpec((None, 1, d), lambda i: (i // per, 0, 0))
    kt = pl.BlockSpec((TOP_K, tm), lambda i: (0, i))
    nt = t // tm

    def per_tile(n):
        return pl.BlockSpec((None, 1, n), lambda i: (i, 0, 0))

    return pl.pallas_call(
        _mid_kernel,
        grid=(nt,),
        in_specs=[pl.BlockSpec((tm, HY_W), lambda i: (i, 0)), pl.BlockSpec((tm, HG_W), lambda i: (i, 0)),
                  pl.BlockSpec((tm, d), lambda i: (i, 0)), full((HY_W, d)), full((HG_W, d)),
                  mspec, mspec, mspec, mspec, full((1, d)), full((N_EXPERTS, d)), full((N_EXPERTS, 1)),
                  full((d, ff)), full((d, ff)), full((ff, d))],
        out_specs=[pl.BlockSpec((tm, d), lambda i: (i, 0)), pl.BlockSpec((tm, d), lambda i: (i, 0)),
                   kt, kt, per_tile(MOE_CHUNKS), per_tile(MOE_CHUNKS), per_tile(LANES), full((N_EXPERTS, 1))],
        out_shape=(jax.ShapeDtypeStruct((t, d), F32), jax.ShapeDtypeStruct((t, d), BF16),
                   jax.ShapeDtypeStruct((TOP_K, t), I32), jax.ShapeDtypeStruct((TOP_K, t), F32),
                   jax.ShapeDtypeStruct((nt, 1, MOE_CHUNKS), I32), jax.ShapeDtypeStruct((nt, 1, MOE_CHUNKS), I32),
                   jax.ShapeDtypeStruct((nt, 1, LANES), I32), jax.ShapeDtypeStruct((N_EXPERTS, 1), I32)),
        scratch_shapes=[pltpu.VMEM((N_EXPERTS, 1), F32)],
        compiler_params=_params(("arbitrary",)),
        name="mid",
    )(yhy, yhg, x2d, woa, wob, g1, sc2, sh2, g2, norm_g.reshape(1, d), wr_t, rbias.reshape(N_EXPERTS, 1),
      swg, swu, swd)


def _pack_bf16_pairs(x):
    n = x.shape[1] // 2
    lo = lax.shift_right_logical(pltpu.bitcast(x[:, :n], I32), 16)
    hi = pltpu.bitcast(x[:, n:], I32) & -65536
    return hi | lo


def _unpack_bf16_pairs(p):
    lo = pltpu.bitcast(lax.shift_left(p, 16), F32).astype(BF16)
    hi = pltpu.bitcast(p & -65536, F32).astype(BF16)
    return lo, hi


def _row_chunks(nch):
    return (nch * SLOT_ALIGN + MOE_ROW_CHUNK - 1) // MOE_ROW_CHUNK


def _start_chunk_copies(c, ce_ref, rel_ref, ps_ref, make):
    per = MOE_ROW_CHUNK // SLOT_ALIGN
    for q in range(per):
        j = c * per + q
        glob = pl.multiple_of(ps_ref[ce_ref[j]] + rel_ref[j], SLOT_ALIGN)
        make(pl.multiple_of(j * SLOT_ALIGN, SLOT_ALIGN), glob).start()


def _wait_row_chunks(n, row_chunk_copy):
    def wait(_, carry):
        row_chunk_copy.wait()
        return carry

    lax.fori_loop(0, n, wait, 0)


def _dispatch_kernel(nch_ref, ce_ref, rel_ref, ps_ref, h_ref, lrow_ref, xs_hbm, xloc, zbuf, sems):
    tm = h_ref.shape[0]
    bm = zbuf.shape[0]
    rc = MOE_ROW_CHUNK
    sem = sems.at[0]

    @pl.when(pl.program_id(0) == 0)
    def _():
        zbuf[...] = jnp.zeros_like(zbuf)

        def zcopy(e):
            start = pl.multiple_of(ps_ref[e + 1] - bm, bm)
            return pltpu.make_async_copy(zbuf, xs_hbm.at[pl.ds(start, bm), :], sem)

        def start(e, carry):
            @pl.when(ps_ref[e + 1] > ps_ref[e])
            def _():
                zcopy(e).start()
            return carry

        def wait(e, carry):
            @pl.when(ps_ref[e + 1] > ps_ref[e])
            def _():
                zcopy(e).wait()
            return carry

        lax.fori_loop(0, N_EXPERTS, start, 0)
        lax.fori_loop(0, N_EXPERTS, wait, 0)

        def tcopy(j):
            return pltpu.make_async_copy(zbuf, xs_hbm.at[pl.ds(pl.multiple_of(j * bm, bm), bm), :], sem)

        def tstart(j, carry):
            tcopy(j).start()
            return carry

        def twait(j, carry):
            tcopy(j).wait()
            return carry

        first_tail = ps_ref[N_EXPERTS] // bm
        lax.fori_loop(first_tail, xs_hbm.shape[0] // bm, tstart, 0)
        lax.fori_loop(first_tail, xs_hbm.shape[0] // bm, twait, 0)

    i = pl.program_id(0)
    slot = i % 2
    buf = xloc.at[slot]
    n_rc = _row_chunks(nch_ref[i])
    lrow = lrow_ref[...].astype(F32)
    sub = rc // 2
    rid = lax.broadcasted_iota(I32, (sub, tm), 0).astype(F32).astype(BF16)
    one = jnp.ones((sub, tm), BF16)

    def select(r0):
        rel = (lrow - jnp.asarray(r0, I32).astype(F32)).astype(BF16)
        onehot = jnp.zeros((sub, tm), BF16)
        for k in range(TOP_K):
            onehot = jnp.where(rid == rel[k:k + 1, :], one, onehot)
        buf[pl.ds(r0, sub), :] = _pack_bf16_pairs(_dot(onehot, h_ref[...]))

    def send(c):
        _start_chunk_copies(c, ce_ref, rel_ref, ps_ref,
                            lambda loc, glob: pltpu.make_async_copy(buf.at[pl.ds(loc, SLOT_ALIGN), :],
                                                                    xs_hbm.at[pl.ds(glob, SLOT_ALIGN), :],
                                                                    sems.at[slot]))

    def retire(tile_slot, n):
        _wait_row_chunks(n, pltpu.make_async_copy(xloc.at[tile_slot, pl.ds(0, rc), :],
                                                  xs_hbm.at[pl.ds(0, rc), :], sems.at[tile_slot]))

    select(0)
    select(sub)

    def body(c, carry):
        send(c - 1)
        r0 = pl.multiple_of(c * rc, rc)
        select(r0)
        select(r0 + sub)
        return carry

    lax.fori_loop(1, n_rc, body, 0)
    send(n_rc - 1)

    @pl.when(i > 0)
    def _():
        retire(1 - slot, _row_chunks(nch_ref[i - 1]))

    @pl.when(i == pl.num_programs(0) - 1)
    def _():
        retire(slot, n_rc)


def _local_rows(tm):
    bound = TOP_K * tm + N_EXPERTS * (SLOT_ALIGN - 1)
    return -(-bound // MOE_ROW_CHUNK) * MOE_ROW_CHUNK


def _moe_smem_specs():
    chunk_list = pl.BlockSpec((MOE_CHUNKS,), lambda i, nch: (i,), memory_space=pltpu.SMEM)
    return [chunk_list, chunk_list,
            pl.BlockSpec((N_EXPERTS + 1,), lambda i, nch: (0,), memory_space=pltpu.SMEM)]


def _dispatch(h2d, lrow_kt, ce, rel, nch, ps, n_slots, tm):
    t, d = h2d.shape
    grid_spec = pltpu.PrefetchScalarGridSpec(
        num_scalar_prefetch=1,
        grid=(t // tm,),
        in_specs=_moe_smem_specs() + [pl.BlockSpec((tm, d), lambda i, nch: (i, 0)),
                                      pl.BlockSpec((TOP_K, tm), lambda i, nch: (0, i))],
        out_specs=pl.BlockSpec(memory_space=pl.ANY),
        scratch_shapes=[pltpu.VMEM((2, _local_rows(tm), d // 2), I32), pltpu.VMEM((FFN_BLOCK, d // 2), I32),
                        pltpu.SemaphoreType.DMA((2,))],
    )

    return pl.pallas_call(
        _dispatch_kernel,
        grid_spec=grid_spec,
        out_shape=jax.ShapeDtypeStruct((n_slots, d // 2), I32),
        compiler_params=_params(("arbitrary",)),
        name="dispatch",
    )(nch, ce, rel, ps, h2d, lrow_kt)


def _ffn_kernel(be_ref, na_ref, x_ref, wg_ref, wu_ref, wd_ref, o_ref):
    i = pl.program_id(0)

    @pl.when(i < na_ref[0])
    def _():
        half = x_ref.shape[1]
        lo, hi = _unpack_bf16_pairs(x_ref[...])

        def proj(w_ref):
            return _dot(lo, w_ref[:half, :].astype(BF16)) + _dot(hi, w_ref[half:, :].astype(BF16))

        act = (_silu(proj(wg_ref)) * proj(wu_ref)).astype(BF16)
        y = _dot(act, wd_ref[...].astype(BF16))
        o_ref[...] = _pack_bf16_pairs(y.astype(BF16).astype(F32))

    @pl.when(i >= na_ref[0])
    def _():
        o_ref[...] = jnp.zeros_like(o_ref)


def _ffn(xs, block_e, n_active, ew_gate, ew_up, ew_down):
    n_slots, half = xs.shape
    d = 2 * half
    bm = FFN_BLOCK
    ff = ew_gate.shape[2]
    grid_spec = pltpu.PrefetchScalarGridSpec(
        num_scalar_prefetch=2,
        grid=(n_slots // bm,),
        in_specs=[pl.BlockSpec((bm, half), lambda i, be, na: (jnp.minimum(i, na[0] - 1), 0)),
                  pl.BlockSpec((None, d, ff), lambda i, be, na: (be[i], 0, 0)),
                  pl.BlockSpec((None, d, ff), lambda i, be, na: (be[i], 0, 0)),
                  pl.BlockSpec((None, ff, d), lambda i, be, na: (be[i], 0, 0))],
        out_specs=pl.BlockSpec((bm, half), lambda i, be, na: (i, 0)),
    )
    return pl.pallas_call(
        _ffn_kernel,
        grid_spec=grid_spec,
        out_shape=jax.ShapeDtypeStruct((n_slots, half), I32),
        compiler_params=_params(("arbitrary",)),
        name="ffn",
    )(block_e, n_active, xs, ew_gate, ew_up, ew_down)


def _combine_kernel(nch_ref, ce_ref, rel_ref, cen_ref, reln_ref, ps_ref, lrow_ref, w_ref, xs_ref, g2_ref, fg_ref,
                    y_hbm, o_ref, yloc, acc, sems):
    tm = xs_ref.shape[0]
    half = yloc.shape[2]
    rc = MOE_ROW_CHUNK
    i = pl.program_id(0)
    last = pl.num_programs(0) - 1
    slot = i % 2
    nxt = jnp.minimum(i + 1, last)
    n_rc = _row_chunks(nch_ref[i])
    n_next = _row_chunks(nch_ref[nxt])

    def fetch(c, ce, rel, s):
        _start_chunk_copies(c, ce, rel, ps_ref,
                            lambda loc, glob: pltpu.make_async_copy(y_hbm.at[pl.ds(glob, SLOT_ALIGN), :],
                                                                    yloc.at[s, pl.ds(loc, SLOT_ALIGN), :],
                                                                    sems.at[s]))

    def arrived(s, n):
        _wait_row_chunks(n, pltpu.make_async_copy(y_hbm.at[pl.ds(0, rc), :], yloc.at[s, pl.ds(0, rc), :],
                                                  sems.at[s]))

    def fetch_own(c, carry):
        fetch(c, ce_ref, rel_ref, slot)
        return carry

    def fetch_next(c, carry):
        fetch(c, cen_ref, reln_ref, 1 - slot)
        return carry

    @pl.when(i == 0)
    def _():
        yloc[...] = jnp.zeros_like(yloc)
        lax.fori_loop(0, n_rc, fetch_own, 0)

    n_prev = _row_chunks(nch_ref[jnp.maximum(i - 1, 0)])
    arrived(slot, jnp.where(i == 0, n_rc, jnp.maximum(n_prev, n_rc)))

    lrow = lrow_ref[...].astype(F32)
    wrow = w_ref[...].astype(BF16)
    acc[...] = jnp.zeros_like(acc)
    sub = rc // 2
    rid = lax.broadcasted_iota(I32, (sub, tm), 0).astype(F32).astype(BF16)

    def weights(r0):
        rel = (lrow - r0.astype(F32)).astype(BF16)
        wt = jnp.zeros((sub, tm), BF16)
        for k in range(TOP_K):
            wt = jnp.where(rid == rel[k:k + 1, :], jnp.broadcast_to(wrow[k:k + 1, :], (sub, tm)), wt)
        return wt

    def gather(c, carry):
        fetch(c, cen_ref, reln_ref, 1 - slot)
        r0 = pl.multiple_of(c * rc, rc)
        wt_a = weights(r0)
        wt_b = weights(r0 + sub)
        lo_a, hi_a = _unpack_bf16_pairs(yloc[slot, pl.ds(r0, sub), :])
        lo_b, hi_b = _unpack_bf16_pairs(yloc[slot, pl.ds(r0 + sub, sub), :])
        acc[:, :half] += _dot_tn(wt_a, lo_a) + _dot_tn(wt_b, lo_b)
        acc[:, half:] += _dot_tn(wt_a, hi_a) + _dot_tn(wt_b, hi_b)
        return carry

    lax.fori_loop(0, n_rc, gather, 0)
    lax.fori_loop(n_rc, n_next, fetch_next, 0)

    @pl.when(i == last)
    def _():
        arrived(1 - slot, jnp.maximum(n_rc, n_next))

    x = xs_ref[...] + g2_ref[...] * acc[...]
    o_ref[...] = x * lax.rsqrt(jnp.mean(x * x, axis=-1, keepdims=True) + NORM_EPS) * fg_ref[...]


def _combine(y_sorted, lrow_kt, w_kt, ce, rel, nch, ps, xs2d, seq, g2, final_g, tm):
    t, d = xs2d.shape
    per = seq // tm
    nt = t // tm
    own, _, ranges = _moe_smem_specs()
    nxt = pl.BlockSpec((MOE_CHUNKS,), lambda i, nch: (jnp.minimum(i + 1, nt - 1),), memory_space=pltpu.SMEM)
    grid_spec = pltpu.PrefetchScalarGridSpec(
        num_scalar_prefetch=1,
        grid=(nt,),
        in_specs=[own, own, nxt, nxt, ranges,
                  pl.BlockSpec((TOP_K, tm), lambda i, nch: (0, i)),
                  pl.BlockSpec((TOP_K, tm), lambda i, nch: (0, i)),
                  pl.BlockSpec((tm, d), lambda i, nch: (i, 0)),
                  pl.BlockSpec((None, 1, d), lambda i, nch: (i // per, 0, 0)),
                  pl.BlockSpec((1, d), lambda i, nch: (0, 0)),
                  pl.BlockSpec(memory_space=pl.ANY)],
        out_specs=pl.BlockSpec((tm, d), lambda i, nch: (i, 0)),
        scratch_shapes=[pltpu.VMEM((2, _local_rows(tm), d // 2), I32), pltpu.VMEM((tm, d), F32),
                        pltpu.SemaphoreType.DMA((2,))],
    )
    return pl.pallas_call(
        _combine_kernel,
        grid_spec=grid_spec,
        out_shape=jax.ShapeDtypeStruct((t, d), F32),
        compiler_params=_params(("arbitrary",)),
        name="combine",
    )(nch, ce, rel, ce, rel, ps, lrow_kt, w_kt, xs2d, g2, final_g.reshape(1, d), y_sorted)


def kernel(x, c, ctx, c_ctx, w_mod, b_mod, norm1_g, norm2_g, w_in, w_out, hy_conv_w, hy_conv_b,
           hy_fw1, hy_fb1, hy_fw2, hy_fb2, hy_fw3, hy_freq, hy_d, hg_lb_logits, hg_norm_g,
           w_router, router_bias, ew_gate, ew_up, ew_down, sw_gate, sw_up, sw_down, final_g):
    b, seq, d = x.shape
    ctx_len = ctx.shape[1]
    t = b * seq
    layer = 0

    lower = jnp.cumsum(jax.nn.softmax(hg_lb_logits.astype(F32), axis=1), axis=1)
    lb_f, lb_b = lower[0, layer], lower[1, layer]

    rows = -(-(b + 1) // 8) * 8
    cc = jnp.concatenate([c, c_ctx[None, :], jnp.zeros((rows - b - 1, d), F32)], axis=0)
    mod = _modulation(cc, w_mod[layer], b_mod[layer])
    sh1, sc1, g1, sh2, sc2, g2 = (m.reshape(b, 1, d) for m in jnp.split(mod[:b], 6, axis=-1))
    csh1, csc1 = (jnp.broadcast_to(m.reshape(1, 1, d), (b, 1, d))
                  for m in jnp.split(mod[b:b + 1], 6, axis=-1)[:2])

    w_in_bf = w_in[layer].astype(BF16)
    x2d = x.reshape(t, d)
    px = _inproj(x2d, seq, norm1_g[layer], sc1, sh1, w_in_bf, min(seq, 1024), 1024)
    lo = 3 * HY_W + HG_W
    pc = _inproj(ctx.reshape(b * ctx_len, d), ctx_len, norm1_g[layer], csc1, csh1,
                 w_in_bf[:, lo:lo + 3 * HG_W], ctx_len, HG_W)
    px3 = px.reshape(b, seq, -1)
    pc3 = pc.reshape(b, ctx_len, -1)

    y_hg = _hgrn(px3, pc3, lb_f, lb_b, hg_norm_g[layer])

    a_mat, s_mat = _dft_mats(seq)
    u1, u2, un = _hyena_filters(seq, hy_fw1[layer], hy_fb1[layer], hy_fw2[layer], hy_fb2[layer],
                                hy_fw3[layer], hy_freq[layer], a_mat, s_mat)
    y_hy = _hyena(px3, hy_conv_w[layer], hy_conv_b[layer], u1, u2, un, hy_d[layer], a_mat, s_mat)

    w_out_bf = w_out[layer].astype(BF16)
    tm = min(seq, MOE_TILE)
    nt = t // tm
    xs, h2, lrow_kt, w_kt, ce, rel, nch, counts = _mid(
        y_hy.reshape(t, HY_W), y_hg.reshape(t, HG_W), x2d, seq, w_out_bf[:HY_W], w_out_bf[HY_W:],
        g1, sc2, sh2, g2, norm2_g[layer], w_router[layer].T, router_bias[layer],
        sw_gate[layer].astype(BF16), sw_up[layer].astype(BF16), sw_down[layer].astype(BF16), tm)

    bm = FFN_BLOCK
    counts = counts.reshape(N_EXPERTS)
    padded = (counts + bm - 1) // bm * bm
    p_ends = jnp.cumsum(padded)
    ps = jnp.concatenate([p_ends - padded, p_ends[-1:]]).astype(I32)
    spare_blocks = -(-2 * MOE_ROW_CHUNK // bm)
    n_blocks = -(-(t * TOP_K + nt * N_EXPERTS * (SLOT_ALIGN - 1)) // bm) + N_EXPERTS + spare_blocks
    block_e = jnp.minimum(jnp.searchsorted(p_ends, jnp.arange(n_blocks, dtype=I32) * bm, side='right'),
                          N_EXPERTS - 1).astype(I32)
    n_active = (p_ends[-1:] // bm).astype(I32)

    ce, rel, nch = ce.reshape(-1), rel.reshape(-1), nch[:, 0, 0]
    x_sorted = _dispatch(h2, lrow_kt, ce, rel, nch, ps, n_blocks * bm, tm)
    y_sorted = _ffn(x_sorted, block_e, n_active, ew_gate[layer], ew_up[layer], ew_down[layer])
    out = _combine(y_sorted, lrow_kt, w_kt, ce, rel, nch, ps, xs, seq, g2, final_g, tm)
    return out.reshape(b, seq, d)
```

```python
import functools
import math

import jax
import jax.numpy as jnp
from jax import lax
from jax.experimental import pallas as pl
from jax.experimental.pallas import tpu as pltpu

F32 = jnp.float32
BF16 = jnp.bfloat16
I32 = jnp.int32
HIGHEST = lax.Precision.HIGHEST

GRID_W = 64
HY_W = 512
HG_W = 512
HY_EMB = 33
HY_BANDS = 16
HY_DECAY_TARGET = 1e-2
HY_FAST_DECAY_PCT = 0.3
HY_SLOW_DECAY_PCT = 1.5
HG_HEAD_DIM = 128
HG_HEADS = 4
HG_SCALE = HG_HEAD_DIM ** -0.5
HG_CHUNK = 64
N_EXPERTS = 256
TOP_K = 8
N_GROUPS = 8
TOPK_GROUPS = 4
GROUP_SIZE = N_EXPERTS // N_GROUPS
ROUTED_SCALE = 2.5
NORM_EPS = 1e-6

VMEM_LIMIT_BYTES = 56 * 1024 * 1024
LANES = 128
FFN_BLOCK = 512
MOE_TILE = 512
SLOT_ALIGN = 8
MOE_ROW_CHUNK = 512
MOE_CHUNKS = 1024
HY_CT = 256


def _params(sem, vmem=VMEM_LIMIT_BYTES):
    return pltpu.CompilerParams(dimension_semantics=sem, vmem_limit_bytes=vmem)


def _silu(x):
    return x * jax.nn.sigmoid(x)


def _dot(a, b):
    return jnp.dot(a, b, preferred_element_type=F32)


def _dot_nt(a, b):
    return lax.dot_general(a, b, (((1,), (1,)), ((), ())), preferred_element_type=F32)


def _dot_tn(a, b):
    return lax.dot_general(a, b, (((0,), (0,)), ((), ())), preferred_element_type=F32)


def _mod_kernel(c_ref, w_ref, b_ref, o_ref):
    s = _silu(c_ref[...])
    o_ref[...] = jnp.dot(s, w_ref[...], preferred_element_type=F32, precision=HIGHEST) + b_ref[...]


def _modulation(cc, w_mod, b_mod):
    rows, d = cc.shape
    n = w_mod.shape[1]
    tn = 1024
    return pl.pallas_call(
        _mod_kernel,
        grid=(n // tn,),
        in_specs=[pl.BlockSpec((rows, d), lambda j: (0, 0)),
                  pl.BlockSpec((d, tn), lambda j: (0, j)),
                  pl.BlockSpec((1, tn), lambda j: (0, j))],
        out_specs=pl.BlockSpec((rows, tn), lambda j: (0, j)),
        out_shape=jax.ShapeDtypeStruct((rows, n), F32),
        compiler_params=_params(("parallel",)),
        name="mod",
    )(cc, w_mod, b_mod.reshape(1, n))


def _rms_mod(x, g, sc, sh):
    y = x * lax.rsqrt(jnp.mean(x * x, axis=-1, keepdims=True) + NORM_EPS) * g
    return y * (1.0 + sc) + sh


def _inproj_kernel(x_ref, g_ref, sc_ref, sh_ref, w_ref, o_ref, h_scr):
    @pl.when(pl.program_id(1) == 0)
    def _():
        h_scr[...] = _rms_mod(x_ref[...], g_ref[...], sc_ref[...], sh_ref[...]).astype(BF16)

    o_ref[...] = _dot(h_scr[...], w_ref[...]).astype(o_ref.dtype)


def _inproj(x2d, seq, g, sc, sh, w_bf, tm, tn):
    t, d = x2d.shape
    n = w_bf.shape[1]
    per = seq // tm
    w_mode = dict(pipeline_mode=pl.Buffered(1)) if tn == n else {}
    return pl.pallas_call(
        _inproj_kernel,
        grid=(t // tm, n // tn),
        in_specs=[pl.BlockSpec((tm, d), lambda i, j: (i, 0)),
                  pl.BlockSpec((1, d), lambda i, j: (0, 0)),
                  pl.BlockSpec((None, 1, d), lambda i, j: (i // per, 0, 0)),
                  pl.BlockSpec((None, 1, d), lambda i, j: (i // per, 0, 0)),
                  pl.BlockSpec((d, tn), lambda i, j: (0, j), **w_mode)],
        out_specs=pl.BlockSpec((tm, tn), lambda i, j: (i, j)),
        out_shape=jax.ShapeDtypeStruct((t, n), BF16),
        scratch_shapes=[pltpu.VMEM((tm, d), BF16)],
        compiler_params=_params(("parallel", "arbitrary")),
        name="inproj",
    )(x2d, g.reshape(1, d), sc, sh, w_bf)


def _hg_steps(chains):
    c = HG_CHUNK
    r = lax.broadcasted_iota(I32, (c, c), 0)
    s = lax.broadcasted_iota(I32, (c, c), 1)
    geo = {False: (r >= s, c // 2 - 1, c - 1), True: (r <= s, c // 2, 0)}
    tri = {rev: jnp.where(g[0], 1.0, 0.0).astype(BF16) for rev, g in geo.items()}

    work = []
    for ch in chains:
        lb = ch["lb"]
        sig = jax.nn.sigmoid(ch["fr"])
        lf = jnp.log(lb + (1.0 - lb) * sig)
        k = (1.0 - lb) * (1.0 - sig)
        hi = lf.astype(BF16)
        lo = (lf - hi.astype(F32)).astype(BF16)
        t = tri[ch["rev"]]
        work.append(dict(k=k, bc=_dot(t, hi) + _dot(t, lo)))
    for ch, w in zip(chains, work):
        mask, mid, last = geo[ch["rev"]]
        bc = w["bc"]
        b_mid = bc[mid:mid + 1, :]
        b_last = bc[last:last + 1, :]
        km = w["k"] * jnp.exp(b_mid - bc)
        kd = (km * jnp.exp(b_last - b_mid)).astype(BF16)
        w["ut"] = _dot_tn(ch["v"], kd)
        w["decay"] = jnp.exp(b_last)
        if ch["q"] is not None:
            qm = ch["q"] * jnp.exp(bc - b_mid)
            w["att"] = _dot_nt(qm.astype(BF16), km.astype(BF16))
            qe = (qm * jnp.exp(b_mid)).astype(BF16)
            w["inter"] = _dot_nt(qe, ch["st"].astype(BF16))
    out = []
    for ch, w in zip(chains, work):
        o = None
        if ch["q"] is not None:
            att = jnp.where(geo[ch["rev"]][0], w["att"], 0.0).astype(BF16)
            o = _dot(att, ch["v"]) + w["inter"]
        out.append((o, ch["st"] * w["decay"] + w["ut"]))
    return out


def _hgrn_kernel(q_ref, ff_ref, fb_ref, i_ref, g_ref, cff_ref, cfb_ref, ci_ref,
                 lbf_ref, lbb_ref, ng_ref, o_ref, qs_scr, of_scr, ob_scr, st_scr):
    seq = q_ref.shape[0]
    ctx = cff_ref.shape[0]
    c = HG_CHUNK
    dh = HG_HEAD_DIM
    rb = min(seq, 256)

    for i in range(seq // rb):
        rows = slice(i * rb, (i + 1) * rb)
        qs_scr[rows, :] = (_silu(q_ref[rows, :].astype(F32)) * HG_SCALE).astype(BF16)
    st_scr[...] = jnp.zeros_like(st_scr)

    def chains(nchunks, ffr, fbr, vr, with_q):
        def body(n, carry):
            chains = []
            for h in range(HG_HEADS):
                cols = slice(h * dh, (h + 1) * dh)
                for rev, fref, lbref in ((False, ffr, lbf_ref), (True, fbr, lbb_ref)):
                    ci = (nchunks - 1 - n) if rev else n
                    rows = pl.ds(pl.multiple_of(ci * c, c), c)
                    chains.append(dict(
                        rev=rev, rows=rows, cols=cols, fr=fref[rows, cols].astype(F32), v=vr[rows, cols],
                        lb=lbref[:, cols], st=st_scr[len(chains)],
                        q=qs_scr[rows, cols].astype(F32) if with_q else None))
            for slot, (ch, (o, st)) in enumerate(zip(chains, _hg_steps(chains))):
                st_scr[slot] = st
                if with_q:
                    (ob_scr if ch["rev"] else of_scr)[ch["rows"], ch["cols"]] = o
            return carry

        lax.fori_loop(0, nchunks, body, 0)

    chains(ctx // c, cff_ref, cfb_ref, ci_ref, False)
    chains(seq // c, ff_ref, fb_ref, i_ref, True)

    ng = ng_ref[...]
    for i in range(seq // rb):
        rows = slice(i * rb, (i + 1) * rb)
        gate = _silu(g_ref[rows, :].astype(F32))
        for h in range(HG_HEADS):
            cols = slice(h * dh, (h + 1) * dh)
            o = of_scr[rows, cols] + ob_scr[rows, cols]
            on = o * lax.rsqrt(jnp.mean(o * o, axis=-1, keepdims=True) + NORM_EPS) * ng
            o_ref[rows, cols] = (on * gate[:, cols]).astype(o_ref.dtype)


def _hgrn(px3, pc3, lb_f, lb_b, norm_g):
    b, seq, _ = px3.shape
    ctx = pc3.shape[1]
    dh = HG_HEAD_DIM
    base = 3 * HY_W // HG_W

    def xspec(j):
        return pl.BlockSpec((None, seq, HG_W), lambda bi: (bi, 0, base + j))

    def cspec(j):
        return pl.BlockSpec((None, ctx, HG_W), lambda bi: (bi, 0, j))

    vec = pl.BlockSpec((1, HG_W), lambda bi: (0, 0))
    return pl.pallas_call(
        _hgrn_kernel,
        grid=(b,),
        in_specs=[xspec(0), xspec(1), xspec(2), xspec(3), xspec(4), cspec(0), cspec(1), cspec(2),
                  vec, vec, pl.BlockSpec((1, dh), lambda bi: (0, 0))],
        out_specs=pl.BlockSpec((None, seq, HG_W), lambda bi: (bi, 0, 0)),
        out_shape=jax.ShapeDtypeStruct((b, seq, HG_W), BF16),
        scratch_shapes=[pltpu.VMEM((seq, HG_W), BF16), pltpu.VMEM((seq, HG_W), F32),
                        pltpu.VMEM((seq, HG_W), F32), pltpu.VMEM((2 * HG_HEADS, dh, dh), F32)],
        compiler_params=_params(("parallel",)),
        name="hgrn",
    )(px3, px3, px3, px3, px3, pc3, pc3, pc3, lb_f.reshape(1, HG_W), lb_b.reshape(1, HG_W),
      norm_g.reshape(1, dh))


def _dft_mats(seq):
    f = jnp.arange(seq, dtype=I32)
    m = (f[:, None] * f[None, :]) % (2 * seq)
    ang = m.astype(F32) * (math.pi / seq)
    return jnp.cos(ang).astype(BF16), jnp.sin(ang).astype(BF16)


def _hyfilt_kernel(feat_ref, w1_ref, b1_ref, w2_ref, b2_ref, fr_ref, w3_ref, t_ref, dl_ref,
                   a_ref, s_ref, u1_ref, u2_ref, un_ref):
    seq = feat_ref.shape[0]
    fr = fr_ref[...]
    h = jnp.sin(fr * (jnp.dot(feat_ref[...], w1_ref[...], preferred_element_type=F32, precision=HIGHEST)
                      + b1_ref[...]))
    h = jnp.sin(fr * (jnp.dot(h, w2_ref[...], preferred_element_type=F32, precision=HIGHEST) + b2_ref[...]))
    window = jnp.exp(-t_ref[...] * dl_ref[...])
    row = lax.broadcasted_iota(I32, (seq, 1), 0)
    sgn = jnp.where(row % 2 == 0, 1.0, -1.0)
    cf = jnp.where(row == 0, 1.0, 2.0) * (1.0 / (2 * seq))
    for o in range(2):
        w3 = w3_ref[:, o * 2 * HY_W:(o + 1) * 2 * HY_W]
        ho = jnp.dot(h, w3, preferred_element_type=F32, precision=HIGHEST)
        fwd = ho[:, :HY_W] * window
        bwd = ho[:, HY_W:] * window
        norm = (jnp.sum(jnp.abs(fwd), axis=0, keepdims=True)
                + jnp.sum(jnp.abs(bwd), axis=0, keepdims=True))
        inv = 1.0 / norm
        ksum = (fwd + bwd) * inv
        kdif = (bwd - fwd) * inv
        kr = _dot(a_ref[...], ksum.astype(BF16))
        ki = _dot(s_ref[...], kdif.astype(BF16))
        u1_ref[o] = (kr * cf).astype(u1_ref.dtype)
        u2_ref[o] = (ki * cf).astype(u2_ref.dtype)
        un_ref[o] = jnp.sum(sgn * ksum, axis=0, keepdims=True) * (1.0 / (2 * seq))


def _hyena_filters(seq, fw1, fb1, fw2, fb2, fw3, freq, a_mat, s_mat):
    pos = jnp.arange(seq, dtype=F32)[:, None]
    t = pos / max(seq - 1, 1)
    w = (2.0 * math.pi / seq) * pos
    bands = jnp.linspace(1e-4, HY_BANDS - 1, HY_BANDS, dtype=F32)[None, :]
    feats = jnp.concatenate([t, jnp.cos(bands * w), -jnp.sin(bands * w)], axis=-1)
    feats = jnp.pad(feats, ((0, 0), (0, LANES - HY_EMB)))
    w1 = jnp.pad(fw1, ((0, LANES - HY_EMB), (0, 0)))
    max_decay = math.log(HY_DECAY_TARGET) / HY_FAST_DECAY_PCT
    min_decay = math.log(HY_DECAY_TARGET) / HY_SLOW_DECAY_PCT
    deltas = jnp.abs(jnp.linspace(min_decay, max_decay, HY_W, dtype=F32))[None, :]
    hid = fw2.shape[0]
    return pl.pallas_call(
        _hyfilt_kernel,
        out_shape=(jax.ShapeDtypeStruct((2, seq, HY_W), BF16),
                   jax.ShapeDtypeStruct((2, seq, HY_W), BF16),
                   jax.ShapeDtypeStruct((2, 1, HY_W), F32)),
        compiler_params=pltpu.CompilerParams(vmem_limit_bytes=VMEM_LIMIT_BYTES),
        name="hyfilt",
    )(feats, w1, fb1.reshape(1, hid), fw2, fb2.reshape(1, hid), freq.reshape(1, hid), fw3, t, deltas,
      a_mat, s_mat)


def _hyena_kernel(x1_ref, x2_ref, v_ref, w1_ref, w2_ref, wv_ref, b1_ref, b2_ref, bv_ref,
                  u1_ref, u2_ref, un_ref, d_ref, a_ref, s_ref, o_ref, z_scr, zb_scr, re_scr, im_scr):
    seq = x1_ref.shape[0]
    rb = min(seq, 512)
    nrb = seq // rb
    row = lax.broadcasted_iota(I32, (rb, 1), 0)
    col = row % GRID_W
    first = col == 0
    lastc = col == GRID_W - 1
    sgn = jnp.where(row % 2 == 0, 1.0, -1.0)

    def conv3(p_ref, w_ref, b_ref, rows):
        p = p_ref[rows, :].astype(F32)
        prev = jnp.where(first, 0.0, pltpu.roll(p, 1, axis=0))
        nxt = jnp.where(lastc, 0.0, pltpu.roll(p, rb - 1, axis=0))
        w = w_ref[...]
        return w[0:1, :] * prev + w[1:2, :] * p + w[2:3, :] * nxt + b_ref[...]

    def forward(o):
        nyq = None
        for i in range(nrb):
            rows = slice(i * rb, (i + 1) * rb)
            part = jnp.sum(sgn * z_scr[rows, :], axis=0, keepdims=True)
            nyq = part if nyq is None else nyq + part
        for i in range(nrb):
            rows = slice(i * rb, (i + 1) * rb)
            p = _dot(a_ref[rows, :], zb_scr[...])
            q = _dot(s_ref[rows, :], zb_scr[...])
            u1 = u1_ref[o, rows, :].astype(F32)
            u2 = u2_ref[o, rows, :].astype(F32)
            re_scr[rows, :] = (p * u1 + q * u2).astype(BF16)
            im_scr[rows, :] = (q * u1 - p * u2).astype(BF16)
        return nyq * un_ref[o]

    def inverse(o, nyq, rows):
        y = _dot(a_ref[rows, :], re_scr[...]) + _dot(s_ref[rows, :], im_scr[...]) + sgn * nyq
        return y + z_scr[rows, :] * d_ref[o:o + 1, :]

    for i in range(nrb):
        rows = slice(i * rb, (i + 1) * rb)
        v = conv3(v_ref, wv_ref, bv_ref, rows)
        z_scr[rows, :] = v
        zb_scr[rows, :] = v.astype(BF16)
    nyq = forward(0)
    for i in range(nrb):
        rows = slice(i * rb, (i + 1) * rb)
        z = conv3(x1_ref, w1_ref, b1_ref, rows) * inverse(0, nyq, rows)
        z_scr[rows, :] = z
        zb_scr[rows, :] = z.astype(BF16)
    nyq = forward(1)
    for i in range(nrb):
        rows = slice(i * rb, (i + 1) * rb)
        y = conv3(x2_ref, w2_ref, b2_ref, rows) * inverse(1, nyq, rows)
        o_ref[rows, :] = y.astype(o_ref.dtype)


def _hyena(px3, conv_w, conv_b, u1, u2, un, d_skip, a_mat, s_mat):
    b, seq, _ = px3.shape
    ct = HY_CT
    nc = HY_W // ct

    def xspec(j):
        return pl.BlockSpec((None, seq, ct), lambda c, bi: (bi, 0, j * nc + c))

    def wspec(j, rows):
        return pl.BlockSpec((rows, ct), lambda c, bi: (0, j * nc + c))

    uspec = pl.BlockSpec((2, seq, ct), lambda c, bi: (0, 0, c))
    const = pl.BlockSpec((seq, seq), lambda c, bi: (0, 0), pipeline_mode=pl.Buffered(1))
    cb = conv_b.reshape(1, 3 * HY_W)
    return pl.pallas_call(
        _hyena_kernel,
        grid=(nc, b),
        in_specs=[xspec(0), xspec(1), xspec(2), wspec(0, 3), wspec(1, 3), wspec(2, 3),
                  wspec(0, 1), wspec(1, 1), wspec(2, 1), uspec, uspec,
                  pl.BlockSpec((2, 1, ct), lambda c, bi: (0, 0, c)),
                  pl.BlockSpec((2, ct), lambda c, bi: (0, c)), const, const],
        out_specs=pl.BlockSpec((None, seq, ct), lambda c, bi: (bi, 0, c)),
        out_shape=jax.ShapeDtypeStruct((b, seq, HY_W), BF16),
        scratch_shapes=[pltpu.VMEM((seq, ct), F32), pltpu.VMEM((seq, ct), BF16),
                        pltpu.VMEM((seq, ct), BF16), pltpu.VMEM((seq, ct), BF16)],
        compiler_params=_params(("parallel", "parallel")),
        name="hyena",
    )(px3, px3, px3, conv_w, conv_w, conv_w, cb, cb, cb, u1, u2, un, d_skip, a_mat, s_mat)


def _select_rows(rows):
    n = rows[0].shape[1]
    idx = lax.broadcasted_iota(I32, (len(rows), n), 0)
    out = jnp.zeros((len(rows), n), rows[0].dtype)
    for k, r in enumerate(rows):
        out = jnp.where(idx == k, r, out)
    return out


def _first_argmax(x, iota, size):
    m = jnp.max(x, axis=0, keepdims=True)
    return jnp.min(jnp.where(x == m, iota, size), axis=0, keepdims=True)


def _mid_kernel(yhy_ref, yhg_ref, x_ref, woa_ref, wob_ref, g1_ref, sc_ref, sh_ref, g2_ref, ng_ref,
                wr_ref, rb_ref, swg_ref, swu_ref, swd_ref,
                xs_ref, h_ref, lrow_ref, w_ref, ce_ref, rel_ref, nch_ref, cnt_ref, base):
    tm = x_ref.shape[0]
    ne = N_EXPERTS

    @pl.when(pl.program_id(0) == 0)
    def _():
        base[...] = jnp.zeros_like(base)

    mix = _dot(yhy_ref[...], woa_ref[...]) + _dot(yhg_ref[...], wob_ref[...])
    xm = x_ref[...] + g1_ref[...] * mix
    h = _rms_mod(xm, ng_ref[...], sc_ref[...], sh_ref[...])
    hb = h.astype(BF16)
    h_ref[...] = hb
    act = (_silu(_dot(hb, swg_ref[...])) * _dot(hb, swu_ref[...])).astype(BF16)
    xs_ref[...] = xm + g2_ref[...] * _dot(act, swd_ref[...])

    logits = lax.dot_general(wr_ref[...], h, (((1,), (1,)), ((), ())), preferred_element_type=F32,
                             precision=HIGHEST)
    scores = jax.nn.sigmoid(logits)
    biased = scores + rb_ref[...]
    neg = -jnp.inf
    iota_g = lax.broadcasted_iota(I32, (GROUP_SIZE, tm), 0)
    grp = []
    for g in range(N_GROUPS):
        blk = biased[g * GROUP_SIZE:(g + 1) * GROUP_SIZE, :]
        m1 = jnp.max(blk, axis=0, keepdims=True)
        i1 = jnp.min(jnp.where(blk == m1, iota_g, GROUP_SIZE), axis=0, keepdims=True)
        m2 = jnp.max(jnp.where(iota_g == i1, neg, blk), axis=0, keepdims=True)
        grp.append(m1 + m2)
    gsc = _select_rows(grp)
    iota8 = lax.broadcasted_iota(I32, (N_GROUPS, tm), 0)
    gsel = iota8 < 0
    for _ in range(TOPK_GROUPS):
        hit = iota8 == _first_argmax(gsc, iota8, N_GROUPS)
        gsel = gsel | hit
        gsc = jnp.where(hit, neg, gsc)
    gself = jnp.where(gsel, 1.0, 0.0)
    iota_e = lax.broadcasted_iota(I32, (ne, tm), 0)
    gid = iota_e // GROUP_SIZE
    emask = jnp.zeros((ne, tm), F32)
    for g in range(N_GROUPS):
        emask = jnp.where(gid == g, gself[g:g + 1, :], emask)
    cur = jnp.where(emask > 0.0, biased, neg)

    sel = iota_e < 0
    idxs, wts = [], []
    for _ in range(TOP_K):
        idx = _first_argmax(cur, iota_e, ne)
        hit = iota_e == idx
        wts.append(jnp.sum(jnp.where(hit, scores, 0.0), axis=0, keepdims=True))
        idxs.append(idx)
        sel = sel | hit
        cur = jnp.where(hit, neg, cur)
    wsum = wts[0]
    for wk in wts[1:]:
        wsum = wsum + wk
    scale = ROUTED_SCALE / wsum

    r = lax.broadcasted_iota(I32, (tm, tm), 0)
    s = lax.broadcasted_iota(I32, (tm, tm), 1)
    upper = jnp.where(r < s, 1.0, 0.0).astype(BF16)
    sel_f = jnp.where(sel, 1.0, 0.0)
    prefix = _dot(sel_f.astype(BF16), upper)
    cnt = jnp.sum(sel_f, axis=1, keepdims=True)
    cnt_al = jnp.floor((cnt + (SLOT_ALIGN - 1)) * (1.0 / SLOT_ALIGN)) * SLOT_ALIGN
    re = lax.broadcasted_iota(I32, (ne, ne), 0)
    ce = lax.broadcasted_iota(I32, (ne, ne), 1)
    lower = jnp.where(ce < re, 1.0, 0.0).astype(BF16)
    loc = _dot(lower, jnp.broadcast_to(cnt_al, (ne, LANES)).astype(BF16))[:, 0:1]
    lrow_all = loc + prefix
    lrows = [jnp.sum(jnp.where(iota_e == idx, lrow_all, 0.0), axis=0, keepdims=True) for idx in idxs]

    nc = ce_ref.shape[-1]
    jrow = lax.broadcasted_iota(I32, (1, nc), 1).astype(F32) * SLOT_ALIGN
    owner = jnp.sum(jnp.where(loc + cnt_al <= jrow, 1.0, 0.0), axis=0, keepdims=True)
    owner = jnp.minimum(owner, ne - 1.0).astype(I32)
    iota_ec = lax.broadcasted_iota(I32, (ne, nc), 0)
    rel = jnp.sum(jnp.where(iota_ec == owner, base[...] - loc, 0.0), axis=0, keepdims=True) + jrow
    base[...] = base[...] + cnt_al
    used = jnp.sum(cnt_al, axis=0, keepdims=True)
    spare = jnp.minimum(jrow - used, (MOE_ROW_CHUNK - SLOT_ALIGN) * 1.0)
    spare = spare + (pl.program_id(0) % 2).astype(F32) * MOE_ROW_CHUNK
    unused = jrow >= used
    owner = jnp.where(unused, ne, owner)
    rel = jnp.where(unused, spare, rel)

    lrow_ref[...] = _select_rows(lrows).astype(I32)
    w_ref[...] = _select_rows([wk * scale for wk in wts])
    ce_ref[...] = owner
    rel_ref[...] = rel.astype(I32)
    nch = used * (1.0 / SLOT_ALIGN)
    nch_ref[...] = jnp.broadcast_to(nch, nch_ref.shape).astype(I32)
    cnt_ref[...] = base[...].astype(I32)


def _mid(yhy, yhg, x2d, seq, woa, wob, g1, sc2, sh2, g2, norm_g, wr_t, rbias, swg, swu, swd, tm):
    t, d = x2d.shape
    per = seq // tm
    ff = swg.shape[1]

    def full(shape):
        return pl.BlockSpec(shape, lambda i: (0,) * len(shape))

    mspec = pl.BlockSpec((None, 1, d), lambda i: (i // per, 0, 0))
    kt = pl.BlockSpec((TOP_K, tm), lambda i: (0, i))
    nt = t // tm

    def per_tile(n):
        return pl.BlockSpec((None, 1, n), lambda i: (i, 0, 0))

    return pl.pallas_call(
        _mid_kernel,
        grid=(nt,),
        in_specs=[pl.BlockSpec((tm, HY_W), lambda i: (i, 0)), pl.BlockSpec((tm, HG_W), lambda i: (i, 0)),
                  pl.BlockSpec((tm, d), lambda i: (i, 0)), full((HY_W, d)), full((HG_W, d)),
                  mspec, mspec, mspec, mspec, full((1, d)), full((N_EXPERTS, d)), full((N_EXPERTS, 1)),
                  full((d, ff)), full((d, ff)), full((ff, d))],
        out_specs=[pl.BlockSpec((tm, d), lambda i: (i, 0)), pl.BlockSpec((tm, d), lambda i: (i, 0)),
                   kt, kt, per_tile(MOE_CHUNKS), per_tile(MOE_CHUNKS), per_tile(LANES), full((N_EXPERTS, 1))],
        out_shape=(jax.ShapeDtypeStruct((t, d), F32), jax.ShapeDtypeStruct((t, d), BF16),
                   jax.ShapeDtypeStruct((TOP_K, t), I32), jax.ShapeDtypeStruct((TOP_K, t), F32),
                   jax.ShapeDtypeStruct((nt, 1, MOE_CHUNKS), I32), jax.ShapeDtypeStruct((nt, 1, MOE_CHUNKS), I32),
                   jax.ShapeDtypeStruct((nt, 1, LANES), I32), jax.ShapeDtypeStruct((N_EXPERTS, 1), I32)),
        scratch_shapes=[pltpu.VMEM((N_EXPERTS, 1), F32)],
        compiler_params=_params(("arbitrary",)),
        name="mid",
    )(yhy, yhg, x2d, woa, wob, g1, sc2, sh2, g2, norm_g.reshape(1, d), wr_t, rbias.reshape(N_EXPERTS, 1),
      swg, swu, swd)


def _pack_bf16_pairs(x):
    n = x.shape[1] // 2
    lo = lax.shift_right_logical(pltpu.bitcast(x[:, :n], I32), 16)
    hi = pltpu.bitcast(x[:, n:], I32) & -65536
    return hi | lo


def _unpack_bf16_pairs(p):
    lo = pltpu.bitcast(lax.shift_left(p, 16), F32).astype(BF16)
    hi = pltpu.bitcast(p & -65536, F32).astype(BF16)
    return lo, hi


def _row_chunks(nch):
    return (nch * SLOT_ALIGN + MOE_ROW_CHUNK - 1) // MOE_ROW_CHUNK


def _start_chunk_copies(c, ce_ref, rel_ref, ps_ref, make):
    per = MOE_ROW_CHUNK // SLOT_ALIGN
    for q in range(per):
        j = c * per + q
        glob = pl.multiple_of(ps_ref[ce_ref[j]] + rel_ref[j], SLOT_ALIGN)
        make(pl.multiple_of(j * SLOT_ALIGN, SLOT_ALIGN), glob).start()


def _wait_row_chunks(n, row_chunk_copy):
    def wait(_, carry):
        row_chunk_copy.wait()
        return carry

    lax.fori_loop(0, n, wait, 0)


def _dispatch_kernel(nch_ref, ce_ref, rel_ref, ps_ref, h_ref, lrow_ref, xs_hbm, xloc, zbuf, sems):
    tm = h_ref.shape[0]
    bm = zbuf.shape[0]
    rc = MOE_ROW_CHUNK
    sem = sems.at[0]

    @pl.when(pl.program_id(0) == 0)
    def _():
        zbuf[...] = jnp.zeros_like(zbuf)

        def zcopy(e):
            start = pl.multiple_of(ps_ref[e + 1] - bm, bm)
            return pltpu.make_async_copy(zbuf, xs_hbm.at[pl.ds(start, bm), :], sem)

        def start(e, carry):
            @pl.when(ps_ref[e + 1] > ps_ref[e])
            def _():
                zcopy(e).start()
            return carry

        def wait(e, carry):
            @pl.when(ps_ref[e + 1] > ps_ref[e])
            def _():
                zcopy(e).wait()
            return carry

        lax.fori_loop(0, N_EXPERTS, start, 0)
        lax.fori_loop(0, N_EXPERTS, wait, 0)

        def tcopy(j):
            return pltpu.make_async_copy(zbuf, xs_hbm.at[pl.ds(pl.multiple_of(j * bm, bm), bm), :], sem)

        def tstart(j, carry):
            tcopy(j).start()
            return carry

        def twait(j, carry):
            tcopy(j).wait()
            return carry

        first_tail = ps_ref[N_EXPERTS] // bm
        lax.fori_loop(first_tail, xs_hbm.shape[0] // bm, tstart, 0)
        lax.fori_loop(first_tail, xs_hbm.shape[0] // bm, twait, 0)

    i = pl.program_id(0)
    slot = i % 2
    buf = xloc.at[slot]
    n_rc = _row_chunks(nch_ref[i])
    lrow = lrow_ref[...].astype(F32)
    sub = rc // 2
    rid = lax.broadcasted_iota(I32, (sub, tm), 0).astype(F32).astype(BF16)
    one = jnp.ones((sub, tm), BF16)

    def select(r0):
        rel = (lrow - jnp.asarray(r0, I32).astype(F32)).astype(BF16)
        onehot = jnp.zeros((sub, tm), BF16)
        for k in range(TOP_K):
            onehot = jnp.where(rid == rel[k:k + 1, :], one, onehot)
        buf[pl.ds(r0, sub), :] = _pack_bf16_pairs(_dot(onehot, h_ref[...]))

    def send(c):
        _start_chunk_copies(c, ce_ref, rel_ref, ps_ref,
                            lambda loc, glob: pltpu.make_async_copy(buf.at[pl.ds(loc, SLOT_ALIGN), :],
                                                                    xs_hbm.at[pl.ds(glob, SLOT_ALIGN), :],
                                                                    sems.at[slot]))

    def retire(tile_slot, n):
        _wait_row_chunks(n, pltpu.make_async_copy(xloc.at[tile_slot, pl.ds(0, rc), :],
                                                  xs_hbm.at[pl.ds(0, rc), :], sems.at[tile_slot]))

    select(0)
    select(sub)

    def body(c, carry):
        send(c - 1)
        r0 = pl.multiple_of(c * rc, rc)
        select(r0)
        select(r0 + sub)
        return carry

    lax.fori_loop(1, n_rc, body, 0)
    send(n_rc - 1)

    @pl.when(i > 0)
    def _():
        retire(1 - slot, _row_chunks(nch_ref[i - 1]))

    @pl.when(i == pl.num_programs(0) - 1)
    def _():
        retire(slot, n_rc)


def _local_rows(tm):
    bound = TOP_K * tm + N_EXPERTS * (SLOT_ALIGN - 1)
    return -(-bound // MOE_ROW_CHUNK) * MOE_ROW_CHUNK


def _moe_smem_specs():
    chunk_list = pl.BlockSpec((MOE_CHUNKS,), lambda i, nch: (i,), memory_space=pltpu.SMEM)
    return [chunk_list, chunk_list,
            pl.BlockSpec((N_EXPERTS + 1,), lambda i, nch: (0,), memory_space=pltpu.SMEM)]


def _dispatch(h2d, lrow_kt, ce, rel, nch, ps, n_slots, tm):
    t, d = h2d.shape
    grid_spec = pltpu.PrefetchScalarGridSpec(
        num_scalar_prefetch=1,
        grid=(t // tm,),
        in_specs=_moe_smem_specs() + [pl.BlockSpec((tm, d), lambda i, nch: (i, 0)),
                                      pl.BlockSpec((TOP_K, tm), lambda i, nch: (0, i))],
        out_specs=pl.BlockSpec(memory_space=pl.ANY),
        scratch_shapes=[pltpu.VMEM((2, _local_rows(tm), d // 2), I32), pltpu.VMEM((FFN_BLOCK, d // 2), I32),
                        pltpu.SemaphoreType.DMA((2,))],
    )

    return pl.pallas_call(
        _dispatch_kernel,
        grid_spec=grid_spec,
        out_shape=jax.ShapeDtypeStruct((n_slots, d // 2), I32),
        compiler_params=_params(("arbitrary",)),
        name="dispatch",
    )(nch, ce, rel, ps, h2d, lrow_kt)


def _ffn_kernel(be_ref, na_ref, x_ref, wg_ref, wu_ref, wd_ref, o_ref, wg_bf, wu_bf, wd_bf):
    i = pl.program_id(0)

    @pl.when(i < na_ref[0])
    def _():
        @pl.when((i == 0) | (be_ref[i] != be_ref[jnp.maximum(i - 1, 0)]))
        def _():
            wg_bf[...] = wg_ref[...].astype(BF16)
            wu_bf[...] = wu_ref[...].astype(BF16)
            wd_bf[...] = wd_ref[...].astype(BF16)

        half = x_ref.shape[1]
        hb = x_ref.shape[0] // 2
        halves = [slice(p * hb, (p + 1) * hb) for p in range(2)]
        gu = []
        for rows in halves:
            lo, hi = _unpack_bf16_pairs(x_ref[rows, :])
            gu.append((_dot(lo, wg_bf[:half, :]) + _dot(hi, wg_bf[half:, :]),
                       _dot(lo, wu_bf[:half, :]) + _dot(hi, wu_bf[half:, :])))
        for rows, (gate, up) in zip(halves, gu):
            act = (_silu(gate) * up).astype(BF16)
            y = _dot(act, wd_bf[...])
            o_ref[rows, :] = _pack_bf16_pairs(y.astype(BF16).astype(F32))

    @pl.when(i >= na_ref[0])
    def _():
        o_ref[...] = jnp.zeros_like(o_ref)


def _ffn(xs, block_e, n_active, ew_gate, ew_up, ew_down):
    n_slots, half = xs.shape
    d = 2 * half
    bm = FFN_BLOCK
    ff = ew_gate.shape[2]
    grid_spec = pltpu.PrefetchScalarGridSpec(
        num_scalar_prefetch=2,
        grid=(n_slots // bm,),
        in_specs=[pl.BlockSpec((bm, half), lambda i, be, na: (jnp.minimum(i, na[0] - 1), 0)),
                  pl.BlockSpec((None, d, ff), lambda i, be, na: (be[i], 0, 0)),
                  pl.BlockSpec((None, d, ff), lambda i, be, na: (be[i], 0, 0)),
                  pl.BlockSpec((None, ff, d), lambda i, be, na: (be[i], 0, 0))],
        out_specs=pl.BlockSpec((bm, half), lambda i, be, na: (i, 0)),
        scratch_shapes=[pltpu.VMEM((d, ff), BF16), pltpu.VMEM((d, ff), BF16), pltpu.VMEM((ff, d), BF16)],
    )
    return pl.pallas_call(
        _ffn_kernel,
        grid_spec=grid_spec,
        out_shape=jax.ShapeDtypeStruct((n_slots, half), I32),
        compiler_params=_params(("arbitrary",)),
        name="ffn",
    )(block_e, n_active, xs, ew_gate, ew_up, ew_down)


def _combine_kernel(nch_ref, ce_ref, rel_ref, cen_ref, reln_ref, ps_ref, lrow_ref, w_ref, xs_ref, g2_ref, fg_ref,
                    y_hbm, o_ref, yloc, acc, sems):
    tm = xs_ref.shape[0]
    half = yloc.shape[2]
    rc = MOE_ROW_CHUNK
    i = pl.program_id(0)
    last = pl.num_programs(0) - 1
    slot = i % 2
    nxt = jnp.minimum(i + 1, last)
    n_rc = _row_chunks(nch_ref[i])
    n_next = _row_chunks(nch_ref[nxt])

    def fetch(c, ce, rel, s):
        _start_chunk_copies(c, ce, rel, ps_ref,
                            lambda loc, glob: pltpu.make_async_copy(y_hbm.at[pl.ds(glob, SLOT_ALIGN), :],
                                                                    yloc.at[s, pl.ds(loc, SLOT_ALIGN), :],
                                                                    sems.at[s]))

    def arrived(s, n):
        _wait_row_chunks(n, pltpu.make_async_copy(y_hbm.at[pl.ds(0, rc), :], yloc.at[s, pl.ds(0, rc), :],
                                                  sems.at[s]))

    def fetch_own(c, carry):
        fetch(c, ce_ref, rel_ref, slot)
        return carry

    def fetch_next(c, carry):
        fetch(c, cen_ref, reln_ref, 1 - slot)
        return carry

    @pl.when(i == 0)
    def _():
        yloc[...] = jnp.zeros_like(yloc)
        lax.fori_loop(0, n_rc, fetch_own, 0)

    n_prev = _row_chunks(nch_ref[jnp.maximum(i - 1, 0)])
    arrived(slot, jnp.where(i == 0, n_rc, jnp.maximum(n_prev, n_rc)))

    lrow = lrow_ref[...].astype(F32)
    wrow = w_ref[...].astype(BF16)
    acc[...] = jnp.zeros_like(acc)
    sub = rc // 2
    rid = lax.broadcasted_iota(I32, (sub, tm), 0).astype(F32).astype(BF16)

    def weights(r0):
        rel = (lrow - r0.astype(F32)).astype(BF16)
        wt = jnp.zeros((sub, tm), BF16)
        for k in range(TOP_K):
            wt = jnp.where(rid == rel[k:k + 1, :], jnp.broadcast_to(wrow[k:k + 1, :], (sub, tm)), wt)
        return wt

    def gather(c, carry):
        fetch(c, cen_ref, reln_ref, 1 - slot)
        r0 = pl.multiple_of(c * rc, rc)
        wt_a = weights(r0)
        wt_b = weights(r0 + sub)
        lo_a, hi_a = _unpack_bf16_pairs(yloc[slot, pl.ds(r0, sub), :])
        lo_b, hi_b = _unpack_bf16_pairs(yloc[slot, pl.ds(r0 + sub, sub), :])
        acc[:, :half] += _dot_tn(wt_a, lo_a) + _dot_tn(wt_b, lo_b)
        acc[:, half:] += _dot_tn(wt_a, hi_a) + _dot_tn(wt_b, hi_b)
        return carry

    lax.fori_loop(0, n_rc, gather, 0)
    lax.fori_loop(n_rc, n_next, fetch_next, 0)

    @pl.when(i == last)
    def _():
        arrived(1 - slot, jnp.maximum(n_rc, n_next))

    x = xs_ref[...] + g2_ref[...] * acc[...]
    o_ref[...] = x * lax.rsqrt(jnp.mean(x * x, axis=-1, keepdims=True) + NORM_EPS) * fg_ref[...]


def _combine(y_sorted, lrow_kt, w_kt, ce, rel, nch, ps, xs2d, seq, g2, final_g, tm):
    t, d = xs2d.shape
    per = seq // tm
    nt = t // tm
    own, _, ranges = _moe_smem_specs()
    nxt = pl.BlockSpec((MOE_CHUNKS,), lambda i, nch: (jnp.minimum(i + 1, nt - 1),), memory_space=pltpu.SMEM)
    grid_spec = pltpu.PrefetchScalarGridSpec(
        num_scalar_prefetch=1,
        grid=(nt,),
        in_specs=[own, own, nxt, nxt, ranges,
                  pl.BlockSpec((TOP_K, tm), lambda i, nch: (0, i)),
                  pl.BlockSpec((TOP_K, tm), lambda i, nch: (0, i)),
                  pl.BlockSpec((tm, d), lambda i, nch: (i, 0)),
                  pl.BlockSpec((None, 1, d), lambda i, nch: (i // per, 0, 0)),
                  pl.BlockSpec((1, d), lambda i, nch: (0, 0)),
                  pl.BlockSpec(memory_space=pl.ANY)],
        out_specs=pl.BlockSpec((tm, d), lambda i, nch: (i, 0)),
        scratch_shapes=[pltpu.VMEM((2, _local_rows(tm), d // 2), I32), pltpu.VMEM((tm, d), F32),
                        pltpu.SemaphoreType.DMA((2,))],
    )
    return pl.pallas_call(
        _combine_kernel,
        grid_spec=grid_spec,
        out_shape=jax.ShapeDtypeStruct((t, d), F32),
        compiler_params=_params(("arbitrary",)),
        name="combine",
    )(nch, ce, rel, ce, rel, ps, lrow_kt, w_kt, xs2d, g2, final_g.reshape(1, d), y_sorted)


def kernel(x, c, ctx, c_ctx, w_mod, b_mod, norm1_g, norm2_g, w_in, w_out, hy_conv_w, hy_conv_b,
           hy_fw1, hy_fb1, hy_fw2, hy_fb2, hy_fw3, hy_freq, hy_d, hg_lb_logits, hg_norm_g,
           w_router, router_bias, ew_gate, ew_up, ew_down, sw_gate, sw_up, sw_down, final_g):
    b, seq, d = x.shape
    ctx_len = ctx.shape[1]
    t = b * seq
    layer = 0

    lower = jnp.cumsum(jax.nn.softmax(hg_lb_logits.astype(F32), axis=1), axis=1)
    lb_f, lb_b = lower[0, layer], lower[1, layer]

    rows = -(-(b + 1) // 8) * 8
    cc = jnp.concatenate([c, c_ctx[None, :], jnp.zeros((rows - b - 1, d), F32)], axis=0)
    mod = _modulation(cc, w_mod[layer], b_mod[layer])
    sh1, sc1, g1, sh2, sc2, g2 = (m.reshape(b, 1, d) for m in jnp.split(mod[:b], 6, axis=-1))
    csh1, csc1 = (jnp.broadcast_to(m.reshape(1, 1, d), (b, 1, d))
                  for m in jnp.split(mod[b:b + 1], 6, axis=-1)[:2])

    w_in_bf = w_in[layer].astype(BF16)
    x2d = x.reshape(t, d)
    px = _inproj(x2d, seq, norm1_g[layer], sc1, sh1, w_in_bf, min(seq, 512), w_in_bf.shape[1])
    lo = 3 * HY_W + HG_W
    pc = _inproj(ctx.reshape(b * ctx_len, d), ctx_len, norm1_g[layer], csc1, csh1,
                 w_in_bf[:, lo:lo + 3 * HG_W], ctx_len, HG_W)
    px3 = px.reshape(b, seq, -1)
    pc3 = pc.reshape(b, ctx_len, -1)

    y_hg = _hgrn(px3, pc3, lb_f, lb_b, hg_norm_g[layer])

    a_mat, s_mat = _dft_mats(seq)
    u1, u2, un = _hyena_filters(seq, hy_fw1[layer], hy_fb1[layer], hy_fw2[layer], hy_fb2[layer],
                                hy_fw3[layer], hy_freq[layer], a_mat, s_mat)
    y_hy = _hyena(px3, hy_conv_w[layer], hy_conv_b[layer], u1, u2, un, hy_d[layer], a_mat, s_mat)

    w_out_bf = w_out[layer].astype(BF16)
    tm = min(seq, MOE_TILE)
    nt = t // tm
    xs, h2, lrow_kt, w_kt, ce, rel, nch, counts = _mid(
        y_hy.reshape(t, HY_W), y_hg.reshape(t, HG_W), x2d, seq, w_out_bf[:HY_W], w_out_bf[HY_W:],
        g1, sc2, sh2, g2, norm2_g[layer], w_router[layer].T, router_bias[layer],
        sw_gate[layer].astype(BF16), sw_up[layer].astype(BF16), sw_down[layer].astype(BF16), tm)

    bm = FFN_BLOCK
    counts = counts.reshape(N_EXPERTS)
    padded = (counts + bm - 1) // bm * bm
    p_ends = jnp.cumsum(padded)
    ps = jnp.concatenate([p_ends - padded, p_ends[-1:]]).astype(I32)
    spare_blocks = -(-2 * MOE_ROW_CHUNK // bm)
    n_blocks = -(-(t * TOP_K + nt * N_EXPERTS * (SLOT_ALIGN - 1)) // bm) + N_EXPERTS + spare_blocks
    block_rows = jnp.arange(n_blocks, dtype=I32) * bm
    block_e = jnp.minimum(jnp.sum(p_ends[None, :] <= block_rows[:, None], axis=1), N_EXPERTS - 1).astype(I32)
    n_active = (p_ends[-1:] // bm).astype(I32)

    ce, rel, nch = ce.reshape(-1), rel.reshape(-1), nch[:, 0, 0]
    x_sorted = _dispatch(h2, lrow_kt, ce, rel, nch, ps, n_blocks * bm, tm)
    y_sorted = _ffn(x_sorted, block_e, n_active, ew_gate[layer], ew_up[layer], ew_down[layer])
    out = _combine(y_sorted, lrow_kt, w_kt, ce, rel, nch, ps, xs, seq, g2, final_g, tm)
    return out.reshape(b, seq, d)
```

```python
import functools
import math

import jax
import jax.numpy as jnp
from jax import lax
from jax.experimental import pallas as pl
from jax.experimental.pallas import tpu as pltpu

F32 = jnp.float32
BF16 = jnp.bfloat16
I32 = jnp.int32
HIGHEST = lax.Precision.HIGHEST

GRID_W = 64
HY_W = 512
HG_W = 512
HY_EMB = 33
HY_BANDS = 16
HY_DECAY_TARGET = 1e-2
HY_FAST_DECAY_PCT = 0.3
HY_SLOW_DECAY_PCT = 1.5
HG_HEAD_DIM = 128
HG_HEADS = 4
HG_SCALE = HG_HEAD_DIM ** -0.5
HG_CHUNK = 64
N_EXPERTS = 256
TOP_K = 8
N_GROUPS = 8
TOPK_GROUPS = 4
GROUP_SIZE = N_EXPERTS // N_GROUPS
ROUTED_SCALE = 2.5
NORM_EPS = 1e-6

VMEM_LIMIT_BYTES = 56 * 1024 * 1024
LANES = 128
FFN_BLOCK = 512
MOE_TILE = 512
SLOT_ALIGN = 8
MOE_ROW_CHUNK = 512
MOE_CHUNKS = 1024
HY_CT = 256


def _params(sem, vmem=VMEM_LIMIT_BYTES):
    return pltpu.CompilerParams(dimension_semantics=sem, vmem_limit_bytes=vmem)


def _silu(x):
    return x * jax.nn.sigmoid(x)


def _dot(a, b):
    return jnp.dot(a, b, preferred_element_type=F32)


def _dot_nt(a, b):
    return lax.dot_general(a, b, (((1,), (1,)), ((), ())), preferred_element_type=F32)


def _dot_tn(a, b):
    return lax.dot_general(a, b, (((0,), (0,)), ((), ())), preferred_element_type=F32)


def _mod_kernel(c_ref, w_ref, b_ref, o_ref):
    s = _silu(c_ref[...])
    o_ref[...] = jnp.dot(s, w_ref[...], preferred_element_type=F32, precision=HIGHEST) + b_ref[...]


def _modulation(cc, w_mod, b_mod):
    rows, d = cc.shape
    n = w_mod.shape[1]
    tn = 1024
    return pl.pallas_call(
        _mod_kernel,
        grid=(n // tn,),
        in_specs=[pl.BlockSpec((rows, d), lambda j: (0, 0)),
                  pl.BlockSpec((d, tn), lambda j: (0, j)),
                  pl.BlockSpec((1, tn), lambda j: (0, j))],
        out_specs=pl.BlockSpec((rows, tn), lambda j: (0, j)),
        out_shape=jax.ShapeDtypeStruct((rows, n), F32),
        compiler_params=_params(("parallel",)),
        name="mod",
    )(cc, w_mod, b_mod.reshape(1, n))


def _rms_mod(x, g, sc, sh):
    y = x * lax.rsqrt(jnp.mean(x * x, axis=-1, keepdims=True) + NORM_EPS) * g
    return y * (1.0 + sc) + sh


def _inproj_kernel(x_ref, g_ref, sc_ref, sh_ref, w_ref, o_ref, h_scr):
    @pl.when(pl.program_id(1) == 0)
    def _():
        h_scr[...] = _rms_mod(x_ref[...], g_ref[...], sc_ref[...], sh_ref[...]).astype(BF16)

    o_ref[...] = _dot(h_scr[...], w_ref[...]).astype(o_ref.dtype)


def _inproj(x2d, seq, g, sc, sh, w_bf, tm, tn):
    t, d = x2d.shape
    n = w_bf.shape[1]
    per = seq // tm
    w_mode = dict(pipeline_mode=pl.Buffered(1)) if tn == n else {}
    return pl.pallas_call(
        _inproj_kernel,
        grid=(t // tm, n // tn),
        in_specs=[pl.BlockSpec((tm, d), lambda i, j: (i, 0)),
                  pl.BlockSpec((1, d), lambda i, j: (0, 0)),
                  pl.BlockSpec((None, 1, d), lambda i, j: (i // per, 0, 0)),
                  pl.BlockSpec((None, 1, d), lambda i, j: (i // per, 0, 0)),
                  pl.BlockSpec((d, tn), lambda i, j: (0, j), **w_mode)],
        out_specs=pl.BlockSpec((tm, tn), lambda i, j: (i, j)),
        out_shape=jax.ShapeDtypeStruct((t, n), BF16),
        scratch_shapes=[pltpu.VMEM((tm, d), BF16)],
        compiler_params=_params(("parallel", "arbitrary")),
        name="inproj",
    )(x2d, g.reshape(1, d), sc, sh, w_bf)


def _hg_steps(chains):
    c = HG_CHUNK
    r = lax.broadcasted_iota(I32, (c, c), 0)
    s = lax.broadcasted_iota(I32, (c, c), 1)
    geo = {False: (r >= s, c // 2 - 1, c - 1), True: (r <= s, c // 2, 0)}
    tri = {rev: jnp.where(g[0], 1.0, 0.0).astype(BF16) for rev, g in geo.items()}

    work = []
    for ch in chains:
        lb = ch["lb"]
        sig = jax.nn.sigmoid(ch["fr"])
        lf = jnp.log(lb + (1.0 - lb) * sig)
        k = (1.0 - lb) * (1.0 - sig)
        hi = lf.astype(BF16)
        lo = (lf - hi.astype(F32)).astype(BF16)
        t = tri[ch["rev"]]
        work.append(dict(k=k, bc=_dot(t, hi) + _dot(t, lo)))
    for ch, w in zip(chains, work):
        mask, mid, last = geo[ch["rev"]]
        bc = w["bc"]
        b_mid = bc[mid:mid + 1, :]
        b_last = bc[last:last + 1, :]
        km = w["k"] * jnp.exp(b_mid - bc)
        kd = (km * jnp.exp(b_last - b_mid)).astype(BF16)
        w["ut"] = _dot_tn(ch["v"], kd)
        w["decay"] = jnp.exp(b_last)
        if ch["q"] is not None:
            qm = ch["q"] * jnp.exp(bc - b_mid)
            w["att"] = _dot_nt(qm.astype(BF16), km.astype(BF16))
            qe = (qm * jnp.exp(b_mid)).astype(BF16)
            w["inter"] = _dot_nt(qe, ch["st"].astype(BF16))
    out = []
    for ch, w in zip(chains, work):
        o = None
        if ch["q"] is not None:
            att = jnp.where(geo[ch["rev"]][0], w["att"], 0.0).astype(BF16)
            o = _dot(att, ch["v"]) + w["inter"]
        out.append((o, ch["st"] * w["decay"] + w["ut"]))
    return out


def _hgrn_kernel(q_ref, ff_ref, fb_ref, i_ref, g_ref, cff_ref, cfb_ref, ci_ref,
                 lbf_ref, lbb_ref, ng_ref, o_ref, qs_scr, of_scr, ob_scr, st_scr):
    seq = q_ref.shape[0]
    ctx = cff_ref.shape[0]
    c = HG_CHUNK
    dh = HG_HEAD_DIM
    rb = min(seq, 256)

    for i in range(seq // rb):
        rows = slice(i * rb, (i + 1) * rb)
        qs_scr[rows, :] = (_silu(q_ref[rows, :].astype(F32)) * HG_SCALE).astype(BF16)
    st_scr[...] = jnp.zeros_like(st_scr)

    def chains(nchunks, ffr, fbr, vr, with_q):
        def body(n, carry):
            chains = []
            for h in range(HG_HEADS):
                cols = slice(h * dh, (h + 1) * dh)
                for rev, fref, lbref in ((False, ffr, lbf_ref), (True, fbr, lbb_ref)):
                    ci = (nchunks - 1 - n) if rev else n
                    rows = pl.ds(pl.multiple_of(ci * c, c), c)
                    chains.append(dict(
                        rev=rev, rows=rows, cols=cols, fr=fref[rows, cols].astype(F32), v=vr[rows, cols],
                        lb=lbref[:, cols], st=st_scr[len(chains)],
                        q=qs_scr[rows, cols].astype(F32) if with_q else None))
            for slot, (ch, (o, st)) in enumerate(zip(chains, _hg_steps(chains))):
                st_scr[slot] = st
                if with_q:
                    (ob_scr if ch["rev"] else of_scr)[ch["rows"], ch["cols"]] = o
            return carry

        lax.fori_loop(0, nchunks, body, 0)

    chains(ctx // c, cff_ref, cfb_ref, ci_ref, False)
    chains(seq // c, ff_ref, fb_ref, i_ref, True)

    ng = ng_ref[...]
    for i in range(seq // rb):
        rows = slice(i * rb, (i + 1) * rb)
        gate = _silu(g_ref[rows, :].astype(F32))
        for h in range(HG_HEADS):
            cols = slice(h * dh, (h + 1) * dh)
            o = of_scr[rows, cols] + ob_scr[rows, cols]
            on = o * lax.rsqrt(jnp.mean(o * o, axis=-1, keepdims=True) + NORM_EPS) * ng
            o_ref[rows, cols] = (on * gate[:, cols]).astype(o_ref.dtype)


def _hgrn(px3, pc3, lb_f, lb_b, norm_g):
    b, seq, _ = px3.shape
    ctx = pc3.shape[1]
    dh = HG_HEAD_DIM
    base = 3 * HY_W // HG_W

    def xspec(j):
        return pl.BlockSpec((None, seq, HG_W), lambda bi: (bi, 0, base + j))

    def cspec(j):
        return pl.BlockSpec((None, ctx, HG_W), lambda bi: (bi, 0, j))

    vec = pl.BlockSpec((1, HG_W), lambda bi: (0, 0))
    return pl.pallas_call(
        _hgrn_kernel,
        grid=(b,),
        in_specs=[xspec(0), xspec(1), xspec(2), xspec(3), xspec(4), cspec(0), cspec(1), cspec(2),
                  vec, vec, pl.BlockSpec((1, dh), lambda bi: (0, 0))],
        out_specs=pl.BlockSpec((None, seq, HG_W), lambda bi: (bi, 0, 0)),
        out_shape=jax.ShapeDtypeStruct((b, seq, HG_W), BF16),
        scratch_shapes=[pltpu.VMEM((seq, HG_W), BF16), pltpu.VMEM((seq, HG_W), F32),
                        pltpu.VMEM((seq, HG_W), F32), pltpu.VMEM((2 * HG_HEADS, dh, dh), F32)],
        compiler_params=_params(("parallel",)),
        name="hgrn",
    )(px3, px3, px3, px3, px3, pc3, pc3, pc3, lb_f.reshape(1, HG_W), lb_b.reshape(1, HG_W),
      norm_g.reshape(1, dh))


def _dft_mats(seq):
    f = jnp.arange(seq, dtype=I32)
    m = (f[:, None] * f[None, :]) % (2 * seq)
    ang = m.astype(F32) * (math.pi / seq)
    return jnp.cos(ang).astype(BF16), jnp.sin(ang).astype(BF16)


def _hyfilt_kernel(feat_ref, w1_ref, b1_ref, w2_ref, b2_ref, fr_ref, w3_ref, t_ref, dl_ref,
                   a_ref, s_ref, u1_ref, u2_ref, un_ref):
    seq = feat_ref.shape[0]
    fr = fr_ref[...]
    h = jnp.sin(fr * (jnp.dot(feat_ref[...], w1_ref[...], preferred_element_type=F32, precision=HIGHEST)
                      + b1_ref[...]))
    h = jnp.sin(fr * (jnp.dot(h, w2_ref[...], preferred_element_type=F32, precision=HIGHEST) + b2_ref[...]))
    window = jnp.exp(-t_ref[...] * dl_ref[...])
    row = lax.broadcasted_iota(I32, (seq, 1), 0)
    sgn = jnp.where(row % 2 == 0, 1.0, -1.0)
    cf = jnp.where(row == 0, 1.0, 2.0) * (1.0 / (2 * seq))
    for o in range(2):
        w3 = w3_ref[:, o * 2 * HY_W:(o + 1) * 2 * HY_W]
        ho = jnp.dot(h, w3, preferred_element_type=F32, precision=HIGHEST)
        fwd = ho[:, :HY_W] * window
        bwd = ho[:, HY_W:] * window
        norm = (jnp.sum(jnp.abs(fwd), axis=0, keepdims=True)
                + jnp.sum(jnp.abs(bwd), axis=0, keepdims=True))
        inv = 1.0 / norm
        ksum = (fwd + bwd) * inv
        kdif = (bwd - fwd) * inv
        kr = _dot(a_ref[...], ksum.astype(BF16))
        ki = _dot(s_ref[...], kdif.astype(BF16))
        u1_ref[o] = (kr * cf).astype(u1_ref.dtype)
        u2_ref[o] = (ki * cf).astype(u2_ref.dtype)
        un_ref[o] = jnp.sum(sgn * ksum, axis=0, keepdims=True) * (1.0 / (2 * seq))


def _hyena_filters(seq, fw1, fb1, fw2, fb2, fw3, freq, a_mat, s_mat):
    pos = jnp.arange(seq, dtype=F32)[:, None]
    t = pos / max(seq - 1, 1)
    w = (2.0 * math.pi / seq) * pos
    bands = jnp.linspace(1e-4, HY_BANDS - 1, HY_BANDS, dtype=F32)[None, :]
    feats = jnp.concatenate([t, jnp.cos(bands * w), -jnp.sin(bands * w)], axis=-1)
    feats = jnp.pad(feats, ((0, 0), (0, LANES - HY_EMB)))
    w1 = jnp.pad(fw1, ((0, LANES - HY_EMB), (0, 0)))
    max_decay = math.log(HY_DECAY_TARGET) / HY_FAST_DECAY_PCT
    min_decay = math.log(HY_DECAY_TARGET) / HY_SLOW_DECAY_PCT
    deltas = jnp.abs(jnp.linspace(min_decay, max_decay, HY_W, dtype=F32))[None, :]
    hid = fw2.shape[0]
    return pl.pallas_call(
        _hyfilt_kernel,
        out_shape=(jax.ShapeDtypeStruct((2, seq, HY_W), BF16),
                   jax.ShapeDtypeStruct((2, seq, HY_W), BF16),
                   jax.ShapeDtypeStruct((2, 1, HY_W), F32)),
        compiler_params=pltpu.CompilerParams(vmem_limit_bytes=VMEM_LIMIT_BYTES),
        name="hyfilt",
    )(feats, w1, fb1.reshape(1, hid), fw2, fb2.reshape(1, hid), freq.reshape(1, hid), fw3, t, deltas,
      a_mat, s_mat)


def _hyena_kernel(x1_ref, x2_ref, v_ref, w1_ref, w2_ref, wv_ref, b1_ref, b2_ref, bv_ref,
                  u1_ref, u2_ref, un_ref, d_ref, a_ref, s_ref, o_ref, z_scr, zb_scr, re_scr, im_scr):
    seq = x1_ref.shape[0]
    rb = min(seq, 512)
    nrb = seq // rb
    row = lax.broadcasted_iota(I32, (rb, 1), 0)
    col = row % GRID_W
    first = col == 0
    lastc = col == GRID_W - 1
    sgn = jnp.where(row % 2 == 0, 1.0, -1.0)

    def conv3(p_ref, w_ref, b_ref, rows):
        p = p_ref[rows, :].astype(F32)
        prev = jnp.where(first, 0.0, pltpu.roll(p, 1, axis=0))
        nxt = jnp.where(lastc, 0.0, pltpu.roll(p, rb - 1, axis=0))
        w = w_ref[...]
        return w[0:1, :] * prev + w[1:2, :] * p + w[2:3, :] * nxt + b_ref[...]

    def forward(o):
        nyq = None
        for i in range(nrb):
            rows = slice(i * rb, (i + 1) * rb)
            part = jnp.sum(sgn * z_scr[rows, :], axis=0, keepdims=True)
            nyq = part if nyq is None else nyq + part
        for i in range(nrb):
            rows = slice(i * rb, (i + 1) * rb)
            p = _dot(a_ref[rows, :], zb_scr[...])
            q = _dot(s_ref[rows, :], zb_scr[...])
            u1 = u1_ref[o, rows, :].astype(F32)
            u2 = u2_ref[o, rows, :].astype(F32)
            re_scr[rows, :] = (p * u1 + q * u2).astype(BF16)
            im_scr[rows, :] = (q * u1 - p * u2).astype(BF16)
        return nyq * un_ref[o]

    def inverse(o, nyq, rows):
        y = _dot(a_ref[rows, :], re_scr[...]) + _dot(s_ref[rows, :], im_scr[...]) + sgn * nyq
        return y + z_scr[rows, :] * d_ref[o:o + 1, :]

    for i in range(nrb):
        rows = slice(i * rb, (i + 1) * rb)
        v = conv3(v_ref, wv_ref, bv_ref, rows)
        z_scr[rows, :] = v
        zb_scr[rows, :] = v.astype(BF16)
    nyq = forward(0)
    for i in range(nrb):
        rows = slice(i * rb, (i + 1) * rb)
        z = conv3(x1_ref, w1_ref, b1_ref, rows) * inverse(0, nyq, rows)
        z_scr[rows, :] = z
        zb_scr[rows, :] = z.astype(BF16)
    nyq = forward(1)
    for i in range(nrb):
        rows = slice(i * rb, (i + 1) * rb)
        y = conv3(x2_ref, w2_ref, b2_ref, rows) * inverse(1, nyq, rows)
        o_ref[rows, :] = y.astype(o_ref.dtype)


def _hyena(px3, conv_w, conv_b, u1, u2, un, d_skip, a_mat, s_mat):
    b, seq, _ = px3.shape
    ct = HY_CT
    nc = HY_W // ct

    def xspec(j):
        return pl.BlockSpec((None, seq, ct), lambda c, bi: (bi, 0, j * nc + c))

    def wspec(j, rows):
        return pl.BlockSpec((rows, ct), lambda c, bi: (0, j * nc + c))

    uspec = pl.BlockSpec((2, seq, ct), lambda c, bi: (0, 0, c))
    const = pl.BlockSpec((seq, seq), lambda c, bi: (0, 0), pipeline_mode=pl.Buffered(1))
    cb = conv_b.reshape(1, 3 * HY_W)
    return pl.pallas_call(
        _hyena_kernel,
        grid=(nc, b),
        in_specs=[xspec(0), xspec(1), xspec(2), wspec(0, 3), wspec(1, 3), wspec(2, 3),
                  wspec(0, 1), wspec(1, 1), wspec(2, 1), uspec, uspec,
                  pl.BlockSpec((2, 1, ct), lambda c, bi: (0, 0, c)),
                  pl.BlockSpec((2, ct), lambda c, bi: (0, c)), const, const],
        out_specs=pl.BlockSpec((None, seq, ct), lambda c, bi: (bi, 0, c)),
        out_shape=jax.ShapeDtypeStruct((b, seq, HY_W), BF16),
        scratch_shapes=[pltpu.VMEM((seq, ct), F32), pltpu.VMEM((seq, ct), BF16),
                        pltpu.VMEM((seq, ct), BF16), pltpu.VMEM((seq, ct), BF16)],
        compiler_params=_params(("parallel", "parallel")),
        name="hyena",
    )(px3, px3, px3, conv_w, conv_w, conv_w, cb, cb, cb, u1, u2, un, d_skip, a_mat, s_mat)


def _select_rows(rows):
    n = rows[0].shape[1]
    idx = lax.broadcasted_iota(I32, (len(rows), n), 0)
    out = jnp.zeros((len(rows), n), rows[0].dtype)
    for k, r in enumerate(rows):
        out = jnp.where(idx == k, r, out)
    return out


def _first_argmax(x, iota, size):
    m = jnp.max(x, axis=0, keepdims=True)
    return jnp.min(jnp.where(x == m, iota, size), axis=0, keepdims=True)


def _mid_kernel(yhy_ref, yhg_ref, x_ref, woa_ref, wob_ref, g1_ref, sc_ref, sh_ref, g2_ref, ng_ref,
                wr_ref, rb_ref, swg_ref, swu_ref, swd_ref,
                xs_ref, h_ref, lrow_ref, w_ref, ce_ref, rel_ref, nch_ref, cnt_ref, base):
    tm = x_ref.shape[0]
    ne = N_EXPERTS

    @pl.when(pl.program_id(0) == 0)
    def _():
        base[...] = jnp.zeros_like(base)

    mix = _dot(yhy_ref[...], woa_ref[...]) + _dot(yhg_ref[...], wob_ref[...])
    xm = x_ref[...] + g1_ref[...] * mix
    h = _rms_mod(xm, ng_ref[...], sc_ref[...], sh_ref[...])
    hb = h.astype(BF16)
    h_ref[...] = hb
    act = (_silu(_dot(hb, swg_ref[...])) * _dot(hb, swu_ref[...])).astype(BF16)
    xs_ref[...] = xm + g2_ref[...] * _dot(act, swd_ref[...])

    logits = lax.dot_general(wr_ref[...], h, (((1,), (1,)), ((), ())), preferred_element_type=F32,
                             precision=HIGHEST)
    scores = jax.nn.sigmoid(logits)
    biased = scores + rb_ref[...]
    neg = -jnp.inf
    iota_g = lax.broadcasted_iota(I32, (GROUP_SIZE, tm), 0)
    grp = []
    for g in range(N_GROUPS):
        blk = biased[g * GROUP_SIZE:(g + 1) * GROUP_SIZE, :]
        m1 = jnp.max(blk, axis=0, keepdims=True)
        i1 = jnp.min(jnp.where(blk == m1, iota_g, GROUP_SIZE), axis=0, keepdims=True)
        m2 = jnp.max(jnp.where(iota_g == i1, neg, blk), axis=0, keepdims=True)
        grp.append(m1 + m2)
    gsc = _select_rows(grp)
    iota8 = lax.broadcasted_iota(I32, (N_GROUPS, tm), 0)
    gsel = iota8 < 0
    for _ in range(TOPK_GROUPS):
        hit = iota8 == _first_argmax(gsc, iota8, N_GROUPS)
        gsel = gsel | hit
        gsc = jnp.where(hit, neg, gsc)
    gself = jnp.where(gsel, 1.0, 0.0)
    iota_e = lax.broadcasted_iota(I32, (ne, tm), 0)
    gid = iota_e // GROUP_SIZE
    emask = jnp.zeros((ne, tm), F32)
    for g in range(N_GROUPS):
        emask = jnp.where(gid == g, gself[g:g + 1, :], emask)
    cur = jnp.where(emask > 0.0, biased, neg)

    sel = iota_e < 0
    idxs, wts = [], []
    for _ in range(TOP_K):
        idx = _first_argmax(cur, iota_e, ne)
        hit = iota_e == idx
        wts.append(jnp.sum(jnp.where(hit, scores, 0.0), axis=0, keepdims=True))
        idxs.append(idx)
        sel = sel | hit
        cur = jnp.where(hit, neg, cur)
    wsum = wts[0]
    for wk in wts[1:]:
        wsum = wsum + wk
    scale = ROUTED_SCALE / wsum

    r = lax.broadcasted_iota(I32, (tm, tm), 0)
    s = lax.broadcasted_iota(I32, (tm, tm), 1)
    upper = jnp.where(r < s, 1.0, 0.0).astype(BF16)
    sel_f = jnp.where(sel, 1.0, 0.0)
    prefix = _dot(sel_f.astype(BF16), upper)
    cnt = jnp.sum(sel_f, axis=1, keepdims=True)
    cnt_al = jnp.floor((cnt + (SLOT_ALIGN - 1)) * (1.0 / SLOT_ALIGN)) * SLOT_ALIGN
    re = lax.broadcasted_iota(I32, (ne, ne), 0)
    ce = lax.broadcasted_iota(I32, (ne, ne), 1)
    lower = jnp.where(ce < re, 1.0, 0.0).astype(BF16)
    loc = _dot(lower, jnp.broadcast_to(cnt_al, (ne, LANES)).astype(BF16))[:, 0:1]
    lrow_all = loc + prefix
    lrows = [jnp.sum(jnp.where(iota_e == idx, lrow_all, 0.0), axis=0, keepdims=True) for idx in idxs]

    nc = ce_ref.shape[-1]
    jrow = lax.broadcasted_iota(I32, (1, nc), 1).astype(F32) * SLOT_ALIGN
    owner = jnp.sum(jnp.where(loc + cnt_al <= jrow, 1.0, 0.0), axis=0, keepdims=True)
    owner = jnp.minimum(owner, ne - 1.0).astype(I32)
    iota_ec = lax.broadcasted_iota(I32, (ne, nc), 0)
    rel = jnp.sum(jnp.where(iota_ec == owner, base[...] - loc, 0.0), axis=0, keepdims=True) + jrow
    base[...] = base[...] + cnt_al
    used = jnp.sum(cnt_al, axis=0, keepdims=True)
    spare = jnp.minimum(jrow - used, (MOE_ROW_CHUNK - SLOT_ALIGN) * 1.0)
    spare = spare + (pl.program_id(0) % 2).astype(F32) * MOE_ROW_CHUNK
    unused = jrow >= used
    owner = jnp.where(unused, ne, owner)
    rel = jnp.where(unused, spare, rel)

    lrow_ref[...] = _select_rows(lrows).astype(I32)
    w_ref[...] = _select_rows([wk * scale for wk in wts])
    ce_ref[...] = owner
    rel_ref[...] = rel.astype(I32)
    nch = used * (1.0 / SLOT_ALIGN)
    nch_ref[...] = jnp.broadcast_to(nch, nch_ref.shape).astype(I32)
    cnt_ref[...] = base[...].astype(I32)


def _mid(yhy, yhg, x2d, seq, woa, wob, g1, sc2, sh2, g2, norm_g, wr_t, rbias, swg, swu, swd, tm):
    t, d = x2d.shape
    per = seq // tm
    ff = swg.shape[1]

    def full(shape):
        return pl.BlockSpec(shape, lambda i: (0,) * len(shape))

    mspec = pl.BlockSpec((None, 1, d), lambda i: (i // per, 0, 0))
    kt = pl.BlockSpec((TOP_K, tm), lambda i: (0, i))
    nt = t // tm

    def per_tile(n):
        return pl.BlockSpec((None, 1, n), lambda i: (i, 0, 0))

    return pl.pallas_call(
        _mid_kernel,
        grid=(nt,),
        in_specs=[pl.BlockSpec((tm, HY_W), lambda i: (i, 0)), pl.BlockSpec((tm, HG_W), lambda i: (i, 0)),
                  pl.BlockSpec((tm, d), lambda i: (i, 0)), full((HY_W, d)), full((HG_W, d)),
                  mspec, mspec, mspec, mspec, full((1, d)), full((N_EXPERTS, d)), full((N_EXPERTS, 1)),
                  full((d, ff)), full((d, ff)), full((ff, d))],
        out_specs=[pl.BlockSpec((tm, d), lambda i: (i, 0)), pl.BlockSpec((tm, d), lambda i: (i, 0)),
                   kt, kt, per_tile(MOE_CHUNKS), per_tile(MOE_CHUNKS), per_tile(LANES), full((N_EXPERTS, 1))],
        out_shape=(jax.ShapeDtypeStruct((t, d), F32), jax.ShapeDtypeStruct((t, d), BF16),
                   jax.ShapeDtypeStruct((TOP_K, t), I32), jax.ShapeDtypeStruct((TOP_K, t), F32),
                   jax.ShapeDtypeStruct((nt, 1, MOE_CHUNKS), I32), jax.ShapeDtypeStruct((nt, 1, MOE_CHUNKS), I32),
                   jax.ShapeDtypeStruct((nt, 1, LANES), I32), jax.ShapeDtypeStruct((N_EXPERTS, 1), I32)),
        scratch_shapes=[pltpu.VMEM((N_EXPERTS, 1), F32)],
        compiler_params=_params(("arbitrary",)),
        name="mid",
    )(yhy, yhg, x2d, woa, wob, g1, sc2, sh2, g2, norm_g.reshape(1, d), wr_t, rbias.reshape(N_EXPERTS, 1),
      swg, swu, swd)


def _pack_bf16_pairs(x):
    n = x.shape[1] // 2
    lo = lax.shift_right_logical(pltpu.bitcast(x[:, :n], I32), 16)
    hi = pltpu.bitcast(x[:, n:], I32) & -65536
    return hi | lo


def _unpack_bf16_pairs(p):
    lo = pltpu.bitcast(lax.shift_left(p, 16), F32).astype(BF16)
    hi = pltpu.bitcast(p & -65536, F32).astype(BF16)
    return lo, hi


def _row_chunks(nch):
    return (nch * SLOT_ALIGN + MOE_ROW_CHUNK - 1) // MOE_ROW_CHUNK


def _start_chunk_copies(c, ce_ref, rel_ref, ps_ref, make):
    per = MOE_ROW_CHUNK // SLOT_ALIGN
    for q in range(per):
        j = c * per + q
        glob = pl.multiple_of(ps_ref[ce_ref[j]] + rel_ref[j], SLOT_ALIGN)
        make(pl.multiple_of(j * SLOT_ALIGN, SLOT_ALIGN), glob).start()


def _wait_row_chunks(n, row_chunk_copy):
    def wait(_, carry):
        row_chunk_copy.wait()
        return carry

    lax.fori_loop(0, n, wait, 0)


def _dispatch_kernel(nch_ref, ce_ref, rel_ref, ps_ref, h_ref, lrow_ref, xs_hbm, xloc, zbuf, sems):
    tm = h_ref.shape[0]
    bm = zbuf.shape[0]
    rc = MOE_ROW_CHUNK
    sem = sems.at[0]

    @pl.when(pl.program_id(0) == 0)
    def _():
        zbuf[...] = jnp.zeros_like(zbuf)

        def zcopy(e):
            start = pl.multiple_of(ps_ref[e + 1] - bm, bm)
            return pltpu.make_async_copy(zbuf, xs_hbm.at[pl.ds(start, bm), :], sem)

        def start(e, carry):
            @pl.when(ps_ref[e + 1] > ps_ref[e])
            def _():
                zcopy(e).start()
            return carry

        def wait(e, carry):
            @pl.when(ps_ref[e + 1] > ps_ref[e])
            def _():
                zcopy(e).wait()
            return carry

        lax.fori_loop(0, N_EXPERTS, start, 0)
        lax.fori_loop(0, N_EXPERTS, wait, 0)

        def tcopy(j):
            return pltpu.make_async_copy(zbuf, xs_hbm.at[pl.ds(pl.multiple_of(j * bm, bm), bm), :], sem)

        def tstart(j, carry):
            tcopy(j).start()
            return carry

        def twait(j, carry):
            tcopy(j).wait()
            return carry

        first_tail = ps_ref[N_EXPERTS] // bm
        lax.fori_loop(first_tail, xs_hbm.shape[0] // bm, tstart, 0)
        lax.fori_loop(first_tail, xs_hbm.shape[0] // bm, twait, 0)

    i = pl.program_id(0)
    slot = i % 2
    buf = xloc.at[slot]
    n_rc = _row_chunks(nch_ref[i])
    lrow = lrow_ref[...].astype(F32)
    sub = rc // 2
    rid = lax.broadcasted_iota(I32, (sub, tm), 0).astype(F32).astype(BF16)
    one = jnp.ones((sub, tm), BF16)

    def select(r0):
        rel = (lrow - jnp.asarray(r0, I32).astype(F32)).astype(BF16)
        onehot = jnp.zeros((sub, tm), BF16)
        for k in range(TOP_K):
            onehot = jnp.where(rid == rel[k:k + 1, :], one, onehot)
        buf[pl.ds(r0, sub), :] = _pack_bf16_pairs(_dot(onehot, h_ref[...]))

    def send(c):
        _start_chunk_copies(c, ce_ref, rel_ref, ps_ref,
                            lambda loc, glob: pltpu.make_async_copy(buf.at[pl.ds(loc, SLOT_ALIGN), :],
                                                                    xs_hbm.at[pl.ds(glob, SLOT_ALIGN), :],
                                                                    sems.at[slot]))

    def retire(tile_slot, n):
        _wait_row_chunks(n, pltpu.make_async_copy(xloc.at[tile_slot, pl.ds(0, rc), :],
                                                  xs_hbm.at[pl.ds(0, rc), :], sems.at[tile_slot]))

    select(0)
    select(sub)

    def body(c, carry):
        send(c - 1)
        r0 = pl.multiple_of(c * rc, rc)
        select(r0)
        select(r0 + sub)
        return carry

    lax.fori_loop(1, n_rc, body, 0)
    send(n_rc - 1)

    @pl.when(i > 0)
    def _():
        retire(1 - slot, _row_chunks(nch_ref[i - 1]))

    @pl.when(i == pl.num_programs(0) - 1)
    def _():
        retire(slot, n_rc)


def _local_rows(tm):
    bound = TOP_K * tm + N_EXPERTS * (SLOT_ALIGN - 1)
    return -(-bound // MOE_ROW_CHUNK) * MOE_ROW_CHUNK


def _moe_smem_specs():
    chunk_list = pl.BlockSpec((MOE_CHUNKS,), lambda i, nch: (i,), memory_space=pltpu.SMEM)
    return [chunk_list, chunk_list,
            pl.BlockSpec((N_EXPERTS + 1,), lambda i, nch: (0,), memory_space=pltpu.SMEM)]


def _dispatch(h2d, lrow_kt, ce, rel, nch, ps, n_slots, tm):
    t, d = h2d.shape
    grid_spec = pltpu.PrefetchScalarGridSpec(
        num_scalar_prefetch=1,
        grid=(t // tm,),
        in_specs=_moe_smem_specs() + [pl.BlockSpec((tm, d), lambda i, nch: (i, 0)),
                                      pl.BlockSpec((TOP_K, tm), lambda i, nch: (0, i))],
        out_specs=pl.BlockSpec(memory_space=pl.ANY),
        scratch_shapes=[pltpu.VMEM((2, _local_rows(tm), d // 2), I32), pltpu.VMEM((FFN_BLOCK, d // 2), I32),
                        pltpu.SemaphoreType.DMA((2,))],
    )

    return pl.pallas_call(
        _dispatch_kernel,
        grid_spec=grid_spec,
        out_shape=jax.ShapeDtypeStruct((n_slots, d // 2), I32),
        compiler_params=_params(("arbitrary",)),
        name="dispatch",
    )(nch, ce, rel, ps, h2d, lrow_kt)


def _ffn_kernel(ps_ref, wg_ref, wu_ref, wd_ref, x_hbm, y_hbm, xbuf, ybuf, wg_bf, wu_bf, wd_bf, xsem, ysem):
    e = pl.program_id(0)
    bm, half = xbuf.shape[1], xbuf.shape[2]
    n_blocks = y_hbm.shape[0] // bm
    g0 = ps_ref[e] // bm
    g1 = ps_ref[e + 1] // bm

    def rows(g):
        return pl.ds(pl.multiple_of(g * bm, bm), bm)

    def x_copy(g, s):
        return pltpu.make_async_copy(x_hbm.at[rows(g), :], xbuf.at[s], xsem.at[s])

    def y_copy(g, s):
        return pltpu.make_async_copy(ybuf.at[s], y_hbm.at[rows(g), :], ysem.at[s])

    @pl.when(e == 0)
    def _():
        ybuf[...] = jnp.zeros_like(ybuf)
        x_copy(0, 0).start()
        y_copy(n_blocks - 2, 0).start()
        y_copy(n_blocks - 1, 1).start()

    @pl.when(g1 > g0)
    def _():
        wg_bf[...] = wg_ref[...].astype(BF16)
        wu_bf[...] = wu_ref[...].astype(BF16)
        wd_bf[...] = wd_ref[...].astype(BF16)

    def block(g, carry):
        s = g % 2
        x_copy(g, s).wait()
        x_copy(g + 1, 1 - s).start()
        hb = bm // 2
        halves = [pl.ds(p * hb, hb) for p in range(2)]
        gu = []
        for r in halves:
            lo, hi = _unpack_bf16_pairs(xbuf[s, r, :])
            gu.append((_dot(lo, wg_bf[:half, :]) + _dot(hi, wg_bf[half:, :]),
                       _dot(lo, wu_bf[:half, :]) + _dot(hi, wu_bf[half:, :])))
        y_copy(g, s).wait()
        for r, (gate, up) in zip(halves, gu):
            act = (_silu(gate) * up).astype(BF16)
            y = _dot(act, wd_bf[...])
            ybuf[s, r, :] = _pack_bf16_pairs(y.astype(BF16).astype(F32))
        y_copy(g, s).start()
        return carry

    lax.fori_loop(g0, g1, block, 0)

    @pl.when(e == pl.num_programs(0) - 1)
    def _():
        x_copy(g1, g1 % 2).wait()
        y_copy(0, 0).wait()
        y_copy(0, 1).wait()
        ybuf[0] = jnp.zeros((bm, half), ybuf.dtype)

        def zstart(g, carry):
            y_copy(g, 0).start()
            return carry

        def zwait(g, carry):
            y_copy(g, 0).wait()
            return carry

        lax.fori_loop(g1, n_blocks, zstart, 0)
        lax.fori_loop(g1, n_blocks, zwait, 0)


def _ffn(xs, ps, ew_gate, ew_up, ew_down):
    n_slots, half = xs.shape
    d = 2 * half
    bm = FFN_BLOCK
    n_exp, _, ff = ew_gate.shape
    grid_spec = pltpu.PrefetchScalarGridSpec(
        num_scalar_prefetch=1,
        grid=(n_exp,),
        in_specs=[pl.BlockSpec((None, d, ff), lambda e, ps: (e, 0, 0)),
                  pl.BlockSpec((None, d, ff), lambda e, ps: (e, 0, 0)),
                  pl.BlockSpec((None, ff, d), lambda e, ps: (e, 0, 0)),
                  pl.BlockSpec(memory_space=pl.ANY)],
        out_specs=pl.BlockSpec(memory_space=pl.ANY),
        scratch_shapes=[pltpu.VMEM((2, bm, half), I32), pltpu.VMEM((2, bm, half), I32),
                        pltpu.VMEM((d, ff), BF16), pltpu.VMEM((d, ff), BF16), pltpu.VMEM((ff, d), BF16),
                        pltpu.SemaphoreType.DMA((2,)), pltpu.SemaphoreType.DMA((2,))],
    )
    return pl.pallas_call(
        _ffn_kernel,
        grid_spec=grid_spec,
        out_shape=jax.ShapeDtypeStruct((n_slots, half), I32),
        compiler_params=_params(("arbitrary",)),
        name="ffn",
    )(ps, ew_gate, ew_up, ew_down, xs)


def _combine_kernel(nch_ref, ce_ref, rel_ref, cen_ref, reln_ref, ps_ref, lrow_ref, w_ref, xs_ref, g2_ref, fg_ref,
                    y_hbm, o_ref, yloc, acc, sems):
    tm = xs_ref.shape[0]
    half = yloc.shape[2]
    rc = MOE_ROW_CHUNK
    i = pl.program_id(0)
    last = pl.num_programs(0) - 1
    slot = i % 2
    nxt = jnp.minimum(i + 1, last)
    n_rc = _row_chunks(nch_ref[i])
    n_next = _row_chunks(nch_ref[nxt])

    def fetch(c, ce, rel, s):
        _start_chunk_copies(c, ce, rel, ps_ref,
                            lambda loc, glob: pltpu.make_async_copy(y_hbm.at[pl.ds(glob, SLOT_ALIGN), :],
                                                                    yloc.at[s, pl.ds(loc, SLOT_ALIGN), :],
                                                                    sems.at[s]))

    def arrived(s, n):
        _wait_row_chunks(n, pltpu.make_async_copy(y_hbm.at[pl.ds(0, rc), :], yloc.at[s, pl.ds(0, rc), :],
                                                  sems.at[s]))

    def fetch_own(c, carry):
        fetch(c, ce_ref, rel_ref, slot)
        return carry

    def fetch_next(c, carry):
        fetch(c, cen_ref, reln_ref, 1 - slot)
        return carry

    @pl.when(i == 0)
    def _():
        yloc[...] = jnp.zeros_like(yloc)
        lax.fori_loop(0, n_rc, fetch_own, 0)

    n_prev = _row_chunks(nch_ref[jnp.maximum(i - 1, 0)])
    arrived(slot, jnp.where(i == 0, n_rc, jnp.maximum(n_prev, n_rc)))

    lrow = lrow_ref[...].astype(F32)
    wrow = w_ref[...].astype(BF16)
    acc[...] = jnp.zeros_like(acc)
    sub = rc // 2
    rid = lax.broadcasted_iota(I32, (sub, tm), 0).astype(F32).astype(BF16)

    def weights(r0):
        rel = (lrow - r0.astype(F32)).astype(BF16)
        wt = jnp.zeros((sub, tm), BF16)
        for k in range(TOP_K):
            wt = jnp.where(rid == rel[k:k + 1, :], jnp.broadcast_to(wrow[k:k + 1, :], (sub, tm)), wt)
        return wt

    def gather(c, carry):
        fetch(c, cen_ref, reln_ref, 1 - slot)
        r0 = pl.multiple_of(c * rc, rc)
        wt_a = weights(r0)
        wt_b = weights(r0 + sub)
        lo_a, hi_a = _unpack_bf16_pairs(yloc[slot, pl.ds(r0, sub), :])
        lo_b, hi_b = _unpack_bf16_pairs(yloc[slot, pl.ds(r0 + sub, sub), :])
        acc[:, :half] += _dot_tn(wt_a, lo_a) + _dot_tn(wt_b, lo_b)
        acc[:, half:] += _dot_tn(wt_a, hi_a) + _dot_tn(wt_b, hi_b)
        return carry

    lax.fori_loop(0, n_rc, gather, 0)
    lax.fori_loop(n_rc, n_next, fetch_next, 0)

    @pl.when(i == last)
    def _():
        arrived(1 - slot, jnp.maximum(n_rc, n_next))

    x = xs_ref[...] + g2_ref[...] * acc[...]
    o_ref[...] = x * lax.rsqrt(jnp.mean(x * x, axis=-1, keepdims=True) + NORM_EPS) * fg_ref[...]


def _combine(y_sorted, lrow_kt, w_kt, ce, rel, nch, ps, xs2d, seq, g2, final_g, tm):
    t, d = xs2d.shape
    per = seq // tm
    nt = t // tm
    own, _, ranges = _moe_smem_specs()
    nxt = pl.BlockSpec((MOE_CHUNKS,), lambda i, nch: (jnp.minimum(i + 1, nt - 1),), memory_space=pltpu.SMEM)
    grid_spec = pltpu.PrefetchScalarGridSpec(
        num_scalar_prefetch=1,
        grid=(nt,),
        in_specs=[own, own, nxt, nxt, ranges,
                  pl.BlockSpec((TOP_K, tm), lambda i, nch: (0, i)),
                  pl.BlockSpec((TOP_K, tm), lambda i, nch: (0, i)),
                  pl.BlockSpec((tm, d), lambda i, nch: (i, 0)),
                  pl.BlockSpec((None, 1, d), lambda i, nch: (i // per, 0, 0)),
                  pl.BlockSpec((1, d), lambda i, nch: (0, 0)),
                  pl.BlockSpec(memory_space=pl.ANY)],
        out_specs=pl.BlockSpec((tm, d), lambda i, nch: (i, 0)),
        scratch_shapes=[pltpu.VMEM((2, _local_rows(tm), d // 2), I32), pltpu.VMEM((tm, d), F32),
                        pltpu.SemaphoreType.DMA((2,))],
    )
    return pl.pallas_call(
        _combine_kernel,
        grid_spec=grid_spec,
        out_shape=jax.ShapeDtypeStruct((t, d), F32),
        compiler_params=_params(("arbitrary",)),
        name="combine",
    )(nch, ce, rel, ce, rel, ps, lrow_kt, w_kt, xs2d, g2, final_g.reshape(1, d), y_sorted)


def kernel(x, c, ctx, c_ctx, w_mod, b_mod, norm1_g, norm2_g, w_in, w_out, hy_conv_w, hy_conv_b,
           hy_fw1, hy_fb1, hy_fw2, hy_fb2, hy_fw3, hy_freq, hy_d, hg_lb_logits, hg_norm_g,
           w_router, router_bias, ew_gate, ew_up, ew_down, sw_gate, sw_up, sw_down, final_g):
    b, seq, d = x.shape
    ctx_len = ctx.shape[1]
    t = b * seq
    layer = 0

    lower = jnp.cumsum(jax.nn.softmax(hg_lb_logits.astype(F32), axis=1), axis=1)
    lb_f, lb_b = lower[0, layer], lower[1, layer]

    rows = -(-(b + 1) // 8) * 8
    cc = jnp.concatenate([c, c_ctx[None, :], jnp.zeros((rows - b - 1, d), F32)], axis=0)
    mod = _modulation(cc, w_mod[layer], b_mod[layer])
    sh1, sc1, g1, sh2, sc2, g2 = (m.reshape(b, 1, d) for m in jnp.split(mod[:b], 6, axis=-1))
    csh1, csc1 = (jnp.broadcast_to(m.reshape(1, 1, d), (b, 1, d))
                  for m in jnp.split(mod[b:b + 1], 6, axis=-1)[:2])

    w_in_bf = w_in[layer].astype(BF16)
    x2d = x.reshape(t, d)
    px = _inproj(x2d, seq, norm1_g[layer], sc1, sh1, w_in_bf, min(seq, 512), w_in_bf.shape[1])
    lo = 3 * HY_W + HG_W
    pc = _inproj(ctx.reshape(b * ctx_len, d), ctx_len, norm1_g[layer], csc1, csh1,
                 w_in_bf[:, lo:lo + 3 * HG_W], ctx_len, HG_W)
    px3 = px.reshape(b, seq, -1)
    pc3 = pc.reshape(b, ctx_len, -1)

    y_hg = _hgrn(px3, pc3, lb_f, lb_b, hg_norm_g[layer])

    a_mat, s_mat = _dft_mats(seq)
    u1, u2, un = _hyena_filters(seq, hy_fw1[layer], hy_fb1[layer], hy_fw2[layer], hy_fb2[layer],
                                hy_fw3[layer], hy_freq[layer], a_mat, s_mat)
    y_hy = _hyena(px3, hy_conv_w[layer], hy_conv_b[layer], u1, u2, un, hy_d[layer], a_mat, s_mat)

    w_out_bf = w_out[layer].astype(BF16)
    tm = min(seq, MOE_TILE)
    nt = t // tm
    xs, h2, lrow_kt, w_kt, ce, rel, nch, counts = _mid(
        y_hy.reshape(t, HY_W), y_hg.reshape(t, HG_W), x2d, seq, w_out_bf[:HY_W], w_out_bf[HY_W:],
        g1, sc2, sh2, g2, norm2_g[layer], w_router[layer].T, router_bias[layer],
        sw_gate[layer].astype(BF16), sw_up[layer].astype(BF16), sw_down[layer].astype(BF16), tm)

    bm = FFN_BLOCK
    counts = counts.reshape(N_EXPERTS)
    padded = (counts + bm - 1) // bm * bm
    p_ends = jnp.cumsum(padded)
    ps = jnp.concatenate([p_ends - padded, p_ends[-1:]]).astype(I32)
    spare_blocks = -(-2 * MOE_ROW_CHUNK // bm)
    n_blocks = -(-(t * TOP_K + nt * N_EXPERTS * (SLOT_ALIGN - 1)) // bm) + N_EXPERTS + spare_blocks

    ce, rel, nch = ce.reshape(-1), rel.reshape(-1), nch[:, 0, 0]
    x_sorted = _dispatch(h2, lrow_kt, ce, rel, nch, ps, n_blocks * bm, tm)
    y_sorted = _ffn(x_sorted, ps, ew_gate[layer], ew_up[layer], ew_down[layer])
    out = _combine(y_sorted, lrow_kt, w_kt, ce, rel, nch, ps, xs, seq, g2, final_g, tm)
    return out.reshape(b, seq, d)
```

```python
import functools
import math

import jax
import jax.numpy as jnp
from jax import lax
from jax.experimental import pallas as pl
from jax.experimental.pallas import tpu as pltpu

F32 = jnp.float32
BF16 = jnp.bfloat16
I32 = jnp.int32
HIGHEST = lax.Precision.HIGHEST

GRID_W = 64
HY_W = 512
HG_W = 512
HY_EMB = 33
HY_BANDS = 16
HY_DECAY_TARGET = 1e-2
HY_FAST_DECAY_PCT = 0.3
HY_SLOW_DECAY_PCT = 1.5
HG_HEAD_DIM = 128
HG_HEADS = 4
HG_SCALE = HG_HEAD_DIM ** -0.5
HG_CHUNK = 64
N_EXPERTS = 256
TOP_K = 8
N_GROUPS = 8
TOPK_GROUPS = 4
GROUP_SIZE = N_EXPERTS // N_GROUPS
ROUTED_SCALE = 2.5
NORM_EPS = 1e-6

VMEM_LIMIT_BYTES = 56 * 1024 * 1024
LANES = 128
FFN_BLOCK = 512
FFN_INPUT_BUFFERS = 4
MOE_TILE = 512
SLOT_ALIGN = 8
MOE_ROW_CHUNK = 512
MOE_CHUNKS = 1024
HY_CT = 256


def _params(sem, vmem=VMEM_LIMIT_BYTES):
    return pltpu.CompilerParams(dimension_semantics=sem, vmem_limit_bytes=vmem)


def _silu(x):
    return x * jax.nn.sigmoid(x)


def _dot(a, b):
    return jnp.dot(a, b, preferred_element_type=F32)


def _dot_nt(a, b):
    return lax.dot_general(a, b, (((1,), (1,)), ((), ())), preferred_element_type=F32)


def _dot_tn(a, b):
    return lax.dot_general(a, b, (((0,), (0,)), ((), ())), preferred_element_type=F32)


def _mod_kernel(c_ref, w_ref, b_ref, o_ref):
    s = _silu(c_ref[...])
    o_ref[...] = jnp.dot(s, w_ref[...], preferred_element_type=F32, precision=HIGHEST) + b_ref[...]


def _modulation(cc, w_mod, b_mod):
    rows, d = cc.shape
    n = w_mod.shape[1]
    tn = 1024
    return pl.pallas_call(
        _mod_kernel,
        grid=(n // tn,),
        in_specs=[pl.BlockSpec((rows, d), lambda j: (0, 0)),
                  pl.BlockSpec((d, tn), lambda j: (0, j)),
                  pl.BlockSpec((1, tn), lambda j: (0, j))],
        out_specs=pl.BlockSpec((rows, tn), lambda j: (0, j)),
        out_shape=jax.ShapeDtypeStruct((rows, n), F32),
        compiler_params=_params(("parallel",)),
        name="mod",
    )(cc, w_mod, b_mod.reshape(1, n))


def _rms_mod(x, g, sc, sh):
    y = x * lax.rsqrt(jnp.mean(x * x, axis=-1, keepdims=True) + NORM_EPS) * g
    return y * (1.0 + sc) + sh


def _inproj_kernel(x_ref, g_ref, sc_ref, sh_ref, w_ref, o_ref, h_scr):
    @pl.when(pl.program_id(1) == 0)
    def _():
        h_scr[...] = _rms_mod(x_ref[...], g_ref[...], sc_ref[...], sh_ref[...]).astype(BF16)

    o_ref[...] = _dot(h_scr[...], w_ref[...]).astype(o_ref.dtype)


def _inproj(x2d, seq, g, sc, sh, w_bf, tm, tn):
    t, d = x2d.shape
    n = w_bf.shape[1]
    per = seq // tm
    w_mode = dict(pipeline_mode=pl.Buffered(1)) if tn == n else {}
    return pl.pallas_call(
        _inproj_kernel,
        grid=(t // tm, n // tn),
        in_specs=[pl.BlockSpec((tm, d), lambda i, j: (i, 0)),
                  pl.BlockSpec((1, d), lambda i, j: (0, 0)),
                  pl.BlockSpec((None, 1, d), lambda i, j: (i // per, 0, 0)),
                  pl.BlockSpec((None, 1, d), lambda i, j: (i // per, 0, 0)),
                  pl.BlockSpec((d, tn), lambda i, j: (0, j), **w_mode)],
        out_specs=pl.BlockSpec((tm, tn), lambda i, j: (i, j)),
        out_shape=jax.ShapeDtypeStruct((t, n), BF16),
        scratch_shapes=[pltpu.VMEM((tm, d), BF16)],
        compiler_params=_params(("parallel", "arbitrary")),
        name="inproj",
    )(x2d, g.reshape(1, d), sc, sh, w_bf)


def _hg_steps(chains):
    c = HG_CHUNK
    r = lax.broadcasted_iota(I32, (c, c), 0)
    s = lax.broadcasted_iota(I32, (c, c), 1)
    geo = {False: (r >= s, c // 2 - 1, c - 1), True: (r <= s, c // 2, 0)}
    tri = {rev: jnp.where(g[0], 1.0, 0.0).astype(BF16) for rev, g in geo.items()}

    work = []
    for ch in chains:
        lb = ch["lb"]
        sig = jax.nn.sigmoid(ch["fr"])
        lf = jnp.log(lb + (1.0 - lb) * sig)
        k = (1.0 - lb) * (1.0 - sig)
        hi = lf.astype(BF16)
        lo = (lf - hi.astype(F32)).astype(BF16)
        t = tri[ch["rev"]]
        work.append(dict(k=k, bc=_dot(t, hi) + _dot(t, lo)))
    for ch, w in zip(chains, work):
        mask, mid, last = geo[ch["rev"]]
        bc = w["bc"]
        b_mid = bc[mid:mid + 1, :]
        b_last = bc[last:last + 1, :]
        km = w["k"] * jnp.exp(b_mid - bc)
        kd = (km * jnp.exp(b_last - b_mid)).astype(BF16)
        w["ut"] = _dot_tn(ch["v"], kd)
        w["decay"] = jnp.exp(b_last)
        if ch["q"] is not None:
            qm = ch["q"] * jnp.exp(bc - b_mid)
            w["att"] = _dot_nt(qm.astype(BF16), km.astype(BF16))
            qe = (qm * jnp.exp(b_mid)).astype(BF16)
            w["inter"] = _dot_nt(qe, ch["st"].astype(BF16))
    out = []
    for ch, w in zip(chains, work):
        o = None
        if ch["q"] is not None:
            att = jnp.where(geo[ch["rev"]][0], w["att"], 0.0).astype(BF16)
            o = _dot(att, ch["v"]) + w["inter"]
        out.append((o, ch["st"] * w["decay"] + w["ut"]))
    return out


def _hgrn_kernel(q_ref, ff_ref, fb_ref, i_ref, g_ref, cff_ref, cfb_ref, ci_ref,
                 lbf_ref, lbb_ref, ng_ref, o_ref, qs_scr, of_scr, ob_scr, st_scr):
    seq = q_ref.shape[0]
    ctx = cff_ref.shape[0]
    c = HG_CHUNK
    dh = HG_HEAD_DIM
    rb = min(seq, 256)

    for i in range(seq // rb):
        rows = slice(i * rb, (i + 1) * rb)
        qs_scr[rows, :] = (_silu(q_ref[rows, :].astype(F32)) * HG_SCALE).astype(BF16)
    st_scr[...] = jnp.zeros_like(st_scr)

    def chains(nchunks, ffr, fbr, vr, with_q):
        def body(n, carry):
            chains = []
            for h in range(HG_HEADS):
                cols = slice(h * dh, (h + 1) * dh)
                for rev, fref, lbref in ((False, ffr, lbf_ref), (True, fbr, lbb_ref)):
                    ci = (nchunks - 1 - n) if rev else n
                    rows = pl.ds(pl.multiple_of(ci * c, c), c)
                    chains.append(dict(
                        rev=rev, rows=rows, cols=cols, fr=fref[rows, cols].astype(F32), v=vr[rows, cols],
                        lb=lbref[:, cols], st=st_scr[len(chains)],
                        q=qs_scr[rows, cols].astype(F32) if with_q else None))
            for slot, (ch, (o, st)) in enumerate(zip(chains, _hg_steps(chains))):
                st_scr[slot] = st
                if with_q:
                    (ob_scr if ch["rev"] else of_scr)[ch["rows"], ch["cols"]] = o
            return carry

        lax.fori_loop(0, nchunks, body, 0)

    chains(ctx // c, cff_ref, cfb_ref, ci_ref, False)
    chains(seq // c, ff_ref, fb_ref, i_ref, True)

    ng = ng_ref[...]
    for i in range(seq // rb):
        rows = slice(i * rb, (i + 1) * rb)
        gate = _silu(g_ref[rows, :].astype(F32))
        for h in range(HG_HEADS):
            cols = slice(h * dh, (h + 1) * dh)
            o = of_scr[rows, cols] + ob_scr[rows, cols]
            on = o * lax.rsqrt(jnp.mean(o * o, axis=-1, keepdims=True) + NORM_EPS) * ng
            o_ref[rows, cols] = (on * gate[:, cols]).astype(o_ref.dtype)


def _hgrn(px3, pc3, lb_f, lb_b, norm_g):
    b, seq, _ = px3.shape
    ctx = pc3.shape[1]
    dh = HG_HEAD_DIM
    base = 3 * HY_W // HG_W

    def xspec(j):
        return pl.BlockSpec((None, seq, HG_W), lambda bi: (bi, 0, base + j))

    def cspec(j):
        return pl.BlockSpec((None, ctx, HG_W), lambda bi: (bi, 0, j))

    vec = pl.BlockSpec((1, HG_W), lambda bi: (0, 0))
    return pl.pallas_call(
        _hgrn_kernel,
        grid=(b,),
        in_specs=[xspec(0), xspec(1), xspec(2), xspec(3), xspec(4), cspec(0), cspec(1), cspec(2),
                  vec, vec, pl.BlockSpec((1, dh), lambda bi: (0, 0))],
        out_specs=pl.BlockSpec((None, seq, HG_W), lambda bi: (bi, 0, 0)),
        out_shape=jax.ShapeDtypeStruct((b, seq, HG_W), BF16),
        scratch_shapes=[pltpu.VMEM((seq, HG_W), BF16), pltpu.VMEM((seq, HG_W), F32),
                        pltpu.VMEM((seq, HG_W), F32), pltpu.VMEM((2 * HG_HEADS, dh, dh), F32)],
        compiler_params=_params(("parallel",)),
        name="hgrn",
    )(px3, px3, px3, px3, px3, pc3, pc3, pc3, lb_f.reshape(1, HG_W), lb_b.reshape(1, HG_W),
      norm_g.reshape(1, dh))


def _dft_mats(seq):
    f = jnp.arange(seq, dtype=I32)
    m = (f[:, None] * f[None, :]) % (2 * seq)
    ang = m.astype(F32) * (math.pi / seq)
    return jnp.cos(ang).astype(BF16), jnp.sin(ang).astype(BF16)


def _hyfilt_kernel(feat_ref, w1_ref, b1_ref, w2_ref, b2_ref, fr_ref, w3_ref, t_ref, dl_ref,
                   a_ref, s_ref, u1_ref, u2_ref, un_ref):
    seq = feat_ref.shape[0]
    fr = fr_ref[...]
    h = jnp.sin(fr * (jnp.dot(feat_ref[...], w1_ref[...], preferred_element_type=F32, precision=HIGHEST)
                      + b1_ref[...]))
    h = jnp.sin(fr * (jnp.dot(h, w2_ref[...], preferred_element_type=F32, precision=HIGHEST) + b2_ref[...]))
    window = jnp.exp(-t_ref[...] * dl_ref[...])
    row = lax.broadcasted_iota(I32, (seq, 1), 0)
    sgn = jnp.where(row % 2 == 0, 1.0, -1.0)
    cf = jnp.where(row == 0, 1.0, 2.0) * (1.0 / (2 * seq))
    for o in range(2):
        w3 = w3_ref[:, o * 2 * HY_W:(o + 1) * 2 * HY_W]
        ho = jnp.dot(h, w3, preferred_element_type=F32, precision=HIGHEST)
        fwd = ho[:, :HY_W] * window
        bwd = ho[:, HY_W:] * window
        norm = (jnp.sum(jnp.abs(fwd), axis=0, keepdims=True)
                + jnp.sum(jnp.abs(bwd), axis=0, keepdims=True))
        inv = 1.0 / norm
        ksum = (fwd + bwd) * inv
        kdif = (bwd - fwd) * inv
        kr = _dot(a_ref[...], ksum.astype(BF16))
        ki = _dot(s_ref[...], kdif.astype(BF16))
        u1_ref[o] = (kr * cf).astype(u1_ref.dtype)
        u2_ref[o] = (ki * cf).astype(u2_ref.dtype)
        un_ref[o] = jnp.sum(sgn * ksum, axis=0, keepdims=True) * (1.0 / (2 * seq))


def _hyena_filters(seq, fw1, fb1, fw2, fb2, fw3, freq, a_mat, s_mat):
    pos = jnp.arange(seq, dtype=F32)[:, None]
    t = pos / max(seq - 1, 1)
    w = (2.0 * math.pi / seq) * pos
    bands = jnp.linspace(1e-4, HY_BANDS - 1, HY_BANDS, dtype=F32)[None, :]
    feats = jnp.concatenate([t, jnp.cos(bands * w), -jnp.sin(bands * w)], axis=-1)
    feats = jnp.pad(feats, ((0, 0), (0, LANES - HY_EMB)))
    w1 = jnp.pad(fw1, ((0, LANES - HY_EMB), (0, 0)))
    max_decay = math.log(HY_DECAY_TARGET) / HY_FAST_DECAY_PCT
    min_decay = math.log(HY_DECAY_TARGET) / HY_SLOW_DECAY_PCT
    deltas = jnp.abs(jnp.linspace(min_decay, max_decay, HY_W, dtype=F32))[None, :]
    hid = fw2.shape[0]
    return pl.pallas_call(
        _hyfilt_kernel,
        out_shape=(jax.ShapeDtypeStruct((2, seq, HY_W), BF16),
                   jax.ShapeDtypeStruct((2, seq, HY_W), BF16),
                   jax.ShapeDtypeStruct((2, 1, HY_W), F32)),
        compiler_params=pltpu.CompilerParams(vmem_limit_bytes=VMEM_LIMIT_BYTES),
        name="hyfilt",
    )(feats, w1, fb1.reshape(1, hid), fw2, fb2.reshape(1, hid), freq.reshape(1, hid), fw3, t, deltas,
      a_mat, s_mat)


def _hyena_kernel(x1_ref, x2_ref, v_ref, w1_ref, w2_ref, wv_ref, b1_ref, b2_ref, bv_ref,
                  u1_ref, u2_ref, un_ref, d_ref, a_ref, s_ref, o_ref, z_scr, zb_scr, re_scr, im_scr):
    seq = x1_ref.shape[0]
    rb = min(seq, 512)
    nrb = seq // rb
    row = lax.broadcasted_iota(I32, (rb, 1), 0)
    col = row % GRID_W
    first = col == 0
    lastc = col == GRID_W - 1
    sgn = jnp.where(row % 2 == 0, 1.0, -1.0)

    def conv3(p_ref, w_ref, b_ref, rows):
        p = p_ref[rows, :].astype(F32)
        prev = jnp.where(first, 0.0, pltpu.roll(p, 1, axis=0))
        nxt = jnp.where(lastc, 0.0, pltpu.roll(p, rb - 1, axis=0))
        w = w_ref[...]
        return w[0:1, :] * prev + w[1:2, :] * p + w[2:3, :] * nxt + b_ref[...]

    def forward(o):
        nyq = None
        for i in range(nrb):
            rows = slice(i * rb, (i + 1) * rb)
            part = jnp.sum(sgn * z_scr[rows, :], axis=0, keepdims=True)
            nyq = part if nyq is None else nyq + part
        for i in range(nrb):
            rows = slice(i * rb, (i + 1) * rb)
            p = _dot(a_ref[rows, :], zb_scr[...])
            q = _dot(s_ref[rows, :], zb_scr[...])
            u1 = u1_ref[o, rows, :].astype(F32)
            u2 = u2_ref[o, rows, :].astype(F32)
            re_scr[rows, :] = (p * u1 + q * u2).astype(BF16)
            im_scr[rows, :] = (q * u1 - p * u2).astype(BF16)
        return nyq * un_ref[o]

    def inverse(o, nyq, rows):
        y = _dot(a_ref[rows, :], re_scr[...]) + _dot(s_ref[rows, :], im_scr[...]) + sgn * nyq
        return y + z_scr[rows, :] * d_ref[o:o + 1, :]

    for i in range(nrb):
        rows = slice(i * rb, (i + 1) * rb)
        v = conv3(v_ref, wv_ref, bv_ref, rows)
        z_scr[rows, :] = v
        zb_scr[rows, :] = v.astype(BF16)
    nyq = forward(0)
    for i in range(nrb):
        rows = slice(i * rb, (i + 1) * rb)
        z = conv3(x1_ref, w1_ref, b1_ref, rows) * inverse(0, nyq, rows)
        z_scr[rows, :] = z
        zb_scr[rows, :] = z.astype(BF16)
    nyq = forward(1)
    for i in range(nrb):
        rows = slice(i * rb, (i + 1) * rb)
        y = conv3(x2_ref, w2_ref, b2_ref, rows) * inverse(1, nyq, rows)
        o_ref[rows, :] = y.astype(o_ref.dtype)


def _hyena(px3, conv_w, conv_b, u1, u2, un, d_skip, a_mat, s_mat):
    b, seq, _ = px3.shape
    ct = HY_CT
    nc = HY_W // ct

    def xspec(j):
        return pl.BlockSpec((None, seq, ct), lambda c, bi: (bi, 0, j * nc + c))

    def wspec(j, rows):
        return pl.BlockSpec((rows, ct), lambda c, bi: (0, j * nc + c))

    uspec = pl.BlockSpec((2, seq, ct), lambda c, bi: (0, 0, c))
    const = pl.BlockSpec((seq, seq), lambda c, bi: (0, 0), pipeline_mode=pl.Buffered(1))
    cb = conv_b.reshape(1, 3 * HY_W)
    return pl.pallas_call(
        _hyena_kernel,
        grid=(nc, b),
        in_specs=[xspec(0), xspec(1), xspec(2), wspec(0, 3), wspec(1, 3), wspec(2, 3),
                  wspec(0, 1), wspec(1, 1), wspec(2, 1), uspec, uspec,
                  pl.BlockSpec((2, 1, ct), lambda c, bi: (0, 0, c)),
                  pl.BlockSpec((2, ct), lambda c, bi: (0, c)), const, const],
        out_specs=pl.BlockSpec((None, seq, ct), lambda c, bi: (bi, 0, c)),
        out_shape=jax.ShapeDtypeStruct((b, seq, HY_W), BF16),
        scratch_shapes=[pltpu.VMEM((seq, ct), F32), pltpu.VMEM((seq, ct), BF16),
                        pltpu.VMEM((seq, ct), BF16), pltpu.VMEM((seq, ct), BF16)],
        compiler_params=_params(("parallel", "parallel")),
        name="hyena",
    )(px3, px3, px3, conv_w, conv_w, conv_w, cb, cb, cb, u1, u2, un, d_skip, a_mat, s_mat)


def _select_rows(rows):
    n = rows[0].shape[1]
    idx = lax.broadcasted_iota(I32, (len(rows), n), 0)
    out = jnp.zeros((len(rows), n), rows[0].dtype)
    for k, r in enumerate(rows):
        out = jnp.where(idx == k, r, out)
    return out


def _first_argmax(x, iota, size):
    m = jnp.max(x, axis=0, keepdims=True)
    return jnp.min(jnp.where(x == m, iota, size), axis=0, keepdims=True)


def _mid_kernel(yhy_ref, yhg_ref, x_ref, woa_ref, wob_ref, g1_ref, sc_ref, sh_ref, g2_ref, ng_ref,
                wr_ref, rb_ref, swg_ref, swu_ref, swd_ref,
                xs_ref, h_ref, lrow_ref, w_ref, ce_ref, rel_ref, nch_ref, cnt_ref, base):
    tm = x_ref.shape[0]
    ne = N_EXPERTS

    @pl.when(pl.program_id(0) == 0)
    def _():
        base[...] = jnp.zeros_like(base)

    mix = _dot(yhy_ref[...], woa_ref[...]) + _dot(yhg_ref[...], wob_ref[...])
    xm = x_ref[...] + g1_ref[...] * mix
    h = _rms_mod(xm, ng_ref[...], sc_ref[...], sh_ref[...])
    hb = h.astype(BF16)
    h_ref[...] = hb
    act = (_silu(_dot(hb, swg_ref[...])) * _dot(hb, swu_ref[...])).astype(BF16)
    xs_ref[...] = xm + g2_ref[...] * _dot(act, swd_ref[...])

    logits = lax.dot_general(wr_ref[...], h, (((1,), (1,)), ((), ())), preferred_element_type=F32,
                             precision=HIGHEST)
    scores = jax.nn.sigmoid(logits)
    biased = scores + rb_ref[...]
    neg = -jnp.inf
    iota_g = lax.broadcasted_iota(I32, (GROUP_SIZE, tm), 0)
    grp = []
    for g in range(N_GROUPS):
        blk = biased[g * GROUP_SIZE:(g + 1) * GROUP_SIZE, :]
        m1 = jnp.max(blk, axis=0, keepdims=True)
        i1 = jnp.min(jnp.where(blk == m1, iota_g, GROUP_SIZE), axis=0, keepdims=True)
        m2 = jnp.max(jnp.where(iota_g == i1, neg, blk), axis=0, keepdims=True)
        grp.append(m1 + m2)
    gsc = _select_rows(grp)
    iota8 = lax.broadcasted_iota(I32, (N_GROUPS, tm), 0)
    gsel = iota8 < 0
    for _ in range(TOPK_GROUPS):
        hit = iota8 == _first_argmax(gsc, iota8, N_GROUPS)
        gsel = gsel | hit
        gsc = jnp.where(hit, neg, gsc)
    gself = jnp.where(gsel, 1.0, 0.0)
    iota_e = lax.broadcasted_iota(I32, (ne, tm), 0)
    gid = iota_e // GROUP_SIZE
    emask = jnp.zeros((ne, tm), F32)
    for g in range(N_GROUPS):
        emask = jnp.where(gid == g, gself[g:g + 1, :], emask)
    cur = jnp.where(emask > 0.0, biased, neg)

    sel = iota_e < 0
    idxs, wts = [], []
    for _ in range(TOP_K):
        idx = _first_argmax(cur, iota_e, ne)
        hit = iota_e == idx
        wts.append(jnp.sum(jnp.where(hit, scores, 0.0), axis=0, keepdims=True))
        idxs.append(idx)
        sel = sel | hit
        cur = jnp.where(hit, neg, cur)
    wsum = wts[0]
    for wk in wts[1:]:
        wsum = wsum + wk
    scale = ROUTED_SCALE / wsum

    r = lax.broadcasted_iota(I32, (tm, tm), 0)
    s = lax.broadcasted_iota(I32, (tm, tm), 1)
    upper = jnp.where(r < s, 1.0, 0.0).astype(BF16)
    sel_f = jnp.where(sel, 1.0, 0.0)
    prefix = _dot(sel_f.astype(BF16), upper)
    cnt = jnp.sum(sel_f, axis=1, keepdims=True)
    cnt_al = jnp.floor((cnt + (SLOT_ALIGN - 1)) * (1.0 / SLOT_ALIGN)) * SLOT_ALIGN
    re = lax.broadcasted_iota(I32, (ne, ne), 0)
    ce = lax.broadcasted_iota(I32, (ne, ne), 1)
    lower = jnp.where(ce < re, 1.0, 0.0).astype(BF16)
    loc = _dot(lower, jnp.broadcast_to(cnt_al, (ne, LANES)).astype(BF16))[:, 0:1]
    lrow_all = loc + prefix
    lrows = [jnp.sum(jnp.where(iota_e == idx, lrow_all, 0.0), axis=0, keepdims=True) for idx in idxs]

    nc = ce_ref.shape[-1]
    jrow = lax.broadcasted_iota(I32, (1, nc), 1).astype(F32) * SLOT_ALIGN
    owner = jnp.sum(jnp.where(loc + cnt_al <= jrow, 1.0, 0.0), axis=0, keepdims=True)
    owner = jnp.minimum(owner, ne - 1.0).astype(I32)
    iota_ec = lax.broadcasted_iota(I32, (ne, nc), 0)
    rel = jnp.sum(jnp.where(iota_ec == owner, base[...] - loc, 0.0), axis=0, keepdims=True) + jrow
    base[...] = base[...] + cnt_al
    used = jnp.sum(cnt_al, axis=0, keepdims=True)
    spare = jnp.minimum(jrow - used, (MOE_ROW_CHUNK - SLOT_ALIGN) * 1.0)
    spare = spare + (pl.program_id(0) % 2).astype(F32) * MOE_ROW_CHUNK
    unused = jrow >= used
    owner = jnp.where(unused, ne, owner)
    rel = jnp.where(unused, spare, rel)

    lrow_ref[...] = _select_rows(lrows).astype(I32)
    w_ref[...] = _select_rows([wk * scale for wk in wts])
    ce_ref[...] = owner
    rel_ref[...] = rel.astype(I32)
    nch = used * (1.0 / SLOT_ALIGN)
    nch_ref[...] = jnp.broadcast_to(nch, nch_ref.shape).astype(I32)
    cnt_ref[...] = base[...].astype(I32)


def _mid(yhy, yhg, x2d, seq, woa, wob, g1, sc2, sh2, g2, norm_g, wr_t, rbias, swg, swu, swd, tm):
    t, d = x2d.shape
    per = seq // tm
    ff = swg.shape[1]

    def full(shape):
        return pl.BlockSpec(shape, lambda i: (0,) * len(shape))

    mspec = pl.BlockSpec((None, 1, d), lambda i: (i // per, 0, 0))
    kt = pl.BlockSpec((TOP_K, tm), lambda i: (0, i))
    nt = t // tm

    def per_tile(n):
        return pl.BlockSpec((None, 1, n), lambda i: (i, 0, 0))

    return pl.pallas_call(
        _mid_kernel,
        grid=(nt,),
        in_specs=[pl.BlockSpec((tm, HY_W), lambda i: (i, 0)), pl.BlockSpec((tm, HG_W), lambda i: (i, 0)),
                  pl.BlockSpec((tm, d), lambda i: (i, 0)), full((HY_W, d)), full((HG_W, d)),
                  mspec, mspec, mspec, mspec, full((1, d)), full((N_EXPERTS, d)), full((N_EXPERTS, 1)),
                  full((d, ff)), full((d, ff)), full((ff, d))],
        out_specs=[pl.BlockSpec((tm, d), lambda i: (i, 0)), pl.BlockSpec((tm, d), lambda i: (i, 0)),
                   kt, kt, per_tile(MOE_CHUNKS), per_tile(MOE_CHUNKS), per_tile(LANES), full((N_EXPERTS, 1))],
        out_shape=(jax.ShapeDtypeStruct((t, d), F32), jax.ShapeDtypeStruct((t, d), BF16),
                   jax.ShapeDtypeStruct((TOP_K, t), I32), jax.ShapeDtypeStruct((TOP_K, t), F32),
                   jax.ShapeDtypeStruct((nt, 1, MOE_CHUNKS), I32), jax.ShapeDtypeStruct((nt, 1, MOE_CHUNKS), I32),
                   jax.ShapeDtypeStruct((nt, 1, LANES), I32), jax.ShapeDtypeStruct((N_EXPERTS, 1), I32)),
        scratch_shapes=[pltpu.VMEM((N_EXPERTS, 1), F32)],
        compiler_params=_params(("arbitrary",)),
        name="mid",
    )(yhy, yhg, x2d, woa, wob, g1, sc2, sh2, g2, norm_g.reshape(1, d), wr_t, rbias.reshape(N_EXPERTS, 1),
      swg, swu, swd)


def _pack_bf16_pairs(x):
    n = x.shape[1] // 2
    lo = lax.shift_right_logical(pltpu.bitcast(x[:, :n], I32), 16)
    hi = pltpu.bitcast(x[:, n:], I32) & -65536
    return hi | lo


def _unpack_bf16_pairs(p):
    lo = pltpu.bitcast(lax.shift_left(p, 16), F32).astype(BF16)
    hi = pltpu.bitcast(p & -65536, F32).astype(BF16)
    return lo, hi


def _row_chunks(nch):
    return (nch * SLOT_ALIGN + MOE_ROW_CHUNK - 1) // MOE_ROW_CHUNK


def _start_chunk_copies(c, ce_ref, rel_ref, ps_ref, make):
    per = MOE_ROW_CHUNK // SLOT_ALIGN
    for q in range(per):
        j = c * per + q
        glob = pl.multiple_of(ps_ref[ce_ref[j]] + rel_ref[j], SLOT_ALIGN)
        make(pl.multiple_of(j * SLOT_ALIGN, SLOT_ALIGN), glob).start()


def _wait_row_chunks(n, row_chunk_copy):
    def wait(_, carry):
        row_chunk_copy.wait()
        return carry

    lax.fori_loop(0, n, wait, 0)


def _dispatch_kernel(nch_ref, ce_ref, rel_ref, ps_ref, h_ref, lrow_ref, xs_hbm, xloc, zbuf, sems):
    tm = h_ref.shape[0]
    bm = zbuf.shape[0]
    rc = MOE_ROW_CHUNK
    sem = sems.at[0]

    @pl.when(pl.program_id(0) == 0)
    def _():
        zbuf[...] = jnp.zeros_like(zbuf)

        def zcopy(e):
            start = pl.multiple_of(ps_ref[e + 1] - bm, bm)
            return pltpu.make_async_copy(zbuf, xs_hbm.at[pl.ds(start, bm), :], sem)

        def start(e, carry):
            @pl.when(ps_ref[e + 1] > ps_ref[e])
            def _():
                zcopy(e).start()
            return carry

        def wait(e, carry):
            @pl.when(ps_ref[e + 1] > ps_ref[e])
            def _():
                zcopy(e).wait()
            return carry

        lax.fori_loop(0, N_EXPERTS, start, 0)
        lax.fori_loop(0, N_EXPERTS, wait, 0)

        def tcopy(j):
            return pltpu.make_async_copy(zbuf, xs_hbm.at[pl.ds(pl.multiple_of(j * bm, bm), bm), :], sem)

        def tstart(j, carry):
            tcopy(j).start()
            return carry

        def twait(j, carry):
            tcopy(j).wait()
            return carry

        first_tail = ps_ref[N_EXPERTS] // bm
        lax.fori_loop(first_tail, xs_hbm.shape[0] // bm, tstart, 0)
        lax.fori_loop(first_tail, xs_hbm.shape[0] // bm, twait, 0)

    i = pl.program_id(0)
    slot = i % 2
    buf = xloc.at[slot]
    n_rc = _row_chunks(nch_ref[i])
    lrow = lrow_ref[...].astype(F32)
    sub = rc // 2
    rid = lax.broadcasted_iota(I32, (sub, tm), 0).astype(F32).astype(BF16)
    one = jnp.ones((sub, tm), BF16)

    def select(r0):
        rel = (lrow - jnp.asarray(r0, I32).astype(F32)).astype(BF16)
        onehot = jnp.zeros((sub, tm), BF16)
        for k in range(TOP_K):
            onehot = jnp.where(rid == rel[k:k + 1, :], one, onehot)
        buf[pl.ds(r0, sub), :] = _pack_bf16_pairs(_dot(onehot, h_ref[...]))

    def send(c):
        _start_chunk_copies(c, ce_ref, rel_ref, ps_ref,
                            lambda loc, glob: pltpu.make_async_copy(buf.at[pl.ds(loc, SLOT_ALIGN), :],
                                                                    xs_hbm.at[pl.ds(glob, SLOT_ALIGN), :],
                                                                    sems.at[slot]))

    def retire(tile_slot, n):
        _wait_row_chunks(n, pltpu.make_async_copy(xloc.at[tile_slot, pl.ds(0, rc), :],
                                                  xs_hbm.at[pl.ds(0, rc), :], sems.at[tile_slot]))

    select(0)
    select(sub)

    def body(c, carry):
        send(c - 1)
        r0 = pl.multiple_of(c * rc, rc)
        select(r0)
        select(r0 + sub)
        return carry

    lax.fori_loop(1, n_rc, body, 0)
    send(n_rc - 1)

    @pl.when(i > 0)
    def _():
        retire(1 - slot, _row_chunks(nch_ref[i - 1]))

    @pl.when(i == pl.num_programs(0) - 1)
    def _():
        retire(slot, n_rc)


def _local_rows(tm):
    bound = TOP_K * tm + N_EXPERTS * (SLOT_ALIGN - 1)
    return -(-bound // MOE_ROW_CHUNK) * MOE_ROW_CHUNK


def _moe_smem_specs():
    chunk_list = pl.BlockSpec((MOE_CHUNKS,), lambda i, nch: (i,), memory_space=pltpu.SMEM)
    return [chunk_list, chunk_list,
            pl.BlockSpec((N_EXPERTS + 1,), lambda i, nch: (0,), memory_space=pltpu.SMEM)]


def _dispatch(h2d, lrow_kt, ce, rel, nch, ps, n_slots, tm):
    t, d = h2d.shape
    grid_spec = pltpu.PrefetchScalarGridSpec(
        num_scalar_prefetch=1,
        grid=(t // tm,),
        in_specs=_moe_smem_specs() + [pl.BlockSpec((tm, d), lambda i, nch: (i, 0)),
                                      pl.BlockSpec((TOP_K, tm), lambda i, nch: (0, i))],
        out_specs=pl.BlockSpec(memory_space=pl.ANY),
        scratch_shapes=[pltpu.VMEM((2, _local_rows(tm), d // 2), I32), pltpu.VMEM((FFN_BLOCK, d // 2), I32),
                        pltpu.SemaphoreType.DMA((2,))],
    )

    return pl.pallas_call(
        _dispatch_kernel,
        grid_spec=grid_spec,
        out_shape=jax.ShapeDtypeStruct((n_slots, d // 2), I32),
        compiler_params=_params(("arbitrary",)),
        name="dispatch",
    )(nch, ce, rel, ps, h2d, lrow_kt)


def _ffn_kernel(ps_ref, wg_ref, wu_ref, wd_ref, x_hbm, y_hbm, xbuf, ybuf, wg_bf, wu_bf, wd_bf, xsem, ysem):
    e = pl.program_id(0)
    nx, bm, half = xbuf.shape
    ahead = nx - 1
    n_blocks = y_hbm.shape[0] // bm
    g0 = ps_ref[e] // bm
    g1 = ps_ref[e + 1] // bm

    def rows(g):
        return pl.ds(pl.multiple_of(g * bm, bm), bm)

    def x_copy(g):
        s = g % nx
        return pltpu.make_async_copy(x_hbm.at[rows(g), :], xbuf.at[s], xsem.at[s])

    def y_copy(g, s):
        return pltpu.make_async_copy(ybuf.at[s], y_hbm.at[rows(g), :], ysem.at[s])

    @pl.when(e == 0)
    def _():
        ybuf[...] = jnp.zeros_like(ybuf)
        for g in range(ahead):
            x_copy(g).start()
        y_copy(n_blocks - 2, 0).start()
        y_copy(n_blocks - 1, 1).start()

    @pl.when(g1 > g0)
    def _():
        wg_bf[...] = wg_ref[...].astype(BF16)
        wu_bf[...] = wu_ref[...].astype(BF16)
        wd_bf[...] = wd_ref[...].astype(BF16)

    def block(g, carry):
        s = g % 2
        xs = g % nx
        x_copy(g).wait()
        x_copy(g + ahead).start()
        hb = bm // 2
        halves = [pl.ds(p * hb, hb) for p in range(2)]
        gu = []
        for r in halves:
            lo, hi = _unpack_bf16_pairs(xbuf[xs, r, :])
            gu.append((_dot(lo, wg_bf[:half, :]) + _dot(hi, wg_bf[half:, :]),
                       _dot(lo, wu_bf[:half, :]) + _dot(hi, wu_bf[half:, :])))
        y_copy(g, s).wait()
        for r, (gate, up) in zip(halves, gu):
            act = (_silu(gate) * up).astype(BF16)
            y = _dot(act, wd_bf[...])
            ybuf[s, r, :] = _pack_bf16_pairs(y.astype(BF16).astype(F32))
        y_copy(g, s).start()
        return carry

    lax.fori_loop(g0, g1, block, 0)

    @pl.when(e == pl.num_programs(0) - 1)
    def _():
        for k in range(ahead):
            x_copy(g1 + k).wait()
        y_copy(0, 0).wait()
        y_copy(0, 1).wait()
        ybuf[0] = jnp.zeros((bm, half), ybuf.dtype)

        def zstart(g, carry):
            y_copy(g, 0).start()
            return carry

        def zwait(g, carry):
            y_copy(g, 0).wait()
            return carry

        lax.fori_loop(g1, n_blocks, zstart, 0)
        lax.fori_loop(g1, n_blocks, zwait, 0)


def _ffn(xs, ps, ew_gate, ew_up, ew_down):
    n_slots, half = xs.shape
    d = 2 * half
    bm = FFN_BLOCK
    n_exp, _, ff = ew_gate.shape
    grid_spec = pltpu.PrefetchScalarGridSpec(
        num_scalar_prefetch=1,
        grid=(n_exp,),
        in_specs=[pl.BlockSpec((None, d, ff), lambda e, ps: (e, 0, 0)),
                  pl.BlockSpec((None, d, ff), lambda e, ps: (e, 0, 0)),
                  pl.BlockSpec((None, ff, d), lambda e, ps: (e, 0, 0)),
                  pl.BlockSpec(memory_space=pl.ANY)],
        out_specs=pl.BlockSpec(memory_space=pl.ANY),
        scratch_shapes=[pltpu.VMEM((FFN_INPUT_BUFFERS, bm, half), I32), pltpu.VMEM((2, bm, half), I32),
                        pltpu.VMEM((d, ff), BF16), pltpu.VMEM((d, ff), BF16), pltpu.VMEM((ff, d), BF16),
                        pltpu.SemaphoreType.DMA((FFN_INPUT_BUFFERS,)), pltpu.SemaphoreType.DMA((2,))],
    )
    return pl.pallas_call(
        _ffn_kernel,
        grid_spec=grid_spec,
        out_shape=jax.ShapeDtypeStruct((n_slots, half), I32),
        compiler_params=_params(("arbitrary",)),
        name="ffn",
    )(ps, ew_gate, ew_up, ew_down, xs)


def _combine_kernel(nch_ref, ce_ref, rel_ref, cen_ref, reln_ref, ps_ref, lrow_ref, w_ref, xs_ref, g2_ref, fg_ref,
                    y_hbm, o_ref, yloc, acc, sems):
    tm = xs_ref.shape[0]
    half = yloc.shape[2]
    rc = MOE_ROW_CHUNK
    i = pl.program_id(0)
    last = pl.num_programs(0) - 1
    slot = i % 2
    nxt = jnp.minimum(i + 1, last)
    n_rc = _row_chunks(nch_ref[i])
    n_next = _row_chunks(nch_ref[nxt])

    def fetch(c, ce, rel, s):
        _start_chunk_copies(c, ce, rel, ps_ref,
                            lambda loc, glob: pltpu.make_async_copy(y_hbm.at[pl.ds(glob, SLOT_ALIGN), :],
                                                                    yloc.at[s, pl.ds(loc, SLOT_ALIGN), :],
                                                                    sems.at[s]))

    def arrived(s, n):
        _wait_row_chunks(n, pltpu.make_async_copy(y_hbm.at[pl.ds(0, rc), :], yloc.at[s, pl.ds(0, rc), :],
                                                  sems.at[s]))

    def fetch_own(c, carry):
        fetch(c, ce_ref, rel_ref, slot)
        return carry

    def fetch_next(c, carry):
        fetch(c, cen_ref, reln_ref, 1 - slot)
        return carry

    @pl.when(i == 0)
    def _():
        yloc[...] = jnp.zeros_like(yloc)
        lax.fori_loop(0, n_rc, fetch_own, 0)

    n_prev = _row_chunks(nch_ref[jnp.maximum(i - 1, 0)])
    arrived(slot, jnp.where(i == 0, n_rc, jnp.maximum(n_prev, n_rc)))

    lrow = lrow_ref[...].astype(F32)
    wrow = w_ref[...].astype(BF16)
    acc[...] = jnp.zeros_like(acc)
    sub = rc // 2
    rid = lax.broadcasted_iota(I32, (sub, tm), 0).astype(F32).astype(BF16)

    def weights(r0):
        rel = (lrow - r0.astype(F32)).astype(BF16)
        wt = jnp.zeros((sub, tm), BF16)
        for k in range(TOP_K):
            wt = jnp.where(rid == rel[k:k + 1, :], jnp.broadcast_to(wrow[k:k + 1, :], (sub, tm)), wt)
        return wt

    def gather(c, carry):
        fetch(c, cen_ref, reln_ref, 1 - slot)
        r0 = pl.multiple_of(c * rc, rc)
        wt_a = weights(r0)
        wt_b = weights(r0 + sub)
        lo_a, hi_a = _unpack_bf16_pairs(yloc[slot, pl.ds(r0, sub), :])
        lo_b, hi_b = _unpack_bf16_pairs(yloc[slot, pl.ds(r0 + sub, sub), :])
        acc[:, :half] += _dot_tn(wt_a, lo_a) + _dot_tn(wt_b, lo_b)
        acc[:, half:] += _dot_tn(wt_a, hi_a) + _dot_tn(wt_b, hi_b)
        return carry

    lax.fori_loop(0, n_rc, gather, 0)
    lax.fori_loop(n_rc, n_next, fetch_next, 0)

    @pl.when(i == last)
    def _():
        arrived(1 - slot, jnp.maximum(n_rc, n_next))

    x = xs_ref[...] + g2_ref[...] * acc[...]
    o_ref[...] = x * lax.rsqrt(jnp.mean(x * x, axis=-1, keepdims=True) + NORM_EPS) * fg_ref[...]


def _combine(y_sorted, lrow_kt, w_kt, ce, rel, nch, ps, xs2d, seq, g2, final_g, tm):
    t, d = xs2d.shape
    per = seq // tm
    nt = t // tm
    own, _, ranges = _moe_smem_specs()
    nxt = pl.BlockSpec((MOE_CHUNKS,), lambda i, nch: (jnp.minimum(i + 1, nt - 1),), memory_space=pltpu.SMEM)
    grid_spec = pltpu.PrefetchScalarGridSpec(
        num_scalar_prefetch=1,
        grid=(nt,),
        in_specs=[own, own, nxt, nxt, ranges,
                  pl.BlockSpec((TOP_K, tm), lambda i, nch: (0, i)),
                  pl.BlockSpec((TOP_K, tm), lambda i, nch: (0, i)),
                  pl.BlockSpec((tm, d), lambda i, nch: (i, 0)),
                  pl.BlockSpec((None, 1, d), lambda i, nch: (i // per, 0, 0)),
                  pl.BlockSpec((1, d), lambda i, nch: (0, 0)),
                  pl.BlockSpec(memory_space=pl.ANY)],
        out_specs=pl.BlockSpec((tm, d), lambda i, nch: (i, 0)),
        scratch_shapes=[pltpu.VMEM((2, _local_rows(tm), d // 2), I32), pltpu.VMEM((tm, d), F32),
                        pltpu.SemaphoreType.DMA((2,))],
    )
    return pl.pallas_call(
        _combine_kernel,
        grid_spec=grid_spec,
        out_shape=jax.ShapeDtypeStruct((t, d), F32),
        compiler_params=_params(("arbitrary",)),
        name="combine",
    )(nch, ce, rel, ce, rel, ps, lrow_kt, w_kt, xs2d, g2, final_g.reshape(1, d), y_sorted)


def kernel(x, c, ctx, c_ctx, w_mod, b_mod, norm1_g, norm2_g, w_in, w_out, hy_conv_w, hy_conv_b,
           hy_fw1, hy_fb1, hy_fw2, hy_fb2, hy_fw3, hy_freq, hy_d, hg_lb_logits, hg_norm_g,
           w_router, router_bias, ew_gate, ew_up, ew_down, sw_gate, sw_up, sw_down, final_g):
    b, seq, d = x.shape
    ctx_len = ctx.shape[1]
    t = b * seq
    layer = 0

    lower = jnp.cumsum(jax.nn.softmax(hg_lb_logits.astype(F32), axis=1), axis=1)
    lb_f, lb_b = lower[0, layer], lower[1, layer]

    rows = -(-(b + 1) // 8) * 8
    cc = jnp.concatenate([c, c_ctx[None, :], jnp.zeros((rows - b - 1, d), F32)], axis=0)
    mod = _modulation(cc, w_mod[layer], b_mod[layer])
    sh1, sc1, g1, sh2, sc2, g2 = (m.reshape(b, 1, d) for m in jnp.split(mod[:b], 6, axis=-1))
    csh1, csc1 = (jnp.broadcast_to(m.reshape(1, 1, d), (b, 1, d))
                  for m in jnp.split(mod[b:b + 1], 6, axis=-1)[:2])

    w_in_bf = w_in[layer].astype(BF16)
    x2d = x.reshape(t, d)
    px = _inproj(x2d, seq, norm1_g[layer], sc1, sh1, w_in_bf, min(seq, 512), w_in_bf.shape[1])
    lo = 3 * HY_W + HG_W
    pc = _inproj(ctx.reshape(b * ctx_len, d), ctx_len, norm1_g[layer], csc1, csh1,
                 w_in_bf[:, lo:lo + 3 * HG_W], ctx_len, HG_W)
    px3 = px.reshape(b, seq, -1)
    pc3 = pc.reshape(b, ctx_len, -1)

    y_hg = _hgrn(px3, pc3, lb_f, lb_b, hg_norm_g[layer])

    a_mat, s_mat = _dft_mats(seq)
    u1, u2, un = _hyena_filters(seq, hy_fw1[layer], hy_fb1[layer], hy_fw2[layer], hy_fb2[layer],
                                hy_fw3[layer], hy_freq[layer], a_mat, s_mat)
    y_hy = _hyena(px3, hy_conv_w[layer], hy_conv_b[layer], u1, u2, un, hy_d[layer], a_mat, s_mat)

    w_out_bf = w_out[layer].astype(BF16)
    tm = min(seq, MOE_TILE)
    nt = t // tm
    xs, h2, lrow_kt, w_kt, ce, rel, nch, counts = _mid(
        y_hy.reshape(t, HY_W), y_hg.reshape(t, HG_W), x2d, seq, w_out_bf[:HY_W], w_out_bf[HY_W:],
        g1, sc2, sh2, g2, norm2_g[layer], w_router[layer].T, router_bias[layer],
        sw_gate[layer].astype(BF16), sw_up[layer].astype(BF16), sw_down[layer].astype(BF16), tm)

    bm = FFN_BLOCK
    counts = counts.reshape(N_EXPERTS)
    padded = (counts + bm - 1) // bm * bm
    p_ends = jnp.cumsum(padded)
    ps = jnp.concatenate([p_ends - padded, p_ends[-1:]]).astype(I32)
    spare_blocks = max(-(-2 * MOE_ROW_CHUNK // bm), FFN_INPUT_BUFFERS - 1)
    n_blocks = -(-(t * TOP_K + nt * N_EXPERTS * (SLOT_ALIGN - 1)) // bm) + N_EXPERTS + spare_blocks

    ce, rel, nch = ce.reshape(-1), rel.reshape(-1), nch[:, 0, 0]
    x_sorted = _dispatch(h2, lrow_kt, ce, rel, nch, ps, n_blocks * bm, tm)
    y_sorted = _ffn(x_sorted, ps, ew_gate[layer], ew_up[layer], ew_down[layer])
    out = _combine(y_sorted, lrow_kt, w_kt, ce, rel, nch, ps, xs, seq, g2, final_g, tm)
    return out.reshape(b, seq, d)
```

```python
import functools
import math

import jax
import jax.numpy as jnp
from jax import lax
from jax.experimental import pallas as pl
from jax.experimental.pallas import tpu as pltpu

F32 = jnp.float32
BF16 = jnp.bfloat16
I32 = jnp.int32
HIGHEST = lax.Precision.HIGHEST

GRID_W = 64
HY_W = 512
HG_W = 512
HY_EMB = 33
HY_BANDS = 16
HY_DECAY_TARGET = 1e-2
HY_FAST_DECAY_PCT = 0.3
HY_SLOW_DECAY_PCT = 1.5
HG_HEAD_DIM = 128
HG_HEADS = 4
HG_SCALE = HG_HEAD_DIM ** -0.5
HG_CHUNK = 64
N_EXPERTS = 256
TOP_K = 8
N_GROUPS = 8
TOPK_GROUPS = 4
GROUP_SIZE = N_EXPERTS // N_GROUPS
ROUTED_SCALE = 2.5
NORM_EPS = 1e-6

VMEM_LIMIT_BYTES = 56 * 1024 * 1024
LANES = 128
FFN_BLOCK = 512
FFN_BUFFERS = 4
MOE_TILE = 512
SLOT_ALIGN = 8
MOE_ROW_CHUNK = 512
MOE_CHUNKS = 1024
HY_CT = 256


def _params(sem, vmem=VMEM_LIMIT_BYTES):
    return pltpu.CompilerParams(dimension_semantics=sem, vmem_limit_bytes=vmem)


def _silu(x):
    return x * jax.nn.sigmoid(x)


def _dot(a, b):
    return jnp.dot(a, b, preferred_element_type=F32)


def _dot_nt(a, b):
    return lax.dot_general(a, b, (((1,), (1,)), ((), ())), preferred_element_type=F32)


def _dot_tn(a, b):
    return lax.dot_general(a, b, (((0,), (0,)), ((), ())), preferred_element_type=F32)


def _mod_kernel(c_ref, w_ref, b_ref, o_ref):
    s = _silu(c_ref[...])
    o_ref[...] = jnp.dot(s, w_ref[...], preferred_element_type=F32, precision=HIGHEST) + b_ref[...]


def _modulation(cc, w_mod, b_mod):
    rows, d = cc.shape
    n = w_mod.shape[1]
    tn = 1024
    return pl.pallas_call(
        _mod_kernel,
        grid=(n // tn,),
        in_specs=[pl.BlockSpec((rows, d), lambda j: (0, 0)),
                  pl.BlockSpec((d, tn), lambda j: (0, j)),
                  pl.BlockSpec((1, tn), lambda j: (0, j))],
        out_specs=pl.BlockSpec((rows, tn), lambda j: (0, j)),
        out_shape=jax.ShapeDtypeStruct((rows, n), F32),
        compiler_params=_params(("parallel",)),
        name="mod",
    )(cc, w_mod, b_mod.reshape(1, n))


def _rms_mod(x, g, sc, sh):
    y = x * lax.rsqrt(jnp.mean(x * x, axis=-1, keepdims=True) + NORM_EPS) * g
    return y * (1.0 + sc) + sh


def _inproj_kernel(x_ref, g_ref, sc_ref, sh_ref, w_ref, o_ref, h_scr):
    @pl.when(pl.program_id(1) == 0)
    def _():
        h_scr[...] = _rms_mod(x_ref[...], g_ref[...], sc_ref[...], sh_ref[...]).astype(BF16)

    o_ref[...] = _dot(h_scr[...], w_ref[...]).astype(o_ref.dtype)


def _inproj(x2d, seq, g, sc, sh, w_bf, tm, tn):
    t, d = x2d.shape
    n = w_bf.shape[1]
    per = seq // tm
    w_mode = dict(pipeline_mode=pl.Buffered(1)) if tn == n else {}
    return pl.pallas_call(
        _inproj_kernel,
        grid=(t // tm, n // tn),
        in_specs=[pl.BlockSpec((tm, d), lambda i, j: (i, 0)),
                  pl.BlockSpec((1, d), lambda i, j: (0, 0)),
                  pl.BlockSpec((None, 1, d), lambda i, j: (i // per, 0, 0)),
                  pl.BlockSpec((None, 1, d), lambda i, j: (i // per, 0, 0)),
                  pl.BlockSpec((d, tn), lambda i, j: (0, j), **w_mode)],
        out_specs=pl.BlockSpec((tm, tn), lambda i, j: (i, j)),
        out_shape=jax.ShapeDtypeStruct((t, n), BF16),
        scratch_shapes=[pltpu.VMEM((tm, d), BF16)],
        compiler_params=_params(("parallel", "arbitrary")),
        name="inproj",
    )(x2d, g.reshape(1, d), sc, sh, w_bf)


def _hg_steps(chains):
    c = HG_CHUNK
    r = lax.broadcasted_iota(I32, (c, c), 0)
    s = lax.broadcasted_iota(I32, (c, c), 1)
    geo = {False: (r >= s, c // 2 - 1, c - 1), True: (r <= s, c // 2, 0)}
    tri = {rev: jnp.where(g[0], 1.0, 0.0).astype(BF16) for rev, g in geo.items()}

    work = []
    for ch in chains:
        lb = ch["lb"]
        sig = jax.nn.sigmoid(ch["fr"])
        lf = jnp.log(lb + (1.0 - lb) * sig)
        k = (1.0 - lb) * (1.0 - sig)
        hi = lf.astype(BF16)
        lo = (lf - hi.astype(F32)).astype(BF16)
        t = tri[ch["rev"]]
        work.append(dict(k=k, bc=_dot(t, hi) + _dot(t, lo)))
    for ch, w in zip(chains, work):
        mask, mid, last = geo[ch["rev"]]
        bc = w["bc"]
        b_mid = bc[mid:mid + 1, :]
        b_last = bc[last:last + 1, :]
        km = w["k"] * jnp.exp(b_mid - bc)
        kd = (km * jnp.exp(b_last - b_mid)).astype(BF16)
        w["ut"] = _dot_tn(ch["v"], kd)
        w["decay"] = jnp.exp(b_last)
        if ch["q"] is not None:
            qm = ch["q"] * jnp.exp(bc - b_mid)
            w["att"] = _dot_nt(qm.astype(BF16), km.astype(BF16))
            qe = (qm * jnp.exp(b_mid)).astype(BF16)
            w["inter"] = _dot_nt(qe, ch["st"].astype(BF16))
    out = []
    for ch, w in zip(chains, work):
        o = None
        if ch["q"] is not None:
            att = jnp.where(geo[ch["rev"]][0], w["att"], 0.0).astype(BF16)
            o = _dot(att, ch["v"]) + w["inter"]
        out.append((o, ch["st"] * w["decay"] + w["ut"]))
    return out


def _hgrn_kernel(q_ref, ff_ref, fb_ref, i_ref, g_ref, cff_ref, cfb_ref, ci_ref,
                 lbf_ref, lbb_ref, ng_ref, o_ref, qs_scr, of_scr, ob_scr, st_scr):
    seq = q_ref.shape[0]
    ctx = cff_ref.shape[0]
    c = HG_CHUNK
    dh = HG_HEAD_DIM
    rb = min(seq, 256)

    for i in range(seq // rb):
        rows = slice(i * rb, (i + 1) * rb)
        qs_scr[rows, :] = (_silu(q_ref[rows, :].astype(F32)) * HG_SCALE).astype(BF16)
    st_scr[...] = jnp.zeros_like(st_scr)

    def chains(nchunks, ffr, fbr, vr, with_q):
        def body(n, carry):
            chains = []
            for h in range(HG_HEADS):
                cols = slice(h * dh, (h + 1) * dh)
                for rev, fref, lbref in ((False, ffr, lbf_ref), (True, fbr, lbb_ref)):
                    ci = (nchunks - 1 - n) if rev else n
                    rows = pl.ds(pl.multiple_of(ci * c, c), c)
                    chains.append(dict(
                        rev=rev, rows=rows, cols=cols, fr=fref[rows, cols].astype(F32), v=vr[rows, cols],
                        lb=lbref[:, cols], st=st_scr[len(chains)],
                        q=qs_scr[rows, cols].astype(F32) if with_q else None))
            for slot, (ch, (o, st)) in enumerate(zip(chains, _hg_steps(chains))):
                st_scr[slot] = st
                if with_q:
                    (ob_scr if ch["rev"] else of_scr)[ch["rows"], ch["cols"]] = o
            return carry

        lax.fori_loop(0, nchunks, body, 0)

    chains(ctx // c, cff_ref, cfb_ref, ci_ref, False)
    chains(seq // c, ff_ref, fb_ref, i_ref, True)

    ng = ng_ref[...]
    for i in range(seq // rb):
        rows = slice(i * rb, (i + 1) * rb)
        gate = _silu(g_ref[rows, :].astype(F32))
        for h in range(HG_HEADS):
            cols = slice(h * dh, (h + 1) * dh)
            o = of_scr[rows, cols] + ob_scr[rows, cols]
            on = o * lax.rsqrt(jnp.mean(o * o, axis=-1, keepdims=True) + NORM_EPS) * ng
            o_ref[rows, cols] = (on * gate[:, cols]).astype(o_ref.dtype)


def _hgrn(px3, pc3, lb_f, lb_b, norm_g):
    b, seq, _ = px3.shape
    ctx = pc3.shape[1]
    dh = HG_HEAD_DIM
    base = 3 * HY_W // HG_W

    def xspec(j):
        return pl.BlockSpec((None, seq, HG_W), lambda bi: (bi, 0, base + j))

    def cspec(j):
        return pl.BlockSpec((None, ctx, HG_W), lambda bi: (bi, 0, j))

    vec = pl.BlockSpec((1, HG_W), lambda bi: (0, 0))
    return pl.pallas_call(
        _hgrn_kernel,
        grid=(b,),
        in_specs=[xspec(0), xspec(1), xspec(2), xspec(3), xspec(4), cspec(0), cspec(1), cspec(2),
                  vec, vec, pl.BlockSpec((1, dh), lambda bi: (0, 0))],
        out_specs=pl.BlockSpec((None, seq, HG_W), lambda bi: (bi, 0, 0)),
        out_shape=jax.ShapeDtypeStruct((b, seq, HG_W), BF16),
        scratch_shapes=[pltpu.VMEM((seq, HG_W), BF16), pltpu.VMEM((seq, HG_W), F32),
                        pltpu.VMEM((seq, HG_W), F32), pltpu.VMEM((2 * HG_HEADS, dh, dh), F32)],
        compiler_params=_params(("parallel",)),
        name="hgrn",
    )(px3, px3, px3, px3, px3, pc3, pc3, pc3, lb_f.reshape(1, HG_W), lb_b.reshape(1, HG_W),
      norm_g.reshape(1, dh))


def _dft_mats(seq):
    f = jnp.arange(seq, dtype=I32)
    m = (f[:, None] * f[None, :]) % (2 * seq)
    ang = m.astype(F32) * (math.pi / seq)
    return jnp.cos(ang).astype(BF16), jnp.sin(ang).astype(BF16)


def _hyfilt_kernel(feat_ref, w1_ref, b1_ref, w2_ref, b2_ref, fr_ref, w3_ref, t_ref, dl_ref,
                   a_ref, s_ref, u1_ref, u2_ref, un_ref):
    seq = feat_ref.shape[0]
    fr = fr_ref[...]
    h = jnp.sin(fr * (jnp.dot(feat_ref[...], w1_ref[...], preferred_element_type=F32, precision=HIGHEST)
                      + b1_ref[...]))
    h = jnp.sin(fr * (jnp.dot(h, w2_ref[...], preferred_element_type=F32, precision=HIGHEST) + b2_ref[...]))
    window = jnp.exp(-t_ref[...] * dl_ref[...])
    row = lax.broadcasted_iota(I32, (seq, 1), 0)
    sgn = jnp.where(row % 2 == 0, 1.0, -1.0)
    cf = jnp.where(row == 0, 1.0, 2.0) * (1.0 / (2 * seq))
    for o in range(2):
        w3 = w3_ref[:, o * 2 * HY_W:(o + 1) * 2 * HY_W]
        ho = jnp.dot(h, w3, preferred_element_type=F32, precision=HIGHEST)
        fwd = ho[:, :HY_W] * window
        bwd = ho[:, HY_W:] * window
        norm = (jnp.sum(jnp.abs(fwd), axis=0, keepdims=True)
                + jnp.sum(jnp.abs(bwd), axis=0, keepdims=True))
        inv = 1.0 / norm
        ksum = (fwd + bwd) * inv
        kdif = (bwd - fwd) * inv
        kr = _dot(a_ref[...], ksum.astype(BF16))
        ki = _dot(s_ref[...], kdif.astype(BF16))
        u1_ref[o] = (kr * cf).astype(u1_ref.dtype)
        u2_ref[o] = (ki * cf).astype(u2_ref.dtype)
        un_ref[o] = jnp.sum(sgn * ksum, axis=0, keepdims=True) * (1.0 / (2 * seq))


def _hyena_filters(seq, fw1, fb1, fw2, fb2, fw3, freq, a_mat, s_mat):
    pos = jnp.arange(seq, dtype=F32)[:, None]
    t = pos / max(seq - 1, 1)
    w = (2.0 * math.pi / seq) * pos
    bands = jnp.linspace(1e-4, HY_BANDS - 1, HY_BANDS, dtype=F32)[None, :]
    feats = jnp.concatenate([t, jnp.cos(bands * w), -jnp.sin(bands * w)], axis=-1)
    feats = jnp.pad(feats, ((0, 0), (0, LANES - HY_EMB)))
    w1 = jnp.pad(fw1, ((0, LANES - HY_EMB), (0, 0)))
    max_decay = math.log(HY_DECAY_TARGET) / HY_FAST_DECAY_PCT
    min_decay = math.log(HY_DECAY_TARGET) / HY_SLOW_DECAY_PCT
    deltas = jnp.abs(jnp.linspace(min_decay, max_decay, HY_W, dtype=F32))[None, :]
    hid = fw2.shape[0]
    return pl.pallas_call(
        _hyfilt_kernel,
        out_shape=(jax.ShapeDtypeStruct((2, seq, HY_W), BF16),
                   jax.ShapeDtypeStruct((2, seq, HY_W), BF16),
                   jax.ShapeDtypeStruct((2, 1, HY_W), F32)),
        compiler_params=pltpu.CompilerParams(vmem_limit_bytes=VMEM_LIMIT_BYTES),
        name="hyfilt",
    )(feats, w1, fb1.reshape(1, hid), fw2, fb2.reshape(1, hid), freq.reshape(1, hid), fw3, t, deltas,
      a_mat, s_mat)


def _hyena_kernel(x1_ref, x2_ref, v_ref, w1_ref, w2_ref, wv_ref, b1_ref, b2_ref, bv_ref,
                  u1_ref, u2_ref, un_ref, d_ref, a_ref, s_ref, o_ref, z_scr, zb_scr, re_scr, im_scr):
    seq = x1_ref.shape[0]
    rb = min(seq, 512)
    nrb = seq // rb
    row = lax.broadcasted_iota(I32, (rb, 1), 0)
    col = row % GRID_W
    first = col == 0
    lastc = col == GRID_W - 1
    sgn = jnp.where(row % 2 == 0, 1.0, -1.0)

    def conv3(p_ref, w_ref, b_ref, rows):
        p = p_ref[rows, :].astype(F32)
        prev = jnp.where(first, 0.0, pltpu.roll(p, 1, axis=0))
        nxt = jnp.where(lastc, 0.0, pltpu.roll(p, rb - 1, axis=0))
        w = w_ref[...]
        return w[0:1, :] * prev + w[1:2, :] * p + w[2:3, :] * nxt + b_ref[...]

    def forward(o):
        nyq = None
        for i in range(nrb):
            rows = slice(i * rb, (i + 1) * rb)
            part = jnp.sum(sgn * z_scr[rows, :], axis=0, keepdims=True)
            nyq = part if nyq is None else nyq + part
        for i in range(nrb):
            rows = slice(i * rb, (i + 1) * rb)
            p = _dot(a_ref[rows, :], zb_scr[...])
            q = _dot(s_ref[rows, :], zb_scr[...])
            u1 = u1_ref[o, rows, :].astype(F32)
            u2 = u2_ref[o, rows, :].astype(F32)
            re_scr[rows, :] = (p * u1 + q * u2).astype(BF16)
            im_scr[rows, :] = (q * u1 - p * u2).astype(BF16)
        return nyq * un_ref[o]

    def inverse(o, nyq, rows):
        y = _dot(a_ref[rows, :], re_scr[...]) + _dot(s_ref[rows, :], im_scr[...]) + sgn * nyq
        return y + z_scr[rows, :] * d_ref[o:o + 1, :]

    for i in range(nrb):
        rows = slice(i * rb, (i + 1) * rb)
        v = conv3(v_ref, wv_ref, bv_ref, rows)
        z_scr[rows, :] = v
        zb_scr[rows, :] = v.astype(BF16)
    nyq = forward(0)
    for i in range(nrb):
        rows = slice(i * rb, (i + 1) * rb)
        z = conv3(x1_ref, w1_ref, b1_ref, rows) * inverse(0, nyq, rows)
        z_scr[rows, :] = z
        zb_scr[rows, :] = z.astype(BF16)
    nyq = forward(1)
    for i in range(nrb):
        rows = slice(i * rb, (i + 1) * rb)
        y = conv3(x2_ref, w2_ref, b2_ref, rows) * inverse(1, nyq, rows)
        o_ref[rows, :] = y.astype(o_ref.dtype)


def _hyena(px3, conv_w, conv_b, u1, u2, un, d_skip, a_mat, s_mat):
    b, seq, _ = px3.shape
    ct = HY_CT
    nc = HY_W // ct

    def xspec(j):
        return pl.BlockSpec((None, seq, ct), lambda c, bi: (bi, 0, j * nc + c))

    def wspec(j, rows):
        return pl.BlockSpec((rows, ct), lambda c, bi: (0, j * nc + c))

    uspec = pl.BlockSpec((2, seq, ct), lambda c, bi: (0, 0, c))
    const = pl.BlockSpec((seq, seq), lambda c, bi: (0, 0), pipeline_mode=pl.Buffered(1))
    cb = conv_b.reshape(1, 3 * HY_W)
    return pl.pallas_call(
        _hyena_kernel,
        grid=(nc, b),
        in_specs=[xspec(0), xspec(1), xspec(2), wspec(0, 3), wspec(1, 3), wspec(2, 3),
                  wspec(0, 1), wspec(1, 1), wspec(2, 1), uspec, uspec,
                  pl.BlockSpec((2, 1, ct), lambda c, bi: (0, 0, c)),
                  pl.BlockSpec((2, ct), lambda c, bi: (0, c)), const, const],
        out_specs=pl.BlockSpec((None, seq, ct), lambda c, bi: (bi, 0, c)),
        out_shape=jax.ShapeDtypeStruct((b, seq, HY_W), BF16),
        scratch_shapes=[pltpu.VMEM((seq, ct), F32), pltpu.VMEM((seq, ct), BF16),
                        pltpu.VMEM((seq, ct), BF16), pltpu.VMEM((seq, ct), BF16)],
        compiler_params=_params(("parallel", "parallel")),
        name="hyena",
    )(px3, px3, px3, conv_w, conv_w, conv_w, cb, cb, cb, u1, u2, un, d_skip, a_mat, s_mat)


def _select_rows(rows):
    n = rows[0].shape[1]
    idx = lax.broadcasted_iota(I32, (len(rows), n), 0)
    out = jnp.zeros((len(rows), n), rows[0].dtype)
    for k, r in enumerate(rows):
        out = jnp.where(idx == k, r, out)
    return out


def _first_argmax(x, iota, size):
    m = jnp.max(x, axis=0, keepdims=True)
    return jnp.min(jnp.where(x == m, iota, size), axis=0, keepdims=True)


def _mid_kernel(yhy_ref, yhg_ref, x_ref, woa_ref, wob_ref, g1_ref, sc_ref, sh_ref, g2_ref, ng_ref,
                wr_ref, rb_ref, swg_ref, swu_ref, swd_ref,
                xs_ref, h_ref, lrow_ref, w_ref, ce_ref, rel_ref, nch_ref, cnt_ref, base, wrh_ref, wrl_ref):
    tm = x_ref.shape[0]
    ne = N_EXPERTS

    @pl.when(pl.program_id(0) == 0)
    def _():
        base[...] = jnp.zeros_like(base)
        wr = wr_ref[...]
        wr_hi = wr.astype(BF16)
        wrh_ref[...] = wr_hi
        wrl_ref[...] = (wr - wr_hi.astype(F32)).astype(BF16)

    mix = _dot(yhy_ref[...], woa_ref[...]) + _dot(yhg_ref[...], wob_ref[...])
    xm = x_ref[...] + g1_ref[...] * mix
    h = _rms_mod(xm, ng_ref[...], sc_ref[...], sh_ref[...])
    hb = h.astype(BF16)
    h_ref[...] = hb
    act = (_silu(_dot(hb, swg_ref[...])) * _dot(hb, swu_ref[...])).astype(BF16)
    xs_ref[...] = xm + g2_ref[...] * _dot(act, swd_ref[...])

    h_lo = (h - hb.astype(F32)).astype(BF16)
    logits = _dot_nt(wrh_ref[...], hb) + (_dot_nt(wrh_ref[...], h_lo) + _dot_nt(wrl_ref[...], hb))
    scores = jax.nn.sigmoid(logits)
    biased = scores + rb_ref[...]
    neg = -jnp.inf
    iota_g = lax.broadcasted_iota(I32, (GROUP_SIZE, tm), 0)
    grp = []
    for g in range(N_GROUPS):
        blk = biased[g * GROUP_SIZE:(g + 1) * GROUP_SIZE, :]
        m1 = jnp.max(blk, axis=0, keepdims=True)
        i1 = jnp.min(jnp.where(blk == m1, iota_g, GROUP_SIZE), axis=0, keepdims=True)
        m2 = jnp.max(jnp.where(iota_g == i1, neg, blk), axis=0, keepdims=True)
        grp.append(m1 + m2)
    gsc = _select_rows(grp)
    iota8 = lax.broadcasted_iota(I32, (N_GROUPS, tm), 0)
    gsel = iota8 < 0
    for _ in range(TOPK_GROUPS):
        hit = iota8 == _first_argmax(gsc, iota8, N_GROUPS)
        gsel = gsel | hit
        gsc = jnp.where(hit, neg, gsc)
    gself = jnp.where(gsel, 1.0, 0.0)
    iota_e = lax.broadcasted_iota(I32, (ne, tm), 0)
    gid = iota_e // GROUP_SIZE
    emask = jnp.zeros((ne, tm), F32)
    for g in range(N_GROUPS):
        emask = jnp.where(gid == g, gself[g:g + 1, :], emask)
    cur = jnp.where(emask > 0.0, biased, neg)

    sel = iota_e < 0
    idxs, wts = [], []
    for _ in range(TOP_K):
        idx = _first_argmax(cur, iota_e, ne)
        hit = iota_e == idx
        wts.append(jnp.sum(jnp.where(hit, scores, 0.0), axis=0, keepdims=True))
        idxs.append(idx)
        sel = sel | hit
        cur = jnp.where(hit, neg, cur)
    wsum = wts[0]
    for wk in wts[1:]:
        wsum = wsum + wk
    scale = ROUTED_SCALE / wsum

    r = lax.broadcasted_iota(I32, (tm, tm), 0)
    s = lax.broadcasted_iota(I32, (tm, tm), 1)
    upper = jnp.where(r < s, 1.0, 0.0).astype(BF16)
    sel_f = jnp.where(sel, 1.0, 0.0)
    prefix = _dot(sel_f.astype(BF16), upper)
    cnt = jnp.sum(sel_f, axis=1, keepdims=True)
    cnt_al = jnp.floor((cnt + (SLOT_ALIGN - 1)) * (1.0 / SLOT_ALIGN)) * SLOT_ALIGN
    re = lax.broadcasted_iota(I32, (ne, ne), 0)
    ce = lax.broadcasted_iota(I32, (ne, ne), 1)
    lower = jnp.where(ce < re, 1.0, 0.0).astype(BF16)
    loc = _dot(lower, jnp.broadcast_to(cnt_al, (ne, LANES)).astype(BF16))[:, 0:1]
    lrow_all = loc + prefix
    lrows = [jnp.sum(jnp.where(iota_e == idx, lrow_all, 0.0), axis=0, keepdims=True) for idx in idxs]

    nc = ce_ref.shape[-1]
    jrow = lax.broadcasted_iota(I32, (1, nc), 1).astype(F32) * SLOT_ALIGN
    owner = jnp.sum(jnp.where(loc + cnt_al <= jrow, 1.0, 0.0), axis=0, keepdims=True)
    owner = jnp.minimum(owner, ne - 1.0).astype(I32)
    iota_ec = lax.broadcasted_iota(I32, (ne, nc), 0)
    rel = jnp.sum(jnp.where(iota_ec == owner, base[...] - loc, 0.0), axis=0, keepdims=True) + jrow
    base[...] = base[...] + cnt_al
    used = jnp.sum(cnt_al, axis=0, keepdims=True)
    spare = jnp.minimum(jrow - used, (MOE_ROW_CHUNK - SLOT_ALIGN) * 1.0)
    spare = spare + (pl.program_id(0) % 2).astype(F32) * MOE_ROW_CHUNK
    unused = jrow >= used
    owner = jnp.where(unused, ne, owner)
    rel = jnp.where(unused, spare, rel)

    lrow_ref[...] = _select_rows(lrows).astype(I32)
    w_ref[...] = _select_rows([wk * scale for wk in wts])
    ce_ref[...] = owner
    rel_ref[...] = rel.astype(I32)
    nch = used * (1.0 / SLOT_ALIGN)
    nch_ref[...] = jnp.broadcast_to(nch, nch_ref.shape).astype(I32)
    cnt_ref[...] = base[...].astype(I32)


def _mid(yhy, yhg, x2d, seq, woa, wob, g1, sc2, sh2, g2, norm_g, wr_t, rbias, swg, swu, swd, tm):
    t, d = x2d.shape
    per = seq // tm
    ff = swg.shape[1]

    def full(shape):
        return pl.BlockSpec(shape, lambda i: (0,) * len(shape))

    mspec = pl.BlockSpec((None, 1, d), lambda i: (i // per, 0, 0))
    kt = pl.BlockSpec((TOP_K, tm), lambda i: (0, i))
    nt = t // tm

    def per_tile(n):
        return pl.BlockSpec((None, 1, n), lambda i: (i, 0, 0))

    return pl.pallas_call(
        _mid_kernel,
        grid=(nt,),
        in_specs=[pl.BlockSpec((tm, HY_W), lambda i: (i, 0)), pl.BlockSpec((tm, HG_W), lambda i: (i, 0)),
                  pl.BlockSpec((tm, d), lambda i: (i, 0)), full((HY_W, d)), full((HG_W, d)),
                  mspec, mspec, mspec, mspec, full((1, d)), full((N_EXPERTS, d)),
                  full((N_EXPERTS, 1)), full((d, ff)), full((d, ff)), full((ff, d))],
        out_specs=[pl.BlockSpec((tm, d), lambda i: (i, 0)), pl.BlockSpec((tm, d), lambda i: (i, 0)),
                   kt, kt, per_tile(MOE_CHUNKS), per_tile(MOE_CHUNKS), per_tile(LANES), full((N_EXPERTS, 1))],
        out_shape=(jax.ShapeDtypeStruct((t, d), F32), jax.ShapeDtypeStruct((t, d), BF16),
                   jax.ShapeDtypeStruct((TOP_K, t), I32), jax.ShapeDtypeStruct((TOP_K, t), F32),
                   jax.ShapeDtypeStruct((nt, 1, MOE_CHUNKS), I32), jax.ShapeDtypeStruct((nt, 1, MOE_CHUNKS), I32),
                   jax.ShapeDtypeStruct((nt, 1, LANES), I32), jax.ShapeDtypeStruct((N_EXPERTS, 1), I32)),
        scratch_shapes=[pltpu.VMEM((N_EXPERTS, 1), F32), pltpu.VMEM((N_EXPERTS, d), BF16),
                        pltpu.VMEM((N_EXPERTS, d), BF16)],
        compiler_params=_params(("arbitrary",)),
        name="mid",
    )(yhy, yhg, x2d, woa, wob, g1, sc2, sh2, g2, norm_g.reshape(1, d), wr_t,
      rbias.reshape(N_EXPERTS, 1), swg, swu, swd)


def _pack_bf16_pairs(x):
    n = x.shape[1] // 2
    lo = lax.shift_right_logical(pltpu.bitcast(x[:, :n], I32), 16)
    hi = pltpu.bitcast(x[:, n:], I32) & -65536
    return hi | lo


def _unpack_bf16_pairs(p):
    lo = pltpu.bitcast(lax.shift_left(p, 16), F32).astype(BF16)
    hi = pltpu.bitcast(p & -65536, F32).astype(BF16)
    return lo, hi


def _row_chunks(nch):
    return (nch * SLOT_ALIGN + MOE_ROW_CHUNK - 1) // MOE_ROW_CHUNK


def _start_chunk_copies(c, ce_ref, rel_ref, ps_ref, make):
    per = MOE_ROW_CHUNK // SLOT_ALIGN
    for q in range(per):
        j = c * per + q
        glob = pl.multiple_of(ps_ref[ce_ref[j]] + rel_ref[j], SLOT_ALIGN)
        make(pl.multiple_of(j * SLOT_ALIGN, SLOT_ALIGN), glob).start()


def _wait_row_chunks(n, row_chunk_copy):
    def wait(_, carry):
        row_chunk_copy.wait()
        return carry

    lax.fori_loop(0, n, wait, 0)


def _dispatch_kernel(nch_ref, ce_ref, rel_ref, ps_ref, h_ref, lrow_ref, xs_hbm, xloc, zbuf, sems):
    tm = h_ref.shape[0]
    bm = zbuf.shape[0]
    rc = MOE_ROW_CHUNK
    sem = sems.at[0]

    @pl.when(pl.program_id(0) == 0)
    def _():
        zbuf[...] = jnp.zeros_like(zbuf)

        def zcopy(e):
            start = pl.multiple_of(ps_ref[e + 1] - bm, bm)
            return pltpu.make_async_copy(zbuf, xs_hbm.at[pl.ds(start, bm), :], sem)

        def start(e, carry):
            @pl.when(ps_ref[e + 1] > ps_ref[e])
            def _():
                zcopy(e).start()
            return carry

        def wait(e, carry):
            @pl.when(ps_ref[e + 1] > ps_ref[e])
            def _():
                zcopy(e).wait()
            return carry

        lax.fori_loop(0, N_EXPERTS, start, 0)
        lax.fori_loop(0, N_EXPERTS, wait, 0)

        def tcopy(j):
            return pltpu.make_async_copy(zbuf, xs_hbm.at[pl.ds(pl.multiple_of(j * bm, bm), bm), :], sem)

        def tstart(j, carry):
            tcopy(j).start()
            return carry

        def twait(j, carry):
            tcopy(j).wait()
            return carry

        first_tail = ps_ref[N_EXPERTS] // bm
        lax.fori_loop(first_tail, xs_hbm.shape[0] // bm, tstart, 0)
        lax.fori_loop(first_tail, xs_hbm.shape[0] // bm, twait, 0)

    i = pl.program_id(0)
    slot = i % 2
    buf = xloc.at[slot]
    n_rc = _row_chunks(nch_ref[i])
    lrow = lrow_ref[...].astype(F32)
    sub = rc // 2
    rid = lax.broadcasted_iota(I32, (sub, tm), 0).astype(F32).astype(BF16)
    one = jnp.ones((sub, tm), BF16)

    def select(r0):
        rel = (lrow - jnp.asarray(r0, I32).astype(F32)).astype(BF16)
        onehot = jnp.zeros((sub, tm), BF16)
        for k in range(TOP_K):
            onehot = jnp.where(rid == rel[k:k + 1, :], one, onehot)
        buf[pl.ds(r0, sub), :] = _pack_bf16_pairs(_dot(onehot, h_ref[...]))

    def send(c):
        _start_chunk_copies(c, ce_ref, rel_ref, ps_ref,
                            lambda loc, glob: pltpu.make_async_copy(buf.at[pl.ds(loc, SLOT_ALIGN), :],
                                                                    xs_hbm.at[pl.ds(glob, SLOT_ALIGN), :],
                                                                    sems.at[slot]))

    def retire(tile_slot, n):
        _wait_row_chunks(n, pltpu.make_async_copy(xloc.at[tile_slot, pl.ds(0, rc), :],
                                                  xs_hbm.at[pl.ds(0, rc), :], sems.at[tile_slot]))

    select(0)
    select(sub)

    def body(c, carry):
        send(c - 1)
        r0 = pl.multiple_of(c * rc, rc)
        select(r0)
        select(r0 + sub)
        return carry

    lax.fori_loop(1, n_rc, body, 0)
    send(n_rc - 1)

    @pl.when(i > 0)
    def _():
        retire(1 - slot, _row_chunks(nch_ref[i - 1]))

    @pl.when(i == pl.num_programs(0) - 1)
    def _():
        retire(slot, n_rc)


def _local_rows(tm):
    bound = TOP_K * tm + N_EXPERTS * (SLOT_ALIGN - 1)
    return -(-bound // MOE_ROW_CHUNK) * MOE_ROW_CHUNK


def _moe_smem_specs():
    chunk_list = pl.BlockSpec((MOE_CHUNKS,), lambda i, nch: (i,), memory_space=pltpu.SMEM)
    return [chunk_list, chunk_list,
            pl.BlockSpec((N_EXPERTS + 1,), lambda i, nch: (0,), memory_space=pltpu.SMEM)]


def _dispatch(h2d, lrow_kt, ce, rel, nch, ps, n_slots, tm):
    t, d = h2d.shape
    grid_spec = pltpu.PrefetchScalarGridSpec(
        num_scalar_prefetch=1,
        grid=(t // tm,),
        in_specs=_moe_smem_specs() + [pl.BlockSpec((tm, d), lambda i, nch: (i, 0)),
                                      pl.BlockSpec((TOP_K, tm), lambda i, nch: (0, i))],
        out_specs=pl.BlockSpec(memory_space=pl.ANY),
        scratch_shapes=[pltpu.VMEM((2, _local_rows(tm), d // 2), I32), pltpu.VMEM((FFN_BLOCK, d // 2), I32),
                        pltpu.SemaphoreType.DMA((2,))],
    )

    return pl.pallas_call(
        _dispatch_kernel,
        grid_spec=grid_spec,
        out_shape=jax.ShapeDtypeStruct((n_slots, d // 2), I32),
        compiler_params=_params(("arbitrary",)),
        name="dispatch",
    )(nch, ce, rel, ps, h2d, lrow_kt)


def _ffn_kernel(ps_ref, wg_ref, wu_ref, wd_ref, x_hbm, y_hbm, xbuf, ybuf, wg_bf, wu_bf, wd_bf, xsem, ysem):
    e = pl.program_id(0)
    nx, bm, half = xbuf.shape
    ny = ybuf.shape[0]
    ahead = nx - 1
    n_blocks = y_hbm.shape[0] // bm
    g0 = ps_ref[e] // bm
    g1 = ps_ref[e + 1] // bm

    def rows(g):
        return pl.ds(pl.multiple_of(g * bm, bm), bm)

    def x_copy(g):
        s = g % nx
        return pltpu.make_async_copy(x_hbm.at[rows(g), :], xbuf.at[s], xsem.at[s])

    def y_copy(g, s):
        return pltpu.make_async_copy(ybuf.at[s], y_hbm.at[rows(g), :], ysem.at[s])

    @pl.when(e == 0)
    def _():
        ybuf[...] = jnp.zeros_like(ybuf)
        for g in range(ahead):
            x_copy(g).start()
        for s in range(ny):
            y_copy(n_blocks - 1 - s, s).start()

    @pl.when(g1 > g0)
    def _():
        wg_bf[...] = wg_ref[...].astype(BF16)
        wu_bf[...] = wu_ref[...].astype(BF16)
        wd_bf[...] = wd_ref[...].astype(BF16)

    def block(g, carry):
        s = g % ny
        xs = g % nx
        x_copy(g).wait()
        x_copy(g + ahead).start()
        hb = bm // 2
        halves = [pl.ds(p * hb, hb) for p in range(2)]
        gu = []
        for r in halves:
            lo, hi = _unpack_bf16_pairs(xbuf[xs, r, :])
            gu.append((_dot(lo, wg_bf[:half, :]) + _dot(hi, wg_bf[half:, :]),
                       _dot(lo, wu_bf[:half, :]) + _dot(hi, wu_bf[half:, :])))
        y_copy(g, s).wait()
        for r, (gate, up) in zip(halves, gu):
            act = (_silu(gate) * up).astype(BF16)
            y = _dot(act, wd_bf[...])
            ybuf[s, r, :] = _pack_bf16_pairs(y.astype(BF16).astype(F32))
        y_copy(g, s).start()
        return carry

    lax.fori_loop(g0, g1, block, 0)

    @pl.when(e == pl.num_programs(0) - 1)
    def _():
        for k in range(ahead):
            x_copy(g1 + k).wait()
        for s in range(ny):
            y_copy(0, s).wait()
        ybuf[0] = jnp.zeros((bm, half), ybuf.dtype)

        def zstart(g, carry):
            y_copy(g, 0).start()
            return carry

        def zwait(g, carry):
            y_copy(g, 0).wait()
            return carry

        lax.fori_loop(g1, n_blocks, zstart, 0)
        lax.fori_loop(g1, n_blocks, zwait, 0)


def _ffn(xs, ps, ew_gate, ew_up, ew_down):
    n_slots, half = xs.shape
    d = 2 * half
    bm = FFN_BLOCK
    n_exp, _, ff = ew_gate.shape
    grid_spec = pltpu.PrefetchScalarGridSpec(
        num_scalar_prefetch=1,
        grid=(n_exp,),
        in_specs=[pl.BlockSpec((None, d, ff), lambda e, ps: (e, 0, 0)),
                  pl.BlockSpec((None, d, ff), lambda e, ps: (e, 0, 0)),
                  pl.BlockSpec((None, ff, d), lambda e, ps: (e, 0, 0)),
                  pl.BlockSpec(memory_space=pl.ANY)],
        out_specs=pl.BlockSpec(memory_space=pl.ANY),
        scratch_shapes=[pltpu.VMEM((FFN_BUFFERS, bm, half), I32), pltpu.VMEM((FFN_BUFFERS, bm, half), I32),
                        pltpu.VMEM((d, ff), BF16), pltpu.VMEM((d, ff), BF16), pltpu.VMEM((ff, d), BF16),
                        pltpu.SemaphoreType.DMA((FFN_BUFFERS,)), pltpu.SemaphoreType.DMA((FFN_BUFFERS,))],
    )
    return pl.pallas_call(
        _ffn_kernel,
        grid_spec=grid_spec,
        out_shape=jax.ShapeDtypeStruct((n_slots, half), I32),
        compiler_params=_params(("arbitrary",)),
        name="ffn",
    )(ps, ew_gate, ew_up, ew_down, xs)


def _combine_kernel(nch_ref, ce_ref, rel_ref, cen_ref, reln_ref, ps_ref, lrow_ref, w_ref, xs_ref, g2_ref, fg_ref,
                    y_hbm, o_ref, yloc, acc, sems):
    tm = xs_ref.shape[0]
    half = yloc.shape[2]
    rc = MOE_ROW_CHUNK
    i = pl.program_id(0)
    last = pl.num_programs(0) - 1
    slot = i % 2
    nxt = jnp.minimum(i + 1, last)
    n_rc = _row_chunks(nch_ref[i])
    n_next = _row_chunks(nch_ref[nxt])

    def fetch(c, ce, rel, s):
        _start_chunk_copies(c, ce, rel, ps_ref,
                            lambda loc, glob: pltpu.make_async_copy(y_hbm.at[pl.ds(glob, SLOT_ALIGN), :],
                                                                    yloc.at[s, pl.ds(loc, SLOT_ALIGN), :],
                                                                    sems.at[s]))

    def arrived(s, n):
        _wait_row_chunks(n, pltpu.make_async_copy(y_hbm.at[pl.ds(0, rc), :], yloc.at[s, pl.ds(0, rc), :],
                                                  sems.at[s]))

    def fetch_own(c, carry):
        fetch(c, ce_ref, rel_ref, slot)
        return carry

    def fetch_next(c, carry):
        fetch(c, cen_ref, reln_ref, 1 - slot)
        return carry

    @pl.when(i == 0)
    def _():
        yloc[...] = jnp.zeros_like(yloc)
        lax.fori_loop(0, n_rc, fetch_own, 0)

    n_prev = _row_chunks(nch_ref[jnp.maximum(i - 1, 0)])
    arrived(slot, jnp.where(i == 0, n_rc, jnp.maximum(n_prev, n_rc)))

    lrow = lrow_ref[...].astype(F32)
    wrow = w_ref[...].astype(BF16)
    acc[...] = jnp.zeros_like(acc)
    sub = rc // 2
    rid = lax.broadcasted_iota(I32, (sub, tm), 0).astype(F32).astype(BF16)

    def weights(r0):
        rel = (lrow - r0.astype(F32)).astype(BF16)
        wt = jnp.zeros((sub, tm), BF16)
        for k in range(TOP_K):
            wt = jnp.where(rid == rel[k:k + 1, :], jnp.broadcast_to(wrow[k:k + 1, :], (sub, tm)), wt)
        return wt

    def gather(c, carry):
        fetch(c, cen_ref, reln_ref, 1 - slot)
        r0 = pl.multiple_of(c * rc, rc)
        wt_a = weights(r0)
        wt_b = weights(r0 + sub)
        lo_a, hi_a = _unpack_bf16_pairs(yloc[slot, pl.ds(r0, sub), :])
        lo_b, hi_b = _unpack_bf16_pairs(yloc[slot, pl.ds(r0 + sub, sub), :])
        acc[:, :half] += _dot_tn(wt_a, lo_a) + _dot_tn(wt_b, lo_b)
        acc[:, half:] += _dot_tn(wt_a, hi_a) + _dot_tn(wt_b, hi_b)
        return carry

    lax.fori_loop(0, n_rc, gather, 0)
    lax.fori_loop(n_rc, n_next, fetch_next, 0)

    @pl.when(i == last)
    def _():
        arrived(1 - slot, jnp.maximum(n_rc, n_next))

    x = xs_ref[...] + g2_ref[...] * acc[...]
    o_ref[...] = x * lax.rsqrt(jnp.mean(x * x, axis=-1, keepdims=True) + NORM_EPS) * fg_ref[...]


def _combine(y_sorted, lrow_kt, w_kt, ce, rel, nch, ps, xs2d, seq, g2, final_g, tm):
    t, d = xs2d.shape
    per = seq // tm
    nt = t // tm
    own, _, ranges = _moe_smem_specs()
    nxt = pl.BlockSpec((MOE_CHUNKS,), lambda i, nch: (jnp.minimum(i + 1, nt - 1),), memory_space=pltpu.SMEM)
    grid_spec = pltpu.PrefetchScalarGridSpec(
        num_scalar_prefetch=1,
        grid=(nt,),
        in_specs=[own, own, nxt, nxt, ranges,
                  pl.BlockSpec((TOP_K, tm), lambda i, nch: (0, i)),
                  pl.BlockSpec((TOP_K, tm), lambda i, nch: (0, i)),
                  pl.BlockSpec((tm, d), lambda i, nch: (i, 0)),
                  pl.BlockSpec((None, 1, d), lambda i, nch: (i // per, 0, 0)),
                  pl.BlockSpec((1, d), lambda i, nch: (0, 0)),
                  pl.BlockSpec(memory_space=pl.ANY)],
        out_specs=pl.BlockSpec((tm, d), lambda i, nch: (i, 0)),
        scratch_shapes=[pltpu.VMEM((2, _local_rows(tm), d // 2), I32), pltpu.VMEM((tm, d), F32),
                        pltpu.SemaphoreType.DMA((2,))],
    )
    return pl.pallas_call(
        _combine_kernel,
        grid_spec=grid_spec,
        out_shape=jax.ShapeDtypeStruct((t, d), F32),
        compiler_params=_params(("arbitrary",)),
        name="combine",
    )(nch, ce, rel, ce, rel, ps, lrow_kt, w_kt, xs2d, g2, final_g.reshape(1, d), y_sorted)


def kernel(x, c, ctx, c_ctx, w_mod, b_mod, norm1_g, norm2_g, w_in, w_out, hy_conv_w, hy_conv_b,
           hy_fw1, hy_fb1, hy_fw2, hy_fb2, hy_fw3, hy_freq, hy_d, hg_lb_logits, hg_norm_g,
           w_router, router_bias, ew_gate, ew_up, ew_down, sw_gate, sw_up, sw_down, final_g):
    b, seq, d = x.shape
    ctx_len = ctx.shape[1]
    t = b * seq
    layer = 0

    lower = jnp.cumsum(jax.nn.softmax(hg_lb_logits.astype(F32), axis=1), axis=1)
    lb_f, lb_b = lower[0, layer], lower[1, layer]

    rows = -(-(b + 1) // 8) * 8
    cc = jnp.concatenate([c, c_ctx[None, :], jnp.zeros((rows - b - 1, d), F32)], axis=0)
    mod = _modulation(cc, w_mod[layer], b_mod[layer])
    sh1, sc1, g1, sh2, sc2, g2 = (m.reshape(b, 1, d) for m in jnp.split(mod[:b], 6, axis=-1))
    csh1, csc1 = (jnp.broadcast_to(m.reshape(1, 1, d), (b, 1, d))
                  for m in jnp.split(mod[b:b + 1], 6, axis=-1)[:2])

    w_in_bf = w_in[layer].astype(BF16)
    x2d = x.reshape(t, d)
    px = _inproj(x2d, seq, norm1_g[layer], sc1, sh1, w_in_bf, min(seq, 512), w_in_bf.shape[1])
    lo = 3 * HY_W + HG_W
    pc = _inproj(ctx.reshape(b * ctx_len, d), ctx_len, norm1_g[layer], csc1, csh1,
                 w_in_bf[:, lo:lo + 3 * HG_W], ctx_len, HG_W)
    px3 = px.reshape(b, seq, -1)
    pc3 = pc.reshape(b, ctx_len, -1)

    y_hg = _hgrn(px3, pc3, lb_f, lb_b, hg_norm_g[layer])

    a_mat, s_mat = _dft_mats(seq)
    u1, u2, un = _hyena_filters(seq, hy_fw1[layer], hy_fb1[layer], hy_fw2[layer], hy_fb2[layer],
                                hy_fw3[layer], hy_freq[layer], a_mat, s_mat)
    y_hy = _hyena(px3, hy_conv_w[layer], hy_conv_b[layer], u1, u2, un, hy_d[layer], a_mat, s_mat)

    w_out_bf = w_out[layer].astype(BF16)
    tm = min(seq, MOE_TILE)
    nt = t // tm
    xs, h2, lrow_kt, w_kt, ce, rel, nch, counts = _mid(
        y_hy.reshape(t, HY_W), y_hg.reshape(t, HG_W), x2d, seq, w_out_bf[:HY_W], w_out_bf[HY_W:],
        g1, sc2, sh2, g2, norm2_g[layer], w_router[layer].T, router_bias[layer],
        sw_gate[layer].astype(BF16), sw_up[layer].astype(BF16), sw_down[layer].astype(BF16), tm)

    bm = FFN_BLOCK
    counts = counts.reshape(N_EXPERTS)
    padded = (counts + bm - 1) // bm * bm
    p_ends = jnp.cumsum(padded)
    ps = jnp.concatenate([p_ends - padded, p_ends[-1:]]).astype(I32)
    spare_blocks = max(-(-2 * MOE_ROW_CHUNK // bm), FFN_BUFFERS)
    n_blocks = -(-(t * TOP_K + nt * N_EXPERTS * (SLOT_ALIGN - 1)) // bm) + N_EXPERTS + spare_blocks

    ce, rel, nch = ce.reshape(-1), rel.reshape(-1), nch[:, 0, 0]
    x_sorted = _dispatch(h2, lrow_kt, ce, rel, nch, ps, n_blocks * bm, tm)
    y_sorted = _ffn(x_sorted, ps, ew_gate[layer], ew_up[layer], ew_down[layer])
    out = _combine(y_sorted, lrow_kt, w_kt, ce, rel, nch, ps, xs, seq, g2, final_g, tm)
    return out.reshape(b, seq, d)
```

```python
import functools
import math

import jax
import jax.numpy as jnp
from jax import lax
from jax.experimental import pallas as pl
from jax.experimental.pallas import tpu as pltpu

F32 = jnp.float32
BF16 = jnp.bfloat16
I32 = jnp.int32
HIGHEST = lax.Precision.HIGHEST

GRID_W = 64
HY_W = 512
HG_W = 512
HY_EMB = 33
HY_BANDS = 16
HY_DECAY_TARGET = 1e-2
HY_FAST_DECAY_PCT = 0.3
HY_SLOW_DECAY_PCT = 1.5
HG_HEAD_DIM = 128
HG_HEADS = 4
HG_SCALE = HG_HEAD_DIM ** -0.5
HG_CHUNK = 64
N_EXPERTS = 256
TOP_K = 8
N_GROUPS = 8
TOPK_GROUPS = 4
GROUP_SIZE = N_EXPERTS // N_GROUPS
ROUTED_SCALE = 2.5
NORM_EPS = 1e-6

VMEM_LIMIT_BYTES = 56 * 1024 * 1024
LANES = 128
FFN_BLOCK = 512
FFN_BUFFERS = 4
MOE_TILE = 512
SLOT_ALIGN = 8
MOE_ROW_CHUNK = 1024
MOE_SUB_ROWS = 256
MOE_CHUNKS = 1024
HY_CT = 256


def _params(sem, vmem=VMEM_LIMIT_BYTES):
    return pltpu.CompilerParams(dimension_semantics=sem, vmem_limit_bytes=vmem)


def _silu(x):
    return x * jax.nn.sigmoid(x)


def _dot(a, b):
    return jnp.dot(a, b, preferred_element_type=F32)


def _dot_nt(a, b):
    return lax.dot_general(a, b, (((1,), (1,)), ((), ())), preferred_element_type=F32)


def _dot_tn(a, b):
    return lax.dot_general(a, b, (((0,), (0,)), ((), ())), preferred_element_type=F32)


def _mod_kernel(c_ref, w_ref, b_ref, o_ref):
    s = _silu(c_ref[...])
    o_ref[...] = jnp.dot(s, w_ref[...], preferred_element_type=F32, precision=HIGHEST) + b_ref[...]


def _modulation(cc, w_mod, b_mod):
    rows, d = cc.shape
    n = w_mod.shape[1]
    tn = 1024
    return pl.pallas_call(
        _mod_kernel,
        grid=(n // tn,),
        in_specs=[pl.BlockSpec((rows, d), lambda j: (0, 0)),
                  pl.BlockSpec((d, tn), lambda j: (0, j)),
                  pl.BlockSpec((1, tn), lambda j: (0, j))],
        out_specs=pl.BlockSpec((rows, tn), lambda j: (0, j)),
        out_shape=jax.ShapeDtypeStruct((rows, n), F32),
        compiler_params=_params(("parallel",)),
        name="mod",
    )(cc, w_mod, b_mod.reshape(1, n))


def _rms_mod(x, g, sc, sh):
    y = x * lax.rsqrt(jnp.mean(x * x, axis=-1, keepdims=True) + NORM_EPS) * g
    return y * (1.0 + sc) + sh


def _inproj_kernel(x_ref, g_ref, sc_ref, sh_ref, w_ref, o_ref, h_scr):
    @pl.when(pl.program_id(1) == 0)
    def _():
        h_scr[...] = _rms_mod(x_ref[...], g_ref[...], sc_ref[...], sh_ref[...]).astype(BF16)

    o_ref[...] = _dot(h_scr[...], w_ref[...]).astype(o_ref.dtype)


def _inproj(x2d, seq, g, sc, sh, w_bf, tm, tn):
    t, d = x2d.shape
    n = w_bf.shape[1]
    per = seq // tm
    w_mode = dict(pipeline_mode=pl.Buffered(1)) if tn == n else {}
    return pl.pallas_call(
        _inproj_kernel,
        grid=(t // tm, n // tn),
        in_specs=[pl.BlockSpec((tm, d), lambda i, j: (i, 0)),
                  pl.BlockSpec((1, d), lambda i, j: (0, 0)),
                  pl.BlockSpec((None, 1, d), lambda i, j: (i // per, 0, 0)),
                  pl.BlockSpec((None, 1, d), lambda i, j: (i // per, 0, 0)),
                  pl.BlockSpec((d, tn), lambda i, j: (0, j), **w_mode)],
        out_specs=pl.BlockSpec((tm, tn), lambda i, j: (i, j)),
        out_shape=jax.ShapeDtypeStruct((t, n), BF16),
        scratch_shapes=[pltpu.VMEM((tm, d), BF16)],
        compiler_params=_params(("parallel", "arbitrary")),
        name="inproj",
    )(x2d, g.reshape(1, d), sc, sh, w_bf)


def _hg_steps(chains):
    c = HG_CHUNK
    r = lax.broadcasted_iota(I32, (c, c), 0)
    s = lax.broadcasted_iota(I32, (c, c), 1)
    geo = {False: (r >= s, c // 2 - 1, c - 1), True: (r <= s, c // 2, 0)}
    tri = {rev: jnp.where(g[0], 1.0, 0.0).astype(BF16) for rev, g in geo.items()}

    work = []
    for ch in chains:
        lb = ch["lb"]
        sig = jax.nn.sigmoid(ch["fr"])
        lf = jnp.log(lb + (1.0 - lb) * sig)
        k = (1.0 - lb) * (1.0 - sig)
        hi = lf.astype(BF16)
        lo = (lf - hi.astype(F32)).astype(BF16)
        t = tri[ch["rev"]]
        work.append(dict(k=k, bc=_dot(t, hi) + _dot(t, lo)))
    for ch, w in zip(chains, work):
        mask, mid, last = geo[ch["rev"]]
        bc = w["bc"]
        b_mid = bc[mid:mid + 1, :]
        b_last = bc[last:last + 1, :]
        km = w["k"] * jnp.exp(b_mid - bc)
        kd = (km * jnp.exp(b_last - b_mid)).astype(BF16)
        w["ut"] = _dot_tn(ch["v"], kd)
        w["decay"] = jnp.exp(b_last)
        if ch["q"] is not None:
            qm = ch["q"] * jnp.exp(bc - b_mid)
            w["att"] = _dot_nt(qm.astype(BF16), km.astype(BF16))
            qe = (qm * jnp.exp(b_mid)).astype(BF16)
            w["inter"] = _dot_nt(qe, ch["st"].astype(BF16))
    out = []
    for ch, w in zip(chains, work):
        o = None
        if ch["q"] is not None:
            att = jnp.where(geo[ch["rev"]][0], w["att"], 0.0).astype(BF16)
            o = _dot(att, ch["v"]) + w["inter"]
        out.append((o, ch["st"] * w["decay"] + w["ut"]))
    return out


def _hgrn_kernel(q_ref, ff_ref, fb_ref, i_ref, g_ref, cff_ref, cfb_ref, ci_ref,
                 lbf_ref, lbb_ref, ng_ref, o_ref, qs_scr, of_scr, ob_scr, st_scr):
    seq = q_ref.shape[0]
    ctx = cff_ref.shape[0]
    c = HG_CHUNK
    dh = HG_HEAD_DIM
    rb = min(seq, 256)

    for i in range(seq // rb):
        rows = slice(i * rb, (i + 1) * rb)
        qs_scr[rows, :] = (_silu(q_ref[rows, :].astype(F32)) * HG_SCALE).astype(BF16)
    st_scr[...] = jnp.zeros_like(st_scr)

    def chains(nchunks, ffr, fbr, vr, with_q):
        def body(n, carry):
            chains = []
            for h in range(HG_HEADS):
                cols = slice(h * dh, (h + 1) * dh)
                for rev, fref, lbref in ((False, ffr, lbf_ref), (True, fbr, lbb_ref)):
                    ci = (nchunks - 1 - n) if rev else n
                    rows = pl.ds(pl.multiple_of(ci * c, c), c)
                    chains.append(dict(
                        rev=rev, rows=rows, cols=cols, fr=fref[rows, cols].astype(F32), v=vr[rows, cols],
                        lb=lbref[:, cols], st=st_scr[len(chains)],
                        q=qs_scr[rows, cols].astype(F32) if with_q else None))
            for slot, (ch, (o, st)) in enumerate(zip(chains, _hg_steps(chains))):
                st_scr[slot] = st
                if with_q:
                    (ob_scr if ch["rev"] else of_scr)[ch["rows"], ch["cols"]] = o
            return carry

        lax.fori_loop(0, nchunks, body, 0)

    chains(ctx // c, cff_ref, cfb_ref, ci_ref, False)
    chains(seq // c, ff_ref, fb_ref, i_ref, True)

    ng = ng_ref[...]
    for i in range(seq // rb):
        rows = slice(i * rb, (i + 1) * rb)
        gate = _silu(g_ref[rows, :].astype(F32))
        for h in range(HG_HEADS):
            cols = slice(h * dh, (h + 1) * dh)
            o = of_scr[rows, cols] + ob_scr[rows, cols]
            on = o * lax.rsqrt(jnp.mean(o * o, axis=-1, keepdims=True) + NORM_EPS) * ng
            o_ref[rows, cols] = (on * gate[:, cols]).astype(o_ref.dtype)


def _hgrn(px3, pc3, lb_f, lb_b, norm_g):
    b, seq, _ = px3.shape
    ctx = pc3.shape[1]
    dh = HG_HEAD_DIM
    base = 3 * HY_W // HG_W

    def xspec(j):
        return pl.BlockSpec((None, seq, HG_W), lambda bi: (bi, 0, base + j))

    def cspec(j):
        return pl.BlockSpec((None, ctx, HG_W), lambda bi: (bi, 0, j))

    vec = pl.BlockSpec((1, HG_W), lambda bi: (0, 0))
    return pl.pallas_call(
        _hgrn_kernel,
        grid=(b,),
        in_specs=[xspec(0), xspec(1), xspec(2), xspec(3), xspec(4), cspec(0), cspec(1), cspec(2),
                  vec, vec, pl.BlockSpec((1, dh), lambda bi: (0, 0))],
        out_specs=pl.BlockSpec((None, seq, HG_W), lambda bi: (bi, 0, 0)),
        out_shape=jax.ShapeDtypeStruct((b, seq, HG_W), BF16),
        scratch_shapes=[pltpu.VMEM((seq, HG_W), BF16), pltpu.VMEM((seq, HG_W), F32),
                        pltpu.VMEM((seq, HG_W), F32), pltpu.VMEM((2 * HG_HEADS, dh, dh), F32)],
        compiler_params=_params(("parallel",)),
        name="hgrn",
    )(px3, px3, px3, px3, px3, pc3, pc3, pc3, lb_f.reshape(1, HG_W), lb_b.reshape(1, HG_W),
      norm_g.reshape(1, dh))


def _dft_mats(seq):
    f = jnp.arange(seq, dtype=I32)
    m = (f[:, None] * f[None, :]) % (2 * seq)
    ang = m.astype(F32) * (math.pi / seq)
    return jnp.cos(ang).astype(BF16), jnp.sin(ang).astype(BF16)


def _hyfilt_kernel(feat_ref, w1_ref, b1_ref, w2_ref, b2_ref, fr_ref, w3_ref, t_ref, dl_ref,
                   a_ref, s_ref, u1_ref, u2_ref, un_ref):
    seq = feat_ref.shape[0]
    fr = fr_ref[...]
    h = jnp.sin(fr * (jnp.dot(feat_ref[...], w1_ref[...], preferred_element_type=F32, precision=HIGHEST)
                      + b1_ref[...]))
    h = jnp.sin(fr * (jnp.dot(h, w2_ref[...], preferred_element_type=F32, precision=HIGHEST) + b2_ref[...]))
    window = jnp.exp(-t_ref[...] * dl_ref[...])
    row = lax.broadcasted_iota(I32, (seq, 1), 0)
    sgn = jnp.where(row % 2 == 0, 1.0, -1.0)
    cf = jnp.where(row == 0, 1.0, 2.0) * (1.0 / (2 * seq))
    for o in range(2):
        w3 = w3_ref[:, o * 2 * HY_W:(o + 1) * 2 * HY_W]
        ho = jnp.dot(h, w3, preferred_element_type=F32, precision=HIGHEST)
        fwd = ho[:, :HY_W] * window
        bwd = ho[:, HY_W:] * window
        norm = (jnp.sum(jnp.abs(fwd), axis=0, keepdims=True)
                + jnp.sum(jnp.abs(bwd), axis=0, keepdims=True))
        inv = 1.0 / norm
        ksum = (fwd + bwd) * inv
        kdif = (bwd - fwd) * inv
        kr = _dot(a_ref[...], ksum.astype(BF16))
        ki = _dot(s_ref[...], kdif.astype(BF16))
        u1_ref[o] = (kr * cf).astype(u1_ref.dtype)
        u2_ref[o] = (ki * cf).astype(u2_ref.dtype)
        un_ref[o] = jnp.sum(sgn * ksum, axis=0, keepdims=True) * (1.0 / (2 * seq))


def _hyena_filters(seq, fw1, fb1, fw2, fb2, fw3, freq, a_mat, s_mat):
    pos = jnp.arange(seq, dtype=F32)[:, None]
    t = pos / max(seq - 1, 1)
    w = (2.0 * math.pi / seq) * pos
    bands = jnp.linspace(1e-4, HY_BANDS - 1, HY_BANDS, dtype=F32)[None, :]
    feats = jnp.concatenate([t, jnp.cos(bands * w), -jnp.sin(bands * w)], axis=-1)
    feats = jnp.pad(feats, ((0, 0), (0, LANES - HY_EMB)))
    w1 = jnp.pad(fw1, ((0, LANES - HY_EMB), (0, 0)))
    max_decay = math.log(HY_DECAY_TARGET) / HY_FAST_DECAY_PCT
    min_decay = math.log(HY_DECAY_TARGET) / HY_SLOW_DECAY_PCT
    deltas = jnp.abs(jnp.linspace(min_decay, max_decay, HY_W, dtype=F32))[None, :]
    hid = fw2.shape[0]
    return pl.pallas_call(
        _hyfilt_kernel,
        out_shape=(jax.ShapeDtypeStruct((2, seq, HY_W), BF16),
                   jax.ShapeDtypeStruct((2, seq, HY_W), BF16),
                   jax.ShapeDtypeStruct((2, 1, HY_W), F32)),
        compiler_params=pltpu.CompilerParams(vmem_limit_bytes=VMEM_LIMIT_BYTES),
        name="hyfilt",
    )(feats, w1, fb1.reshape(1, hid), fw2, fb2.reshape(1, hid), freq.reshape(1, hid), fw3, t, deltas,
      a_mat, s_mat)


def _hyena_kernel(x1_ref, x2_ref, v_ref, w1_ref, w2_ref, wv_ref, b1_ref, b2_ref, bv_ref,
                  u1_ref, u2_ref, un_ref, d_ref, a_ref, s_ref, o_ref, z_scr, zb_scr, re_scr, im_scr):
    seq = x1_ref.shape[0]
    rb = min(seq, 512)
    nrb = seq // rb
    row = lax.broadcasted_iota(I32, (rb, 1), 0)
    col = row % GRID_W
    first = col == 0
    lastc = col == GRID_W - 1
    sgn = jnp.where(row % 2 == 0, 1.0, -1.0)

    def conv3(p_ref, w_ref, b_ref, rows):
        p = p_ref[rows, :].astype(F32)
        prev = jnp.where(first, 0.0, pltpu.roll(p, 1, axis=0))
        nxt = jnp.where(lastc, 0.0, pltpu.roll(p, rb - 1, axis=0))
        w = w_ref[...]
        return w[0:1, :] * prev + w[1:2, :] * p + w[2:3, :] * nxt + b_ref[...]

    def forward(o):
        nyq = None
        for i in range(nrb):
            rows = slice(i * rb, (i + 1) * rb)
            part = jnp.sum(sgn * z_scr[rows, :], axis=0, keepdims=True)
            nyq = part if nyq is None else nyq + part
        for i in range(nrb):
            rows = slice(i * rb, (i + 1) * rb)
            p = _dot(a_ref[rows, :], zb_scr[...])
            q = _dot(s_ref[rows, :], zb_scr[...])
            u1 = u1_ref[o, rows, :].astype(F32)
            u2 = u2_ref[o, rows, :].astype(F32)
            re_scr[rows, :] = (p * u1 + q * u2).astype(BF16)
            im_scr[rows, :] = (q * u1 - p * u2).astype(BF16)
        return nyq * un_ref[o]

    def inverse(o, nyq, rows):
        y = _dot(a_ref[rows, :], re_scr[...]) + _dot(s_ref[rows, :], im_scr[...]) + sgn * nyq
        return y + z_scr[rows, :] * d_ref[o:o + 1, :]

    for i in range(nrb):
        rows = slice(i * rb, (i + 1) * rb)
        v = conv3(v_ref, wv_ref, bv_ref, rows)
        z_scr[rows, :] = v
        zb_scr[rows, :] = v.astype(BF16)
    nyq = forward(0)
    for i in range(nrb):
        rows = slice(i * rb, (i + 1) * rb)
        z = conv3(x1_ref, w1_ref, b1_ref, rows) * inverse(0, nyq, rows)
        z_scr[rows, :] = z
        zb_scr[rows, :] = z.astype(BF16)
    nyq = forward(1)
    for i in range(nrb):
        rows = slice(i * rb, (i + 1) * rb)
        y = conv3(x2_ref, w2_ref, b2_ref, rows) * inverse(1, nyq, rows)
        o_ref[rows, :] = y.astype(o_ref.dtype)


def _hyena(px3, conv_w, conv_b, u1, u2, un, d_skip, a_mat, s_mat):
    b, seq, _ = px3.shape
    ct = HY_CT
    nc = HY_W // ct

    def xspec(j):
        return pl.BlockSpec((None, seq, ct), lambda c, bi: (bi, 0, j * nc + c))

    def wspec(j, rows):
        return pl.BlockSpec((rows, ct), lambda c, bi: (0, j * nc + c))

    uspec = pl.BlockSpec((2, seq, ct), lambda c, bi: (0, 0, c))
    const = pl.BlockSpec((seq, seq), lambda c, bi: (0, 0), pipeline_mode=pl.Buffered(1))
    cb = conv_b.reshape(1, 3 * HY_W)
    return pl.pallas_call(
        _hyena_kernel,
        grid=(nc, b),
        in_specs=[xspec(0), xspec(1), xspec(2), wspec(0, 3), wspec(1, 3), wspec(2, 3),
                  wspec(0, 1), wspec(1, 1), wspec(2, 1), uspec, uspec,
                  pl.BlockSpec((2, 1, ct), lambda c, bi: (0, 0, c)),
                  pl.BlockSpec((2, ct), lambda c, bi: (0, c)), const, const],
        out_specs=pl.BlockSpec((None, seq, ct), lambda c, bi: (bi, 0, c)),
        out_shape=jax.ShapeDtypeStruct((b, seq, HY_W), BF16),
        scratch_shapes=[pltpu.VMEM((seq, ct), F32), pltpu.VMEM((seq, ct), BF16),
                        pltpu.VMEM((seq, ct), BF16), pltpu.VMEM((seq, ct), BF16)],
        compiler_params=_params(("parallel", "parallel")),
        name="hyena",
    )(px3, px3, px3, conv_w, conv_w, conv_w, cb, cb, cb, u1, u2, un, d_skip, a_mat, s_mat)


def _select_rows(rows):
    n = rows[0].shape[1]
    idx = lax.broadcasted_iota(I32, (len(rows), n), 0)
    out = jnp.zeros((len(rows), n), rows[0].dtype)
    for k, r in enumerate(rows):
        out = jnp.where(idx == k, r, out)
    return out


def _first_argmax(x, iota, size):
    m = jnp.max(x, axis=0, keepdims=True)
    return jnp.min(jnp.where(x == m, iota, size), axis=0, keepdims=True)


def _mid_kernel(yhy_ref, yhg_ref, x_ref, woa_ref, wob_ref, g1_ref, sc_ref, sh_ref, g2_ref, ng_ref,
                wr_ref, rb_ref, swg_ref, swu_ref, swd_ref,
                xs_ref, h_ref, lrow_ref, w_ref, ce_ref, rel_ref, nch_ref, cnt_ref, base, wrh_ref, wrl_ref):
    tm = x_ref.shape[0]
    ne = N_EXPERTS

    @pl.when(pl.program_id(0) == 0)
    def _():
        base[...] = jnp.zeros_like(base)
        wr = wr_ref[...]
        wr_hi = wr.astype(BF16)
        wrh_ref[...] = wr_hi
        wrl_ref[...] = (wr - wr_hi.astype(F32)).astype(BF16)

    mix = _dot(yhy_ref[...], woa_ref[...]) + _dot(yhg_ref[...], wob_ref[...])
    xm = x_ref[...] + g1_ref[...] * mix
    h = _rms_mod(xm, ng_ref[...], sc_ref[...], sh_ref[...])
    hb = h.astype(BF16)
    h_ref[...] = hb
    act = (_silu(_dot(hb, swg_ref[...])) * _dot(hb, swu_ref[...])).astype(BF16)
    xs_ref[...] = xm + g2_ref[...] * _dot(act, swd_ref[...])

    h_lo = (h - hb.astype(F32)).astype(BF16)
    logits = _dot_nt(wrh_ref[...], hb) + (_dot_nt(wrh_ref[...], h_lo) + _dot_nt(wrl_ref[...], hb))
    scores = jax.nn.sigmoid(logits)
    biased = scores + rb_ref[...]
    neg = -jnp.inf
    iota_g = lax.broadcasted_iota(I32, (GROUP_SIZE, tm), 0)
    grp = []
    for g in range(N_GROUPS):
        blk = biased[g * GROUP_SIZE:(g + 1) * GROUP_SIZE, :]
        m1 = jnp.max(blk, axis=0, keepdims=True)
        i1 = jnp.min(jnp.where(blk == m1, iota_g, GROUP_SIZE), axis=0, keepdims=True)
        m2 = jnp.max(jnp.where(iota_g == i1, neg, blk), axis=0, keepdims=True)
        grp.append(m1 + m2)
    gsc = _select_rows(grp)
    iota8 = lax.broadcasted_iota(I32, (N_GROUPS, tm), 0)
    gsel = iota8 < 0
    for _ in range(TOPK_GROUPS):
        hit = iota8 == _first_argmax(gsc, iota8, N_GROUPS)
        gsel = gsel | hit
        gsc = jnp.where(hit, neg, gsc)
    gself = jnp.where(gsel, 1.0, 0.0)
    iota_e = lax.broadcasted_iota(I32, (ne, tm), 0)
    gid = iota_e // GROUP_SIZE
    emask = jnp.zeros((ne, tm), F32)
    for g in range(N_GROUPS):
        emask = jnp.where(gid == g, gself[g:g + 1, :], emask)
    cur = jnp.where(emask > 0.0, biased, neg)

    sel = iota_e < 0
    idxs, wts = [], []
    for _ in range(TOP_K):
        idx = _first_argmax(cur, iota_e, ne)
        hit = iota_e == idx
        wts.append(jnp.sum(jnp.where(hit, scores, 0.0), axis=0, keepdims=True))
        idxs.append(idx)
        sel = sel | hit
        cur = jnp.where(hit, neg, cur)
    wsum = wts[0]
    for wk in wts[1:]:
        wsum = wsum + wk
    scale = ROUTED_SCALE / wsum

    r = lax.broadcasted_iota(I32, (tm, tm), 0)
    s = lax.broadcasted_iota(I32, (tm, tm), 1)
    upper = jnp.where(r < s, 1.0, 0.0).astype(BF16)
    sel_f = jnp.where(sel, 1.0, 0.0)
    prefix = _dot(sel_f.astype(BF16), upper)
    cnt = jnp.sum(sel_f, axis=1, keepdims=True)
    cnt_al = jnp.floor((cnt + (SLOT_ALIGN - 1)) * (1.0 / SLOT_ALIGN)) * SLOT_ALIGN
    re = lax.broadcasted_iota(I32, (ne, ne), 0)
    ce = lax.broadcasted_iota(I32, (ne, ne), 1)
    lower = jnp.where(ce < re, 1.0, 0.0).astype(BF16)
    loc = _dot(lower, jnp.broadcast_to(cnt_al, (ne, LANES)).astype(BF16))[:, 0:1]
    lrow_all = loc + prefix
    lrows = [jnp.sum(jnp.where(iota_e == idx, lrow_all, 0.0), axis=0, keepdims=True) for idx in idxs]

    nc = ce_ref.shape[-1]
    jrow = lax.broadcasted_iota(I32, (1, nc), 1).astype(F32) * SLOT_ALIGN
    owner = jnp.sum(jnp.where(loc + cnt_al <= jrow, 1.0, 0.0), axis=0, keepdims=True)
    owner = jnp.minimum(owner, ne - 1.0).astype(I32)
    iota_ec = lax.broadcasted_iota(I32, (ne, nc), 0)
    rel = jnp.sum(jnp.where(iota_ec == owner, base[...] - loc, 0.0), axis=0, keepdims=True) + jrow
    base[...] = base[...] + cnt_al
    used = jnp.sum(cnt_al, axis=0, keepdims=True)
    spare = jnp.minimum(jrow - used, (MOE_ROW_CHUNK - SLOT_ALIGN) * 1.0)
    spare = spare + (pl.program_id(0) % 2).astype(F32) * MOE_ROW_CHUNK
    unused = jrow >= used
    owner = jnp.where(unused, ne, owner)
    rel = jnp.where(unused, spare, rel)

    lrow_ref[...] = _select_rows(lrows).astype(I32)
    w_ref[...] = _select_rows([wk * scale for wk in wts])
    ce_ref[...] = owner
    rel_ref[...] = rel.astype(I32)
    nch = used * (1.0 / SLOT_ALIGN)
    nch_ref[...] = jnp.broadcast_to(nch, nch_ref.shape).astype(I32)
    cnt_ref[...] = base[...].astype(I32)


def _mid(yhy, yhg, x2d, seq, woa, wob, g1, sc2, sh2, g2, norm_g, wr_t, rbias, swg, swu, swd, tm):
    t, d = x2d.shape
    per = seq // tm
    ff = swg.shape[1]

    def full(shape):
        return pl.BlockSpec(shape, lambda i: (0,) * len(shape))

    mspec = pl.BlockSpec((None, 1, d), lambda i: (i // per, 0, 0))
    kt = pl.BlockSpec((TOP_K, tm), lambda i: (0, i))
    nt = t // tm

    def per_tile(n):
        return pl.BlockSpec((None, 1, n), lambda i: (i, 0, 0))

    return pl.pallas_call(
        _mid_kernel,
        grid=(nt,),
        in_specs=[pl.BlockSpec((tm, HY_W), lambda i: (i, 0)), pl.BlockSpec((tm, HG_W), lambda i: (i, 0)),
                  pl.BlockSpec((tm, d), lambda i: (i, 0)), full((HY_W, d)), full((HG_W, d)),
                  mspec, mspec, mspec, mspec, full((1, d)), full((N_EXPERTS, d)),
                  full((N_EXPERTS, 1)), full((d, ff)), full((d, ff)), full((ff, d))],
        out_specs=[pl.BlockSpec((tm, d), lambda i: (i, 0)), pl.BlockSpec((tm, d), lambda i: (i, 0)),
                   kt, kt, per_tile(MOE_CHUNKS), per_tile(MOE_CHUNKS), per_tile(LANES), full((N_EXPERTS, 1))],
        out_shape=(jax.ShapeDtypeStruct((t, d), F32), jax.ShapeDtypeStruct((t, d), BF16),
                   jax.ShapeDtypeStruct((TOP_K, t), I32), jax.ShapeDtypeStruct((TOP_K, t), F32),
                   jax.ShapeDtypeStruct((nt, 1, MOE_CHUNKS), I32), jax.ShapeDtypeStruct((nt, 1, MOE_CHUNKS), I32),
                   jax.ShapeDtypeStruct((nt, 1, LANES), I32), jax.ShapeDtypeStruct((N_EXPERTS, 1), I32)),
        scratch_shapes=[pltpu.VMEM((N_EXPERTS, 1), F32), pltpu.VMEM((N_EXPERTS, d), BF16),
                        pltpu.VMEM((N_EXPERTS, d), BF16)],
        compiler_params=_params(("arbitrary",)),
        name="mid",
    )(yhy, yhg, x2d, woa, wob, g1, sc2, sh2, g2, norm_g.reshape(1, d), wr_t,
      rbias.reshape(N_EXPERTS, 1), swg, swu, swd)


def _pack_bf16_pairs(x):
    n = x.shape[1] // 2
    lo = lax.shift_right_logical(pltpu.bitcast(x[:, :n], I32), 16)
    hi = pltpu.bitcast(x[:, n:], I32) & -65536
    return hi | lo


def _unpack_bf16_pairs(p):
    lo = pltpu.bitcast(lax.shift_left(p, 16), F32).astype(BF16)
    hi = pltpu.bitcast(p & -65536, F32).astype(BF16)
    return lo, hi


def _row_chunks(nch):
    return (nch * SLOT_ALIGN + MOE_ROW_CHUNK - 1) // MOE_ROW_CHUNK


def _start_chunk_copies(c, ce_ref, rel_ref, ps_ref, make):
    per = MOE_ROW_CHUNK // SLOT_ALIGN
    for q in range(per):
        j = c * per + q
        glob = pl.multiple_of(ps_ref[ce_ref[j]] + rel_ref[j], SLOT_ALIGN)
        make(pl.multiple_of(j * SLOT_ALIGN, SLOT_ALIGN), glob).start()


def _wait_row_chunks(n, row_chunk_copy):
    def wait(_, carry):
        row_chunk_copy.wait()
        return carry

    lax.fori_loop(0, n, wait, 0)


def _dispatch_kernel(nch_ref, ce_ref, rel_ref, ps_ref, h_ref, lrow_ref, xs_hbm, xloc, zbuf, sems):
    tm = h_ref.shape[0]
    bm = zbuf.shape[0]
    rc = MOE_ROW_CHUNK
    sem = sems.at[0]

    @pl.when(pl.program_id(0) == 0)
    def _():
        zbuf[...] = jnp.zeros_like(zbuf)

        def zcopy(e):
            start = pl.multiple_of(ps_ref[e + 1] - bm, bm)
            return pltpu.make_async_copy(zbuf, xs_hbm.at[pl.ds(start, bm), :], sem)

        def start(e, carry):
            @pl.when(ps_ref[e + 1] > ps_ref[e])
            def _():
                zcopy(e).start()
            return carry

        def wait(e, carry):
            @pl.when(ps_ref[e + 1] > ps_ref[e])
            def _():
                zcopy(e).wait()
            return carry

        lax.fori_loop(0, N_EXPERTS, start, 0)
        lax.fori_loop(0, N_EXPERTS, wait, 0)

        def tcopy(j):
            return pltpu.make_async_copy(zbuf, xs_hbm.at[pl.ds(pl.multiple_of(j * bm, bm), bm), :], sem)

        def tstart(j, carry):
            tcopy(j).start()
            return carry

        def twait(j, carry):
            tcopy(j).wait()
            return carry

        first_tail = ps_ref[N_EXPERTS] // bm
        lax.fori_loop(first_tail, xs_hbm.shape[0] // bm, tstart, 0)
        lax.fori_loop(first_tail, xs_hbm.shape[0] // bm, twait, 0)

    i = pl.program_id(0)
    slot = i % 2
    buf = xloc.at[slot]
    n_rc = _row_chunks(nch_ref[i])
    lrow = lrow_ref[...].astype(F32)
    sub = MOE_SUB_ROWS
    rid = lax.broadcasted_iota(I32, (sub, tm), 0).astype(F32).astype(BF16)
    one = jnp.ones((sub, tm), BF16)

    def select(r0):
        rel = (lrow - jnp.asarray(r0, I32).astype(F32)).astype(BF16)
        onehot = jnp.zeros((sub, tm), BF16)
        for k in range(TOP_K):
            onehot = jnp.where(rid == rel[k:k + 1, :], one, onehot)
        buf[pl.ds(r0, sub), :] = _pack_bf16_pairs(_dot(onehot, h_ref[...]))

    def send(c):
        _start_chunk_copies(c, ce_ref, rel_ref, ps_ref,
                            lambda loc, glob: pltpu.make_async_copy(buf.at[pl.ds(loc, SLOT_ALIGN), :],
                                                                    xs_hbm.at[pl.ds(glob, SLOT_ALIGN), :],
                                                                    sems.at[slot]))

    def retire(tile_slot, n):
        _wait_row_chunks(n, pltpu.make_async_copy(xloc.at[tile_slot, pl.ds(0, rc), :],
                                                  xs_hbm.at[pl.ds(0, rc), :], sems.at[tile_slot]))

    for j in range(rc // sub):
        select(j * sub)

    def body(c, carry):
        send(c - 1)
        r0 = pl.multiple_of(c * rc, rc)
        for j in range(rc // sub):
            select(r0 + j * sub)
        return carry

    lax.fori_loop(1, n_rc, body, 0)
    send(n_rc - 1)

    @pl.when(i > 0)
    def _():
        retire(1 - slot, _row_chunks(nch_ref[i - 1]))

    @pl.when(i == pl.num_programs(0) - 1)
    def _():
        retire(slot, n_rc)


def _local_rows(tm):
    bound = TOP_K * tm + N_EXPERTS * (SLOT_ALIGN - 1)
    return -(-bound // MOE_ROW_CHUNK) * MOE_ROW_CHUNK


def _moe_smem_specs():
    chunk_list = pl.BlockSpec((MOE_CHUNKS,), lambda i, nch: (i,), memory_space=pltpu.SMEM)
    return [chunk_list, chunk_list,
            pl.BlockSpec((N_EXPERTS + 1,), lambda i, nch: (0,), memory_space=pltpu.SMEM)]


def _dispatch(h2d, lrow_kt, ce, rel, nch, ps, n_slots, tm):
    t, d = h2d.shape
    grid_spec = pltpu.PrefetchScalarGridSpec(
        num_scalar_prefetch=1,
        grid=(t // tm,),
        in_specs=_moe_smem_specs() + [pl.BlockSpec((tm, d), lambda i, nch: (i, 0)),
                                      pl.BlockSpec((TOP_K, tm), lambda i, nch: (0, i))],
        out_specs=pl.BlockSpec(memory_space=pl.ANY),
        scratch_shapes=[pltpu.VMEM((2, _local_rows(tm), d // 2), I32), pltpu.VMEM((FFN_BLOCK, d // 2), I32),
                        pltpu.SemaphoreType.DMA((2,))],
    )

    return pl.pallas_call(
        _dispatch_kernel,
        grid_spec=grid_spec,
        out_shape=jax.ShapeDtypeStruct((n_slots, d // 2), I32),
        compiler_params=_params(("arbitrary",)),
        name="dispatch",
    )(nch, ce, rel, ps, h2d, lrow_kt)


def _ffn_kernel(ps_ref, wg_ref, wu_ref, wd_ref, x_hbm, y_hbm, xbuf, ybuf, wg_bf, wu_bf, wd_bf, xsem, ysem):
    e = pl.program_id(0)
    nx, bm, half = xbuf.shape
    ny = ybuf.shape[0]
    ahead = nx - 1
    n_blocks = y_hbm.shape[0] // bm
    g0 = ps_ref[e] // bm
    g1 = ps_ref[e + 1] // bm

    def rows(g):
        return pl.ds(pl.multiple_of(g * bm, bm), bm)

    def x_copy(g):
        s = g % nx
        return pltpu.make_async_copy(x_hbm.at[rows(g), :], xbuf.at[s], xsem.at[s])

    def y_copy(g, s):
        return pltpu.make_async_copy(ybuf.at[s], y_hbm.at[rows(g), :], ysem.at[s])

    @pl.when(e == 0)
    def _():
        ybuf[...] = jnp.zeros_like(ybuf)
        for g in range(ahead):
            x_copy(g).start()
        for s in range(ny):
            y_copy(n_blocks - 1 - s, s).start()

    @pl.when(g1 > g0)
    def _():
        wg_bf[...] = wg_ref[...].astype(BF16)
        wu_bf[...] = wu_ref[...].astype(BF16)
        wd_bf[...] = wd_ref[...].astype(BF16)

    def block(g, carry):
        s = g % ny
        xs = g % nx
        x_copy(g).wait()
        x_copy(g + ahead).start()
        hb = bm // 2
        halves = [pl.ds(p * hb, hb) for p in range(2)]
        gu = []
        for r in halves:
            lo, hi = _unpack_bf16_pairs(xbuf[xs, r, :])
            gu.append((_dot(lo, wg_bf[:half, :]) + _dot(hi, wg_bf[half:, :]),
                       _dot(lo, wu_bf[:half, :]) + _dot(hi, wu_bf[half:, :])))
        y_copy(g, s).wait()
        for r, (gate, up) in zip(halves, gu):
            act = (_silu(gate) * up).astype(BF16)
            y = _dot(act, wd_bf[...])
            ybuf[s, r, :] = _pack_bf16_pairs(y.astype(BF16).astype(F32))
        y_copy(g, s).start(priority=1)
        return carry

    lax.fori_loop(g0, g1, block, 0)

    @pl.when(e == pl.num_programs(0) - 1)
    def _():
        for k in range(ahead):
            x_copy(g1 + k).wait()
        for s in range(ny):
            y_copy(0, s).wait()
        ybuf[0] = jnp.zeros((bm, half), ybuf.dtype)

        def zstart(g, carry):
            y_copy(g, 0).start()
            return carry

        def zwait(g, carry):
            y_copy(g, 0).wait()
            return carry

        lax.fori_loop(g1, n_blocks, zstart, 0)
        lax.fori_loop(g1, n_blocks, zwait, 0)


def _ffn(xs, ps, ew_gate, ew_up, ew_down):
    n_slots, half = xs.shape
    d = 2 * half
    bm = FFN_BLOCK
    n_exp, _, ff = ew_gate.shape
    grid_spec = pltpu.PrefetchScalarGridSpec(
        num_scalar_prefetch=1,
        grid=(n_exp,),
        in_specs=[pl.BlockSpec((None, d, ff), lambda e, ps: (e, 0, 0)),
                  pl.BlockSpec((None, d, ff), lambda e, ps: (e, 0, 0)),
                  pl.BlockSpec((None, ff, d), lambda e, ps: (e, 0, 0)),
                  pl.BlockSpec(memory_space=pl.ANY)],
        out_specs=pl.BlockSpec(memory_space=pl.ANY),
        scratch_shapes=[pltpu.VMEM((FFN_BUFFERS, bm, half), I32), pltpu.VMEM((FFN_BUFFERS, bm, half), I32),
                        pltpu.VMEM((d, ff), BF16), pltpu.VMEM((d, ff), BF16), pltpu.VMEM((ff, d), BF16),
                        pltpu.SemaphoreType.DMA((FFN_BUFFERS,)), pltpu.SemaphoreType.DMA((FFN_BUFFERS,))],
    )
    return pl.pallas_call(
        _ffn_kernel,
        grid_spec=grid_spec,
        out_shape=jax.ShapeDtypeStruct((n_slots, half), I32),
        compiler_params=_params(("arbitrary",)),
        name="ffn",
    )(ps, ew_gate, ew_up, ew_down, xs)


def _combine_kernel(nch_ref, ce_ref, rel_ref, cen_ref, reln_ref, ps_ref, lrow_ref, w_ref, xs_ref, g2_ref, fg_ref,
                    y_hbm, o_ref, yloc, acc, sems):
    tm = xs_ref.shape[0]
    half = yloc.shape[2]
    rc = MOE_ROW_CHUNK
    i = pl.program_id(0)
    last = pl.num_programs(0) - 1
    slot = i % 2
    nxt = jnp.minimum(i + 1, last)
    n_rc = _row_chunks(nch_ref[i])
    n_next = _row_chunks(nch_ref[nxt])

    def fetch(c, ce, rel, s):
        _start_chunk_copies(c, ce, rel, ps_ref,
                            lambda loc, glob: pltpu.make_async_copy(y_hbm.at[pl.ds(glob, SLOT_ALIGN), :],
                                                                    yloc.at[s, pl.ds(loc, SLOT_ALIGN), :],
                                                                    sems.at[s]))

    def arrived(s, n):
        _wait_row_chunks(n, pltpu.make_async_copy(y_hbm.at[pl.ds(0, rc), :], yloc.at[s, pl.ds(0, rc), :],
                                                  sems.at[s]))

    def fetch_own(c, carry):
        fetch(c, ce_ref, rel_ref, slot)
        return carry

    def fetch_next(c, carry):
        fetch(c, cen_ref, reln_ref, 1 - slot)
        return carry

    @pl.when(i == 0)
    def _():
        yloc[...] = jnp.zeros_like(yloc)
        lax.fori_loop(0, n_rc, fetch_own, 0)

    n_prev = _row_chunks(nch_ref[jnp.maximum(i - 1, 0)])
    arrived(slot, jnp.where(i == 0, n_rc, jnp.maximum(n_prev, n_rc)))

    lrow = lrow_ref[...].astype(F32)
    wrow = w_ref[...].astype(BF16)
    acc[...] = jnp.zeros_like(acc)
    sub = MOE_SUB_ROWS
    rid = lax.broadcasted_iota(I32, (sub, tm), 0).astype(F32).astype(BF16)

    def weights(r0):
        rel = (lrow - r0.astype(F32)).astype(BF16)
        wt = jnp.zeros((sub, tm), BF16)
        for k in range(TOP_K):
            wt = jnp.where(rid == rel[k:k + 1, :], jnp.broadcast_to(wrow[k:k + 1, :], (sub, tm)), wt)
        return wt

    def gather(c, carry):
        fetch(c, cen_ref, reln_ref, 1 - slot)
        r0 = pl.multiple_of(c * rc, rc)
        lo_sum, hi_sum = None, None
        for j in range(rc // sub):
            wt = weights(r0 + j * sub)
            lo, hi = _unpack_bf16_pairs(yloc[slot, pl.ds(r0 + j * sub, sub), :])
            lo_sum = _dot_tn(wt, lo) if j == 0 else lo_sum + _dot_tn(wt, lo)
            hi_sum = _dot_tn(wt, hi) if j == 0 else hi_sum + _dot_tn(wt, hi)
        acc[:, :half] += lo_sum
        acc[:, half:] += hi_sum
        return carry

    lax.fori_loop(0, n_rc, gather, 0)
    lax.fori_loop(n_rc, n_next, fetch_next, 0)

    @pl.when(i == last)
    def _():
        arrived(1 - slot, jnp.maximum(n_rc, n_next))

    x = xs_ref[...] + g2_ref[...] * acc[...]
    o_ref[...] = x * lax.rsqrt(jnp.mean(x * x, axis=-1, keepdims=True) + NORM_EPS) * fg_ref[...]


def _combine(y_sorted, lrow_kt, w_kt, ce, rel, nch, ps, xs2d, seq, g2, final_g, tm):
    t, d = xs2d.shape
    per = seq // tm
    nt = t // tm
    own, _, ranges = _moe_smem_specs()
    nxt = pl.BlockSpec((MOE_CHUNKS,), lambda i, nch: (jnp.minimum(i + 1, nt - 1),), memory_space=pltpu.SMEM)
    grid_spec = pltpu.PrefetchScalarGridSpec(
        num_scalar_prefetch=1,
        grid=(nt,),
        in_specs=[own, own, nxt, nxt, ranges,
                  pl.BlockSpec((TOP_K, tm), lambda i, nch: (0, i)),
                  pl.BlockSpec((TOP_K, tm), lambda i, nch: (0, i)),
                  pl.BlockSpec((tm, d), lambda i, nch: (i, 0)),
                  pl.BlockSpec((None, 1, d), lambda i, nch: (i // per, 0, 0)),
                  pl.BlockSpec((1, d), lambda i, nch: (0, 0)),
                  pl.BlockSpec(memory_space=pl.ANY)],
        out_specs=pl.BlockSpec((tm, d), lambda i, nch: (i, 0)),
        scratch_shapes=[pltpu.VMEM((2, _local_rows(tm), d // 2), I32), pltpu.VMEM((tm, d), F32),
                        pltpu.SemaphoreType.DMA((2,))],
    )
    return pl.pallas_call(
        _combine_kernel,
        grid_spec=grid_spec,
        out_shape=jax.ShapeDtypeStruct((t, d), F32),
        compiler_params=_params(("arbitrary",)),
        name="combine",
    )(nch, ce, rel, ce, rel, ps, lrow_kt, w_kt, xs2d, g2, final_g.reshape(1, d), y_sorted)


def kernel(x, c, ctx, c_ctx, w_mod, b_mod, norm1_g, norm2_g, w_in, w_out, hy_conv_w, hy_conv_b,
           hy_fw1, hy_fb1, hy_fw2, hy_fb2, hy_fw3, hy_freq, hy_d, hg_lb_logits, hg_norm_g,
           w_router, router_bias, ew_gate, ew_up, ew_down, sw_gate, sw_up, sw_down, final_g):
    b, seq, d = x.shape
    ctx_len = ctx.shape[1]
    t = b * seq
    layer = 0

    lower = jnp.cumsum(jax.nn.softmax(hg_lb_logits.astype(F32), axis=1), axis=1)
    lb_f, lb_b = lower[0, layer], lower[1, layer]

    rows = -(-(b + 1) // 8) * 8
    cc = jnp.concatenate([c, c_ctx[None, :], jnp.zeros((rows - b - 1, d), F32)], axis=0)
    mod = _modulation(cc, w_mod[layer], b_mod[layer])
    sh1, sc1, g1, sh2, sc2, g2 = (m.reshape(b, 1, d) for m in jnp.split(mod[:b], 6, axis=-1))
    csh1, csc1 = (jnp.broadcast_to(m.reshape(1, 1, d), (b, 1, d))
                  for m in jnp.split(mod[b:b + 1], 6, axis=-1)[:2])

    w_in_bf = w_in[layer].astype(BF16)
    x2d = x.reshape(t, d)
    px = _inproj(x2d, seq, norm1_g[layer], sc1, sh1, w_in_bf, min(seq, 512), w_in_bf.shape[1])
    lo = 3 * HY_W + HG_W
    pc = _inproj(ctx.reshape(b * ctx_len, d), ctx_len, norm1_g[layer], csc1, csh1,
                 w_in_bf[:, lo:lo + 3 * HG_W], ctx_len, HG_W)
    px3 = px.reshape(b, seq, -1)
    pc3 = pc.reshape(b, ctx_len, -1)

    y_hg = _hgrn(px3, pc3, lb_f, lb_b, hg_norm_g[layer])

    a_mat, s_mat = _dft_mats(seq)
    u1, u2, un = _hyena_filters(seq, hy_fw1[layer], hy_fb1[layer], hy_fw2[layer], hy_fb2[layer],
                                hy_fw3[layer], hy_freq[layer], a_mat, s_mat)
    y_hy = _hyena(px3, hy_conv_w[layer], hy_conv_b[layer], u1, u2, un, hy_d[layer], a_mat, s_mat)

    w_out_bf = w_out[layer].astype(BF16)
    tm = min(seq, MOE_TILE)
    nt = t // tm
    xs, h2, lrow_kt, w_kt, ce, rel, nch, counts = _mid(
        y_hy.reshape(t, HY_W), y_hg.reshape(t, HG_W), x2d, seq, w_out_bf[:HY_W], w_out_bf[HY_W:],
        g1, sc2, sh2, g2, norm2_g[layer], w_router[layer].T, router_bias[layer],
        sw_gate[layer].astype(BF16), sw_up[layer].astype(BF16), sw_down[layer].astype(BF16), tm)

    bm = FFN_BLOCK
    counts = counts.reshape(N_EXPERTS)
    padded = (counts + bm - 1) // bm * bm
    p_ends = jnp.cumsum(padded)
    ps = jnp.concatenate([p_ends - padded, p_ends[-1:]]).astype(I32)
    spare_blocks = max(-(-2 * MOE_ROW_CHUNK // bm), FFN_BUFFERS)
    n_blocks = -(-(t * TOP_K + nt * N_EXPERTS * (SLOT_ALIGN - 1)) // bm) + N_EXPERTS + spare_blocks

    ce, rel, nch = ce.reshape(-1), rel.reshape(-1), nch[:, 0, 0]
    x_sorted = _dispatch(h2, lrow_kt, ce, rel, nch, ps, n_blocks * bm, tm)
    y_sorted = _ffn(x_sorted, ps, ew_gate[layer], ew_up[layer], ew_down[layer])
    out = _combine(y_sorted, lrow_kt, w_kt, ce, rel, nch, ps, xs, seq, g2, final_g, tm)
    return out.reshape(b, seq, d)
```

```python
import functools
import math

import jax
import jax.numpy as jnp
from jax import lax
from jax.experimental import pallas as pl
from jax.experimental.pallas import tpu as pltpu

F32 = jnp.float32
BF16 = jnp.bfloat16
I32 = jnp.int32
HIGHEST = lax.Precision.HIGHEST

GRID_W = 64
HY_W = 512
HG_W = 512
HY_EMB = 33
HY_BANDS = 16
HY_DECAY_TARGET = 1e-2
HY_FAST_DECAY_PCT = 0.3
HY_SLOW_DECAY_PCT = 1.5
HG_HEAD_DIM = 128
HG_HEADS = 4
HG_SCALE = HG_HEAD_DIM ** -0.5
HG_CHUNK = 64
N_EXPERTS = 256
TOP_K = 8
N_GROUPS = 8
TOPK_GROUPS = 4
GROUP_SIZE = N_EXPERTS // N_GROUPS
ROUTED_SCALE = 2.5
NORM_EPS = 1e-6

VMEM_LIMIT_BYTES = 56 * 1024 * 1024
LANES = 128
FFN_BLOCK = 512
FFN_BUFFERS = 4
MOE_TILE = 512
SLOT_ALIGN = 8
MOE_ROW_CHUNK = 1024
MOE_SUB_ROWS = 256
MOE_CHUNKS = 1024
HY_CT = 256


def _params(sem, vmem=VMEM_LIMIT_BYTES):
    return pltpu.CompilerParams(dimension_semantics=sem, vmem_limit_bytes=vmem)


def _silu(x):
    return x * jax.nn.sigmoid(x)


def _dot(a, b):
    return jnp.dot(a, b, preferred_element_type=F32)


def _dot_nt(a, b):
    return lax.dot_general(a, b, (((1,), (1,)), ((), ())), preferred_element_type=F32)


def _dot_tn(a, b):
    return lax.dot_general(a, b, (((0,), (0,)), ((), ())), preferred_element_type=F32)


def _mod_kernel(c_ref, w_ref, b_ref, o_ref):
    s = _silu(c_ref[...])
    o_ref[...] = jnp.dot(s, w_ref[...], preferred_element_type=F32, precision=HIGHEST) + b_ref[...]


def _modulation(cc, w_mod, b_mod):
    rows, d = cc.shape
    n = w_mod.shape[1]
    tn = 1024
    return pl.pallas_call(
        _mod_kernel,
        grid=(n // tn,),
        in_specs=[pl.BlockSpec((rows, d), lambda j: (0, 0)),
                  pl.BlockSpec((d, tn), lambda j: (0, j)),
                  pl.BlockSpec((1, tn), lambda j: (0, j))],
        out_specs=pl.BlockSpec((rows, tn), lambda j: (0, j)),
        out_shape=jax.ShapeDtypeStruct((rows, n), F32),
        compiler_params=_params(("parallel",)),
        name="mod",
    )(cc, w_mod, b_mod.reshape(1, n))


def _rms_mod(x, g, sc, sh):
    y = x * lax.rsqrt(jnp.mean(x * x, axis=-1, keepdims=True) + NORM_EPS) * g
    return y * (1.0 + sc) + sh


def _inproj_kernel(x_ref, g_ref, sc_ref, sh_ref, w_ref, o_ref, h_scr):
    @pl.when(pl.program_id(1) == 0)
    def _():
        h_scr[...] = _rms_mod(x_ref[...], g_ref[...], sc_ref[...], sh_ref[...]).astype(BF16)

    o_ref[...] = _dot(h_scr[...], w_ref[...]).astype(o_ref.dtype)


def _inproj(x2d, seq, g, sc, sh, w_bf, tm, tn):
    t, d = x2d.shape
    n = w_bf.shape[1]
    per = seq // tm
    w_mode = dict(pipeline_mode=pl.Buffered(1)) if tn == n else {}
    return pl.pallas_call(
        _inproj_kernel,
        grid=(t // tm, n // tn),
        in_specs=[pl.BlockSpec((tm, d), lambda i, j: (i, 0)),
                  pl.BlockSpec((1, d), lambda i, j: (0, 0)),
                  pl.BlockSpec((None, 1, d), lambda i, j: (i // per, 0, 0)),
                  pl.BlockSpec((None, 1, d), lambda i, j: (i // per, 0, 0)),
                  pl.BlockSpec((d, tn), lambda i, j: (0, j), **w_mode)],
        out_specs=pl.BlockSpec((tm, tn), lambda i, j: (i, j)),
        out_shape=jax.ShapeDtypeStruct((t, n), BF16),
        scratch_shapes=[pltpu.VMEM((tm, d), BF16)],
        compiler_params=_params(("parallel", "arbitrary")),
        name="inproj",
    )(x2d, g.reshape(1, d), sc, sh, w_bf)


def _hg_steps(chains):
    c = HG_CHUNK
    r = lax.broadcasted_iota(I32, (c, c), 0)
    s = lax.broadcasted_iota(I32, (c, c), 1)
    geo = {False: (r >= s, c // 2 - 1, c - 1), True: (r <= s, c // 2, 0)}
    tri = {rev: jnp.where(g[0], 1.0, 0.0).astype(BF16) for rev, g in geo.items()}

    work = []
    for ch in chains:
        lb = ch["lb"]
        sig = jax.nn.sigmoid(ch["fr"])
        lf = jnp.log(lb + (1.0 - lb) * sig)
        k = (1.0 - lb) * (1.0 - sig)
        hi = lf.astype(BF16)
        lo = (lf - hi.astype(F32)).astype(BF16)
        t = tri[ch["rev"]]
        work.append(dict(k=k, bc=_dot(t, hi) + _dot(t, lo)))
    for ch, w in zip(chains, work):
        mask, mid, last = geo[ch["rev"]]
        bc = w["bc"]
        b_mid = bc[mid:mid + 1, :]
        b_last = bc[last:last + 1, :]
        km = w["k"] * jnp.exp(b_mid - bc)
        kd = (km * jnp.exp(b_last - b_mid)).astype(BF16)
        w["ut"] = _dot_tn(ch["v"], kd)
        w["decay"] = jnp.exp(b_last)
        if ch["q"] is not None:
            qm = ch["q"] * jnp.exp(bc - b_mid)
            w["att"] = _dot_nt(qm.astype(BF16), km.astype(BF16))
            qe = (qm * jnp.exp(b_mid)).astype(BF16)
            w["inter"] = _dot_nt(qe, ch["st"].astype(BF16))
    out = []
    for ch, w in zip(chains, work):
        o = None
        if ch["q"] is not None:
            att = jnp.where(geo[ch["rev"]][0], w["att"], 0.0).astype(BF16)
            o = _dot(att, ch["v"]) + w["inter"]
        out.append((o, ch["st"] * w["decay"] + w["ut"]))
    return out


def _hgrn_kernel(q_ref, ff_ref, fb_ref, i_ref, g_ref, cff_ref, cfb_ref, ci_ref,
                 lbf_ref, lbb_ref, ng_ref, o_ref, qs_scr, of_scr, ob_scr, st_scr):
    seq = q_ref.shape[0]
    ctx = cff_ref.shape[0]
    c = HG_CHUNK
    dh = HG_HEAD_DIM
    rb = min(seq, 256)

    for i in range(seq // rb):
        rows = slice(i * rb, (i + 1) * rb)
        qs_scr[rows, :] = (_silu(q_ref[rows, :].astype(F32)) * HG_SCALE).astype(BF16)
    st_scr[...] = jnp.zeros_like(st_scr)

    def chains(nchunks, ffr, fbr, vr, with_q):
        def body(n, carry):
            chains = []
            for h in range(HG_HEADS):
                cols = slice(h * dh, (h + 1) * dh)
                for rev, fref, lbref in ((False, ffr, lbf_ref), (True, fbr, lbb_ref)):
                    ci = (nchunks - 1 - n) if rev else n
                    rows = pl.ds(pl.multiple_of(ci * c, c), c)
                    chains.append(dict(
                        rev=rev, rows=rows, cols=cols, fr=fref[rows, cols].astype(F32), v=vr[rows, cols],
                        lb=lbref[:, cols], st=st_scr[len(chains)],
                        q=qs_scr[rows, cols].astype(F32) if with_q else None))
            for slot, (ch, (o, st)) in enumerate(zip(chains, _hg_steps(chains))):
                st_scr[slot] = st
                if with_q:
                    (ob_scr if ch["rev"] else of_scr)[ch["rows"], ch["cols"]] = o
            return carry

        lax.fori_loop(0, nchunks, body, 0)

    chains(ctx // c, cff_ref, cfb_ref, ci_ref, False)
    chains(seq // c, ff_ref, fb_ref, i_ref, True)

    ng = ng_ref[...]
    for i in range(seq // rb):
        rows = slice(i * rb, (i + 1) * rb)
        gate = _silu(g_ref[rows, :].astype(F32))
        for h in range(HG_HEADS):
            cols = slice(h * dh, (h + 1) * dh)
            o = of_scr[rows, cols] + ob_scr[rows, cols]
            on = o * lax.rsqrt(jnp.mean(o * o, axis=-1, keepdims=True) + NORM_EPS) * ng
            o_ref[rows, cols] = (on * gate[:, cols]).astype(o_ref.dtype)


def _hgrn(px3, pc3, lb_f, lb_b, norm_g):
    b, seq, _ = px3.shape
    ctx = pc3.shape[1]
    dh = HG_HEAD_DIM
    base = 3 * HY_W // HG_W

    def xspec(j):
        return pl.BlockSpec((None, seq, HG_W), lambda bi: (bi, 0, base + j))

    def cspec(j):
        return pl.BlockSpec((None, ctx, HG_W), lambda bi: (bi, 0, j))

    vec = pl.BlockSpec((1, HG_W), lambda bi: (0, 0))
    return pl.pallas_call(
        _hgrn_kernel,
        grid=(b,),
        in_specs=[xspec(0), xspec(1), xspec(2), xspec(3), xspec(4), cspec(0), cspec(1), cspec(2),
                  vec, vec, pl.BlockSpec((1, dh), lambda bi: (0, 0))],
        out_specs=pl.BlockSpec((None, seq, HG_W), lambda bi: (bi, 0, 0)),
        out_shape=jax.ShapeDtypeStruct((b, seq, HG_W), BF16),
        scratch_shapes=[pltpu.VMEM((seq, HG_W), BF16), pltpu.VMEM((seq, HG_W), F32),
                        pltpu.VMEM((seq, HG_W), F32), pltpu.VMEM((2 * HG_HEADS, dh, dh), F32)],
        compiler_params=_params(("parallel",)),
        name="hgrn",
    )(px3, px3, px3, px3, px3, pc3, pc3, pc3, lb_f.reshape(1, HG_W), lb_b.reshape(1, HG_W),
      norm_g.reshape(1, dh))


def _dft_mats(seq):
    f = jnp.arange(seq, dtype=I32)
    m = (f[:, None] * f[None, :]) % (2 * seq)
    ang = m.astype(F32) * (math.pi / seq)
    return jnp.cos(ang).astype(BF16), jnp.sin(ang).astype(BF16)


def _hyfilt_kernel(feat_ref, w1_ref, b1_ref, w2_ref, b2_ref, fr_ref, w3_ref, t_ref, dl_ref,
                   a_ref, s_ref, u1_ref, u2_ref, un_ref):
    seq = feat_ref.shape[0]
    fr = fr_ref[...]
    h = jnp.sin(fr * (jnp.dot(feat_ref[...], w1_ref[...], preferred_element_type=F32, precision=HIGHEST)
                      + b1_ref[...]))
    h = jnp.sin(fr * (jnp.dot(h, w2_ref[...], preferred_element_type=F32, precision=HIGHEST) + b2_ref[...]))
    window = jnp.exp(-t_ref[...] * dl_ref[...])
    row = lax.broadcasted_iota(I32, (seq, 1), 0)
    sgn = jnp.where(row % 2 == 0, 1.0, -1.0)
    cf = jnp.where(row == 0, 1.0, 2.0) * (1.0 / (2 * seq))
    for o in range(2):
        w3 = w3_ref[:, o * 2 * HY_W:(o + 1) * 2 * HY_W]
        ho = jnp.dot(h, w3, preferred_element_type=F32, precision=HIGHEST)
        fwd = ho[:, :HY_W] * window
        bwd = ho[:, HY_W:] * window
        norm = (jnp.sum(jnp.abs(fwd), axis=0, keepdims=True)
                + jnp.sum(jnp.abs(bwd), axis=0, keepdims=True))
        inv = 1.0 / norm
        ksum = (fwd + bwd) * inv
        kdif = (bwd - fwd) * inv
        kr = _dot(a_ref[...], ksum.astype(BF16))
        ki = _dot(s_ref[...], kdif.astype(BF16))
        u1_ref[o] = (kr * cf).astype(u1_ref.dtype)
        u2_ref[o] = (ki * cf).astype(u2_ref.dtype)
        un_ref[o] = jnp.sum(sgn * ksum, axis=0, keepdims=True) * (1.0 / (2 * seq))


def _hyena_filters(seq, fw1, fb1, fw2, fb2, fw3, freq, a_mat, s_mat):
    pos = jnp.arange(seq, dtype=F32)[:, None]
    t = pos / max(seq - 1, 1)
    w = (2.0 * math.pi / seq) * pos
    bands = jnp.linspace(1e-4, HY_BANDS - 1, HY_BANDS, dtype=F32)[None, :]
    feats = jnp.concatenate([t, jnp.cos(bands * w), -jnp.sin(bands * w)], axis=-1)
    feats = jnp.pad(feats, ((0, 0), (0, LANES - HY_EMB)))
    w1 = jnp.pad(fw1, ((0, LANES - HY_EMB), (0, 0)))
    max_decay = math.log(HY_DECAY_TARGET) / HY_FAST_DECAY_PCT
    min_decay = math.log(HY_DECAY_TARGET) / HY_SLOW_DECAY_PCT
    deltas = jnp.abs(jnp.linspace(min_decay, max_decay, HY_W, dtype=F32))[None, :]
    hid = fw2.shape[0]
    return pl.pallas_call(
        _hyfilt_kernel,
        out_shape=(jax.ShapeDtypeStruct((2, seq, HY_W), BF16),
                   jax.ShapeDtypeStruct((2, seq, HY_W), BF16),
                   jax.ShapeDtypeStruct((2, 1, HY_W), F32)),
        compiler_params=pltpu.CompilerParams(vmem_limit_bytes=VMEM_LIMIT_BYTES),
        name="hyfilt",
    )(feats, w1, fb1.reshape(1, hid), fw2, fb2.reshape(1, hid), freq.reshape(1, hid), fw3, t, deltas,
      a_mat, s_mat)


def _hyena_kernel(x1_ref, x2_ref, v_ref, w1_ref, w2_ref, wv_ref, b1_ref, b2_ref, bv_ref,
                  u1_ref, u2_ref, un_ref, d_ref, a_ref, s_ref, o_ref, z_scr, zb_scr, re_scr, im_scr):
    seq = x1_ref.shape[0]
    rb = min(seq, 512)
    nrb = seq // rb
    row = lax.broadcasted_iota(I32, (rb, 1), 0)
    col = row % GRID_W
    first = col == 0
    lastc = col == GRID_W - 1
    sgn = jnp.where(row % 2 == 0, 1.0, -1.0)

    def conv3(p_ref, w_ref, b_ref, rows):
        p = p_ref[rows, :].astype(F32)
        prev = jnp.where(first, 0.0, pltpu.roll(p, 1, axis=0))
        nxt = jnp.where(lastc, 0.0, pltpu.roll(p, rb - 1, axis=0))
        w = w_ref[...]
        return w[0:1, :] * prev + w[1:2, :] * p + w[2:3, :] * nxt + b_ref[...]

    def forward(o):
        nyq = None
        for i in range(nrb):
            rows = slice(i * rb, (i + 1) * rb)
            part = jnp.sum(sgn * z_scr[rows, :], axis=0, keepdims=True)
            nyq = part if nyq is None else nyq + part
        for i in range(nrb):
            rows = slice(i * rb, (i + 1) * rb)
            p = _dot(a_ref[rows, :], zb_scr[...])
            q = _dot(s_ref[rows, :], zb_scr[...])
            u1 = u1_ref[o, rows, :].astype(F32)
            u2 = u2_ref[o, rows, :].astype(F32)
            re_scr[rows, :] = (p * u1 + q * u2).astype(BF16)
            im_scr[rows, :] = (q * u1 - p * u2).astype(BF16)
        return nyq * un_ref[o]

    def inverse(o, nyq, rows):
        y = _dot(a_ref[rows, :], re_scr[...]) + _dot(s_ref[rows, :], im_scr[...]) + sgn * nyq
        return y + z_scr[rows, :] * d_ref[o:o + 1, :]

    for i in range(nrb):
        rows = slice(i * rb, (i + 1) * rb)
        v = conv3(v_ref, wv_ref, bv_ref, rows)
        z_scr[rows, :] = v
        zb_scr[rows, :] = v.astype(BF16)
    nyq = forward(0)
    for i in range(nrb):
        rows = slice(i * rb, (i + 1) * rb)
        z = conv3(x1_ref, w1_ref, b1_ref, rows) * inverse(0, nyq, rows)
        z_scr[rows, :] = z
        zb_scr[rows, :] = z.astype(BF16)
    nyq = forward(1)
    for i in range(nrb):
        rows = slice(i * rb, (i + 1) * rb)
        y = conv3(x2_ref, w2_ref, b2_ref, rows) * inverse(1, nyq, rows)
        o_ref[rows, :] = y.astype(o_ref.dtype)


def _hyena(px3, conv_w, conv_b, u1, u2, un, d_skip, a_mat, s_mat):
    b, seq, _ = px3.shape
    ct = HY_CT
    nc = HY_W // ct

    def xspec(j):
        return pl.BlockSpec((None, seq, ct), lambda c, bi: (bi, 0, j * nc + c))

    def wspec(j, rows):
        return pl.BlockSpec((rows, ct), lambda c, bi: (0, j * nc + c))

    uspec = pl.BlockSpec((2, seq, ct), lambda c, bi: (0, 0, c))
    const = pl.BlockSpec((seq, seq), lambda c, bi: (0, 0), pipeline_mode=pl.Buffered(1))
    cb = conv_b.reshape(1, 3 * HY_W)
    return pl.pallas_call(
        _hyena_kernel,
        grid=(nc, b),
        in_specs=[xspec(0), xspec(1), xspec(2), wspec(0, 3), wspec(1, 3), wspec(2, 3),
                  wspec(0, 1), wspec(1, 1), wspec(2, 1), uspec, uspec,
                  pl.BlockSpec((2, 1, ct), lambda c, bi: (0, 0, c)),
                  pl.BlockSpec((2, ct), lambda c, bi: (0, c)), const, const],
        out_specs=pl.BlockSpec((None, seq, ct), lambda c, bi: (bi, 0, c)),
        out_shape=jax.ShapeDtypeStruct((b, seq, HY_W), BF16),
        scratch_shapes=[pltpu.VMEM((seq, ct), F32), pltpu.VMEM((seq, ct), BF16),
                        pltpu.VMEM((seq, ct), BF16), pltpu.VMEM((seq, ct), BF16)],
        compiler_params=_params(("parallel", "parallel")),
        name="hyena",
    )(px3, px3, px3, conv_w, conv_w, conv_w, cb, cb, cb, u1, u2, un, d_skip, a_mat, s_mat)


def _select_rows(rows):
    n = rows[0].shape[1]
    idx = lax.broadcasted_iota(I32, (len(rows), n), 0)
    out = jnp.zeros((len(rows), n), rows[0].dtype)
    for k, r in enumerate(rows):
        out = jnp.where(idx == k, r, out)
    return out


def _first_argmax(x, iota, size):
    m = jnp.max(x, axis=0, keepdims=True)
    return jnp.min(jnp.where(x == m, iota, size), axis=0, keepdims=True)


def _mid_kernel(yhy_ref, yhg_ref, x_ref, woa_ref, wob_ref, g1_ref, sc_ref, sh_ref, g2_ref, ng_ref,
                wr_ref, rb_ref, swg_ref, swu_ref, swd_ref,
                xs_ref, h_ref, lrow_ref, w_ref, ce_ref, rel_ref, nch_ref, cnt_ref, base, wrh_ref, wrl_ref):
    tm = x_ref.shape[0]
    ne = N_EXPERTS

    @pl.when(pl.program_id(0) == 0)
    def _():
        base[...] = jnp.zeros_like(base)
        wr = wr_ref[...]
        wr_hi = wr.astype(BF16)
        wrh_ref[...] = wr_hi
        wrl_ref[...] = (wr - wr_hi.astype(F32)).astype(BF16)

    mix = _dot(yhy_ref[...], woa_ref[...]) + _dot(yhg_ref[...], wob_ref[...])
    xm = x_ref[...] + g1_ref[...] * mix
    h = _rms_mod(xm, ng_ref[...], sc_ref[...], sh_ref[...])
    hb = h.astype(BF16)
    h_ref[...] = hb
    act = (_silu(_dot(hb, swg_ref[...])) * _dot(hb, swu_ref[...])).astype(BF16)
    xs_ref[...] = xm + g2_ref[...] * _dot(act, swd_ref[...])

    h_lo = (h - hb.astype(F32)).astype(BF16)
    logits = _dot_nt(wrh_ref[...], hb) + (_dot_nt(wrh_ref[...], h_lo) + _dot_nt(wrl_ref[...], hb))
    scores = jax.nn.sigmoid(logits)
    biased = scores + rb_ref[...]
    neg = -jnp.inf
    iota_g = lax.broadcasted_iota(I32, (GROUP_SIZE, tm), 0)
    grp = []
    for g in range(N_GROUPS):
        blk = biased[g * GROUP_SIZE:(g + 1) * GROUP_SIZE, :]
        m1 = jnp.max(blk, axis=0, keepdims=True)
        i1 = jnp.min(jnp.where(blk == m1, iota_g, GROUP_SIZE), axis=0, keepdims=True)
        m2 = jnp.max(jnp.where(iota_g == i1, neg, blk), axis=0, keepdims=True)
        grp.append(m1 + m2)
    gsc = _select_rows(grp)
    iota8 = lax.broadcasted_iota(I32, (N_GROUPS, tm), 0)
    gsel = iota8 < 0
    for _ in range(TOPK_GROUPS):
        hit = iota8 == _first_argmax(gsc, iota8, N_GROUPS)
        gsel = gsel | hit
        gsc = jnp.where(hit, neg, gsc)
    gself = jnp.where(gsel, 1.0, 0.0)
    iota_e = lax.broadcasted_iota(I32, (ne, tm), 0)
    gid = iota_e // GROUP_SIZE
    emask = jnp.zeros((ne, tm), F32)
    for g in range(N_GROUPS):
        emask = jnp.where(gid == g, gself[g:g + 1, :], emask)
    cur = jnp.where(emask > 0.0, biased, neg)

    sel_f = jnp.zeros((ne, tm), F32)
    idxs, wts = [], []
    for _ in range(TOP_K):
        idx = _first_argmax(cur, iota_e, ne)
        hit = iota_e == idx
        wts.append(jnp.sum(jnp.where(hit, scores, 0.0), axis=0, keepdims=True))
        idxs.append(idx)
        sel_f = jnp.where(hit, 1.0, sel_f)
        cur = jnp.where(hit, neg, cur)
    wsum = wts[0]
    for wk in wts[1:]:
        wsum = wsum + wk
    scale = ROUTED_SCALE / wsum

    r = lax.broadcasted_iota(I32, (tm, tm), 0)
    s = lax.broadcasted_iota(I32, (tm, tm), 1)
    upper = jnp.where(r < s, 1.0, 0.0).astype(BF16)
    prefix = _dot(sel_f.astype(BF16), upper)
    cnt = jnp.sum(sel_f, axis=1, keepdims=True)
    cnt_al = jnp.floor((cnt + (SLOT_ALIGN - 1)) * (1.0 / SLOT_ALIGN)) * SLOT_ALIGN
    re = lax.broadcasted_iota(I32, (ne, ne), 0)
    ce = lax.broadcasted_iota(I32, (ne, ne), 1)
    lower = jnp.where(ce < re, 1.0, 0.0).astype(BF16)
    loc = _dot(lower, jnp.broadcast_to(cnt_al, (ne, LANES)).astype(BF16))[:, 0:1]
    lrow_all = loc + prefix
    lrows = [jnp.sum(jnp.where(iota_e == idx, lrow_all, 0.0), axis=0, keepdims=True) for idx in idxs]

    nc = ce_ref.shape[-1]
    jrow = lax.broadcasted_iota(I32, (1, nc), 1).astype(F32) * SLOT_ALIGN
    owner = jnp.sum(jnp.where(loc + cnt_al <= jrow, 1.0, 0.0), axis=0, keepdims=True)
    owner = jnp.minimum(owner, ne - 1.0).astype(I32)
    iota_ec = lax.broadcasted_iota(I32, (ne, nc), 0)
    rel = jnp.sum(jnp.where(iota_ec == owner, base[...] - loc, 0.0), axis=0, keepdims=True) + jrow
    base[...] = base[...] + cnt_al
    used = jnp.sum(cnt_al, axis=0, keepdims=True)
    spare = jnp.minimum(jrow - used, (MOE_ROW_CHUNK - SLOT_ALIGN) * 1.0)
    spare = spare + (pl.program_id(0) % 2).astype(F32) * MOE_ROW_CHUNK
    unused = jrow >= used
    owner = jnp.where(unused, ne, owner)
    rel = jnp.where(unused, spare, rel)

    lrow_ref[...] = _select_rows(lrows).astype(I32)
    w_ref[...] = _select_rows([wk * scale for wk in wts])
    ce_ref[...] = owner
    rel_ref[...] = rel.astype(I32)
    nch = used * (1.0 / SLOT_ALIGN)
    nch_ref[...] = jnp.broadcast_to(nch, nch_ref.shape).astype(I32)
    cnt_ref[...] = base[...].astype(I32)


def _mid(yhy, yhg, x2d, seq, woa, wob, g1, sc2, sh2, g2, norm_g, wr_t, rbias, swg, swu, swd, tm):
    t, d = x2d.shape
    per = seq // tm
    ff = swg.shape[1]

    def full(shape):
        return pl.BlockSpec(shape, lambda i: (0,) * len(shape))

    mspec = pl.BlockSpec((None, 1, d), lambda i: (i // per, 0, 0))
    kt = pl.BlockSpec((TOP_K, tm), lambda i: (0, i))
    nt = t // tm

    def per_tile(n):
        return pl.BlockSpec((None, 1, n), lambda i: (i, 0, 0))

    return pl.pallas_call(
        _mid_kernel,
        grid=(nt,),
        in_specs=[pl.BlockSpec((tm, HY_W), lambda i: (i, 0)), pl.BlockSpec((tm, HG_W), lambda i: (i, 0)),
                  pl.BlockSpec((tm, d), lambda i: (i, 0)), full((HY_W, d)), full((HG_W, d)),
                  mspec, mspec, mspec, mspec, full((1, d)), full((N_EXPERTS, d)),
                  full((N_EXPERTS, 1)), full((d, ff)), full((d, ff)), full((ff, d))],
        out_specs=[pl.BlockSpec((tm, d), lambda i: (i, 0)), pl.BlockSpec((tm, d), lambda i: (i, 0)),
                   kt, kt, per_tile(MOE_CHUNKS), per_tile(MOE_CHUNKS), per_tile(LANES), full((N_EXPERTS, 1))],
        out_shape=(jax.ShapeDtypeStruct((t, d), F32), jax.ShapeDtypeStruct((t, d), BF16),
                   jax.ShapeDtypeStruct((TOP_K, t), I32), jax.ShapeDtypeStruct((TOP_K, t), F32),
                   jax.ShapeDtypeStruct((nt, 1, MOE_CHUNKS), I32), jax.ShapeDtypeStruct((nt, 1, MOE_CHUNKS), I32),
                   jax.ShapeDtypeStruct((nt, 1, LANES), I32), jax.ShapeDtypeStruct((N_EXPERTS, 1), I32)),
        scratch_shapes=[pltpu.VMEM((N_EXPERTS, 1), F32), pltpu.VMEM((N_EXPERTS, d), BF16),
                        pltpu.VMEM((N_EXPERTS, d), BF16)],
        compiler_params=_params(("arbitrary",)),
        name="mid",
    )(yhy, yhg, x2d, woa, wob, g1, sc2, sh2, g2, norm_g.reshape(1, d), wr_t,
      rbias.reshape(N_EXPERTS, 1), swg, swu, swd)


def _pack_bf16_pairs(x):
    n = x.shape[1] // 2
    lo = lax.shift_right_logical(pltpu.bitcast(x[:, :n], I32), 16)
    hi = pltpu.bitcast(x[:, n:], I32) & -65536
    return hi | lo


def _unpack_bf16_pairs(p):
    lo = pltpu.bitcast(lax.shift_left(p, 16), F32).astype(BF16)
    hi = pltpu.bitcast(p & -65536, F32).astype(BF16)
    return lo, hi


def _row_chunks(nch):
    return (nch * SLOT_ALIGN + MOE_ROW_CHUNK - 1) // MOE_ROW_CHUNK


def _start_chunk_copies(c, ce_ref, rel_ref, ps_ref, make):
    per = MOE_ROW_CHUNK // SLOT_ALIGN
    for q in range(per):
        j = c * per + q
        glob = pl.multiple_of(ps_ref[ce_ref[j]] + rel_ref[j], SLOT_ALIGN)
        make(pl.multiple_of(j * SLOT_ALIGN, SLOT_ALIGN), glob).start()


def _wait_row_chunks(n, row_chunk_copy):
    def wait(_, carry):
        row_chunk_copy.wait()
        return carry

    lax.fori_loop(0, n, wait, 0)


def _dispatch_kernel(nch_ref, ce_ref, rel_ref, ps_ref, h_ref, lrow_ref, xs_hbm, xloc, zbuf, sems):
    tm = h_ref.shape[0]
    bm = zbuf.shape[0]
    rc = MOE_ROW_CHUNK
    sem = sems.at[0]

    @pl.when(pl.program_id(0) == 0)
    def _():
        zbuf[...] = jnp.zeros_like(zbuf)

        def zcopy(e):
            start = pl.multiple_of(ps_ref[e + 1] - bm, bm)
            return pltpu.make_async_copy(zbuf, xs_hbm.at[pl.ds(start, bm), :], sem)

        def start(e, carry):
            @pl.when(ps_ref[e + 1] > ps_ref[e])
            def _():
                zcopy(e).start()
            return carry

        def wait(e, carry):
            @pl.when(ps_ref[e + 1] > ps_ref[e])
            def _():
                zcopy(e).wait()
            return carry

        lax.fori_loop(0, N_EXPERTS, start, 0)
        lax.fori_loop(0, N_EXPERTS, wait, 0)

        def tcopy(j):
            return pltpu.make_async_copy(zbuf, xs_hbm.at[pl.ds(pl.multiple_of(j * bm, bm), bm), :], sem)

        def tstart(j, carry):
            tcopy(j).start()
            return carry

        def twait(j, carry):
            tcopy(j).wait()
            return carry

        first_tail = ps_ref[N_EXPERTS] // bm
        lax.fori_loop(first_tail, xs_hbm.shape[0] // bm, tstart, 0)
        lax.fori_loop(first_tail, xs_hbm.shape[0] // bm, twait, 0)

    i = pl.program_id(0)
    slot = i % 2
    buf = xloc.at[slot]
    n_rc = _row_chunks(nch_ref[i])
    lrow = lrow_ref[...].astype(F32)
    sub = MOE_SUB_ROWS
    rid = lax.broadcasted_iota(I32, (sub, tm), 0).astype(F32).astype(BF16)
    one = jnp.ones((sub, tm), BF16)

    def select(r0):
        rel = (lrow - jnp.asarray(r0, I32).astype(F32)).astype(BF16)
        onehot = jnp.zeros((sub, tm), BF16)
        for k in range(TOP_K):
            onehot = jnp.where(rid == rel[k:k + 1, :], one, onehot)
        buf[pl.ds(r0, sub), :] = _pack_bf16_pairs(_dot(onehot, h_ref[...]))

    def send(c):
        _start_chunk_copies(c, ce_ref, rel_ref, ps_ref,
                            lambda loc, glob: pltpu.make_async_copy(buf.at[pl.ds(loc, SLOT_ALIGN), :],
                                                                    xs_hbm.at[pl.ds(glob, SLOT_ALIGN), :],
                                                                    sems.at[slot]))

    def retire(tile_slot, n):
        _wait_row_chunks(n, pltpu.make_async_copy(xloc.at[tile_slot, pl.ds(0, rc), :],
                                                  xs_hbm.at[pl.ds(0, rc), :], sems.at[tile_slot]))

    for j in range(rc // sub):
        select(j * sub)

    def body(c, carry):
        send(c - 1)
        r0 = pl.multiple_of(c * rc, rc)
        for j in range(rc // sub):
            select(r0 + j * sub)
        return carry

    lax.fori_loop(1, n_rc, body, 0)
    send(n_rc - 1)

    @pl.when(i > 0)
    def _():
        retire(1 - slot, _row_chunks(nch_ref[i - 1]))

    @pl.when(i == pl.num_programs(0) - 1)
    def _():
        retire(slot, n_rc)


def _local_rows(tm):
    bound = TOP_K * tm + N_EXPERTS * (SLOT_ALIGN - 1)
    return -(-bound // MOE_ROW_CHUNK) * MOE_ROW_CHUNK


def _moe_smem_specs():
    chunk_list = pl.BlockSpec((MOE_CHUNKS,), lambda i, nch: (i,), memory_space=pltpu.SMEM)
    return [chunk_list, chunk_list,
            pl.BlockSpec((N_EXPERTS + 1,), lambda i, nch: (0,), memory_space=pltpu.SMEM)]


def _dispatch(h2d, lrow_kt, ce, rel, nch, ps, n_slots, tm):
    t, d = h2d.shape
    grid_spec = pltpu.PrefetchScalarGridSpec(
        num_scalar_prefetch=1,
        grid=(t // tm,),
        in_specs=_moe_smem_specs() + [pl.BlockSpec((tm, d), lambda i, nch: (i, 0)),
                                      pl.BlockSpec((TOP_K, tm), lambda i, nch: (0, i))],
        out_specs=pl.BlockSpec(memory_space=pl.ANY),
        scratch_shapes=[pltpu.VMEM((2, _local_rows(tm), d // 2), I32), pltpu.VMEM((FFN_BLOCK, d // 2), I32),
                        pltpu.SemaphoreType.DMA((2,))],
    )

    return pl.pallas_call(
        _dispatch_kernel,
        grid_spec=grid_spec,
        out_shape=jax.ShapeDtypeStruct((n_slots, d // 2), I32),
        compiler_params=_params(("arbitrary",)),
        name="dispatch",
    )(nch, ce, rel, ps, h2d, lrow_kt)


def _ffn_kernel(ps_ref, wg_ref, wu_ref, wd_ref, x_hbm, y_hbm, xbuf, ybuf, wg_bf, wu_bf, wd_bf, xsem, ysem):
    e = pl.program_id(0)
    nx, bm, half = xbuf.shape
    ny = ybuf.shape[0]
    ahead = nx - 1
    n_blocks = y_hbm.shape[0] // bm
    g0 = ps_ref[e] // bm
    g1 = ps_ref[e + 1] // bm

    def rows(g):
        return pl.ds(pl.multiple_of(g * bm, bm), bm)

    def x_copy(g):
        s = g % nx
        return pltpu.make_async_copy(x_hbm.at[rows(g), :], xbuf.at[s], xsem.at[s])

    def y_copy(g, s):
        return pltpu.make_async_copy(ybuf.at[s], y_hbm.at[rows(g), :], ysem.at[s])

    @pl.when(e == 0)
    def _():
        ybuf[...] = jnp.zeros_like(ybuf)
        for g in range(ahead):
            x_copy(g).start()
        for s in range(ny):
            y_copy(n_blocks - 1 - s, s).start()

    @pl.when(g1 > g0)
    def _():
        wg_bf[...] = wg_ref[...].astype(BF16)
        wu_bf[...] = wu_ref[...].astype(BF16)
        wd_bf[...] = wd_ref[...].astype(BF16)

    def block(g, carry):
        s = g % ny
        xs = g % nx
        x_copy(g).wait()
        y_copy(g, s).wait()
        x_copy(g + ahead).start()
        hb = bm // 2
        halves = [pl.ds(p * hb, hb) for p in range(2)]
        gu = []
        for r in halves:
            lo, hi = _unpack_bf16_pairs(xbuf[xs, r, :])
            gu.append((_dot(lo, wg_bf[:half, :]) + _dot(hi, wg_bf[half:, :]),
                       _dot(lo, wu_bf[:half, :]) + _dot(hi, wu_bf[half:, :])))
        for r, (gate, up) in zip(halves, gu):
            act = (_silu(gate) * up).astype(BF16)
            y = _dot(act, wd_bf[...])
            ybuf[s, r, :] = _pack_bf16_pairs(y.astype(BF16).astype(F32))
        y_copy(g, s).start(priority=1)
        return carry

    lax.fori_loop(g0, g1, block, 0)

    @pl.when(e == pl.num_programs(0) - 1)
    def _():
        for k in range(ahead):
            x_copy(g1 + k).wait()
        for s in range(ny):
            y_copy(0, s).wait()
        ybuf[0] = jnp.zeros((bm, half), ybuf.dtype)

        def zstart(g, carry):
            y_copy(g, 0).start()
            return carry

        def zwait(g, carry):
            y_copy(g, 0).wait()
            return carry

        lax.fori_loop(g1, n_blocks, zstart, 0)
        lax.fori_loop(g1, n_blocks, zwait, 0)


def _ffn(xs, ps, ew_gate, ew_up, ew_down):
    n_slots, half = xs.shape
    d = 2 * half
    bm = FFN_BLOCK
    n_exp, _, ff = ew_gate.shape
    grid_spec = pltpu.PrefetchScalarGridSpec(
        num_scalar_prefetch=1,
        grid=(n_exp,),
        in_specs=[pl.BlockSpec((None, d, ff), lambda e, ps: (e, 0, 0)),
                  pl.BlockSpec((None, d, ff), lambda e, ps: (e, 0, 0)),
                  pl.BlockSpec((None, ff, d), lambda e, ps: (e, 0, 0)),
                  pl.BlockSpec(memory_space=pl.ANY)],
        out_specs=pl.BlockSpec(memory_space=pl.ANY),
        scratch_shapes=[pltpu.VMEM((FFN_BUFFERS, bm, half), I32), pltpu.VMEM((FFN_BUFFERS, bm, half), I32),
                        pltpu.VMEM((d, ff), BF16), pltpu.VMEM((d, ff), BF16), pltpu.VMEM((ff, d), BF16),
                        pltpu.SemaphoreType.DMA((FFN_BUFFERS,)), pltpu.SemaphoreType.DMA((FFN_BUFFERS,))],
    )
    return pl.pallas_call(
        _ffn_kernel,
        grid_spec=grid_spec,
        out_shape=jax.ShapeDtypeStruct((n_slots, half), I32),
        compiler_params=_params(("arbitrary",)),
        name="ffn",
    )(ps, ew_gate, ew_up, ew_down, xs)


def _combine_kernel(nch_ref, ce_ref, rel_ref, cen_ref, reln_ref, ps_ref, lrow_ref, w_ref, xs_ref, g2_ref, fg_ref,
                    y_hbm, o_ref, yloc, acc, sems):
    tm = xs_ref.shape[0]
    half = yloc.shape[2]
    rc = MOE_ROW_CHUNK
    i = pl.program_id(0)
    last = pl.num_programs(0) - 1
    slot = i % 2
    nxt = jnp.minimum(i + 1, last)
    n_rc = _row_chunks(nch_ref[i])
    n_next = _row_chunks(nch_ref[nxt])

    def fetch(c, ce, rel, s):
        _start_chunk_copies(c, ce, rel, ps_ref,
                            lambda loc, glob: pltpu.make_async_copy(y_hbm.at[pl.ds(glob, SLOT_ALIGN), :],
                                                                    yloc.at[s, pl.ds(loc, SLOT_ALIGN), :],
                                                                    sems.at[s]))

    def arrived(s, n):
        _wait_row_chunks(n, pltpu.make_async_copy(y_hbm.at[pl.ds(0, rc), :], yloc.at[s, pl.ds(0, rc), :],
                                                  sems.at[s]))

    def fetch_own(c, carry):
        fetch(c, ce_ref, rel_ref, slot)
        return carry

    def fetch_next(c, carry):
        fetch(c, cen_ref, reln_ref, 1 - slot)
        return carry

    @pl.when(i == 0)
    def _():
        yloc[...] = jnp.zeros_like(yloc)
        lax.fori_loop(0, n_rc, fetch_own, 0)

    n_prev = _row_chunks(nch_ref[jnp.maximum(i - 1, 0)])
    arrived(slot, jnp.where(i == 0, n_rc, jnp.maximum(n_prev, n_rc)))

    lrow = lrow_ref[...].astype(F32)
    wrow = w_ref[...].astype(BF16)
    acc[...] = jnp.zeros_like(acc)
    sub = MOE_SUB_ROWS
    rid = lax.broadcasted_iota(I32, (sub, tm), 0).astype(F32).astype(BF16)

    def weights(r0):
        rel = (lrow - r0.astype(F32)).astype(BF16)
        wt = jnp.zeros((sub, tm), BF16)
        for k in range(TOP_K):
            wt = jnp.where(rid == rel[k:k + 1, :], jnp.broadcast_to(wrow[k:k + 1, :], (sub, tm)), wt)
        return wt

    def gather(c, carry):
        fetch(c, cen_ref, reln_ref, 1 - slot)
        r0 = pl.multiple_of(c * rc, rc)
        lo_sum, hi_sum = None, None
        for j in range(rc // sub):
            wt = weights(r0 + j * sub)
            lo, hi = _unpack_bf16_pairs(yloc[slot, pl.ds(r0 + j * sub, sub), :])
            lo_sum = _dot_tn(wt, lo) if j == 0 else lo_sum + _dot_tn(wt, lo)
            hi_sum = _dot_tn(wt, hi) if j == 0 else hi_sum + _dot_tn(wt, hi)
        acc[:, :half] += lo_sum
        acc[:, half:] += hi_sum
        return carry

    lax.fori_loop(0, n_rc, gather, 0)
    lax.fori_loop(n_rc, n_next, fetch_next, 0)

    @pl.when(i == last)
    def _():
        arrived(1 - slot, jnp.maximum(n_rc, n_next))

    x = xs_ref[...] + g2_ref[...] * acc[...]
    o_ref[...] = x * lax.rsqrt(jnp.mean(x * x, axis=-1, keepdims=True) + NORM_EPS) * fg_ref[...]


def _combine(y_sorted, lrow_kt, w_kt, ce, rel, nch, ps, xs2d, seq, g2, final_g, tm):
    t, d = xs2d.shape
    per = seq // tm
    nt = t // tm
    own, _, ranges = _moe_smem_specs()
    nxt = pl.BlockSpec((MOE_CHUNKS,), lambda i, nch: (jnp.minimum(i + 1, nt - 1),), memory_space=pltpu.SMEM)
    grid_spec = pltpu.PrefetchScalarGridSpec(
        num_scalar_prefetch=1,
        grid=(nt,),
        in_specs=[own, own, nxt, nxt, ranges,
                  pl.BlockSpec((TOP_K, tm), lambda i, nch: (0, i)),
                  pl.BlockSpec((TOP_K, tm), lambda i, nch: (0, i)),
                  pl.BlockSpec((tm, d), lambda i, nch: (i, 0)),
                  pl.BlockSpec((None, 1, d), lambda i, nch: (i // per, 0, 0)),
                  pl.BlockSpec((1, d), lambda i, nch: (0, 0)),
                  pl.BlockSpec(memory_space=pl.ANY)],
        out_specs=pl.BlockSpec((tm, d), lambda i, nch: (i, 0)),
        scratch_shapes=[pltpu.VMEM((2, _local_rows(tm), d // 2), I32), pltpu.VMEM((tm, d), F32),
                        pltpu.SemaphoreType.DMA((2,))],
    )
    return pl.pallas_call(
        _combine_kernel,
        grid_spec=grid_spec,
        out_shape=jax.ShapeDtypeStruct((t, d), F32),
        compiler_params=_params(("arbitrary",)),
        name="combine",
    )(nch, ce, rel, ce, rel, ps, lrow_kt, w_kt, xs2d, g2, final_g.reshape(1, d), y_sorted)


def kernel(x, c, ctx, c_ctx, w_mod, b_mod, norm1_g, norm2_g, w_in, w_out, hy_conv_w, hy_conv_b,
           hy_fw1, hy_fb1, hy_fw2, hy_fb2, hy_fw3, hy_freq, hy_d, hg_lb_logits, hg_norm_g,
           w_router, router_bias, ew_gate, ew_up, ew_down, sw_gate, sw_up, sw_down, final_g):
    b, seq, d = x.shape
    ctx_len = ctx.shape[1]
    t = b * seq
    layer = 0

    lower = jnp.cumsum(jax.nn.softmax(hg_lb_logits.astype(F32), axis=1), axis=1)
    lb_f, lb_b = lower[0, layer], lower[1, layer]

    rows = -(-(b + 1) // 8) * 8
    cc = jnp.concatenate([c, c_ctx[None, :], jnp.zeros((rows - b - 1, d), F32)], axis=0)
    mod = _modulation(cc, w_mod[layer], b_mod[layer])
    sh1, sc1, g1, sh2, sc2, g2 = (m.reshape(b, 1, d) for m in jnp.split(mod[:b], 6, axis=-1))
    csh1, csc1 = (jnp.broadcast_to(m.reshape(1, 1, d), (b, 1, d))
                  for m in jnp.split(mod[b:b + 1], 6, axis=-1)[:2])

    w_in_bf = w_in[layer].astype(BF16)
    x2d = x.reshape(t, d)
    px = _inproj(x2d, seq, norm1_g[layer], sc1, sh1, w_in_bf, min(seq, 512), w_in_bf.shape[1])
    lo = 3 * HY_W + HG_W
    pc = _inproj(ctx.reshape(b * ctx_len, d), ctx_len, norm1_g[layer], csc1, csh1,
                 w_in_bf[:, lo:lo + 3 * HG_W], ctx_len, HG_W)
    px3 = px.reshape(b, seq, -1)
    pc3 = pc.reshape(b, ctx_len, -1)

    y_hg = _hgrn(px3, pc3, lb_f, lb_b, hg_norm_g[layer])

    a_mat, s_mat = _dft_mats(seq)
    u1, u2, un = _hyena_filters(seq, hy_fw1[layer], hy_fb1[layer], hy_fw2[layer], hy_fb2[layer],
                                hy_fw3[layer], hy_freq[layer], a_mat, s_mat)
    y_hy = _hyena(px3, hy_conv_w[layer], hy_conv_b[layer], u1, u2, un, hy_d[layer], a_mat, s_mat)

    w_out_bf = w_out[layer].astype(BF16)
    tm = min(seq, MOE_TILE)
    nt = t // tm
    xs, h2, lrow_kt, w_kt, ce, rel, nch, counts = _mid(
        y_hy.reshape(t, HY_W), y_hg.reshape(t, HG_W), x2d, seq, w_out_bf[:HY_W], w_out_bf[HY_W:],
        g1, sc2, sh2, g2, norm2_g[layer], w_router[layer].T, router_bias[layer],
        sw_gate[layer].astype(BF16), sw_up[layer].astype(BF16), sw_down[layer].astype(BF16), tm)

    bm = FFN_BLOCK
    counts = counts.reshape(N_EXPERTS)
    padded = (counts + bm - 1) // bm * bm
    p_ends = jnp.cumsum(padded)
    ps = jnp.concatenate([p_ends - padded, p_ends[-1:]]).astype(I32)
    spare_blocks = max(-(-2 * MOE_ROW_CHUNK // bm), FFN_BUFFERS)
    n_blocks = -(-(t * TOP_K + nt * N_EXPERTS * (SLOT_ALIGN - 1)) // bm) + N_EXPERTS + spare_blocks

    ce, rel, nch = ce.reshape(-1), rel.reshape(-1), nch[:, 0, 0]
    x_sorted = _dispatch(h2, lrow_kt, ce, rel, nch, ps, n_blocks * bm, tm)
    y_sorted = _ffn(x_sorted, ps, ew_gate[layer], ew_up[layer], ew_down[layer])
    out = _combine(y_sorted, lrow_kt, w_kt, ce, rel, nch, ps, xs, seq, g2, final_g, tm)
    return out.reshape(b, seq, d)
```

```python
import functools
import math

import jax
import jax.numpy as jnp
from jax import lax
from jax.experimental import pallas as pl
from jax.experimental.pallas import tpu as pltpu

F32 = jnp.float32
BF16 = jnp.bfloat16
I32 = jnp.int32
HIGHEST = lax.Precision.HIGHEST

GRID_W = 64
HY_W = 512
HG_W = 512
HY_EMB = 33
HY_BANDS = 16
HY_DECAY_TARGET = 1e-2
HY_FAST_DECAY_PCT = 0.3
HY_SLOW_DECAY_PCT = 1.5
HG_HEAD_DIM = 128
HG_HEADS = 4
HG_SCALE = HG_HEAD_DIM ** -0.5
HG_CHUNK = 64
N_EXPERTS = 256
TOP_K = 8
N_GROUPS = 8
TOPK_GROUPS = 4
GROUP_SIZE = N_EXPERTS // N_GROUPS
ROUTED_SCALE = 2.5
NORM_EPS = 1e-6

VMEM_LIMIT_BYTES = 56 * 1024 * 1024
LANES = 128
FFN_BLOCK = 512
FFN_BUFFERS = 4
MOE_TILE = 512
SLOT_ALIGN = 8
MOE_ROW_CHUNK = 1024
MOE_SUB_ROWS = 256
MOE_CHUNKS = 1024
HY_CT = 256


def _params(sem, vmem=VMEM_LIMIT_BYTES):
    return pltpu.CompilerParams(dimension_semantics=sem, vmem_limit_bytes=vmem)


def _silu(x):
    return x * jax.nn.sigmoid(x)


def _dot(a, b):
    return jnp.dot(a, b, preferred_element_type=F32)


def _dot_nt(a, b):
    return lax.dot_general(a, b, (((1,), (1,)), ((), ())), preferred_element_type=F32)


def _dot_tn(a, b):
    return lax.dot_general(a, b, (((0,), (0,)), ((), ())), preferred_element_type=F32)


def _mod_kernel(c_ref, w_ref, b_ref, o_ref):
    s = _silu(c_ref[...])
    o_ref[...] = jnp.dot(s, w_ref[...], preferred_element_type=F32, precision=HIGHEST) + b_ref[...]


def _modulation(cc, w_mod, b_mod):
    rows, d = cc.shape
    n = w_mod.shape[1]
    tn = 1024
    return pl.pallas_call(
        _mod_kernel,
        grid=(n // tn,),
        in_specs=[pl.BlockSpec((rows, d), lambda j: (0, 0)),
                  pl.BlockSpec((d, tn), lambda j: (0, j)),
                  pl.BlockSpec((1, tn), lambda j: (0, j))],
        out_specs=pl.BlockSpec((rows, tn), lambda j: (0, j)),
        out_shape=jax.ShapeDtypeStruct((rows, n), F32),
        compiler_params=_params(("parallel",)),
        name="mod",
    )(cc, w_mod, b_mod.reshape(1, n))


def _rms_mod(x, g, sc, sh):
    y = x * lax.rsqrt(jnp.mean(x * x, axis=-1, keepdims=True) + NORM_EPS) * g
    return y * (1.0 + sc) + sh


def _inproj_kernel(x_ref, g_ref, sc_ref, sh_ref, w_ref, o_ref, h_scr):
    @pl.when(pl.program_id(1) == 0)
    def _():
        h_scr[...] = _rms_mod(x_ref[...], g_ref[...], sc_ref[...], sh_ref[...]).astype(BF16)

    o_ref[...] = _dot(h_scr[...], w_ref[...]).astype(o_ref.dtype)


def _inproj(x2d, seq, g, sc, sh, w_bf, tm, tn):
    t, d = x2d.shape
    n = w_bf.shape[1]
    per = seq // tm
    w_mode = dict(pipeline_mode=pl.Buffered(1)) if tn == n else {}
    return pl.pallas_call(
        _inproj_kernel,
        grid=(t // tm, n // tn),
        in_specs=[pl.BlockSpec((tm, d), lambda i, j: (i, 0)),
                  pl.BlockSpec((1, d), lambda i, j: (0, 0)),
                  pl.BlockSpec((None, 1, d), lambda i, j: (i // per, 0, 0)),
                  pl.BlockSpec((None, 1, d), lambda i, j: (i // per, 0, 0)),
                  pl.BlockSpec((d, tn), lambda i, j: (0, j), **w_mode)],
        out_specs=pl.BlockSpec((tm, tn), lambda i, j: (i, j)),
        out_shape=jax.ShapeDtypeStruct((t, n), BF16),
        scratch_shapes=[pltpu.VMEM((tm, d), BF16)],
        compiler_params=_params(("parallel", "arbitrary")),
        name="inproj",
    )(x2d, g.reshape(1, d), sc, sh, w_bf)


def _hg_geometry():
    c = HG_CHUNK
    r = lax.broadcasted_iota(I32, (c, c), 0)
    s = lax.broadcasted_iota(I32, (c, c), 1)
    return {False: (r >= s, c // 2 - 1, c - 1), True: (r <= s, c // 2, 0)}


def _hg_gates(fr, lb, rev):
    sig = jax.nn.sigmoid(fr)
    lf = jnp.log(lb + (1.0 - lb) * sig)
    k = (1.0 - lb) * (1.0 - sig)
    hi = lf.astype(BF16)
    lo = (lf - hi.astype(F32)).astype(BF16)
    tri = jnp.where(_hg_geometry()[rev][0], 1.0, 0.0).astype(BF16)
    return k, _dot(tri, hi) + _dot(tri, lo)


def _hg_steps(chains):
    geo = _hg_geometry()
    work = []
    for ch in chains:
        w = dict(k=ch["k"], bc=ch["bc"])
        work.append(w)
        mask, mid, last = geo[ch["rev"]]
        bc = w["bc"]
        b_mid = bc[mid:mid + 1, :]
        b_last = bc[last:last + 1, :]
        km = w["k"] * jnp.exp(b_mid - bc)
        kd = (km * jnp.exp(b_last - b_mid)).astype(BF16)
        w["ut"] = _dot_tn(ch["v"], kd)
        w["decay"] = jnp.exp(b_last)
        if ch["q"] is not None:
            qm = ch["q"] * jnp.exp(bc - b_mid)
            w["att"] = _dot_nt(qm.astype(BF16), km.astype(BF16))
            qe = (qm * jnp.exp(b_mid)).astype(BF16)
            w["inter"] = _dot_nt(qe, ch["st"].astype(BF16))
    ahead = [_hg_gates(ch["fr_next"], ch["lb"], ch["rev"]) for ch in chains]
    out = []
    for ch, w, nxt in zip(chains, work, ahead):
        o = None
        if ch["q"] is not None:
            att = jnp.where(geo[ch["rev"]][0], w["att"], 0.0).astype(BF16)
            o = _dot(att, ch["v"]) + w["inter"]
        out.append((o, ch["st"] * w["decay"] + w["ut"], nxt))
    return out


def _hgrn_kernel(q_ref, ff_ref, fb_ref, i_ref, g_ref, cff_ref, cfb_ref, ci_ref,
                 lbf_ref, lbb_ref, ng_ref, o_ref, qs_scr, of_scr, ob_scr, st_scr, k_scr, bc_scr):
    seq = q_ref.shape[0]
    ctx = cff_ref.shape[0]
    c = HG_CHUNK
    dh = HG_HEAD_DIM
    rb = min(seq, 256)

    for i in range(seq // rb):
        rows = slice(i * rb, (i + 1) * rb)
        qs_scr[rows, :] = (_silu(q_ref[rows, :].astype(F32)) * HG_SCALE).astype(BF16)
    st_scr[...] = jnp.zeros_like(st_scr)

    def chains(nchunks, ffr, fbr, vr, with_q):
        layout = [(h, rev, fref, lbref) for h in range(HG_HEADS)
                  for rev, fref, lbref in ((False, ffr, lbf_ref), (True, fbr, lbb_ref))]

        def chunk_rows(n, rev):
            ci = (nchunks - 1 - n) if rev else n
            return pl.ds(ci * c if isinstance(ci, int) else pl.multiple_of(ci * c, c), c)

        for slot, (h, rev, fref, lbref) in enumerate(layout):
            cols = slice(h * dh, (h + 1) * dh)
            k0, bc0 = _hg_gates(fref[chunk_rows(0, rev), cols].astype(F32), lbref[:, cols], rev)
            k_scr[slot] = k0
            bc_scr[slot] = bc0

        def body(n, carry):
            n_next = jnp.minimum(n + 1, nchunks - 1)
            chains = []
            for slot, (h, rev, fref, lbref) in enumerate(layout):
                cols = slice(h * dh, (h + 1) * dh)
                rows = chunk_rows(n, rev)
                chains.append(dict(
                    rev=rev, rows=rows, cols=cols, k=k_scr[slot], bc=bc_scr[slot],
                    fr_next=fref[chunk_rows(n_next, rev), cols].astype(F32), v=vr[rows, cols],
                    lb=lbref[:, cols], st=st_scr[slot],
                    q=qs_scr[rows, cols].astype(F32) if with_q else None))
            for slot, (ch, (o, st, (k_next, bc_next))) in enumerate(zip(chains, _hg_steps(chains))):
                st_scr[slot] = st
                k_scr[slot] = k_next
                bc_scr[slot] = bc_next
                if with_q:
                    (ob_scr if ch["rev"] else of_scr)[ch["rows"], ch["cols"]] = o
            return carry

        lax.fori_loop(0, nchunks, body, 0)

    chains(ctx // c, cff_ref, cfb_ref, ci_ref, False)
    chains(seq // c, ff_ref, fb_ref, i_ref, True)

    ng = ng_ref[...]
    for i in range(seq // rb):
        rows = slice(i * rb, (i + 1) * rb)
        gate = _silu(g_ref[rows, :].astype(F32))
        for h in range(HG_HEADS):
            cols = slice(h * dh, (h + 1) * dh)
            o = of_scr[rows, cols] + ob_scr[rows, cols]
            on = o * lax.rsqrt(jnp.mean(o * o, axis=-1, keepdims=True) + NORM_EPS) * ng
            o_ref[rows, cols] = (on * gate[:, cols]).astype(o_ref.dtype)


def _hgrn(px3, pc3, lb_f, lb_b, norm_g):
    b, seq, _ = px3.shape
    ctx = pc3.shape[1]
    dh = HG_HEAD_DIM
    base = 3 * HY_W // HG_W

    def xspec(j):
        return pl.BlockSpec((None, seq, HG_W), lambda bi: (bi, 0, base + j))

    def cspec(j):
        return pl.BlockSpec((None, ctx, HG_W), lambda bi: (bi, 0, j))

    vec = pl.BlockSpec((1, HG_W), lambda bi: (0, 0))
    return pl.pallas_call(
        _hgrn_kernel,
        grid=(b,),
        in_specs=[xspec(0), xspec(1), xspec(2), xspec(3), xspec(4), cspec(0), cspec(1), cspec(2),
                  vec, vec, pl.BlockSpec((1, dh), lambda bi: (0, 0))],
        out_specs=pl.BlockSpec((None, seq, HG_W), lambda bi: (bi, 0, 0)),
        out_shape=jax.ShapeDtypeStruct((b, seq, HG_W), BF16),
        scratch_shapes=[pltpu.VMEM((seq, HG_W), BF16), pltpu.VMEM((seq, HG_W), F32),
                        pltpu.VMEM((seq, HG_W), F32), pltpu.VMEM((2 * HG_HEADS, dh, dh), F32),
                        pltpu.VMEM((2 * HG_HEADS, HG_CHUNK, dh), F32), pltpu.VMEM((2 * HG_HEADS, HG_CHUNK, dh), F32)],
        compiler_params=_params(("parallel",)),
        name="hgrn",
    )(px3, px3, px3, px3, px3, pc3, pc3, pc3, lb_f.reshape(1, HG_W), lb_b.reshape(1, HG_W),
      norm_g.reshape(1, dh))


def _dft_mats(seq):
    f = jnp.arange(seq, dtype=I32)
    m = (f[:, None] * f[None, :]) % (2 * seq)
    ang = m.astype(F32) * (math.pi / seq)
    return jnp.cos(ang).astype(BF16), jnp.sin(ang).astype(BF16)


def _hyfilt_kernel(feat_ref, w1_ref, b1_ref, w2_ref, b2_ref, fr_ref, w3_ref, t_ref, dl_ref,
                   a_ref, s_ref, u1_ref, u2_ref, un_ref):
    seq = feat_ref.shape[0]
    fr = fr_ref[...]
    h = jnp.sin(fr * (jnp.dot(feat_ref[...], w1_ref[...], preferred_element_type=F32, precision=HIGHEST)
                      + b1_ref[...]))
    h = jnp.sin(fr * (jnp.dot(h, w2_ref[...], preferred_element_type=F32, precision=HIGHEST) + b2_ref[...]))
    window = jnp.exp(-t_ref[...] * dl_ref[...])
    row = lax.broadcasted_iota(I32, (seq, 1), 0)
    sgn = jnp.where(row % 2 == 0, 1.0, -1.0)
    cf = jnp.where(row == 0, 1.0, 2.0) * (1.0 / (2 * seq))
    for o in range(2):
        w3 = w3_ref[:, o * 2 * HY_W:(o + 1) * 2 * HY_W]
        ho = jnp.dot(h, w3, preferred_element_type=F32, precision=HIGHEST)
        fwd = ho[:, :HY_W] * window
        bwd = ho[:, HY_W:] * window
        norm = (jnp.sum(jnp.abs(fwd), axis=0, keepdims=True)
                + jnp.sum(jnp.abs(bwd), axis=0, keepdims=True))
        inv = 1.0 / norm
        ksum = (fwd + bwd) * inv
        kdif = (bwd - fwd) * inv
        kr = _dot(a_ref[...], ksum.astype(BF16))
        ki = _dot(s_ref[...], kdif.astype(BF16))
        u1_ref[o] = (kr * cf).astype(u1_ref.dtype)
        u2_ref[o] = (ki * cf).astype(u2_ref.dtype)
        un_ref[o] = jnp.sum(sgn * ksum, axis=0, keepdims=True) * (1.0 / (2 * seq))


def _hyena_filters(seq, fw1, fb1, fw2, fb2, fw3, freq, a_mat, s_mat):
    pos = jnp.arange(seq, dtype=F32)[:, None]
    t = pos / max(seq - 1, 1)
    w = (2.0 * math.pi / seq) * pos
    bands = jnp.linspace(1e-4, HY_BANDS - 1, HY_BANDS, dtype=F32)[None, :]
    feats = jnp.concatenate([t, jnp.cos(bands * w), -jnp.sin(bands * w)], axis=-1)
    feats = jnp.pad(feats, ((0, 0), (0, LANES - HY_EMB)))
    w1 = jnp.pad(fw1, ((0, LANES - HY_EMB), (0, 0)))
    max_decay = math.log(HY_DECAY_TARGET) / HY_FAST_DECAY_PCT
    min_decay = math.log(HY_DECAY_TARGET) / HY_SLOW_DECAY_PCT
    deltas = jnp.abs(jnp.linspace(min_decay, max_decay, HY_W, dtype=F32))[None, :]
    hid = fw2.shape[0]
    return pl.pallas_call(
        _hyfilt_kernel,
        out_shape=(jax.ShapeDtypeStruct((2, seq, HY_W), BF16),
                   jax.ShapeDtypeStruct((2, seq, HY_W), BF16),
                   jax.ShapeDtypeStruct((2, 1, HY_W), F32)),
        compiler_params=pltpu.CompilerParams(vmem_limit_bytes=VMEM_LIMIT_BYTES),
        name="hyfilt",
    )(feats, w1, fb1.reshape(1, hid), fw2, fb2.reshape(1, hid), freq.reshape(1, hid), fw3, t, deltas,
      a_mat, s_mat)


def _hyena_kernel(x1_ref, x2_ref, v_ref, w1_ref, w2_ref, wv_ref, b1_ref, b2_ref, bv_ref,
                  u1_ref, u2_ref, un_ref, d_ref, a_ref, s_ref, o_ref, z_scr, zb_scr, re_scr, im_scr):
    seq = x1_ref.shape[0]
    rb = min(seq, 512)
    nrb = seq // rb
    row = lax.broadcasted_iota(I32, (rb, 1), 0)
    col = row % GRID_W
    first = col == 0
    lastc = col == GRID_W - 1
    sgn = jnp.where(row % 2 == 0, 1.0, -1.0)

    def conv3(p_ref, w_ref, b_ref, rows):
        p = p_ref[rows, :].astype(F32)
        prev = jnp.where(first, 0.0, pltpu.roll(p, 1, axis=0))
        nxt = jnp.where(lastc, 0.0, pltpu.roll(p, rb - 1, axis=0))
        w = w_ref[...]
        return w[0:1, :] * prev + w[1:2, :] * p + w[2:3, :] * nxt + b_ref[...]

    def forward(o):
        nyq = None
        for i in range(nrb):
            rows = slice(i * rb, (i + 1) * rb)
            part = jnp.sum(sgn * z_scr[rows, :], axis=0, keepdims=True)
            nyq = part if nyq is None else nyq + part
        for i in range(nrb):
            rows = slice(i * rb, (i + 1) * rb)
            p = _dot(a_ref[rows, :], zb_scr[...])
            q = _dot(s_ref[rows, :], zb_scr[...])
            u1 = u1_ref[o, rows, :].astype(F32)
            u2 = u2_ref[o, rows, :].astype(F32)
            re_scr[rows, :] = (p * u1 + q * u2).astype(BF16)
            im_scr[rows, :] = (q * u1 - p * u2).astype(BF16)
        return nyq * un_ref[o]

    def inverse(o, nyq, rows):
        y = _dot(a_ref[rows, :], re_scr[...]) + _dot(s_ref[rows, :], im_scr[...]) + sgn * nyq
        return y + z_scr[rows, :] * d_ref[o:o + 1, :]

    for i in range(nrb):
        rows = slice(i * rb, (i + 1) * rb)
        v = conv3(v_ref, wv_ref, bv_ref, rows)
        z_scr[rows, :] = v
        zb_scr[rows, :] = v.astype(BF16)
    nyq = forward(0)
    for i in range(nrb):
        rows = slice(i * rb, (i + 1) * rb)
        z = conv3(x1_ref, w1_ref, b1_ref, rows) * inverse(0, nyq, rows)
        z_scr[rows, :] = z
        zb_scr[rows, :] = z.astype(BF16)
    nyq = forward(1)
    for i in range(nrb):
        rows = slice(i * rb, (i + 1) * rb)
        y = conv3(x2_ref, w2_ref, b2_ref, rows) * inverse(1, nyq, rows)
        o_ref[rows, :] = y.astype(o_ref.dtype)


def _hyena(px3, conv_w, conv_b, u1, u2, un, d_skip, a_mat, s_mat):
    b, seq, _ = px3.shape
    ct = HY_CT
    nc = HY_W // ct

    def xspec(j):
        return pl.BlockSpec((None, seq, ct), lambda c, bi: (bi, 0, j * nc + c))

    def wspec(j, rows):
        return pl.BlockSpec((rows, ct), lambda c, bi: (0, j * nc + c))

    uspec = pl.BlockSpec((2, seq, ct), lambda c, bi: (0, 0, c))
    const = pl.BlockSpec((seq, seq), lambda c, bi: (0, 0), pipeline_mode=pl.Buffered(1))
    cb = conv_b.reshape(1, 3 * HY_W)
    return pl.pallas_call(
        _hyena_kernel,
        grid=(nc, b),
        in_specs=[xspec(0), xspec(1), xspec(2), wspec(0, 3), wspec(1, 3), wspec(2, 3),
                  wspec(0, 1), wspec(1, 1), wspec(2, 1), uspec, uspec,
                  pl.BlockSpec((2, 1, ct), lambda c, bi: (0, 0, c)),
                  pl.BlockSpec((2, ct), lambda c, bi: (0, c)), const, const],
        out_specs=pl.BlockSpec((None, seq, ct), lambda c, bi: (bi, 0, c)),
        out_shape=jax.ShapeDtypeStruct((b, seq, HY_W), BF16),
        scratch_shapes=[pltpu.VMEM((seq, ct), F32), pltpu.VMEM((seq, ct), BF16),
                        pltpu.VMEM((seq, ct), BF16), pltpu.VMEM((seq, ct), BF16)],
        compiler_params=_params(("parallel", "parallel")),
        name="hyena",
    )(px3, px3, px3, conv_w, conv_w, conv_w, cb, cb, cb, u1, u2, un, d_skip, a_mat, s_mat)


def _select_rows(rows):
    n = rows[0].shape[1]
    idx = lax.broadcasted_iota(I32, (len(rows), n), 0)
    out = jnp.zeros((len(rows), n), rows[0].dtype)
    for k, r in enumerate(rows):
        out = jnp.where(idx == k, r, out)
    return out


def _first_argmax(x, iota, size):
    m = jnp.max(x, axis=0, keepdims=True)
    return jnp.min(jnp.where(x == m, iota, size), axis=0, keepdims=True)


def _mid_kernel(yhy_ref, yhg_ref, x_ref, woa_ref, wob_ref, g1_ref, sc_ref, sh_ref, g2_ref, ng_ref,
                wr_ref, rb_ref, swg_ref, swu_ref, swd_ref,
                xs_ref, h_ref, lrow_ref, w_ref, ce_ref, rel_ref, nch_ref, cnt_ref, base, wrh_ref, wrl_ref):
    tm = x_ref.shape[0]
    ne = N_EXPERTS

    @pl.when(pl.program_id(0) == 0)
    def _():
        base[...] = jnp.zeros_like(base)
        wr = wr_ref[...]
        wr_hi = wr.astype(BF16)
        wrh_ref[...] = wr_hi
        wrl_ref[...] = (wr - wr_hi.astype(F32)).astype(BF16)

    mix = _dot(yhy_ref[...], woa_ref[...]) + _dot(yhg_ref[...], wob_ref[...])
    xm = x_ref[...] + g1_ref[...] * mix
    h = _rms_mod(xm, ng_ref[...], sc_ref[...], sh_ref[...])
    hb = h.astype(BF16)
    h_ref[...] = hb
    act = (_silu(_dot(hb, swg_ref[...])) * _dot(hb, swu_ref[...])).astype(BF16)
    xs_ref[...] = xm + g2_ref[...] * _dot(act, swd_ref[...])

    h_lo = (h - hb.astype(F32)).astype(BF16)
    logits = _dot_nt(wrh_ref[...], hb) + (_dot_nt(wrh_ref[...], h_lo) + _dot_nt(wrl_ref[...], hb))
    scores = jax.nn.sigmoid(logits)
    biased = scores + rb_ref[...]
    neg = -jnp.inf
    iota_g = lax.broadcasted_iota(I32, (GROUP_SIZE, tm), 0)
    grp = []
    for g in range(N_GROUPS):
        blk = biased[g * GROUP_SIZE:(g + 1) * GROUP_SIZE, :]
        m1 = jnp.max(blk, axis=0, keepdims=True)
        i1 = jnp.min(jnp.where(blk == m1, iota_g, GROUP_SIZE), axis=0, keepdims=True)
        m2 = jnp.max(jnp.where(iota_g == i1, neg, blk), axis=0, keepdims=True)
        grp.append(m1 + m2)
    gsc = _select_rows(grp)
    iota8 = lax.broadcasted_iota(I32, (N_GROUPS, tm), 0)
    gsel = iota8 < 0
    for _ in range(TOPK_GROUPS):
        hit = iota8 == _first_argmax(gsc, iota8, N_GROUPS)
        gsel = gsel | hit
        gsc = jnp.where(hit, neg, gsc)
    gself = jnp.where(gsel, 1.0, 0.0)
    iota_e = lax.broadcasted_iota(I32, (ne, tm), 0)
    gid = iota_e // GROUP_SIZE
    emask = jnp.zeros((ne, tm), F32)
    for g in range(N_GROUPS):
        emask = jnp.where(gid == g, gself[g:g + 1, :], emask)
    cur = jnp.where(emask > 0.0, biased, neg)

    sel_f = jnp.zeros((ne, tm), F32)
    idxs, wts = [], []
    for _ in range(TOP_K):
        idx = _first_argmax(cur, iota_e, ne)
        hit = iota_e == idx
        wts.append(jnp.sum(jnp.where(hit, scores, 0.0), axis=0, keepdims=True))
        idxs.append(idx)
        sel_f = jnp.where(hit, 1.0, sel_f)
        cur = jnp.where(hit, neg, cur)
    wsum = wts[0]
    for wk in wts[1:]:
        wsum = wsum + wk
    scale = ROUTED_SCALE / wsum

    r = lax.broadcasted_iota(I32, (tm, tm), 0)
    s = lax.broadcasted_iota(I32, (tm, tm), 1)
    upper = jnp.where(r < s, 1.0, 0.0).astype(BF16)
    prefix = _dot(sel_f.astype(BF16), upper)
    cnt = jnp.sum(sel_f, axis=1, keepdims=True)
    cnt_al = jnp.floor((cnt + (SLOT_ALIGN - 1)) * (1.0 / SLOT_ALIGN)) * SLOT_ALIGN
    re = lax.broadcasted_iota(I32, (ne, ne), 0)
    ce = lax.broadcasted_iota(I32, (ne, ne), 1)
    lower = jnp.where(ce < re, 1.0, 0.0).astype(BF16)
    loc = _dot(lower, jnp.broadcast_to(cnt_al, (ne, LANES)).astype(BF16))[:, 0:1]
    lrow_all = loc + prefix
    lrows = [jnp.sum(jnp.where(iota_e == idx, lrow_all, 0.0), axis=0, keepdims=True) for idx in idxs]

    nc = ce_ref.shape[-1]
    jrow = lax.broadcasted_iota(I32, (1, nc), 1).astype(F32) * SLOT_ALIGN
    owner = jnp.sum(jnp.where(loc + cnt_al <= jrow, 1.0, 0.0), axis=0, keepdims=True)
    owner = jnp.minimum(owner, ne - 1.0).astype(I32)
    iota_ec = lax.broadcasted_iota(I32, (ne, nc), 0)
    rel = jnp.sum(jnp.where(iota_ec == owner, base[...] - loc, 0.0), axis=0, keepdims=True) + jrow
    base[...] = base[...] + cnt_al
    used = jnp.sum(cnt_al, axis=0, keepdims=True)
    spare = jnp.minimum(jrow - used, (MOE_ROW_CHUNK - SLOT_ALIGN) * 1.0)
    spare = spare + (pl.program_id(0) % 2).astype(F32) * MOE_ROW_CHUNK
    unused = jrow >= used
    owner = jnp.where(unused, ne, owner)
    rel = jnp.where(unused, spare, rel)

    lrow_ref[...] = _select_rows(lrows).astype(I32)
    w_ref[...] = _select_rows([wk * scale for wk in wts])
    ce_ref[...] = owner
    rel_ref[...] = rel.astype(I32)
    nch = used * (1.0 / SLOT_ALIGN)
    nch_ref[...] = jnp.broadcast_to(nch, nch_ref.shape).astype(I32)
    cnt_ref[...] = base[...].astype(I32)


def _mid(yhy, yhg, x2d, seq, woa, wob, g1, sc2, sh2, g2, norm_g, wr_t, rbias, swg, swu, swd, tm):
    t, d = x2d.shape
    per = seq // tm
    ff = swg.shape[1]

    def full(shape):
        return pl.BlockSpec(shape, lambda i: (0,) * len(shape))

    mspec = pl.BlockSpec((None, 1, d), lambda i: (i // per, 0, 0))
    kt = pl.BlockSpec((TOP_K, tm), lambda i: (0, i))
    nt = t // tm

    def per_tile(n):
        return pl.BlockSpec((None, 1, n), lambda i: (i, 0, 0))

    return pl.pallas_call(
        _mid_kernel,
        grid=(nt,),
        in_specs=[pl.BlockSpec((tm, HY_W), lambda i: (i, 0)), pl.BlockSpec((tm, HG_W), lambda i: (i, 0)),
                  pl.BlockSpec((tm, d), lambda i: (i, 0)), full((HY_W, d)), full((HG_W, d)),
                  mspec, mspec, mspec, mspec, full((1, d)), full((N_EXPERTS, d)),
                  full((N_EXPERTS, 1)), full((d, ff)), full((d, ff)), full((ff, d))],
        out_specs=[pl.BlockSpec((tm, d), lambda i: (i, 0)), pl.BlockSpec((tm, d), lambda i: (i, 0)),
                   kt, kt, per_tile(MOE_CHUNKS), per_tile(MOE_CHUNKS), per_tile(LANES), full((N_EXPERTS, 1))],
        out_shape=(jax.ShapeDtypeStruct((t, d), F32), jax.ShapeDtypeStruct((t, d), BF16),
                   jax.ShapeDtypeStruct((TOP_K, t), I32), jax.ShapeDtypeStruct((TOP_K, t), F32),
                   jax.ShapeDtypeStruct((nt, 1, MOE_CHUNKS), I32), jax.ShapeDtypeStruct((nt, 1, MOE_CHUNKS), I32),
                   jax.ShapeDtypeStruct((nt, 1, LANES), I32), jax.ShapeDtypeStruct((N_EXPERTS, 1), I32)),
        scratch_shapes=[pltpu.VMEM((N_EXPERTS, 1), F32), pltpu.VMEM((N_EXPERTS, d), BF16),
                        pltpu.VMEM((N_EXPERTS, d), BF16)],
        compiler_params=_params(("arbitrary",)),
        name="mid",
    )(yhy, yhg, x2d, woa, wob, g1, sc2, sh2, g2, norm_g.reshape(1, d), wr_t,
      rbias.reshape(N_EXPERTS, 1), swg, swu, swd)


def _pack_bf16_pairs(x):
    n = x.shape[1] // 2
    lo = lax.shift_right_logical(pltpu.bitcast(x[:, :n], I32), 16)
    hi = pltpu.bitcast(x[:, n:], I32) & -65536
    return hi | lo


def _unpack_bf16_pairs(p):
    lo = pltpu.bitcast(lax.shift_left(p, 16), F32).astype(BF16)
    hi = pltpu.bitcast(p & -65536, F32).astype(BF16)
    return lo, hi


def _row_chunks(nch):
    return (nch * SLOT_ALIGN + MOE_ROW_CHUNK - 1) // MOE_ROW_CHUNK


def _start_chunk_copies(c, ce_ref, rel_ref, ps_ref, make):
    per = MOE_ROW_CHUNK // SLOT_ALIGN
    for q in range(per):
        j = c * per + q
        glob = pl.multiple_of(ps_ref[ce_ref[j]] + rel_ref[j], SLOT_ALIGN)
        make(pl.multiple_of(j * SLOT_ALIGN, SLOT_ALIGN), glob).start()


def _wait_row_chunks(n, row_chunk_copy):
    def wait(_, carry):
        row_chunk_copy.wait()
        return carry

    lax.fori_loop(0, n, wait, 0)


def _dispatch_kernel(nch_ref, ce_ref, rel_ref, ps_ref, h_ref, lrow_ref, xs_hbm, xloc, zbuf, sems):
    tm = h_ref.shape[0]
    bm = zbuf.shape[0]
    rc = MOE_ROW_CHUNK
    sem = sems.at[0]

    @pl.when(pl.program_id(0) == 0)
    def _():
        zbuf[...] = jnp.zeros_like(zbuf)

        def zcopy(e):
            start = pl.multiple_of(ps_ref[e + 1] - bm, bm)
            return pltpu.make_async_copy(zbuf, xs_hbm.at[pl.ds(start, bm), :], sem)

        def start(e, carry):
            @pl.when(ps_ref[e + 1] > ps_ref[e])
            def _():
                zcopy(e).start()
            return carry

        def wait(e, carry):
            @pl.when(ps_ref[e + 1] > ps_ref[e])
            def _():
                zcopy(e).wait()
            return carry

        lax.fori_loop(0, N_EXPERTS, start, 0)
        lax.fori_loop(0, N_EXPERTS, wait, 0)

        def tcopy(j):
            return pltpu.make_async_copy(zbuf, xs_hbm.at[pl.ds(pl.multiple_of(j * bm, bm), bm), :], sem)

        def tstart(j, carry):
            tcopy(j).start()
            return carry

        def twait(j, carry):
            tcopy(j).wait()
            return carry

        first_tail = ps_ref[N_EXPERTS] // bm
        lax.fori_loop(first_tail, xs_hbm.shape[0] // bm, tstart, 0)
        lax.fori_loop(first_tail, xs_hbm.shape[0] // bm, twait, 0)

    i = pl.program_id(0)
    slot = i % 2
    buf = xloc.at[slot]
    n_rc = _row_chunks(nch_ref[i])
    lrow = lrow_ref[...].astype(F32)
    sub = MOE_SUB_ROWS
    rid = lax.broadcasted_iota(I32, (sub, tm), 0).astype(F32).astype(BF16)
    one = jnp.ones((sub, tm), BF16)

    def select(r0):
        rel = (lrow - jnp.asarray(r0, I32).astype(F32)).astype(BF16)
        onehot = jnp.zeros((sub, tm), BF16)
        for k in range(TOP_K):
            onehot = jnp.where(rid == rel[k:k + 1, :], one, onehot)
        buf[pl.ds(r0, sub), :] = _pack_bf16_pairs(_dot(onehot, h_ref[...]))

    def send(c):
        _start_chunk_copies(c, ce_ref, rel_ref, ps_ref,
                            lambda loc, glob: pltpu.make_async_copy(buf.at[pl.ds(loc, SLOT_ALIGN), :],
                                                                    xs_hbm.at[pl.ds(glob, SLOT_ALIGN), :],
                                                                    sems.at[slot]))

    def retire(tile_slot, n):
        _wait_row_chunks(n, pltpu.make_async_copy(xloc.at[tile_slot, pl.ds(0, rc), :],
                                                  xs_hbm.at[pl.ds(0, rc), :], sems.at[tile_slot]))

    for j in range(rc // sub):
        select(j * sub)

    def body(c, carry):
        send(c - 1)
        r0 = pl.multiple_of(c * rc, rc)
        for j in range(rc // sub):
            select(r0 + j * sub)
        return carry

    lax.fori_loop(1, n_rc, body, 0)
    send(n_rc - 1)

    @pl.when(i > 0)
    def _():
        retire(1 - slot, _row_chunks(nch_ref[i - 1]))

    @pl.when(i == pl.num_programs(0) - 1)
    def _():
        retire(slot, n_rc)


def _local_rows(tm):
    bound = TOP_K * tm + N_EXPERTS * (SLOT_ALIGN - 1)
    return -(-bound // MOE_ROW_CHUNK) * MOE_ROW_CHUNK


def _moe_smem_specs():
    chunk_list = pl.BlockSpec((MOE_CHUNKS,), lambda i, nch: (i,), memory_space=pltpu.SMEM)
    return [chunk_list, chunk_list,
            pl.BlockSpec((N_EXPERTS + 1,), lambda i, nch: (0,), memory_space=pltpu.SMEM)]


def _dispatch(h2d, lrow_kt, ce, rel, nch, ps, n_slots, tm):
    t, d = h2d.shape
    grid_spec = pltpu.PrefetchScalarGridSpec(
        num_scalar_prefetch=1,
        grid=(t // tm,),
        in_specs=_moe_smem_specs() + [pl.BlockSpec((tm, d), lambda i, nch: (i, 0)),
                                      pl.BlockSpec((TOP_K, tm), lambda i, nch: (0, i))],
        out_specs=pl.BlockSpec(memory_space=pl.ANY),
        scratch_shapes=[pltpu.VMEM((2, _local_rows(tm), d // 2), I32), pltpu.VMEM((FFN_BLOCK, d // 2), I32),
                        pltpu.SemaphoreType.DMA((2,))],
    )

    return pl.pallas_call(
        _dispatch_kernel,
        grid_spec=grid_spec,
        out_shape=jax.ShapeDtypeStruct((n_slots, d // 2), I32),
        compiler_params=_params(("arbitrary",)),
        name="dispatch",
    )(nch, ce, rel, ps, h2d, lrow_kt)


def _ffn_kernel(ps_ref, wg_ref, wu_ref, wd_ref, x_hbm, y_hbm, xbuf, ybuf, wg_bf, wu_bf, wd_bf, xsem, ysem):
    e = pl.program_id(0)
    nx, bm, half = xbuf.shape
    ny = ybuf.shape[0]
    ahead = nx - 1
    n_blocks = y_hbm.shape[0] // bm
    g0 = ps_ref[e] // bm
    g1 = ps_ref[e + 1] // bm

    def rows(g):
        return pl.ds(pl.multiple_of(g * bm, bm), bm)

    def x_copy(g):
        s = g % nx
        return pltpu.make_async_copy(x_hbm.at[rows(g), :], xbuf.at[s], xsem.at[s])

    def y_copy(g, s):
        return pltpu.make_async_copy(ybuf.at[s], y_hbm.at[rows(g), :], ysem.at[s])

    @pl.when(e == 0)
    def _():
        ybuf[...] = jnp.zeros_like(ybuf)
        for g in range(ahead):
            x_copy(g).start()
        for s in range(ny):
            y_copy(n_blocks - 1 - s, s).start()

    @pl.when(g1 > g0)
    def _():
        wg_bf[...] = wg_ref[...].astype(BF16)
        wu_bf[...] = wu_ref[...].astype(BF16)
        wd_bf[...] = wd_ref[...].astype(BF16)

    def block(g, carry):
        s = g % ny
        xs = g % nx
        x_copy(g).wait()
        y_copy(g, s).wait()
        x_copy(g + ahead).start()
        hb = bm // 2
        halves = [pl.ds(p * hb, hb) for p in range(2)]
        gu = []
        for r in halves:
            lo, hi = _unpack_bf16_pairs(xbuf[xs, r, :])
            gu.append((_dot(lo, wg_bf[:half, :]) + _dot(hi, wg_bf[half:, :]),
                       _dot(lo, wu_bf[:half, :]) + _dot(hi, wu_bf[half:, :])))
        for r, (gate, up) in zip(halves, gu):
            act = (_silu(gate) * up).astype(BF16)
            y = _dot(act, wd_bf[...])
            ybuf[s, r, :] = _pack_bf16_pairs(y.astype(BF16).astype(F32))
        y_copy(g, s).start(priority=1)
        return carry

    lax.fori_loop(g0, g1, block, 0)

    @pl.when(e == pl.num_programs(0) - 1)
    def _():
        for k in range(ahead):
            x_copy(g1 + k).wait()
        for s in range(ny):
            y_copy(0, s).wait()
        ybuf[0] = jnp.zeros((bm, half), ybuf.dtype)

        def zstart(g, carry):
            y_copy(g, 0).start()
            return carry

        def zwait(g, carry):
            y_copy(g, 0).wait()
            return carry

        lax.fori_loop(g1, n_blocks, zstart, 0)
        lax.fori_loop(g1, n_blocks, zwait, 0)


def _ffn(xs, ps, ew_gate, ew_up, ew_down):
    n_slots, half = xs.shape
    d = 2 * half
    bm = FFN_BLOCK
    n_exp, _, ff = ew_gate.shape
    grid_spec = pltpu.PrefetchScalarGridSpec(
        num_scalar_prefetch=1,
        grid=(n_exp,),
        in_specs=[pl.BlockSpec((None, d, ff), lambda e, ps: (e, 0, 0)),
                  pl.BlockSpec((None, d, ff), lambda e, ps: (e, 0, 0)),
                  pl.BlockSpec((None, ff, d), lambda e, ps: (e, 0, 0)),
                  pl.BlockSpec(memory_space=pl.ANY)],
        out_specs=pl.BlockSpec(memory_space=pl.ANY),
        scratch_shapes=[pltpu.VMEM((FFN_BUFFERS, bm, half), I32), pltpu.VMEM((FFN_BUFFERS, bm, half), I32),
                        pltpu.VMEM((d, ff), BF16), pltpu.VMEM((d, ff), BF16), pltpu.VMEM((ff, d), BF16),
                        pltpu.SemaphoreType.DMA((FFN_BUFFERS,)), pltpu.SemaphoreType.DMA((FFN_BUFFERS,))],
    )
    return pl.pallas_call(
        _ffn_kernel,
        grid_spec=grid_spec,
        out_shape=jax.ShapeDtypeStruct((n_slots, half), I32),
        compiler_params=_params(("arbitrary",)),
        name="ffn",
    )(ps, ew_gate, ew_up, ew_down, xs)


def _combine_kernel(nch_ref, ce_ref, rel_ref, cen_ref, reln_ref, ps_ref, lrow_ref, w_ref, xs_ref, g2_ref, fg_ref,
                    y_hbm, o_ref, yloc, acc, sems):
    tm = xs_ref.shape[0]
    half = yloc.shape[2]
    rc = MOE_ROW_CHUNK
    i = pl.program_id(0)
    last = pl.num_programs(0) - 1
    slot = i % 2
    nxt = jnp.minimum(i + 1, last)
    n_rc = _row_chunks(nch_ref[i])
    n_next = _row_chunks(nch_ref[nxt])

    def fetch(c, ce, rel, s):
        _start_chunk_copies(c, ce, rel, ps_ref,
                            lambda loc, glob: pltpu.make_async_copy(y_hbm.at[pl.ds(glob, SLOT_ALIGN), :],
                                                                    yloc.at[s, pl.ds(loc, SLOT_ALIGN), :],
                                                                    sems.at[s]))

    def arrived(s, n):
        _wait_row_chunks(n, pltpu.make_async_copy(y_hbm.at[pl.ds(0, rc), :], yloc.at[s, pl.ds(0, rc), :],
                                                  sems.at[s]))

    def fetch_own(c, carry):
        fetch(c, ce_ref, rel_ref, slot)
        return carry

    def fetch_next(c, carry):
        fetch(c, cen_ref, reln_ref, 1 - slot)
        return carry

    @pl.when(i == 0)
    def _():
        yloc[...] = jnp.zeros_like(yloc)
        lax.fori_loop(0, n_rc, fetch_own, 0)

    n_prev = _row_chunks(nch_ref[jnp.maximum(i - 1, 0)])
    arrived(slot, jnp.where(i == 0, n_rc, jnp.maximum(n_prev, n_rc)))

    lrow = lrow_ref[...].astype(F32)
    wrow = w_ref[...].astype(BF16)
    acc[...] = jnp.zeros_like(acc)
    sub = MOE_SUB_ROWS
    rid = lax.broadcasted_iota(I32, (sub, tm), 0).astype(F32).astype(BF16)

    def weights(r0):
        rel = (lrow - r0.astype(F32)).astype(BF16)
        wt = jnp.zeros((sub, tm), BF16)
        for k in range(TOP_K):
            wt = jnp.where(rid == rel[k:k + 1, :], jnp.broadcast_to(wrow[k:k + 1, :], (sub, tm)), wt)
        return wt

    def gather(c, carry):
        fetch(c, cen_ref, reln_ref, 1 - slot)
        r0 = pl.multiple_of(c * rc, rc)
        lo_sum, hi_sum = None, None
        for j in range(rc // sub):
            wt = weights(r0 + j * sub)
            lo, hi = _unpack_bf16_pairs(yloc[slot, pl.ds(r0 + j * sub, sub), :])
            lo_sum = _dot_tn(wt, lo) if j == 0 else lo_sum + _dot_tn(wt, lo)
            hi_sum = _dot_tn(wt, hi) if j == 0 else hi_sum + _dot_tn(wt, hi)
        acc[:, :half] += lo_sum
        acc[:, half:] += hi_sum
        return carry

    lax.fori_loop(0, n_rc, gather, 0)
    lax.fori_loop(n_rc, n_next, fetch_next, 0)

    @pl.when(i == last)
    def _():
        arrived(1 - slot, jnp.maximum(n_rc, n_next))

    x = xs_ref[...] + g2_ref[...] * acc[...]
    o_ref[...] = x * lax.rsqrt(jnp.mean(x * x, axis=-1, keepdims=True) + NORM_EPS) * fg_ref[...]


def _combine(y_sorted, lrow_kt, w_kt, ce, rel, nch, ps, xs2d, seq, g2, final_g, tm):
    t, d = xs2d.shape
    per = seq // tm
    nt = t // tm
    own, _, ranges = _moe_smem_specs()
    nxt = pl.BlockSpec((MOE_CHUNKS,), lambda i, nch: (jnp.minimum(i + 1, nt - 1),), memory_space=pltpu.SMEM)
    grid_spec = pltpu.PrefetchScalarGridSpec(
        num_scalar_prefetch=1,
        grid=(nt,),
        in_specs=[own, own, nxt, nxt, ranges,
                  pl.BlockSpec((TOP_K, tm), lambda i, nch: (0, i)),
                  pl.BlockSpec((TOP_K, tm), lambda i, nch: (0, i)),
                  pl.BlockSpec((tm, d), lambda i, nch: (i, 0)),
                  pl.BlockSpec((None, 1, d), lambda i, nch: (i // per, 0, 0)),
                  pl.BlockSpec((1, d), lambda i, nch: (0, 0)),
                  pl.BlockSpec(memory_space=pl.ANY)],
        out_specs=pl.BlockSpec((tm, d), lambda i, nch: (i, 0)),
        scratch_shapes=[pltpu.VMEM((2, _local_rows(tm), d // 2), I32), pltpu.VMEM((tm, d), F32),
                        pltpu.SemaphoreType.DMA((2,))],
    )
    return pl.pallas_call(
        _combine_kernel,
        grid_spec=grid_spec,
        out_shape=jax.ShapeDtypeStruct((t, d), F32),
        compiler_params=_params(("arbitrary",)),
        name="combine",
    )(nch, ce, rel, ce, rel, ps, lrow_kt, w_kt, xs2d, g2, final_g.reshape(1, d), y_sorted)


def kernel(x, c, ctx, c_ctx, w_mod, b_mod, norm1_g, norm2_g, w_in, w_out, hy_conv_w, hy_conv_b,
           hy_fw1, hy_fb1, hy_fw2, hy_fb2, hy_fw3, hy_freq, hy_d, hg_lb_logits, hg_norm_g,
           w_router, router_bias, ew_gate, ew_up, ew_down, sw_gate, sw_up, sw_down, final_g):
    b, seq, d = x.shape
    ctx_len = ctx.shape[1]
    t = b * seq
    layer = 0

    lower = jnp.cumsum(jax.nn.softmax(hg_lb_logits.astype(F32), axis=1), axis=1)
    lb_f, lb_b = lower[0, layer], lower[1, layer]

    rows = -(-(b + 1) // 8) * 8
    cc = jnp.concatenate([c, c_ctx[None, :], jnp.zeros((rows - b - 1, d), F32)], axis=0)
    mod = _modulation(cc, w_mod[layer], b_mod[layer])
    sh1, sc1, g1, sh2, sc2, g2 = (m.reshape(b, 1, d) for m in jnp.split(mod[:b], 6, axis=-1))
    csh1, csc1 = (jnp.broadcast_to(m.reshape(1, 1, d), (b, 1, d))
                  for m in jnp.split(mod[b:b + 1], 6, axis=-1)[:2])

    w_in_bf = w_in[layer].astype(BF16)
    x2d = x.reshape(t, d)
    px = _inproj(x2d, seq, norm1_g[layer], sc1, sh1, w_in_bf, min(seq, 512), w_in_bf.shape[1])
    lo = 3 * HY_W + HG_W
    pc = _inproj(ctx.reshape(b * ctx_len, d), ctx_len, norm1_g[layer], csc1, csh1,
                 w_in_bf[:, lo:lo + 3 * HG_W], ctx_len, HG_W)
    px3 = px.reshape(b, seq, -1)
    pc3 = pc.reshape(b, ctx_len, -1)

    y_hg = _hgrn(px3, pc3, lb_f, lb_b, hg_norm_g[layer])

    a_mat, s_mat = _dft_mats(seq)
    u1, u2, un = _hyena_filters(seq, hy_fw1[layer], hy_fb1[layer], hy_fw2[layer], hy_fb2[layer],
                                hy_fw3[layer], hy_freq[layer], a_mat, s_mat)
    y_hy = _hyena(px3, hy_conv_w[layer], hy_conv_b[layer], u1, u2, un, hy_d[layer], a_mat, s_mat)

    w_out_bf = w_out[layer].astype(BF16)
    tm = min(seq, MOE_TILE)
    nt = t // tm
    xs, h2, lrow_kt, w_kt, ce, rel, nch, counts = _mid(
        y_hy.reshape(t, HY_W), y_hg.reshape(t, HG_W), x2d, seq, w_out_bf[:HY_W], w_out_bf[HY_W:],
        g1, sc2, sh2, g2, norm2_g[layer], w_router[layer].T, router_bias[layer],
        sw_gate[layer].astype(BF16), sw_up[layer].astype(BF16), sw_down[layer].astype(BF16), tm)

    bm = FFN_BLOCK
    counts = counts.reshape(N_EXPERTS)
    padded = (counts + bm - 1) // bm * bm
    p_ends = jnp.cumsum(padded)
    ps = jnp.concatenate([p_ends - padded, p_ends[-1:]]).astype(I32)
    spare_blocks = max(-(-2 * MOE_ROW_CHUNK // bm), FFN_BUFFERS)
    n_blocks = -(-(t * TOP_K + nt * N_EXPERTS * (SLOT_ALIGN - 1)) // bm) + N_EXPERTS + spare_blocks

    ce, rel, nch = ce.reshape(-1), rel.reshape(-1), nch[:, 0, 0]
    x_sorted = _dispatch(h2, lrow_kt, ce, rel, nch, ps, n_blocks * bm, tm)
    y_sorted = _ffn(x_sorted, ps, ew_gate[layer], ew_up[layer], ew_down[layer])
    out = _combine(y_sorted, lrow_kt, w_kt, ce, rel, nch, ps, xs, seq, g2, final_g, tm)
    return out.reshape(b, seq, d)
```

```python
import functools
import math

import jax
import jax.numpy as jnp
from jax import lax
from jax.experimental import pallas as pl
from jax.experimental.pallas import tpu as pltpu

F32 = jnp.float32
BF16 = jnp.bfloat16
I32 = jnp.int32
HIGHEST = lax.Precision.HIGHEST

GRID_W = 64
HY_W = 512
HG_W = 512
HY_EMB = 33
HY_BANDS = 16
HY_DECAY_TARGET = 1e-2
HY_FAST_DECAY_PCT = 0.3
HY_SLOW_DECAY_PCT = 1.5
HG_HEAD_DIM = 128
HG_HEADS = 4
HG_SCALE = HG_HEAD_DIM ** -0.5
HG_CHUNK = 64
N_EXPERTS = 256
TOP_K = 8
N_GROUPS = 8
TOPK_GROUPS = 4
GROUP_SIZE = N_EXPERTS // N_GROUPS
ROUTED_SCALE = 2.5
NORM_EPS = 1e-6

VMEM_LIMIT_BYTES = 56 * 1024 * 1024
LANES = 128
FFN_BLOCK = 512
FFN_BUFFERS = 4
MOE_TILE = 512
SLOT_ALIGN = 16
MOE_ROW_CHUNK = 1024
MOE_SUB_ROWS = 256
MOE_CHUNKS = 1024
HY_CT = 256


def _params(sem, vmem=VMEM_LIMIT_BYTES):
    return pltpu.CompilerParams(dimension_semantics=sem, vmem_limit_bytes=vmem)


def _silu(x):
    return x * jax.nn.sigmoid(x)


def _dot(a, b):
    return jnp.dot(a, b, preferred_element_type=F32)


def _dot_nt(a, b):
    return lax.dot_general(a, b, (((1,), (1,)), ((), ())), preferred_element_type=F32)


def _dot_tn(a, b):
    return lax.dot_general(a, b, (((0,), (0,)), ((), ())), preferred_element_type=F32)


def _mod_kernel(c_ref, w_ref, b_ref, o_ref):
    s = _silu(c_ref[...])
    o_ref[...] = jnp.dot(s, w_ref[...], preferred_element_type=F32, precision=HIGHEST) + b_ref[...]


def _modulation(cc, w_mod, b_mod):
    rows, d = cc.shape
    n = w_mod.shape[1]
    tn = 1024
    return pl.pallas_call(
        _mod_kernel,
        grid=(n // tn,),
        in_specs=[pl.BlockSpec((rows, d), lambda j: (0, 0)),
                  pl.BlockSpec((d, tn), lambda j: (0, j)),
                  pl.BlockSpec((1, tn), lambda j: (0, j))],
        out_specs=pl.BlockSpec((rows, tn), lambda j: (0, j)),
        out_shape=jax.ShapeDtypeStruct((rows, n), F32),
        compiler_params=_params(("parallel",)),
        name="mod",
    )(cc, w_mod, b_mod.reshape(1, n))


def _rms_mod(x, g, sc, sh):
    y = x * lax.rsqrt(jnp.mean(x * x, axis=-1, keepdims=True) + NORM_EPS) * g
    return y * (1.0 + sc) + sh


def _inproj_kernel(x_ref, g_ref, sc_ref, sh_ref, w_ref, o_ref, h_scr):
    @pl.when(pl.program_id(1) == 0)
    def _():
        h_scr[...] = _rms_mod(x_ref[...], g_ref[...], sc_ref[...], sh_ref[...]).astype(BF16)

    o_ref[...] = _dot(h_scr[...], w_ref[...]).astype(o_ref.dtype)


def _inproj(x2d, seq, g, sc, sh, w_bf, tm, tn):
    t, d = x2d.shape
    n = w_bf.shape[1]
    per = seq // tm
    w_mode = dict(pipeline_mode=pl.Buffered(1)) if tn == n else {}
    return pl.pallas_call(
        _inproj_kernel,
        grid=(t // tm, n // tn),
        in_specs=[pl.BlockSpec((tm, d), lambda i, j: (i, 0)),
                  pl.BlockSpec((1, d), lambda i, j: (0, 0)),
                  pl.BlockSpec((None, 1, d), lambda i, j: (i // per, 0, 0)),
                  pl.BlockSpec((None, 1, d), lambda i, j: (i // per, 0, 0)),
                  pl.BlockSpec((d, tn), lambda i, j: (0, j), **w_mode)],
        out_specs=pl.BlockSpec((tm, tn), lambda i, j: (i, j)),
        out_shape=jax.ShapeDtypeStruct((t, n), BF16),
        scratch_shapes=[pltpu.VMEM((tm, d), BF16)],
        compiler_params=_params(("parallel", "arbitrary")),
        name="inproj",
    )(x2d, g.reshape(1, d), sc, sh, w_bf)


def _hg_geometry():
    c = HG_CHUNK
    r = lax.broadcasted_iota(I32, (c, c), 0)
    s = lax.broadcasted_iota(I32, (c, c), 1)
    return {False: (r >= s, c // 2 - 1, c - 1), True: (r <= s, c // 2, 0)}


def _hg_gates(fr, lb, rev):
    sig = jax.nn.sigmoid(fr)
    lf = jnp.log(lb + (1.0 - lb) * sig)
    k = (1.0 - lb) * (1.0 - sig)
    hi = lf.astype(BF16)
    lo = (lf - hi.astype(F32)).astype(BF16)
    tri = jnp.where(_hg_geometry()[rev][0], 1.0, 0.0).astype(BF16)
    return k, _dot(tri, hi) + _dot(tri, lo)


def _hg_steps(chains):
    geo = _hg_geometry()
    work = []
    for ch in chains:
        w = dict(k=ch["k"], bc=ch["bc"])
        work.append(w)
        mask, mid, last = geo[ch["rev"]]
        bc = w["bc"]
        b_mid = bc[mid:mid + 1, :]
        b_last = bc[last:last + 1, :]
        km = w["k"] * jnp.exp(b_mid - bc)
        kd = (km * jnp.exp(b_last - b_mid)).astype(BF16)
        w["ut"] = _dot_tn(ch["v"], kd)
        w["decay"] = jnp.exp(b_last)
        if ch["q"] is not None:
            qm = ch["q"] * jnp.exp(bc - b_mid)
            w["att"] = _dot_nt(qm.astype(BF16), km.astype(BF16))
            qe = (qm * jnp.exp(b_mid)).astype(BF16)
            w["inter"] = _dot_nt(qe, ch["st"].astype(BF16))
    ahead = [_hg_gates(ch["fr_next"], ch["lb"], ch["rev"]) for ch in chains]
    out = []
    for ch, w, nxt in zip(chains, work, ahead):
        o = None
        if ch["q"] is not None:
            att = jnp.where(geo[ch["rev"]][0], w["att"], 0.0).astype(BF16)
            o = _dot(att, ch["v"]) + w["inter"]
        out.append((o, ch["st"] * w["decay"] + w["ut"], nxt))
    return out


def _hgrn_kernel(q_ref, ff_ref, fb_ref, i_ref, g_ref, cff_ref, cfb_ref, ci_ref,
                 lbf_ref, lbb_ref, ng_ref, o_ref, qs_scr, of_scr, ob_scr, st_scr, k_scr, bc_scr):
    seq = q_ref.shape[0]
    ctx = cff_ref.shape[0]
    c = HG_CHUNK
    dh = HG_HEAD_DIM
    rb = min(seq, 256)

    for i in range(seq // rb):
        rows = slice(i * rb, (i + 1) * rb)
        qs_scr[rows, :] = (_silu(q_ref[rows, :].astype(F32)) * HG_SCALE).astype(BF16)
    st_scr[...] = jnp.zeros_like(st_scr)

    def chains(nchunks, ffr, fbr, vr, with_q):
        layout = [(h, rev, fref, lbref) for h in range(HG_HEADS)
                  for rev, fref, lbref in ((False, ffr, lbf_ref), (True, fbr, lbb_ref))]

        def chunk_rows(n, rev):
            ci = (nchunks - 1 - n) if rev else n
            return pl.ds(ci * c if isinstance(ci, int) else pl.multiple_of(ci * c, c), c)

        for slot, (h, rev, fref, lbref) in enumerate(layout):
            cols = slice(h * dh, (h + 1) * dh)
            k0, bc0 = _hg_gates(fref[chunk_rows(0, rev), cols].astype(F32), lbref[:, cols], rev)
            k_scr[slot] = k0
            bc_scr[slot] = bc0

        def body(n, carry):
            n_next = jnp.minimum(n + 1, nchunks - 1)
            chains = []
            for slot, (h, rev, fref, lbref) in enumerate(layout):
                cols = slice(h * dh, (h + 1) * dh)
                rows = chunk_rows(n, rev)
                chains.append(dict(
                    rev=rev, rows=rows, cols=cols, k=k_scr[slot], bc=bc_scr[slot],
                    fr_next=fref[chunk_rows(n_next, rev), cols].astype(F32), v=vr[rows, cols],
                    lb=lbref[:, cols], st=st_scr[slot],
                    q=qs_scr[rows, cols].astype(F32) if with_q else None))
            for slot, (ch, (o, st, (k_next, bc_next))) in enumerate(zip(chains, _hg_steps(chains))):
                st_scr[slot] = st
                k_scr[slot] = k_next
                bc_scr[slot] = bc_next
                if with_q:
                    (ob_scr if ch["rev"] else of_scr)[ch["rows"], ch["cols"]] = o
            return carry

        lax.fori_loop(0, nchunks, body, 0)

    chains(ctx // c, cff_ref, cfb_ref, ci_ref, False)
    chains(seq // c, ff_ref, fb_ref, i_ref, True)

    ng = ng_ref[...]
    for i in range(seq // rb):
        rows = slice(i * rb, (i + 1) * rb)
        gate = _silu(g_ref[rows, :].astype(F32))
        for h in range(HG_HEADS):
            cols = slice(h * dh, (h + 1) * dh)
            o = of_scr[rows, cols] + ob_scr[rows, cols]
            on = o * lax.rsqrt(jnp.mean(o * o, axis=-1, keepdims=True) + NORM_EPS) * ng
            o_ref[rows, cols] = (on * gate[:, cols]).astype(o_ref.dtype)


def _hgrn(px3, pc3, lb_f, lb_b, norm_g):
    b, seq, _ = px3.shape
    ctx = pc3.shape[1]
    dh = HG_HEAD_DIM
    base = 3 * HY_W // HG_W

    def xspec(j):
        return pl.BlockSpec((None, seq, HG_W), lambda bi: (bi, 0, base + j))

    def cspec(j):
        return pl.BlockSpec((None, ctx, HG_W), lambda bi: (bi, 0, j))

    vec = pl.BlockSpec((1, HG_W), lambda bi: (0, 0))
    return pl.pallas_call(
        _hgrn_kernel,
        grid=(b,),
        in_specs=[xspec(0), xspec(1), xspec(2), xspec(3), xspec(4), cspec(0), cspec(1), cspec(2),
                  vec, vec, pl.BlockSpec((1, dh), lambda bi: (0, 0))],
        out_specs=pl.BlockSpec((None, seq, HG_W), lambda bi: (bi, 0, 0)),
        out_shape=jax.ShapeDtypeStruct((b, seq, HG_W), BF16),
        scratch_shapes=[pltpu.VMEM((seq, HG_W), BF16), pltpu.VMEM((seq, HG_W), F32),
                        pltpu.VMEM((seq, HG_W), F32), pltpu.VMEM((2 * HG_HEADS, dh, dh), F32),
                        pltpu.VMEM((2 * HG_HEADS, HG_CHUNK, dh), F32), pltpu.VMEM((2 * HG_HEADS, HG_CHUNK, dh), F32)],
        compiler_params=_params(("parallel",)),
        name="hgrn",
    )(px3, px3, px3, px3, px3, pc3, pc3, pc3, lb_f.reshape(1, HG_W), lb_b.reshape(1, HG_W),
      norm_g.reshape(1, dh))


def _dft_mats(seq):
    f = jnp.arange(seq, dtype=I32)
    m = (f[:, None] * f[None, :]) % (2 * seq)
    ang = m.astype(F32) * (math.pi / seq)
    return jnp.cos(ang).astype(BF16), jnp.sin(ang).astype(BF16)


def _hyfilt_kernel(feat_ref, w1_ref, b1_ref, w2_ref, b2_ref, fr_ref, w3_ref, t_ref, dl_ref,
                   a_ref, s_ref, u1_ref, u2_ref, un_ref):
    seq = feat_ref.shape[0]
    fr = fr_ref[...]
    h = jnp.sin(fr * (jnp.dot(feat_ref[...], w1_ref[...], preferred_element_type=F32, precision=HIGHEST)
                      + b1_ref[...]))
    h = jnp.sin(fr * (jnp.dot(h, w2_ref[...], preferred_element_type=F32, precision=HIGHEST) + b2_ref[...]))
    window = jnp.exp(-t_ref[...] * dl_ref[...])
    row = lax.broadcasted_iota(I32, (seq, 1), 0)
    sgn = jnp.where(row % 2 == 0, 1.0, -1.0)
    cf = jnp.where(row == 0, 1.0, 2.0) * (1.0 / (2 * seq))
    for o in range(2):
        w3 = w3_ref[:, o * 2 * HY_W:(o + 1) * 2 * HY_W]
        ho = jnp.dot(h, w3, preferred_element_type=F32, precision=HIGHEST)
        fwd = ho[:, :HY_W] * window
        bwd = ho[:, HY_W:] * window
        norm = (jnp.sum(jnp.abs(fwd), axis=0, keepdims=True)
                + jnp.sum(jnp.abs(bwd), axis=0, keepdims=True))
        inv = 1.0 / norm
        ksum = (fwd + bwd) * inv
        kdif = (bwd - fwd) * inv
        kr = _dot(a_ref[...], ksum.astype(BF16))
        ki = _dot(s_ref[...], kdif.astype(BF16))
        u1_ref[o] = (kr * cf).astype(u1_ref.dtype)
        u2_ref[o] = (ki * cf).astype(u2_ref.dtype)
        un_ref[o] = jnp.sum(sgn * ksum, axis=0, keepdims=True) * (1.0 / (2 * seq))


def _hyena_filters(seq, fw1, fb1, fw2, fb2, fw3, freq, a_mat, s_mat):
    pos = jnp.arange(seq, dtype=F32)[:, None]
    t = pos / max(seq - 1, 1)
    w = (2.0 * math.pi / seq) * pos
    bands = jnp.linspace(1e-4, HY_BANDS - 1, HY_BANDS, dtype=F32)[None, :]
    feats = jnp.concatenate([t, jnp.cos(bands * w), -jnp.sin(bands * w)], axis=-1)
    feats = jnp.pad(feats, ((0, 0), (0, LANES - HY_EMB)))
    w1 = jnp.pad(fw1, ((0, LANES - HY_EMB), (0, 0)))
    max_decay = math.log(HY_DECAY_TARGET) / HY_FAST_DECAY_PCT
    min_decay = math.log(HY_DECAY_TARGET) / HY_SLOW_DECAY_PCT
    deltas = jnp.abs(jnp.linspace(min_decay, max_decay, HY_W, dtype=F32))[None, :]
    hid = fw2.shape[0]
    return pl.pallas_call(
        _hyfilt_kernel,
        out_shape=(jax.ShapeDtypeStruct((2, seq, HY_W), BF16),
                   jax.ShapeDtypeStruct((2, seq, HY_W), BF16),
                   jax.ShapeDtypeStruct((2, 1, HY_W), F32)),
        compiler_params=pltpu.CompilerParams(vmem_limit_bytes=VMEM_LIMIT_BYTES),
        name="hyfilt",
    )(feats, w1, fb1.reshape(1, hid), fw2, fb2.reshape(1, hid), freq.reshape(1, hid), fw3, t, deltas,
      a_mat, s_mat)


def _hyena_kernel(x1_ref, x2_ref, v_ref, w1_ref, w2_ref, wv_ref, b1_ref, b2_ref, bv_ref,
                  u1_ref, u2_ref, un_ref, d_ref, a_ref, s_ref, o_ref, z_scr, zb_scr, re_scr, im_scr):
    seq = x1_ref.shape[0]
    rb = min(seq, 512)
    nrb = seq // rb
    row = lax.broadcasted_iota(I32, (rb, 1), 0)
    col = row % GRID_W
    first = col == 0
    lastc = col == GRID_W - 1
    sgn = jnp.where(row % 2 == 0, 1.0, -1.0)

    def conv3(p_ref, w_ref, b_ref, rows):
        p = p_ref[rows, :].astype(F32)
        prev = jnp.where(first, 0.0, pltpu.roll(p, 1, axis=0))
        nxt = jnp.where(lastc, 0.0, pltpu.roll(p, rb - 1, axis=0))
        w = w_ref[...]
        return w[0:1, :] * prev + w[1:2, :] * p + w[2:3, :] * nxt + b_ref[...]

    def forward(o):
        nyq = None
        for i in range(nrb):
            rows = slice(i * rb, (i + 1) * rb)
            part = jnp.sum(sgn * z_scr[rows, :], axis=0, keepdims=True)
            nyq = part if nyq is None else nyq + part
        for i in range(nrb):
            rows = slice(i * rb, (i + 1) * rb)
            p = _dot(a_ref[rows, :], zb_scr[...])
            q = _dot(s_ref[rows, :], zb_scr[...])
            u1 = u1_ref[o, rows, :].astype(F32)
            u2 = u2_ref[o, rows, :].astype(F32)
            re_scr[rows, :] = (p * u1 + q * u2).astype(BF16)
            im_scr[rows, :] = (q * u1 - p * u2).astype(BF16)
        return nyq * un_ref[o]

    def inverse(o, nyq, rows):
        y = _dot(a_ref[rows, :], re_scr[...]) + _dot(s_ref[rows, :], im_scr[...]) + sgn * nyq
        return y + z_scr[rows, :] * d_ref[o:o + 1, :]

    for i in range(nrb):
        rows = slice(i * rb, (i + 1) * rb)
        v = conv3(v_ref, wv_ref, bv_ref, rows)
        z_scr[rows, :] = v
        zb_scr[rows, :] = v.astype(BF16)
    nyq = forward(0)
    for i in range(nrb):
        rows = slice(i * rb, (i + 1) * rb)
        z = conv3(x1_ref, w1_ref, b1_ref, rows) * inverse(0, nyq, rows)
        z_scr[rows, :] = z
        zb_scr[rows, :] = z.astype(BF16)
    nyq = forward(1)
    for i in range(nrb):
        rows = slice(i * rb, (i + 1) * rb)
        y = conv3(x2_ref, w2_ref, b2_ref, rows) * inverse(1, nyq, rows)
        o_ref[rows, :] = y.astype(o_ref.dtype)


def _hyena(px3, conv_w, conv_b, u1, u2, un, d_skip, a_mat, s_mat):
    b, seq, _ = px3.shape
    ct = HY_CT
    nc = HY_W // ct

    def xspec(j):
        return pl.BlockSpec((None, seq, ct), lambda c, bi: (bi, 0, j * nc + c))

    def wspec(j, rows):
        return pl.BlockSpec((rows, ct), lambda c, bi: (0, j * nc + c))

    uspec = pl.BlockSpec((2, seq, ct), lambda c, bi: (0, 0, c))
    const = pl.BlockSpec((seq, seq), lambda c, bi: (0, 0), pipeline_mode=pl.Buffered(1))
    cb = conv_b.reshape(1, 3 * HY_W)
    return pl.pallas_call(
        _hyena_kernel,
        grid=(nc, b),
        in_specs=[xspec(0), xspec(1), xspec(2), wspec(0, 3), wspec(1, 3), wspec(2, 3),
                  wspec(0, 1), wspec(1, 1), wspec(2, 1), uspec, uspec,
                  pl.BlockSpec((2, 1, ct), lambda c, bi: (0, 0, c)),
                  pl.BlockSpec((2, ct), lambda c, bi: (0, c)), const, const],
        out_specs=pl.BlockSpec((None, seq, ct), lambda c, bi: (bi, 0, c)),
        out_shape=jax.ShapeDtypeStruct((b, seq, HY_W), BF16),
        scratch_shapes=[pltpu.VMEM((seq, ct), F32), pltpu.VMEM((seq, ct), BF16),
                        pltpu.VMEM((seq, ct), BF16), pltpu.VMEM((seq, ct), BF16)],
        compiler_params=_params(("parallel", "parallel")),
        name="hyena",
    )(px3, px3, px3, conv_w, conv_w, conv_w, cb, cb, cb, u1, u2, un, d_skip, a_mat, s_mat)


def _select_rows(rows):
    n = rows[0].shape[1]
    idx = lax.broadcasted_iota(I32, (len(rows), n), 0)
    out = jnp.zeros((len(rows), n), rows[0].dtype)
    for k, r in enumerate(rows):
        out = jnp.where(idx == k, r, out)
    return out


def _first_argmax(x, iota, size):
    m = jnp.max(x, axis=0, keepdims=True)
    return jnp.min(jnp.where(x == m, iota, size), axis=0, keepdims=True)


def _mid_kernel(yhy_ref, yhg_ref, x_ref, woa_ref, wob_ref, g1_ref, sc_ref, sh_ref, g2_ref, ng_ref,
                wr_ref, rb_ref, swg_ref, swu_ref, swd_ref,
                xs_ref, h_ref, lrow_ref, w_ref, ce_ref, rel_ref, nch_ref, cnt_ref, base, wrh_ref, wrl_ref):
    tm = x_ref.shape[0]
    ne = N_EXPERTS

    @pl.when(pl.program_id(0) == 0)
    def _():
        base[...] = jnp.zeros_like(base)
        wr = wr_ref[...]
        wr_hi = wr.astype(BF16)
        wrh_ref[...] = wr_hi
        wrl_ref[...] = (wr - wr_hi.astype(F32)).astype(BF16)

    mix = _dot(yhy_ref[...], woa_ref[...]) + _dot(yhg_ref[...], wob_ref[...])
    xm = x_ref[...] + g1_ref[...] * mix
    h = _rms_mod(xm, ng_ref[...], sc_ref[...], sh_ref[...])
    hb = h.astype(BF16)
    h_ref[...] = hb
    act = (_silu(_dot(hb, swg_ref[...])) * _dot(hb, swu_ref[...])).astype(BF16)
    xs_ref[...] = xm + g2_ref[...] * _dot(act, swd_ref[...])

    h_lo = (h - hb.astype(F32)).astype(BF16)
    logits = _dot_nt(wrh_ref[...], hb) + (_dot_nt(wrh_ref[...], h_lo) + _dot_nt(wrl_ref[...], hb))
    scores = jax.nn.sigmoid(logits)
    biased = scores + rb_ref[...]
    neg = -jnp.inf
    iota_g = lax.broadcasted_iota(I32, (GROUP_SIZE, tm), 0)
    grp = []
    for g in range(N_GROUPS):
        blk = biased[g * GROUP_SIZE:(g + 1) * GROUP_SIZE, :]
        m1 = jnp.max(blk, axis=0, keepdims=True)
        i1 = jnp.min(jnp.where(blk == m1, iota_g, GROUP_SIZE), axis=0, keepdims=True)
        m2 = jnp.max(jnp.where(iota_g == i1, neg, blk), axis=0, keepdims=True)
        grp.append(m1 + m2)
    gsc = _select_rows(grp)
    iota8 = lax.broadcasted_iota(I32, (N_GROUPS, tm), 0)
    gsel = iota8 < 0
    for _ in range(TOPK_GROUPS):
        hit = iota8 == _first_argmax(gsc, iota8, N_GROUPS)
        gsel = gsel | hit
        gsc = jnp.where(hit, neg, gsc)
    gself = jnp.where(gsel, 1.0, 0.0)
    iota_e = lax.broadcasted_iota(I32, (ne, tm), 0)
    gid = iota_e // GROUP_SIZE
    emask = jnp.zeros((ne, tm), F32)
    for g in range(N_GROUPS):
        emask = jnp.where(gid == g, gself[g:g + 1, :], emask)
    cur = jnp.where(emask > 0.0, biased, neg)

    sel_f = jnp.zeros((ne, tm), F32)
    idxs, wts = [], []
    for _ in range(TOP_K):
        idx = _first_argmax(cur, iota_e, ne)
        hit = iota_e == idx
        wts.append(jnp.sum(jnp.where(hit, scores, 0.0), axis=0, keepdims=True))
        idxs.append(idx)
        sel_f = jnp.where(hit, 1.0, sel_f)
        cur = jnp.where(hit, neg, cur)
    wsum = wts[0]
    for wk in wts[1:]:
        wsum = wsum + wk
    scale = ROUTED_SCALE / wsum

    r = lax.broadcasted_iota(I32, (tm, tm), 0)
    s = lax.broadcasted_iota(I32, (tm, tm), 1)
    upper = jnp.where(r < s, 1.0, 0.0).astype(BF16)
    prefix = _dot(sel_f.astype(BF16), upper)
    cnt = jnp.sum(sel_f, axis=1, keepdims=True)
    cnt_al = jnp.floor((cnt + (SLOT_ALIGN - 1)) * (1.0 / SLOT_ALIGN)) * SLOT_ALIGN
    re = lax.broadcasted_iota(I32, (ne, ne), 0)
    ce = lax.broadcasted_iota(I32, (ne, ne), 1)
    lower = jnp.where(ce < re, 1.0, 0.0).astype(BF16)
    loc = _dot(lower, jnp.broadcast_to(cnt_al, (ne, LANES)).astype(BF16))[:, 0:1]
    lrow_all = loc + prefix
    lrows = [jnp.sum(jnp.where(iota_e == idx, lrow_all, 0.0), axis=0, keepdims=True) for idx in idxs]

    nc = ce_ref.shape[-1]
    jrow = lax.broadcasted_iota(I32, (1, nc), 1).astype(F32) * SLOT_ALIGN
    owner = jnp.sum(jnp.where(loc + cnt_al <= jrow, 1.0, 0.0), axis=0, keepdims=True)
    owner = jnp.minimum(owner, ne - 1.0).astype(I32)
    iota_ec = lax.broadcasted_iota(I32, (ne, nc), 0)
    rel = jnp.sum(jnp.where(iota_ec == owner, base[...] - loc, 0.0), axis=0, keepdims=True) + jrow
    base[...] = base[...] + cnt_al
    used = jnp.sum(cnt_al, axis=0, keepdims=True)
    spare = jnp.minimum(jrow - used, (MOE_ROW_CHUNK - SLOT_ALIGN) * 1.0)
    spare = spare + (pl.program_id(0) % 2).astype(F32) * MOE_ROW_CHUNK
    unused = jrow >= used
    owner = jnp.where(unused, ne, owner)
    rel = jnp.where(unused, spare, rel)

    lrow_ref[...] = _select_rows(lrows).astype(I32)
    w_ref[...] = _select_rows([wk * scale for wk in wts])
    ce_ref[...] = owner
    rel_ref[...] = rel.astype(I32)
    nch = used * (1.0 / SLOT_ALIGN)
    nch_ref[...] = jnp.broadcast_to(nch, nch_ref.shape).astype(I32)
    cnt_ref[...] = base[...].astype(I32)


def _mid(yhy, yhg, x2d, seq, woa, wob, g1, sc2, sh2, g2, norm_g, wr_t, rbias, swg, swu, swd, tm):
    t, d = x2d.shape
    per = seq // tm
    ff = swg.shape[1]

    def full(shape):
        return pl.BlockSpec(shape, lambda i: (0,) * len(shape))

    mspec = pl.BlockSpec((None, 1, d), lambda i: (i // per, 0, 0))
    kt = pl.BlockSpec((TOP_K, tm), lambda i: (0, i))
    nt = t // tm

    def per_tile(n):
        return pl.BlockSpec((None, 1, n), lambda i: (i, 0, 0))

    return pl.pallas_call(
        _mid_kernel,
        grid=(nt,),
        in_specs=[pl.BlockSpec((tm, HY_W), lambda i: (i, 0)), pl.BlockSpec((tm, HG_W), lambda i: (i, 0)),
                  pl.BlockSpec((tm, d), lambda i: (i, 0)), full((HY_W, d)), full((HG_W, d)),
                  mspec, mspec, mspec, mspec, full((1, d)), full((N_EXPERTS, d)),
                  full((N_EXPERTS, 1)), full((d, ff)), full((d, ff)), full((ff, d))],
        out_specs=[pl.BlockSpec((tm, d), lambda i: (i, 0)), pl.BlockSpec((tm, d), lambda i: (i, 0)),
                   kt, kt, per_tile(MOE_CHUNKS), per_tile(MOE_CHUNKS), per_tile(LANES), full((N_EXPERTS, 1))],
        out_shape=(jax.ShapeDtypeStruct((t, d), F32), jax.ShapeDtypeStruct((t, d), BF16),
                   jax.ShapeDtypeStruct((TOP_K, t), I32), jax.ShapeDtypeStruct((TOP_K, t), F32),
                   jax.ShapeDtypeStruct((nt, 1, MOE_CHUNKS), I32), jax.ShapeDtypeStruct((nt, 1, MOE_CHUNKS), I32),
                   jax.ShapeDtypeStruct((nt, 1, LANES), I32), jax.ShapeDtypeStruct((N_EXPERTS, 1), I32)),
        scratch_shapes=[pltpu.VMEM((N_EXPERTS, 1), F32), pltpu.VMEM((N_EXPERTS, d), BF16),
                        pltpu.VMEM((N_EXPERTS, d), BF16)],
        compiler_params=_params(("arbitrary",)),
        name="mid",
    )(yhy, yhg, x2d, woa, wob, g1, sc2, sh2, g2, norm_g.reshape(1, d), wr_t,
      rbias.reshape(N_EXPERTS, 1), swg, swu, swd)


def _row_chunks(nch):
    return (nch * SLOT_ALIGN + MOE_ROW_CHUNK - 1) // MOE_ROW_CHUNK


def _start_chunk_copies(c, ce_ref, rel_ref, ps_ref, make):
    per = MOE_ROW_CHUNK // SLOT_ALIGN
    for q in range(per):
        j = c * per + q
        glob = pl.multiple_of(ps_ref[ce_ref[j]] + rel_ref[j], SLOT_ALIGN)
        make(pl.multiple_of(j * SLOT_ALIGN, SLOT_ALIGN), glob).start()


def _wait_row_chunks(n, row_chunk_copy):
    def wait(_, carry):
        row_chunk_copy.wait()
        return carry

    lax.fori_loop(0, n, wait, 0)


def _dispatch_kernel(nch_ref, ce_ref, rel_ref, ps_ref, h_ref, lrow_ref, xs_hbm, xloc, zbuf, sems):
    tm = h_ref.shape[0]
    bm = zbuf.shape[0]
    rc = MOE_ROW_CHUNK
    sem = sems.at[0]

    @pl.when(pl.program_id(0) == 0)
    def _():
        zbuf[...] = jnp.zeros_like(zbuf)

        def zcopy(e):
            start = pl.multiple_of(ps_ref[e + 1] - bm, bm)
            return pltpu.make_async_copy(zbuf, xs_hbm.at[pl.ds(start, bm), :], sem)

        def start(e, carry):
            @pl.when(ps_ref[e + 1] > ps_ref[e])
            def _():
                zcopy(e).start()
            return carry

        def wait(e, carry):
            @pl.when(ps_ref[e + 1] > ps_ref[e])
            def _():
                zcopy(e).wait()
            return carry

        lax.fori_loop(0, N_EXPERTS, start, 0)
        lax.fori_loop(0, N_EXPERTS, wait, 0)

        def tcopy(j):
            return pltpu.make_async_copy(zbuf, xs_hbm.at[pl.ds(pl.multiple_of(j * bm, bm), bm), :], sem)

        def tstart(j, carry):
            tcopy(j).start()
            return carry

        def twait(j, carry):
            tcopy(j).wait()
            return carry

        first_tail = ps_ref[N_EXPERTS] // bm
        lax.fori_loop(first_tail, xs_hbm.shape[0] // bm, tstart, 0)
        lax.fori_loop(first_tail, xs_hbm.shape[0] // bm, twait, 0)

    i = pl.program_id(0)
    slot = i % 2
    buf = xloc.at[slot]
    n_rc = _row_chunks(nch_ref[i])
    lrow = lrow_ref[...].astype(F32)
    sub = MOE_SUB_ROWS
    rid = lax.broadcasted_iota(I32, (sub, tm), 0).astype(F32).astype(BF16)
    one = jnp.ones((sub, tm), BF16)

    def select(r0):
        rel = (lrow - jnp.asarray(r0, I32).astype(F32)).astype(BF16)
        onehot = jnp.zeros((sub, tm), BF16)
        for k in range(TOP_K):
            onehot = jnp.where(rid == rel[k:k + 1, :], one, onehot)
        buf[pl.ds(r0, sub), :] = _dot(onehot, h_ref[...]).astype(BF16)

    def send(c):
        _start_chunk_copies(c, ce_ref, rel_ref, ps_ref,
                            lambda loc, glob: pltpu.make_async_copy(buf.at[pl.ds(loc, SLOT_ALIGN), :],
                                                                    xs_hbm.at[pl.ds(glob, SLOT_ALIGN), :],
                                                                    sems.at[slot]))

    def retire(tile_slot, n):
        _wait_row_chunks(n, pltpu.make_async_copy(xloc.at[tile_slot, pl.ds(0, rc), :],
                                                  xs_hbm.at[pl.ds(0, rc), :], sems.at[tile_slot]))

    for j in range(rc // sub):
        select(j * sub)

    def body(c, carry):
        send(c - 1)
        r0 = pl.multiple_of(c * rc, rc)
        for j in range(rc // sub):
            select(r0 + j * sub)
        return carry

    lax.fori_loop(1, n_rc, body, 0)
    send(n_rc - 1)

    @pl.when(i > 0)
    def _():
        retire(1 - slot, _row_chunks(nch_ref[i - 1]))

    @pl.when(i == pl.num_programs(0) - 1)
    def _():
        retire(slot, n_rc)


def _local_rows(tm):
    bound = TOP_K * tm + N_EXPERTS * (SLOT_ALIGN - 1)
    return -(-bound // MOE_ROW_CHUNK) * MOE_ROW_CHUNK


def _moe_smem_specs():
    chunk_list = pl.BlockSpec((MOE_CHUNKS,), lambda i, nch: (i,), memory_space=pltpu.SMEM)
    return [chunk_list, chunk_list,
            pl.BlockSpec((N_EXPERTS + 1,), lambda i, nch: (0,), memory_space=pltpu.SMEM)]


def _dispatch(h2d, lrow_kt, ce, rel, nch, ps, n_slots, tm):
    t, d = h2d.shape
    grid_spec = pltpu.PrefetchScalarGridSpec(
        num_scalar_prefetch=1,
        grid=(t // tm,),
        in_specs=_moe_smem_specs() + [pl.BlockSpec((tm, d), lambda i, nch: (i, 0)),
                                      pl.BlockSpec((TOP_K, tm), lambda i, nch: (0, i))],
        out_specs=pl.BlockSpec(memory_space=pl.ANY),
        scratch_shapes=[pltpu.VMEM((2, _local_rows(tm), d), BF16), pltpu.VMEM((FFN_BLOCK, d), BF16),
                        pltpu.SemaphoreType.DMA((2,))],
    )

    return pl.pallas_call(
        _dispatch_kernel,
        grid_spec=grid_spec,
        out_shape=jax.ShapeDtypeStruct((n_slots, d), BF16),
        compiler_params=_params(("arbitrary",)),
        name="dispatch",
    )(nch, ce, rel, ps, h2d, lrow_kt)


def _ffn_kernel(ps_ref, wg_ref, wu_ref, wd_ref, x_hbm, y_hbm, xbuf, ybuf, wg_bf, wu_bf, wd_bf, xsem, ysem):
    e = pl.program_id(0)
    nx, bm, d = xbuf.shape
    ny = ybuf.shape[0]
    ahead = nx - 1
    n_blocks = y_hbm.shape[0] // bm
    g0 = ps_ref[e] // bm
    g1 = ps_ref[e + 1] // bm

    def rows(g):
        return pl.ds(pl.multiple_of(g * bm, bm), bm)

    def x_copy(g):
        s = g % nx
        return pltpu.make_async_copy(x_hbm.at[rows(g), :], xbuf.at[s], xsem.at[s])

    def y_copy(g, s):
        return pltpu.make_async_copy(ybuf.at[s], y_hbm.at[rows(g), :], ysem.at[s])

    @pl.when(e == 0)
    def _():
        ybuf[...] = jnp.zeros_like(ybuf)
        for g in range(ahead):
            x_copy(g).start()
        for s in range(ny):
            y_copy(n_blocks - 1 - s, s).start()

    @pl.when(g1 > g0)
    def _():
        wg_bf[...] = wg_ref[...].astype(BF16)
        wu_bf[...] = wu_ref[...].astype(BF16)
        wd_bf[...] = wd_ref[...].astype(BF16)

    def block(g, carry):
        s = g % ny
        xs = g % nx
        x_copy(g).wait()
        y_copy(g, s).wait()
        x_copy(g + ahead).start()
        hb = bm // 2
        halves = [pl.ds(p * hb, hb) for p in range(2)]
        gu = []
        for r in halves:
            x = xbuf[xs, r, :]
            gu.append((_dot(x, wg_bf[...]), _dot(x, wu_bf[...])))
        for r, (gate, up) in zip(halves, gu):
            act = (_silu(gate) * up).astype(BF16)
            y = _dot(act, wd_bf[...])
            ybuf[s, r, :] = y.astype(BF16)
        y_copy(g, s).start(priority=1)
        return carry

    lax.fori_loop(g0, g1, block, 0)

    @pl.when(e == pl.num_programs(0) - 1)
    def _():
        for k in range(ahead):
            x_copy(g1 + k).wait()
        for s in range(ny):
            y_copy(0, s).wait()
        ybuf[0] = jnp.zeros((bm, d), ybuf.dtype)

        def zstart(g, carry):
            y_copy(g, 0).start()
            return carry

        def zwait(g, carry):
            y_copy(g, 0).wait()
            return carry

        lax.fori_loop(g1, n_blocks, zstart, 0)
        lax.fori_loop(g1, n_blocks, zwait, 0)


def _ffn(xs, ps, ew_gate, ew_up, ew_down):
    n_slots, d = xs.shape
    bm = FFN_BLOCK
    n_exp, _, ff = ew_gate.shape
    grid_spec = pltpu.PrefetchScalarGridSpec(
        num_scalar_prefetch=1,
        grid=(n_exp,),
        in_specs=[pl.BlockSpec((None, d, ff), lambda e, ps: (e, 0, 0)),
                  pl.BlockSpec((None, d, ff), lambda e, ps: (e, 0, 0)),
                  pl.BlockSpec((None, ff, d), lambda e, ps: (e, 0, 0)),
                  pl.BlockSpec(memory_space=pl.ANY)],
        out_specs=pl.BlockSpec(memory_space=pl.ANY),
        scratch_shapes=[pltpu.VMEM((FFN_BUFFERS, bm, d), BF16), pltpu.VMEM((FFN_BUFFERS, bm, d), BF16),
                        pltpu.VMEM((d, ff), BF16), pltpu.VMEM((d, ff), BF16), pltpu.VMEM((ff, d), BF16),
                        pltpu.SemaphoreType.DMA((FFN_BUFFERS,)), pltpu.SemaphoreType.DMA((FFN_BUFFERS,))],
    )
    return pl.pallas_call(
        _ffn_kernel,
        grid_spec=grid_spec,
        out_shape=jax.ShapeDtypeStruct((n_slots, d), BF16),
        compiler_params=_params(("arbitrary",)),
        name="ffn",
    )(ps, ew_gate, ew_up, ew_down, xs)


def _combine_kernel(nch_ref, ce_ref, rel_ref, cen_ref, reln_ref, ps_ref, lrow_ref, w_ref, xs_ref, g2_ref, fg_ref,
                    y_hbm, o_ref, yloc, acc, sems):
    tm = xs_ref.shape[0]
    rc = MOE_ROW_CHUNK
    i = pl.program_id(0)
    last = pl.num_programs(0) - 1
    slot = i % 2
    nxt = jnp.minimum(i + 1, last)
    n_rc = _row_chunks(nch_ref[i])
    n_next = _row_chunks(nch_ref[nxt])

    def fetch(c, ce, rel, s):
        _start_chunk_copies(c, ce, rel, ps_ref,
                            lambda loc, glob: pltpu.make_async_copy(y_hbm.at[pl.ds(glob, SLOT_ALIGN), :],
                                                                    yloc.at[s, pl.ds(loc, SLOT_ALIGN), :],
                                                                    sems.at[s]))

    def arrived(s, n):
        _wait_row_chunks(n, pltpu.make_async_copy(y_hbm.at[pl.ds(0, rc), :], yloc.at[s, pl.ds(0, rc), :],
                                                  sems.at[s]))

    def fetch_own(c, carry):
        fetch(c, ce_ref, rel_ref, slot)
        return carry

    def fetch_next(c, carry):
        fetch(c, cen_ref, reln_ref, 1 - slot)
        return carry

    @pl.when(i == 0)
    def _():
        yloc[...] = jnp.zeros_like(yloc)
        lax.fori_loop(0, n_rc, fetch_own, 0)

    n_prev = _row_chunks(nch_ref[jnp.maximum(i - 1, 0)])
    arrived(slot, jnp.where(i == 0, n_rc, jnp.maximum(n_prev, n_rc)))

    lrow = lrow_ref[...].astype(F32)
    wrow = w_ref[...].astype(BF16)
    acc[...] = jnp.zeros_like(acc)
    sub = MOE_SUB_ROWS
    rid = lax.broadcasted_iota(I32, (sub, tm), 0).astype(F32).astype(BF16)

    def weights(r0):
        rel = (lrow - r0.astype(F32)).astype(BF16)
        wt = jnp.zeros((sub, tm), BF16)
        for k in range(TOP_K):
            wt = jnp.where(rid == rel[k:k + 1, :], jnp.broadcast_to(wrow[k:k + 1, :], (sub, tm)), wt)
        return wt

    def gather(c, carry):
        fetch(c, cen_ref, reln_ref, 1 - slot)
        r0 = pl.multiple_of(c * rc, rc)
        total = None
        for j in range(rc // sub):
            part = _dot_tn(weights(r0 + j * sub), yloc[slot, pl.ds(r0 + j * sub, sub), :])
            total = part if j == 0 else total + part
        acc[...] += total
        return carry

    lax.fori_loop(0, n_rc, gather, 0)
    lax.fori_loop(n_rc, n_next, fetch_next, 0)

    @pl.when(i == last)
    def _():
        arrived(1 - slot, jnp.maximum(n_rc, n_next))

    x = xs_ref[...] + g2_ref[...] * acc[...]
    o_ref[...] = x * lax.rsqrt(jnp.mean(x * x, axis=-1, keepdims=True) + NORM_EPS) * fg_ref[...]


def _combine(y_sorted, lrow_kt, w_kt, ce, rel, nch, ps, xs2d, seq, g2, final_g, tm):
    t, d = xs2d.shape
    per = seq // tm
    nt = t // tm
    own, _, ranges = _moe_smem_specs()
    nxt = pl.BlockSpec((MOE_CHUNKS,), lambda i, nch: (jnp.minimum(i + 1, nt - 1),), memory_space=pltpu.SMEM)
    grid_spec = pltpu.PrefetchScalarGridSpec(
        num_scalar_prefetch=1,
        grid=(nt,),
        in_specs=[own, own, nxt, nxt, ranges,
                  pl.BlockSpec((TOP_K, tm), lambda i, nch: (0, i)),
                  pl.BlockSpec((TOP_K, tm), lambda i, nch: (0, i)),
                  pl.BlockSpec((tm, d), lambda i, nch: (i, 0)),
                  pl.BlockSpec((None, 1, d), lambda i, nch: (i // per, 0, 0)),
                  pl.BlockSpec((1, d), lambda i, nch: (0, 0)),
                  pl.BlockSpec(memory_space=pl.ANY)],
        out_specs=pl.BlockSpec((tm, d), lambda i, nch: (i, 0)),
        scratch_shapes=[pltpu.VMEM((2, _local_rows(tm), d), BF16), pltpu.VMEM((tm, d), F32),
                        pltpu.SemaphoreType.DMA((2,))],
    )
    return pl.pallas_call(
        _combine_kernel,
        grid_spec=grid_spec,
        out_shape=jax.ShapeDtypeStruct((t, d), F32),
        compiler_params=_params(("arbitrary",)),
        name="combine",
    )(nch, ce, rel, ce, rel, ps, lrow_kt, w_kt, xs2d, g2, final_g.reshape(1, d), y_sorted)


def kernel(x, c, ctx, c_ctx, w_mod, b_mod, norm1_g, norm2_g, w_in, w_out, hy_conv_w, hy_conv_b,
           hy_fw1, hy_fb1, hy_fw2, hy_fb2, hy_fw3, hy_freq, hy_d, hg_lb_logits, hg_norm_g,
           w_router, router_bias, ew_gate, ew_up, ew_down, sw_gate, sw_up, sw_down, final_g):
    b, seq, d = x.shape
    ctx_len = ctx.shape[1]
    t = b * seq
    layer = 0

    lower = jnp.cumsum(jax.nn.softmax(hg_lb_logits.astype(F32), axis=1), axis=1)
    lb_f, lb_b = lower[0, layer], lower[1, layer]

    rows = -(-(b + 1) // 8) * 8
    cc = jnp.concatenate([c, c_ctx[None, :], jnp.zeros((rows - b - 1, d), F32)], axis=0)
    mod = _modulation(cc, w_mod[layer], b_mod[layer])
    sh1, sc1, g1, sh2, sc2, g2 = (m.reshape(b, 1, d) for m in jnp.split(mod[:b], 6, axis=-1))
    csh1, csc1 = (jnp.broadcast_to(m.reshape(1, 1, d), (b, 1, d))
                  for m in jnp.split(mod[b:b + 1], 6, axis=-1)[:2])

    w_in_bf = w_in[layer].astype(BF16)
    x2d = x.reshape(t, d)
    px = _inproj(x2d, seq, norm1_g[layer], sc1, sh1, w_in_bf, min(seq, 512), w_in_bf.shape[1])
    lo = 3 * HY_W + HG_W
    pc = _inproj(ctx.reshape(b * ctx_len, d), ctx_len, norm1_g[layer], csc1, csh1,
                 w_in_bf[:, lo:lo + 3 * HG_W], ctx_len, HG_W)
    px3 = px.reshape(b, seq, -1)
    pc3 = pc.reshape(b, ctx_len, -1)

    y_hg = _hgrn(px3, pc3, lb_f, lb_b, hg_norm_g[layer])

    a_mat, s_mat = _dft_mats(seq)
    u1, u2, un = _hyena_filters(seq, hy_fw1[layer], hy_fb1[layer], hy_fw2[layer], hy_fb2[layer],
                                hy_fw3[layer], hy_freq[layer], a_mat, s_mat)
    y_hy = _hyena(px3, hy_conv_w[layer], hy_conv_b[layer], u1, u2, un, hy_d[layer], a_mat, s_mat)

    w_out_bf = w_out[layer].astype(BF16)
    tm = min(seq, MOE_TILE)
    nt = t // tm
    xs, h2, lrow_kt, w_kt, ce, rel, nch, counts = _mid(
        y_hy.reshape(t, HY_W), y_hg.reshape(t, HG_W), x2d, seq, w_out_bf[:HY_W], w_out_bf[HY_W:],
        g1, sc2, sh2, g2, norm2_g[layer], w_router[layer].T, router_bias[layer],
        sw_gate[layer].astype(BF16), sw_up[layer].astype(BF16), sw_down[layer].astype(BF16), tm)

    bm = FFN_BLOCK
    counts = counts.reshape(N_EXPERTS)
    padded = (counts + bm - 1) // bm * bm
    p_ends = jnp.cumsum(padded)
    ps = jnp.concatenate([p_ends - padded, p_ends[-1:]]).astype(I32)
    spare_blocks = max(-(-2 * MOE_ROW_CHUNK // bm), FFN_BUFFERS)
    n_blocks = -(-(t * TOP_K + nt * N_EXPERTS * (SLOT_ALIGN - 1)) // bm) + N_EXPERTS + spare_blocks

    ce, rel, nch = ce.reshape(-1), rel.reshape(-1), nch[:, 0, 0]
    x_sorted = _dispatch(h2, lrow_kt, ce, rel, nch, ps, n_blocks * bm, tm)
    y_sorted = _ffn(x_sorted, ps, ew_gate[layer], ew_up[layer], ew_down[layer])
    out = _combine(y_sorted, lrow_kt, w_kt, ce, rel, nch, ps, xs, seq, g2, final_g, tm)
    return out.reshape(b, seq, d)
```
